```python
import jax
import jax.numpy as jnp
from jax import lax
import numpy as np

D_MODEL = 1024
BATCH = 16
SEQ = 4096
DEPTH = 4

GRID_W = 64
CTX_LEN = 256
HEAD_DIM = 64
ATTN_WIDTH = D_MODEL // 2
ATTN_HEADS = ATTN_WIDTH // HEAD_DIM
ATTN_KV_HEADS = ATTN_HEADS // 4
KV_WIDTH = ATTN_KV_HEADS * HEAD_DIM
CONV_WIDTH = D_MODEL // 4
CONV_SIZE = 3
HGRN_WIDTH = D_MODEL // 4
HGRN_HEADS = HGRN_WIDTH // HEAD_DIM
HGRN_CHUNK = 64
MIX_WIDTH = ATTN_WIDTH + CONV_WIDTH + HGRN_WIDTH
IN_SIZES = (ATTN_WIDTH, KV_WIDTH, KV_WIDTH, CONV_WIDTH, CONV_WIDTH, CONV_WIDTH, HGRN_WIDTH, HGRN_WIDTH, HGRN_WIDTH, HGRN_WIDTH, HGRN_WIDTH)
IN_WIDTH = sum(IN_SIZES)
Q_BLOCK = 128
ROPE_THETA = 10000.0
ATTN_SCALE = HEAD_DIM ** -0.5
D_FF = 256 * ((8 * D_MODEL // 3 + 255) // 256)
N_EXPERTS = 8
TOP_K = 2
D_FF_EXPERT = 7 * D_MODEL // 2
MOE_BLOCK = 512
N_DENSE = (DEPTH + 1) // 2
N_MOE = DEPTH // 2
N_MOD = 6
EPS = 1e-6

kernel_name = 'hybrid_dit_attn_conv_hgrn2_moe'


def rmsnorm(x, g):
    xf = x.astype(jnp.float32)
    y = xf * lax.rsqrt(jnp.mean(xf * xf, axis=-1, keepdims=True) + EPS)
    return (y * g.astype(jnp.float32)).astype(x.dtype)


def modulate(h, shift, scale):
    return h * (1.0 + scale) + shift


def heads(a, n):
    return a.reshape(a.shape[0], a.shape[1], n, a.shape[-1] // n)


def split_proj(p):
    out = []
    start = 0
    for size in IN_SIZES:
        out.append(p[..., start:start + size])
        start += size
    return out


def axial_rope_tables(rows):
    row = jnp.repeat(jnp.arange(rows, dtype=jnp.float32), GRID_W)
    col = jnp.tile(jnp.arange(GRID_W, dtype=jnp.float32), rows)
    axis_dim = HEAD_DIM // 2
    inv_freq = ROPE_THETA ** (-jnp.arange(0, axis_dim, 2, dtype=jnp.float32) / axis_dim)
    ang = jnp.concatenate([row[:, None] * inv_freq, col[:, None] * inv_freq], axis=-1)
    return jnp.cos(ang), jnp.sin(ang)


def apply_axial_rope(x, cos, sin):
    shp = x.shape
    nf = HEAD_DIM // 4
    xf = x.astype(jnp.float32).reshape(shp[0], shp[1], shp[2], 2, 2, nf)
    x1 = xf[..., 0, :]
    x2 = xf[..., 1, :]
    c = cos.reshape(shp[1], 1, 2, nf)
    s = sin.reshape(shp[1], 1, 2, nf)
    out = jnp.stack([x1 * c - x2 * s, x2 * c + x1 * s], axis=-2)
    return out.reshape(shp).astype(x.dtype)


def attend(q, k, v):
    s = jnp.einsum('bqgrd,bkgd->bgrqk', q, k).astype(jnp.float32) * ATTN_SCALE
    p = jax.nn.softmax(s, axis=-1).astype(v.dtype)
    return jnp.einsum('bgrqk,bkgd->bqgrd', p, v)


def blocked_attention(q, k, v):
    B, L, G, R, Dh = q.shape
    nb = L // Q_BLOCK
    qb = jnp.moveaxis(q.reshape(B, nb, Q_BLOCK, G, R, Dh), 1, 0)
    o = lax.map(lambda blk: attend(blk, k, v), qb)
    return jnp.moveaxis(o, 0, 1).reshape(B, L, G * R * Dh)


def short_conv(u, w):
    return lax.conv_general_dilated(u, w[:, None, :].astype(u.dtype), window_strides=(1,), padding='SAME',
                                    dimension_numbers=('NWC', 'WIO', 'NWC'), feature_group_count=u.shape[-1])


def hgrn_gates(z, lb_d):
    zf = z.astype(jnp.float32)
    log_f = jnp.logaddexp(jnp.log(lb_d), jnp.log1p(-lb_d) + jax.nn.log_sigmoid(zf))
    key = (1.0 - lb_d) * jax.nn.sigmoid(-zf)
    return heads(log_f, HGRN_HEADS), heads(key, HGRN_HEADS)


def gla_chunk_scan(q, k, v, log_f, s0):
    B, L, H, K = q.shape
    V = v.shape[-1]
    n = L // HGRN_CHUNK

    def to_chunks(a):
        return a.reshape(B, n, HGRN_CHUNK, H, a.shape[-1]).transpose(1, 0, 3, 2, 4)

    mask = jnp.tril(jnp.ones((HGRN_CHUNK, HGRN_CHUNK), dtype=bool))[:, :, None]

    def step(S, inp):
        qc, kc, vc, gc = inp
        b = jnp.cumsum(gc, axis=-2)
        diff = b[..., :, None, :] - b[..., None, :, :]
        decay = jnp.exp(jnp.where(mask, diff, -jnp.inf))
        A = jnp.einsum('bhtk,bhsk,bhtsk->bhts', qc, kc, decay)
        b_last = b[..., -1:, :]
        o = jnp.einsum('bhts,bhsv->bhtv', A, vc) + jnp.einsum('bhtk,bhkv->bhtv', qc * jnp.exp(b), S)
        S_new = jnp.exp(b_last[..., 0, :])[..., None] * S + jnp.einsum('bhsk,bhsv->bhkv', kc * jnp.exp(b_last - b), vc)
        return S_new, o

    S, o = lax.scan(step, s0, (to_chunks(q), to_chunks(k), to_chunks(v), to_chunks(log_f)))
    o = o.transpose(1, 0, 3, 2, 4).reshape(B, L, H, V)
    return o, S


def gla_final_state(k, v, log_f):
    b = jnp.cumsum(log_f, axis=1)
    return jnp.einsum('blhk,blhv->bhkv', k * jnp.exp(b[:, -1:] - b), v)


def hgrn_readout(o, out_gate, g):
    o = rmsnorm(o, g.reshape(HGRN_HEADS, HEAD_DIM))
    o = o.reshape(o.shape[0], o.shape[1], HGRN_WIDTH).astype(out_gate.dtype)
    return o * jax.nn.silu(out_gate)


def mixer(h_lat, h_ctx, w_in, w_out, q_g, k_g, conv_w, lb, hgrn_g, cos, sin, need_ctx):
    B, L, _ = h_lat.shape
    R = ATTN_HEADS // ATTN_KV_HEADS
    q_l, k_l, v_l, cb_l, cc_l, ch_l, zf_l, zb_l, i_l, hq_l, og_l = split_proj(h_lat @ w_in)
    q_c, k_c, v_c, cb_c, cc_c, ch_c, zf_c, zb_c, i_c, hq_c, og_c = split_proj(h_ctx @ w_in)

    ql = apply_axial_rope(rmsnorm(heads(q_l, ATTN_HEADS), q_g), cos, sin).reshape(B, L, ATTN_KV_HEADS, R, HEAD_DIM)
    kl = apply_axial_rope(rmsnorm(heads(k_l, ATTN_KV_HEADS), k_g), cos, sin)
    kc = rmsnorm(heads(k_c, ATTN_KV_HEADS), k_g)
    vl = heads(v_l, ATTN_KV_HEADS)
    vc = heads(v_c, ATTN_KV_HEADS)
    k_all = jnp.concatenate([kc, kl], axis=1)
    v_all = jnp.concatenate([vc, vl], axis=1)
    attn_l = blocked_attention(ql, k_all, v_all)

    conv_l = cb_l * short_conv(cc_l * ch_l, conv_w)

    s0 = jnp.zeros((B, HGRN_HEADS, HEAD_DIM, HEAD_DIM), jnp.float32)
    gf_l, kf_l = hgrn_gates(zf_l, lb[0])
    gb_l, kb_l = hgrn_gates(zb_l, lb[1])
    gf_c, kf_c = hgrn_gates(zf_c, lb[0])
    gb_c, kb_c = hgrn_gates(zb_c, lb[1])
    hv_l = heads(i_l.astype(jnp.float32), HGRN_HEADS)
    hv_c = heads(i_c.astype(jnp.float32), HGRN_HEADS)
    hql = heads(jax.nn.silu(hq_l.astype(jnp.float32)), HGRN_HEADS)
    flip = lambda a: jnp.flip(a, axis=1)
    if need_ctx:
        hqc = heads(jax.nn.silu(hq_c.astype(jnp.float32)), HGRN_HEADS)
        o_cf, s_cf = gla_chunk_scan(hqc, kf_c, hv_c, gf_c, s0)
        o_cb, s_cb = gla_chunk_scan(flip(hqc), flip(kb_c), flip(hv_c), flip(gb_c), s0)
        hgrn_c = hgrn_readout(o_cf + flip(o_cb), og_c, hgrn_g)
    else:
        s_cf = gla_final_state(kf_c, hv_c, gf_c)
        s_cb = gla_final_state(flip(kb_c), flip(hv_c), flip(gb_c))
    o_lf, _ = gla_chunk_scan(hql, kf_l, hv_l, gf_l, s_cf)
    o_lb, _ = gla_chunk_scan(flip(hql), flip(kb_l), flip(hv_l), flip(gb_l), s_cb)
    hgrn_l = hgrn_readout(o_lf + flip(o_lb), og_l, hgrn_g)

    y_lat = jnp.concatenate([attn_l, conv_l, hgrn_l], axis=-1) @ w_out
    if not need_ctx:
        return y_lat, None
    qc = rmsnorm(heads(q_c, ATTN_HEADS), q_g).reshape(B, h_ctx.shape[1], ATTN_KV_HEADS, R, HEAD_DIM)
    attn_c = attend(qc, kc, vc).reshape(B, h_ctx.shape[1], ATTN_WIDTH)
    conv_c = cb_c * short_conv(cc_c * ch_c, conv_w)
    y_ctx = jnp.concatenate([attn_c, conv_c, hgrn_c], axis=-1) @ w_out
    return y_lat, y_ctx


def swiglu(h, w_gate, w_up, w_down):
    return (jax.nn.silu(h @ w_gate) * (h @ w_up)) @ w_down


def moe_swiglu(h, w_router, w_gate, w_up, w_down):
    N, D = h.shape
    logits = (h @ w_router).astype(jnp.float32)
    top_logit, top_idx = lax.top_k(logits, TOP_K)
    top_w = jax.nn.softmax(top_logit, axis=-1)
    n_assign = N * TOP_K
    expert = top_idx.reshape(n_assign)
    token = jnp.broadcast_to(jnp.arange(N, dtype=jnp.int32)[:, None], (N, TOP_K)).reshape(n_assign)
    weight = top_w.reshape(n_assign).astype(h.dtype)
    order = jnp.argsort(expert)
    e_s = expert[order]
    t_s = token[order]
    w_s = weight[order]
    counts = jnp.bincount(expert, length=N_EXPERTS)
    padded = (counts + MOE_BLOCK - 1) // MOE_BLOCK * MOE_BLOCK
    pad_end = jnp.cumsum(padded)
    pad_start = pad_end - padded
    start = jnp.cumsum(counts) - counts
    dest = pad_start[e_s] + jnp.arange(n_assign) - start[e_s]
    n_blocks = -(-n_assign // MOE_BLOCK) + N_EXPERTS
    n_slots = n_blocks * MOE_BLOCK
    slot_tok = jnp.zeros((n_slots,), jnp.int32).at[dest].set(t_s)
    slot_w = jnp.zeros((n_slots,), h.dtype).at[dest].set(w_s)
    block_expert = jnp.minimum(jnp.searchsorted(pad_end, jnp.arange(n_blocks) * MOE_BLOCK, side='right'), N_EXPERTS - 1)
    xs = h[slot_tok].reshape(n_blocks, MOE_BLOCK, D)

    def run_block(args):
        xb, e = args
        return swiglu(xb, w_gate[e], w_up[e], w_down[e])

    ys = lax.map(run_block, (xs, block_expert)).reshape(n_slots, D)
    return jax.ops.segment_sum(ys * slot_w[:, None], slot_tok, num_segments=N)


def setup_inputs(seed: int = 0) -> dict:
    key = jax.random.key(seed)
    ks = jax.random.split(key, 21)

    def nrm(k, shape, scale):
        return jax.random.normal(k, shape, jnp.float32) * scale

    return {
        'x': nrm(ks[0], (BATCH, SEQ, D_MODEL), 1.0),
        'c': nrm(ks[1], (BATCH, D_MODEL), 1.0),
        'ctx': nrm(ks[2], (BATCH, CTX_LEN, D_MODEL), 1.0),
        'c_ctx': nrm(ks[3], (D_MODEL,), 1.0),
        'ada_w': nrm(ks[4], (DEPTH, D_MODEL, N_MOD * D_MODEL), 0.5 * D_MODEL ** -0.5),
        'ada_b': nrm(ks[5], (DEPTH, N_MOD * D_MODEL), 0.02),
        'norm_g': 1.0 + nrm(ks[6], (DEPTH, 4, D_MODEL), 0.05),
        'w_in': nrm(ks[7], (DEPTH, D_MODEL, IN_WIDTH), D_MODEL ** -0.5),
        'w_out': nrm(ks[8], (DEPTH, MIX_WIDTH, D_MODEL), MIX_WIDTH ** -0.5),
        'attn_q_g': 1.0 + nrm(ks[9], (DEPTH, HEAD_DIM), 0.05),
        'attn_k_g': 1.0 + nrm(ks[10], (DEPTH, HEAD_DIM), 0.05),
        'conv_w': nrm(ks[11], (DEPTH, CONV_SIZE, CONV_WIDTH), CONV_SIZE ** -0.5),
        'hgrn_lb': nrm(ks[12], (DEPTH, 2, HGRN_WIDTH), 0.5),
        'hgrn_g': 1.0 + nrm(ks[13], (DEPTH, HGRN_WIDTH), 0.05),
        'ffn_w_gate': nrm(ks[14], (N_DENSE, D_MODEL, D_FF), D_MODEL ** -0.5),
        'ffn_w_up': nrm(ks[15], (N_DENSE, D_MODEL, D_FF), D_MODEL ** -0.5),
        'ffn_w_down': nrm(ks[16], (N_DENSE, D_FF, D_MODEL), D_FF ** -0.5),
        'moe_router': nrm(ks[17], (N_MOE, D_MODEL, N_EXPERTS), D_MODEL ** -0.5),
        'moe_w_gate': nrm(ks[18], (N_MOE, N_EXPERTS, D_MODEL, D_FF_EXPERT), D_MODEL ** -0.5),
        'moe_w_up': nrm(ks[19], (N_MOE, N_EXPERTS, D_MODEL, D_FF_EXPERT), D_MODEL ** -0.5),
        'moe_w_down': nrm(ks[20], (N_MOE, N_EXPERTS, D_FF_EXPERT, D_MODEL), D_FF_EXPERT ** -0.5),
    }


def reference(x, c, ctx, c_ctx, ada_w, ada_b, norm_g, w_in, w_out, attn_q_g, attn_k_g, conv_w, hgrn_lb, hgrn_g,
              ffn_w_gate, ffn_w_up, ffn_w_down, moe_router, moe_w_gate, moe_w_up, moe_w_down):
    B, L, D = x.shape
    n_ctx = ctx.shape[1]
    ROWS = L // GRID_W
    cos, sin = axial_rope_tables(ROWS)
    lb_all = jnp.cumsum(jax.nn.softmax(hgrn_lb.astype(jnp.float32), axis=0), axis=0)
    lb_all = lb_all - lb_all[0]
    silu_c = jax.nn.silu(c)
    silu_cc = jax.nn.silu(c_ctx)
    for l in range(DEPTH):
        last = l == DEPTH - 1
        mod_lat = (silu_c @ ada_w[l] + ada_b[l])[:, None, :]
        mod_ctx = silu_cc @ ada_w[l] + ada_b[l]
        sh1, sc1, g1, sh2, sc2, g2 = jnp.split(mod_lat, N_MOD, axis=-1)
        csh1, csc1, cg1, csh2, csc2, cg2 = jnp.split(mod_ctx, N_MOD, axis=-1)

        h_lat = modulate(rmsnorm(x, norm_g[l, 0]), sh1, sc1)
        h_ctx = modulate(rmsnorm(ctx, norm_g[l, 0]), csh1, csc1)
        y_lat, y_ctx = mixer(h_lat, h_ctx, w_in[l], w_out[l], attn_q_g[l], attn_k_g[l], conv_w[l], lb_all[l],
                             hgrn_g[l], cos, sin, not last)
        x = x + g1 * rmsnorm(y_lat, norm_g[l, 1])
        if not last:
            ctx = ctx + cg1 * rmsnorm(y_ctx, norm_g[l, 1])

        f_in = modulate(rmsnorm(x, norm_g[l, 2]), sh2, sc2).reshape(B * L, D)
        if not last:
            fc_in = modulate(rmsnorm(ctx, norm_g[l, 2]), csh2, csc2).reshape(B * n_ctx, D)
            f_in = jnp.concatenate([f_in, fc_in], axis=0)
        if l % 2 == 0:
            f = swiglu(f_in, ffn_w_gate[l // 2], ffn_w_up[l // 2], ffn_w_down[l // 2])
        else:
            f = moe_swiglu(f_in, moe_router[l // 2], moe_w_gate[l // 2], moe_w_up[l // 2], moe_w_down[l // 2])
        x = x + g2 * rmsnorm(f[:B * L].reshape(B, L, D), norm_g[l, 3])
        if not last:
            ctx = ctx + cg2 * rmsnorm(f[B * L:].reshape(B, n_ctx, D), norm_g[l, 3])
    return x
```

```python
import functools

import jax
import jax.numpy as jnp
from jax import lax
from jax.experimental import pallas as pl
from jax.experimental.pallas import tpu as pltpu
from jax.experimental.pallas import tpu_sc as plsc

F32 = jnp.float32
BF16 = jnp.bfloat16

HEAD_DIM = 64
GRID_W = 64
ROPE_THETA = 10000.0
ATTN_SCALE = HEAD_DIM ** -0.5
HGRN_CHUNK = 64
N_EXPERTS = 8
N_MOD = 6
EPS = 1e-6

LANES = 128
TM = 256
FFN_TM = 512
MOE_BLK = 512
MOE_TF = 512
GATHER_WIN = 128
GATHER_SPLIT = 4
NEG_BIG = -1e30
VMEM_LIMIT = 56 * 1024 * 1024


def _cparams(*sem):
    return pltpu.CompilerParams(dimension_semantics=sem, vmem_limit_bytes=VMEM_LIMIT)


def _sigmoid(z):
    return 1.0 / (1.0 + jnp.exp(-z))


def _dot(a, b):
    return jnp.dot(a, b, preferred_element_type=F32)


def _dot_nt(a, b):
    return lax.dot_general(a, b, (((1,), (1,)), ((), ())), preferred_element_type=F32)


def _dot_tn(a, b):
    return lax.dot_general(a, b, (((0,), (0,)), ((), ())), preferred_element_type=F32)


def _rms(x, g):
    return x * lax.rsqrt(jnp.mean(x * x, axis=-1, keepdims=True) + EPS) * g


def _ada_kernel(c_ref, w_ref, b_ref, o_ref):
    c = c_ref[...]
    s = (c * _sigmoid(c)).astype(BF16)
    o_ref[...] = _dot(s, w_ref[...].astype(BF16)) + b_ref[...]


def _ada_call(c_all, ada_w, ada_b):
    depth, d, n = ada_w.shape
    r = c_all.shape[0]
    tn = 512
    return pl.pallas_call(
        _ada_kernel,
        grid=(depth, n // tn),
        in_specs=[
            pl.BlockSpec((r, d), lambda l, j: (0, 0)),
            pl.BlockSpec((None, d, tn), lambda l, j: (l, 0, j)),
            pl.BlockSpec((None, 1, tn), lambda l, j: (l, 0, j)),
        ],
        out_specs=pl.BlockSpec((None, r, tn), lambda l, j: (l, 0, j)),
        out_shape=jax.ShapeDtypeStruct((depth, r, n), F32),
        compiler_params=_cparams("parallel", "parallel"),
        name="ada",
    )(c_all, ada_w, ada_b.reshape(depth, 1, n))


def _inproj_kernel(x_ref, mod_ref, ng_ref, w_ref, cos_ref, sin_ref, qg_ref, kg_ref, gc_ref,
                   q_ref, k_ref, vv_ref, cbog_ref, u_ref, qv_ref, kk_ref, gg_ref):
    x = x_ref[...]
    h = _rms(x, ng_ref[0:1, :]) * (1.0 + mod_ref[1:2, :]) + mod_ref[0:1, :]
    hb = h.astype(BF16)

    def proj(c0, n):
        return _dot(hb, w_ref[:, c0:c0 + n])

    cos = cos_ref[...]
    sin = sin_ref[...]
    lane = lax.broadcasted_iota(jnp.int32, (TM, LANES), 1)
    first_half = (lane % 32) < 16
    lo = lane < HEAD_DIM

    def rope(v):
        partner = jnp.where(first_half, pltpu.roll(v, LANES - 16, 1), pltpu.roll(v, 16, 1))
        return v * cos + partner * sin

    qg = qg_ref[...]
    for hp in range(4):
        q2 = proj(2 * LANES * hp, 2 * LANES)
        for s in range(2):
            qh = q2[:, LANES * s:LANES * (s + 1)]
            r = lax.rsqrt(jnp.sum(qh * qh, axis=-1, keepdims=True) * (1.0 / HEAD_DIM) + EPS)
            q_ref[2 * hp + s] = (rope(qh * r * qg) * ATTN_SCALE).astype(BF16)

    kv = proj(8 * LANES, 2 * LANES)
    kx = kv[:, :LANES]
    v = kv[:, LANES:]
    k2 = kx * kx
    s_lo = jnp.sum(jnp.where(lo, k2, 0.0), axis=-1, keepdims=True)
    s_hi = jnp.sum(jnp.where(lo, 0.0, k2), axis=-1, keepdims=True)
    r = lax.rsqrt(jnp.where(lo, s_lo, s_hi) * (1.0 / HEAD_DIM) + EPS)
    k_ref[...] = rope(kx * r * kg_ref[...]).astype(BF16)
    vs = pltpu.roll(v, HEAD_DIM, 1)
    one = jnp.ones_like(v)
    vv_ref[0] = jnp.where(lo, v, one).astype(BF16)
    vv_ref[1] = jnp.where(lo, one, vs).astype(BF16)
    vv_ref[2] = jnp.where(lo, vs, one).astype(BF16)
    vv_ref[3] = jnp.where(lo, one, v).astype(BF16)

    c0 = 10 * LANES
    c3 = proj(c0, 6 * LANES)
    cw = 2 * LANES
    u_ref[...] = c3[:, cw:2 * cw] * c3[:, 2 * cw:3 * cw]

    hg = proj(c0 + 6 * LANES, 10 * LANES)
    zf, zb, iv, hq, og = (hg[:, cw * i:cw * (i + 1)] for i in range(5))
    cbog_ref[:, :cw] = c3[:, :cw].astype(BF16)
    cbog_ref[:, cw:] = og.astype(BF16)
    qv_ref[:, :cw] = (hq * _sigmoid(hq)).astype(BF16)
    qv_ref[:, cw:] = iv.astype(BF16)
    for d, z in enumerate((zf, zb)):
        log_lb = gc_ref[0:1, cw * d:cw * (d + 1)]
        log1m_lb = gc_ref[1:2, cw * d:cw * (d + 1)]
        one_m_lb = gc_ref[2:3, cw * d:cw * (d + 1)]
        t = jnp.exp(-jnp.abs(z))
        log_sig = jnp.minimum(z, 0.0) - jnp.log1p(t)
        a2 = log1m_lb + log_sig
        log_f = jnp.maximum(log_lb, a2) + jnp.log1p(jnp.exp(-jnp.abs(log_lb - a2)))
        sig_neg = jnp.where(z >= 0, t, 1.0) / (1.0 + t)
        gg_ref[:, cw * d:cw * (d + 1)] = log_f
        kk_ref[:, cw * d:cw * (d + 1)] = (one_m_lb * sig_neg).astype(BF16)


def _inproj_call(x_all, mod, ng, w1, cos_t, sin_t, qg, kg, gc):
    b, t, d = x_all.shape
    nt = t // TM
    nw = w1.shape[1]
    tok = lambda w: pl.BlockSpec((None, TM, w), lambda i, j: (i, j, 0))
    const = lambda r, w: pl.BlockSpec((r, w), lambda i, j: (0, 0))
    return pl.pallas_call(
        _inproj_kernel,
        grid=(b, nt),
        in_specs=[
            tok(d),
            pl.BlockSpec((None, None, N_MOD, d), lambda i, j: (i, jnp.minimum(j, 1), 0, 0)),
            const(8, d),
            const(d, nw),
            pl.BlockSpec((TM, LANES), lambda i, j: (j, 0)),
            pl.BlockSpec((TM, LANES), lambda i, j: (j, 0)),
            const(1, LANES),
            const(1, LANES),
            const(8, 4 * LANES),
        ],
        out_specs=[
            pl.BlockSpec((None, 8, TM, LANES), lambda i, j: (i, 0, j, 0)),
            tok(LANES),
            pl.BlockSpec((None, 4, TM, LANES), lambda i, j: (i, 0, j, 0)),
            tok(4 * LANES),
            tok(2 * LANES),
            tok(4 * LANES),
            tok(4 * LANES),
            tok(4 * LANES),
        ],
        out_shape=[
            jax.ShapeDtypeStruct((b, 8, t, LANES), BF16),
            jax.ShapeDtypeStruct((b, t, LANES), BF16),
            jax.ShapeDtypeStruct((b, 4, t, LANES), BF16),
            jax.ShapeDtypeStruct((b, t, 4 * LANES), BF16),
            jax.ShapeDtypeStruct((b, t, 2 * LANES), F32),
            jax.ShapeDtypeStruct((b, t, 4 * LANES), BF16),
            jax.ShapeDtypeStruct((b, t, 4 * LANES), BF16),
            jax.ShapeDtypeStruct((b, t, 4 * LANES), F32),
        ],
        compiler_params=_cparams("parallel", "parallel"),
        name="inproj",
    )(x_all, mod, ng, w1, cos_t, sin_t, qg, kg, gc)


def _attn_kernel(q_ref, k_ref, vv_ref, o_ref, m_ref, acc_ref, *, n_kv):
    qi = pl.program_id(1)
    nkv = jnp.where(qi == 0, 1, n_kv)
    m_ref[...] = jnp.full(m_ref.shape, NEG_BIG, F32)
    acc_ref[...] = jnp.zeros(acc_ref.shape, F32)
    q = q_ref[...].reshape(8 * TM, LANES)

    def body(j, carry):
        r0 = pl.multiple_of(j * TM, TM)
        kt = k_ref[pl.ds(r0, TM), :]
        s_all = _dot_nt(q, kt)
        for h in range(8):
            s = s_all[h * TM:(h + 1) * TM]
            m_old = m_ref[h]
            m_new = jnp.maximum(m_old, jnp.max(s, axis=-1, keepdims=True))
            p = jnp.exp(s - m_new).astype(BF16)
            pv = _dot(p, vv_ref[2 * (h // 4) + (h % 2), pl.ds(r0, TM), :])
            acc_ref[h] = jnp.exp(m_old - m_new) * acc_ref[h] + pv
            m_ref[h] = m_new
        return carry

    lax.fori_loop(0, nkv, body, 0)
    lo = lax.broadcasted_iota(jnp.int32, (TM, LANES), 1) < HEAD_DIM
    for j in range(4):
        ae = acc_ref[2 * j]
        ao = acc_ref[2 * j + 1]
        oe = ae / pltpu.roll(ae, HEAD_DIM, 1)
        oo = ao / pltpu.roll(ao, HEAD_DIM, 1)
        o_ref[:, LANES * j:LANES * (j + 1)] = jnp.where(lo, oe, oo).astype(BF16)


def _attn_call(q, k, vv):
    b, _, t, _ = q.shape
    nt = t // TM
    return pl.pallas_call(
        functools.partial(_attn_kernel, n_kv=nt),
        grid=(b, nt),
        in_specs=[
            pl.BlockSpec((None, 8, TM, LANES), lambda i, j: (i, 0, j, 0)),
            pl.BlockSpec((None, t, LANES), lambda i, j: (i, 0, 0)),
            pl.BlockSpec((None, 4, t, LANES), lambda i, j: (i, 0, 0, 0)),
        ],
        out_specs=pl.BlockSpec((None, TM, 4 * LANES), lambda i, j: (i, j, 0)),
        out_shape=jax.ShapeDtypeStruct((b, t, 4 * LANES), BF16),
        scratch_shapes=[pltpu.VMEM((8, TM, 1), F32), pltpu.VMEM((8, TM, LANES), F32)],
        compiler_params=_cparams("parallel", "parallel"),
        name="attn",
    )(q, k, vv)


def _split3(g):
    hi = g.astype(BF16)
    r1 = g - hi.astype(F32)
    mid = r1.astype(BF16)
    lo = (r1 - mid.astype(F32)).astype(BF16)
    return hi, mid, lo


def _hgrn_kernel(qvf_ref, kf_ref, gf_ref, qvb_ref, kb_ref, gb_ref, tri_ref, trij_ref, j_ref, e_ref, bd_ref,
                 of_ref, ob_ref, st_ref, b_s, k_s):
    hw = 2 * LANES
    nch = TM // HGRN_CHUNK

    @pl.when(pl.program_id(1) == 0)
    def _():
        st_ref[...] = jnp.zeros(st_ref.shape, F32)

    flip = j_ref[...]

    def cumsum(m_ref, g):
        m = m_ref[...]
        hi, mid, lo = _split3(g)
        return _dot(m, hi) + _dot(m, mid) + _dot(m, lo)

    qvf = qvf_ref[...]
    fl = _dot(flip, jnp.concatenate([qvb_ref[...], kb_ref[...]], axis=1))
    q = jnp.concatenate([qvf[:, :hw].astype(F32), fl[:, :hw]], axis=0)
    v = jnp.concatenate([qvf[:, hw:].astype(F32), fl[:, hw:2 * hw]], axis=0)
    k = jnp.concatenate([kf_ref[...].astype(F32), fl[:, 2 * hw:]], axis=0)
    b = jnp.concatenate([cumsum(tri_ref, gf_ref[...]), cumsum(trij_ref, gb_ref[...])], axis=0)

    ng = 2 * nch
    b_s[...] = b.reshape(ng, HGRN_CHUNK, hw)
    k_s[...] = k.reshape(ng, HGRN_CHUNK, hw)
    q3 = q.reshape(ng, HGRN_CHUNK, hw)
    b3 = b.reshape(ng, HGRN_CHUNK, hw)
    tix = lax.broadcasted_iota(jnp.int32, (ng, HGRN_CHUNK, hw), 1)

    def body(s, a):
        bs = b_s[:, pl.ds(s, 1), :]
        ks = k_s[:, pl.ds(s, 1), :]
        x = jnp.where(tix >= s, q3 * jnp.exp(jnp.minimum(b3 - bs, 0.0)) * ks, 0.0)
        return a + _dot(x.reshape(2 * TM, hw).astype(BF16), e_ref[s])

    a = lax.fori_loop(0, HGRN_CHUNK, body, jnp.zeros((2 * TM, hw), F32))

    bd = bd_ref[...]
    for d in range(2):
        st = st_ref[d]
        outs = []
        for c in range(nch):
            r0 = TM * d + HGRN_CHUNK * c
            bc = b[r0:r0 + HGRN_CHUNK]
            qc = q[r0:r0 + HGRN_CHUNK]
            kc = k[r0:r0 + HGRN_CHUNK]
            vc = v[r0:r0 + HGRN_CHUNK]
            ac = a[r0:r0 + HGRN_CHUNK]
            bl = bc[HGRN_CHUNK - 1:HGRN_CHUNK, :]
            qe = (qc * jnp.exp(bc)).astype(BF16)
            ke = (kc * jnp.exp(bl - bc)).astype(BF16)
            vbd = (jnp.concatenate([vc] * 4, axis=0) * bd).astype(BF16)
            outs.append(_dot(ac.astype(BF16), vbd) + _dot_nt(qe, st.astype(BF16)))
            st = st * jnp.exp(bl) + _dot_tn(vc.astype(BF16), ke) * bd
        st_ref[d] = st
        od = jnp.concatenate(outs, axis=0).astype(BF16)
        if d == 0:
            of_ref[...] = od
        else:
            ob_ref[...] = _dot(flip, od).astype(BF16)


def _hgrn_consts():
    hw = 2 * LANES
    r = jnp.arange(TM)
    same_chunk = (r[:, None] // HGRN_CHUNK) == (r[None, :] // HGRN_CHUNK)
    tri = (same_chunk & (r[None, :] <= r[:, None])).astype(F32)
    flip = (r[:, None] + r[None, :] == TM - 1).astype(F32)
    trij = tri @ flip
    c = jnp.arange(hw)
    s = jnp.arange(HGRN_CHUNK)
    e = ((c[None, :, None] // HEAD_DIM == c[None, None, :] // HGRN_CHUNK)
         & (c[None, None, :] % HGRN_CHUNK == s[:, None, None])).astype(BF16)
    bd = (c[:, None] // HEAD_DIM == c[None, :] // HEAD_DIM).astype(F32)
    return tri.astype(BF16), trij.astype(BF16), flip.astype(BF16), e, bd


def _hgrn_call(qv, kk, gg, consts):
    b, t, _ = qv.shape
    nt = t // TM
    hw = 2 * LANES
    tri, trij, flip, e, bd = consts
    fwd = lambda i, j: j
    bwd = lambda i, j: jnp.where(j == 0, 0, nt - j)
    const2 = lambda a: pl.BlockSpec(a.shape, lambda i, j: (0,) * a.ndim)
    return pl.pallas_call(
        _hgrn_kernel,
        grid=(b, nt),
        in_specs=[
            pl.BlockSpec((None, TM, 2 * hw), lambda i, j: (i, fwd(i, j), 0)),
            pl.BlockSpec((None, TM, hw), lambda i, j: (i, fwd(i, j), 0)),
            pl.BlockSpec((None, TM, hw), lambda i, j: (i, fwd(i, j), 0)),
            pl.BlockSpec((None, TM, 2 * hw), lambda i, j: (i, bwd(i, j), 0)),
            pl.BlockSpec((None, TM, hw), lambda i, j: (i, bwd(i, j), 1)),
            pl.BlockSpec((None, TM, hw), lambda i, j: (i, bwd(i, j), 1)),
            const2(tri), const2(trij), const2(flip), const2(e), const2(bd),
        ],
        out_specs=[
            pl.BlockSpec((None, TM, hw), lambda i, j: (i, fwd(i, j), 0)),
            pl.BlockSpec((None, TM, hw), lambda i, j: (i, bwd(i, j), 0)),
        ],
        out_shape=[jax.ShapeDtypeStruct((b, t, hw), BF16)] * 2,
        scratch_shapes=[
            pltpu.VMEM((2, hw, hw), F32),
            pltpu.VMEM((2 * TM // HGRN_CHUNK, HGRN_CHUNK, hw), F32),
            pltpu.VMEM((2 * TM // HGRN_CHUNK, HGRN_CHUNK, hw), F32),
        ],
        compiler_params=_cparams("parallel", "arbitrary"),
        name="hgrn",
    )(qv, kk, gg, qv, kk, gg, tri, trij, flip, e, bd)


def _outproj_kernel(x_ref, mod_ref, ng_ref, attn_ref, cbog_ref, u_ref, up_ref, un_ref, of_ref, ob_ref,
                    hgg_ref, cw_ref, hn_ref, w_ref, o_ref, *, n_tiles):
    t = pl.program_id(1)
    cw = 2 * LANES
    u = u_ref[...]
    row = lax.broadcasted_iota(jnp.int32, (TM, cw), 0)
    prev_row = jnp.where(t >= 2, up_ref[7:8, :], 0.0)
    next_row = jnp.where((t >= 1) & (t < n_tiles - 1), un_ref[0:1, :], 0.0)
    u_prev = jnp.where(row == 0, prev_row, pltpu.roll(u, 1, 0))
    u_next = jnp.where(row == TM - 1, next_row, pltpu.roll(u, TM - 1, 0))
    cbog = cbog_ref[...].astype(F32)
    conv = cbog[:, :cw] * (cw_ref[0:1, :] * u_prev + cw_ref[1:2, :] * u + cw_ref[2:3, :] * u_next)

    o = of_ref[...].astype(F32) + ob_ref[...].astype(F32)
    ms = _dot((o * o).astype(BF16), hn_ref[...])
    og = cbog[:, cw:]
    hg = o * lax.rsqrt(ms + EPS) * hgg_ref[...] * (og * _sigmoid(og))

    y = (_dot(attn_ref[...], w_ref[0:2 * cw, :]) + _dot(conv.astype(BF16), w_ref[2 * cw:3 * cw, :])
         + _dot(hg.astype(BF16), w_ref[3 * cw:4 * cw, :]))
    o_ref[...] = x_ref[...] + mod_ref[2:3, :] * _rms(y, ng_ref[1:2, :])


def _outproj_call(x_all, mod, ng, attn, cbog, u, o_f, o_b, hgg, conv_w, hn, w_out):
    b, t, d = x_all.shape
    nt = t // TM
    cw = 2 * LANES
    nb8 = t // 8
    tok = lambda w: pl.BlockSpec((None, TM, w), lambda i, j: (i, j, 0))
    const = lambda r, w: pl.BlockSpec((r, w), lambda i, j: (0, 0))
    return pl.pallas_call(
        functools.partial(_outproj_kernel, n_tiles=nt),
        grid=(b, nt),
        in_specs=[
            tok(d),
            pl.BlockSpec((None, None, N_MOD, d), lambda i, j: (i, jnp.minimum(j, 1), 0, 0)),
            const(8, d),
            tok(2 * cw),
            tok(2 * cw),
            tok(cw),
            pl.BlockSpec((None, 8, cw), lambda i, j: (i, jnp.maximum(j * (TM // 8) - 1, 0), 0)),
            pl.BlockSpec((None, 8, cw), lambda i, j: (i, jnp.minimum((j + 1) * (TM // 8), nb8 - 1), 0)),
            tok(cw),
            tok(cw),
            const(1, cw),
            const(8, cw),
            const(cw, cw),
            const(d, d),
        ],
        out_specs=tok(d),
        out_shape=jax.ShapeDtypeStruct((b, t, d), F32),
        compiler_params=_cparams("parallel", "parallel"),
        name="outproj",
    )(x_all, mod, ng, attn, cbog, u, u, u, o_f, o_b, hgg, conv_w, hn, w_out)


def _mod_rows(modt_ref, i, rows):
    sub = rows // modt_ref.shape[0]
    return jnp.concatenate(
        [jnp.broadcast_to(modt_ref[s, i:i + 1, :], (sub, modt_ref.shape[2])) for s in range(modt_ref.shape[0])],
        axis=0)


def _store_split(ref, val):
    w = ref.shape[2]
    for c in range(ref.shape[0]):
        ref[c] = val[:, w * c:w * (c + 1)]


def _load_split(ref):
    return jnp.concatenate([ref[c] for c in range(ref.shape[0])], axis=1)


def _split_spec(rows):
    return lambda d: pl.BlockSpec((GATHER_SPLIT, rows, d // GATHER_SPLIT), lambda i, *_: (0, i, 0))


def _ffn_kernel(x_ref, modt_ref, ng_ref, wg_ref, wu_ref, wd_ref, o_ref, *, tf):
    x = x_ref[...]
    rows = x.shape[0]
    h = (_rms(x, ng_ref[2:3, :]) * (1.0 + _mod_rows(modt_ref, 4, rows)) + _mod_rows(modt_ref, 3, rows)).astype(BF16)
    acc = jnp.zeros(x.shape, F32)
    for f0 in range(0, wg_ref.shape[1], tf):
        g = _dot(h, wg_ref[:, f0:f0 + tf])
        up = _dot(h, wu_ref[:, f0:f0 + tf])
        acc = acc + _dot((g * _sigmoid(g) * up).astype(BF16), wd_ref[f0:f0 + tf, :])
    o_ref[...] = x + _mod_rows(modt_ref, 5, rows) * _rms(acc, ng_ref[3:4, :])


def _ffn_call(x_flat, modt, ng, wg, wu, wd):
    n, d = x_flat.shape
    dff = wg.shape[1]
    nsub = FFN_TM // TM
    resident = lambda a: pl.BlockSpec(a.shape, lambda i: (0, 0), pipeline_mode=pl.Buffered(1))
    return pl.pallas_call(
        functools.partial(_ffn_kernel, tf=256),
        grid=(n // FFN_TM,),
        in_specs=[
            pl.BlockSpec((FFN_TM, d), lambda i: (i, 0)),
            pl.BlockSpec((nsub, N_MOD, d), lambda i: (i, 0, 0)),
            pl.BlockSpec((8, d), lambda i: (0, 0)),
            resident(wg), resident(wu), resident(wd),
        ],
        out_specs=pl.BlockSpec((FFN_TM, d), lambda i: (i, 0)),
        out_shape=jax.ShapeDtypeStruct((n, d), F32),
        compiler_params=_cparams("parallel"),
        name="ffn",
    )(x_flat, modt, ng, wg, wu, wd)


def _router_kernel(x_ref, modt_ref, ng_ref, wr_ref, h_ref, r_ref):
    x = x_ref[...]
    rows = x.shape[0]
    h = _rms(x, ng_ref[2:3, :]) * (1.0 + _mod_rows(modt_ref, 4, rows)) + _mod_rows(modt_ref, 3, rows)
    _store_split(h_ref, h)
    h_hi = h.astype(BF16)
    h_lo = (h - h_hi.astype(F32)).astype(BF16)
    logits = _dot(h_hi, wr_ref[0]) + _dot(h_lo, wr_ref[0]) + _dot(h_hi, wr_ref[1])
    lane = lax.broadcasted_iota(jnp.int32, logits.shape, 1)
    lg = jnp.where(lane < N_EXPERTS, logits, NEG_BIG)
    m1 = jnp.max(lg, axis=-1, keepdims=True)
    i1 = jnp.min(jnp.where(lg == m1, lane, LANES), axis=-1, keepdims=True)
    lg2 = jnp.where(lane == i1, NEG_BIG, lg)
    m2 = jnp.max(lg2, axis=-1, keepdims=True)
    i2 = jnp.min(jnp.where(lg2 == m2, lane, LANES), axis=-1, keepdims=True)
    e2 = jnp.exp(m2 - m1)
    w1 = 1.0 / (1.0 + e2)
    w2 = e2 / (1.0 + e2)
    r_ref[...] = jnp.where(lane == 0, i1.astype(F32),
                           jnp.where(lane == 1, i2.astype(F32),
                                     jnp.where(lane == 2, w1, jnp.where(lane == 3, w2, 0.0))))


def _router_call(x_flat, modt, ng, wr):
    n, d = x_flat.shape
    nsub = FFN_TM // TM
    return pl.pallas_call(
        _router_kernel,
        grid=(n // FFN_TM,),
        in_specs=[
            pl.BlockSpec((FFN_TM, d), lambda i: (i, 0)),
            pl.BlockSpec((nsub, N_MOD, d), lambda i: (i, 0, 0)),
            pl.BlockSpec((8, d), lambda i: (0, 0)),
            pl.BlockSpec(wr.shape, lambda i: (0, 0, 0)),
        ],
        out_specs=[_split_spec(FFN_TM)(d), pl.BlockSpec((FFN_TM, LANES), lambda i: (i, 0))],
        out_shape=[jax.ShapeDtypeStruct((GATHER_SPLIT, n, d // GATHER_SPLIT), F32),
                   jax.ShapeDtypeStruct((n, LANES), F32)],
        compiler_params=_cparams("parallel"),
        name="router",
    )(x_flat, modt, ng, wr)


def _gather_flat(data, idx):
    m = idx.shape[0]
    w = data.shape[1]
    mesh = plsc.VectorSubcoreMesh(core_axis_name="core", subcore_axis_name="subcore")

    @functools.partial(pl.kernel, out_type=jax.ShapeDtypeStruct((m, w), data.dtype), mesh=mesh)
    def gather(x_hbm, i_hbm, o_hbm):
        def body(i_vmem, o_vmem):
            pltpu.sync_copy(x_hbm.at[i_vmem.at[0]], o_vmem)

        pltpu.emit_pipeline(
            body,
            grid=(m // GATHER_WIN,),
            in_specs=[pl.BlockSpec((1, GATHER_WIN), lambda i: (0, i))],
            out_specs=[pl.BlockSpec((GATHER_WIN, w), lambda i: (i, 0))],
            core_axis_name=("core", "subcore"),
            dimension_semantics=(pltpu.PARALLEL,),
        )(i_hbm, o_hbm)

    return gather(data, idx.reshape(1, m))


def _gather_rows(data, idx):
    s, r, w = data.shape
    m = idx.shape[0]
    idx_all = (idx[None, :] + (jnp.arange(s, dtype=jnp.int32) * r)[:, None]).reshape(s * m)
    return _gather_flat(data.reshape(s * r, w), idx_all).reshape(s, m, w)


def _expert_kernel(be_ref, nv_ref, xs_ref, wg_ref, wu_ref, wd_ref, o_ref, acc_ref, *, nf):
    i = pl.program_id(0)
    f = pl.program_id(1)
    valid = i < nv_ref[0]

    @pl.when(valid)
    def _():
        xb = _load_split(xs_ref).astype(BF16)
        g = _dot(xb, wg_ref[...])
        up = _dot(xb, wu_ref[...])
        part = _dot((g * _sigmoid(g) * up).astype(BF16), wd_ref[...])

        @pl.when(f == 0)
        def _():
            acc_ref[...] = part

        @pl.when(f > 0)
        def _():
            acc_ref[...] += part

    @pl.when(f == nf - 1)
    def _():
        _store_split(o_ref, jnp.where(valid, acc_ref[...], 0.0))


def _expert_call(block_expert, n_valid, xs, wg, wu, wd):
    n_slots = xs.shape[1]
    d = wg.shape[1]
    dff = wg.shape[2]
    nf = dff // MOE_TF
    nb = n_slots // MOE_BLK
    grid_spec = pltpu.PrefetchScalarGridSpec(
        num_scalar_prefetch=2,
        grid=(nb, nf),
        in_specs=[
            _split_spec(MOE_BLK)(d),
            pl.BlockSpec((None, d, MOE_TF), lambda i, f, be, nv: (be[i], 0, f)),
            pl.BlockSpec((None, d, MOE_TF), lambda i, f, be, nv: (be[i], 0, f)),
            pl.BlockSpec((None, MOE_TF, d), lambda i, f, be, nv: (be[i], f, 0)),
        ],
        out_specs=_split_spec(MOE_BLK)(d),
        scratch_shapes=[pltpu.VMEM((MOE_BLK, d), F32)],
    )
    return pl.pallas_call(
        functools.partial(_expert_kernel, nf=nf),
        grid_spec=grid_spec,
        out_shape=jax.ShapeDtypeStruct((GATHER_SPLIT, n_slots, d // GATHER_SPLIT), F32),
        compiler_params=_cparams("arbitrary", "arbitrary"),
        name="experts",
    )(block_expert, n_valid, xs, wg, wu, wd)


def _combine_kernel(x_ref, modt_ref, ng_ref, y1_ref, y2_ref, r_ref, o_ref):
    x = x_ref[...]
    rows = x.shape[0]
    f = r_ref[:, 2:3] * _load_split(y1_ref) + r_ref[:, 3:4] * _load_split(y2_ref)
    o_ref[...] = x + _mod_rows(modt_ref, 5, rows) * _rms(f, ng_ref[3:4, :])


def _combine_call(x_flat, modt, ng, y1, y2, rinfo):
    n, d = x_flat.shape
    nsub = FFN_TM // TM
    tok = lambda w: pl.BlockSpec((FFN_TM, w), lambda i: (i, 0))
    return pl.pallas_call(
        _combine_kernel,
        grid=(n // FFN_TM,),
        in_specs=[tok(d), pl.BlockSpec((nsub, N_MOD, d), lambda i: (i, 0, 0)), pl.BlockSpec((8, d), lambda i: (0, 0)),
                  _split_spec(FFN_TM)(d), _split_spec(FFN_TM)(d), tok(LANES)],
        out_specs=tok(d),
        out_shape=jax.ShapeDtypeStruct((n, d), F32),
        compiler_params=_cparams("parallel"),
        name="combine",
    )(x_flat, modt, ng, y1, y2, rinfo)


def _moe_layer(x_flat, modt, ng, w_router, wg, wu, wd):
    n, d = x_flat.shape
    wr = jnp.pad(w_router, ((0, 0), (0, LANES - N_EXPERTS)))
    wr_hi = wr.astype(BF16)
    wr_lo = (wr - wr_hi.astype(F32)).astype(BF16)
    h, rinfo = _router_call(x_flat, modt, ng, jnp.stack([wr_hi, wr_lo]))

    n_assign = 2 * n
    expert = rinfo[:, :2].astype(jnp.int32).reshape(n_assign)
    onehot = (expert[:, None] == jnp.arange(N_EXPERTS, dtype=jnp.int32)[None, :]).astype(jnp.int32)
    csum = jnp.cumsum(onehot, axis=0)
    rank = jnp.sum((csum - onehot) * onehot, axis=1)
    counts = csum[-1]
    padded = (counts + MOE_BLK - 1) // MOE_BLK * MOE_BLK
    pad_end = jnp.cumsum(padded)
    pad_start = pad_end - padded
    dest = jnp.sum(onehot * pad_start[None, :], axis=1) + rank
    nb = -(-n_assign // MOE_BLK) + N_EXPERTS
    row_quantum = 32 * GATHER_WIN // GATHER_SPLIT
    nb = -(-nb * MOE_BLK // row_quantum) * row_quantum // MOE_BLK
    n_slots = nb * MOE_BLK
    slot_tok = jnp.zeros((n_slots,), jnp.int32).at[dest].set(jnp.arange(n_assign, dtype=jnp.int32) // 2)
    block_expert = jnp.minimum(
        jnp.searchsorted(pad_end, jnp.arange(nb, dtype=jnp.int32) * MOE_BLK, side="right"), N_EXPERTS - 1
    ).astype(jnp.int32)
    n_valid = (pad_end[-1:] // MOE_BLK).astype(jnp.int32)

    xs = _gather_rows(h, slot_tok)
    ys = _expert_call(block_expert, n_valid, xs, wg, wu, wd)
    dest2 = dest.reshape(n, 2)
    y1 = _gather_rows(ys, dest2[:, 0])
    y2 = _gather_rows(ys, dest2[:, 1])
    return _combine_call(x_flat, modt, ng, y1, y2, rinfo)


def _rope_tables(n_ctx, length):
    rows = length // GRID_W
    row = jnp.repeat(jnp.arange(rows, dtype=F32), GRID_W)
    col = jnp.tile(jnp.arange(GRID_W, dtype=F32), rows)
    axis_dim = HEAD_DIM // 2
    inv_freq = ROPE_THETA ** (-jnp.arange(0, axis_dim, 2, dtype=F32) / axis_dim)
    ar = row[:, None] * inv_freq
    ac = col[:, None] * inv_freq
    cos = jnp.concatenate([jnp.cos(ar), jnp.cos(ar), jnp.cos(ac), jnp.cos(ac)], axis=1)
    sin = jnp.concatenate([-jnp.sin(ar), jnp.sin(ar), -jnp.sin(ac), jnp.sin(ac)], axis=1)
    cos = jnp.concatenate([jnp.ones((n_ctx, HEAD_DIM), F32), cos], axis=0)
    sin = jnp.concatenate([jnp.zeros((n_ctx, HEAD_DIM), F32), sin], axis=0)
    return jnp.tile(cos, (1, 2)), jnp.tile(sin, (1, 2))


def _inproj_weights(w_in_l):
    d = w_in_l.shape[0]
    aw = d // 2
    nh = aw // HEAD_DIM
    wq = w_in_l[:, :aw].reshape(d, nh, HEAD_DIM)
    z = jnp.zeros((d, nh // 2, HEAD_DIM), w_in_l.dtype)
    q_pad = jnp.concatenate([
        jnp.concatenate([wq[:, :nh // 2], z], axis=2),
        jnp.concatenate([z, wq[:, nh // 2:]], axis=2),
    ], axis=1).reshape(d, nh * LANES)
    return jnp.concatenate([q_pad, w_in_l[:, aw:]], axis=1).astype(BF16)


def kernel(x, c, ctx, c_ctx, ada_w, ada_b, norm_g, w_in, w_out, attn_q_g, attn_k_g, conv_w, hgrn_lb, hgrn_g,
           ffn_w_gate, ffn_w_up, ffn_w_down, moe_router, moe_w_gate, moe_w_up, moe_w_down):
    bsz, length, d = x.shape
    n_ctx = ctx.shape[1]
    depth = w_in.shape[0]
    t = n_ctx + length
    nt = t // TM
    assert n_ctx == TM and length % TM == 0 and d == 8 * LANES
    assert (bsz * t) % FFN_TM == 0 and (bsz * t) % (32 * GATHER_WIN // GATHER_SPLIT) == 0

    r_pad = -(-(bsz + 1) // 8) * 8
    c_all = jnp.concatenate([c, c_ctx[None, :], jnp.zeros((r_pad - bsz - 1, d), F32)], axis=0)
    mod_all = _ada_call(c_all, ada_w, ada_b).reshape(depth, r_pad, N_MOD, d)

    cos_t, sin_t = _rope_tables(n_ctx, length)
    lb_all = jnp.cumsum(jax.nn.softmax(hgrn_lb.astype(F32), axis=0), axis=0)
    lb_all = lb_all - lb_all[0]
    hconsts = _hgrn_consts()
    cw = 2 * LANES
    cidx = jnp.arange(cw)
    head_mean = ((cidx[:, None] // HEAD_DIM == cidx[None, :] // HEAD_DIM).astype(F32) / HEAD_DIM).astype(BF16)

    x_all = jnp.concatenate([ctx, x], axis=1)
    for l in range(depth):
        mod_lat = mod_all[l, :bsz]
        mod_ctx = jnp.broadcast_to(mod_all[l, bsz][None], (bsz, N_MOD, d))
        mod = jnp.stack([mod_ctx, mod_lat], axis=1)
        modt = jnp.concatenate([mod[:, :1], jnp.broadcast_to(mod[:, 1:], (bsz, nt - 1, N_MOD, d))], axis=1)
        modt = modt.reshape(bsz * nt, N_MOD, d)
        ng = jnp.pad(norm_g[l], ((0, 4), (0, 0)))
        lb = lb_all[l].reshape(1, 2 * cw)
        gc = jnp.concatenate([jnp.log(lb), jnp.log1p(-lb), 1.0 - lb, jnp.zeros((5, 2 * cw), F32)], axis=0)
        qg = jnp.tile(attn_q_g[l], 2)[None, :]
        kg = jnp.tile(attn_k_g[l], 2)[None, :]

        q, k, vv, cbog, u, qv, kk, gg = _inproj_call(x_all, mod, ng, _inproj_weights(w_in[l]), cos_t, sin_t, qg, kg, gc)
        attn = _attn_call(q, k, vv)
        o_f, o_b = _hgrn_call(qv, kk, gg, hconsts)
        x_all = _outproj_call(x_all, mod, ng, attn, cbog, u, o_f, o_b, hgrn_g[l][None, :],
                              jnp.pad(conv_w[l], ((0, 5), (0, 0))), head_mean, w_out[l].astype(BF16))

        x_flat = x_all.reshape(bsz * t, d)
        if l % 2 == 0:
            x_flat = _ffn_call(x_flat, modt, ng, ffn_w_gate[l // 2].astype(BF16), ffn_w_up[l // 2].astype(BF16),
                               ffn_w_down[l // 2].astype(BF16))
        else:
            x_flat = _moe_layer(x_flat, modt, ng, moe_router[l // 2], moe_w_gate[l // 2].astype(BF16),
                                moe_w_up[l // 2].astype(BF16), moe_w_down[l // 2].astype(BF16))
        x_all = x_flat.reshape(bsz, t, d)
    return x_all[:, n_ctx:, :]
```

```python
import functools

import jax
import jax.numpy as jnp
from jax import lax
from jax.experimental import pallas as pl
from jax.experimental.pallas import tpu as pltpu
from jax.experimental.pallas import tpu_sc as plsc

F32 = jnp.float32
BF16 = jnp.bfloat16

HEAD_DIM = 64
GRID_W = 64
ROPE_THETA = 10000.0
ATTN_SCALE = HEAD_DIM ** -0.5
LOG2_E = 1.4426950408889634
HGRN_CHUNK = 64
HGRN_SUB = 16
ATTN_AHEAD = 4
ATTN_UNROLL = 4
N_EXPERTS = 8
N_MOD = 6
EPS = 1e-6

LANES = 128
TM = 256
FFN_TM = 512
MOE_BLK = 512
MOE_TF = 1792
GATHER_WIN = 128
GATHER_SPLIT = 4
NEG_BIG = -1e30
VMEM_LIMIT = 56 * 1024 * 1024


def _cparams(*sem):
    return pltpu.CompilerParams(dimension_semantics=sem, vmem_limit_bytes=VMEM_LIMIT)


def _sigmoid(z):
    return 1.0 / (1.0 + jnp.exp(-z))


def _dot(a, b):
    return jnp.dot(a, b, preferred_element_type=F32)


def _dot_nt(a, b):
    return lax.dot_general(a, b, (((1,), (1,)), ((), ())), preferred_element_type=F32)


def _dot_tn(a, b):
    return lax.dot_general(a, b, (((0,), (0,)), ((), ())), preferred_element_type=F32)


def _rms(x, g):
    return x * lax.rsqrt(jnp.mean(x * x, axis=-1, keepdims=True) + EPS) * g


def _ada_kernel(c_ref, w_ref, b_ref, o_ref):
    c = c_ref[...]
    s = (c * _sigmoid(c)).astype(BF16)
    o_ref[...] = _dot(s, w_ref[...].astype(BF16)) + b_ref[...]


def _ada_call(c_all, ada_w, ada_b):
    depth, d, n = ada_w.shape
    r = c_all.shape[0]
    tn = 512
    return pl.pallas_call(
        _ada_kernel,
        grid=(depth, n // tn),
        in_specs=[
            pl.BlockSpec((r, d), lambda l, j: (0, 0)),
            pl.BlockSpec((None, d, tn), lambda l, j: (l, 0, j)),
            pl.BlockSpec((None, 1, tn), lambda l, j: (l, 0, j)),
        ],
        out_specs=pl.BlockSpec((None, r, tn), lambda l, j: (l, 0, j)),
        out_shape=jax.ShapeDtypeStruct((depth, r, n), F32),
        compiler_params=_cparams("parallel", "parallel"),
        name="ada",
    )(c_all, ada_w, ada_b.reshape(depth, 1, n))


def _inproj_kernel(x_ref, mod_ref, ng_ref, w_ref, cos_ref, sin_ref, qg_ref, kg_ref, gc_ref,
                   q_ref, k_ref, vt_ref, cbog_ref, u_ref, qv_ref, kk_ref, gg_ref):
    x = x_ref[...]
    h = _rms(x, ng_ref[0:1, :]) * (1.0 + mod_ref[1:2, :]) + mod_ref[0:1, :]
    hb = h.astype(BF16)

    def proj(c0, n):
        return _dot(hb, w_ref[:, c0:c0 + n])

    cos = cos_ref[...]
    sin = sin_ref[...]
    lane = lax.broadcasted_iota(jnp.int32, (TM, LANES), 1)
    first_half = (lane % 32) < 16
    lo = lane < HEAD_DIM

    def rope(v):
        partner = jnp.where(first_half, pltpu.roll(v, LANES - 16, 1), pltpu.roll(v, 16, 1))
        return v * cos + partner * sin

    qg = qg_ref[...]
    for hp in range(4):
        q2 = proj(2 * LANES * hp, 2 * LANES)
        for s in range(2):
            qh = q2[:, LANES * s:LANES * (s + 1)]
            r = lax.rsqrt(jnp.sum(qh * qh, axis=-1, keepdims=True) * (1.0 / HEAD_DIM) + EPS)
            q_ref[2 * hp + s] = (rope(qh * r * qg) * (ATTN_SCALE * LOG2_E)).astype(BF16)

    kv = proj(8 * LANES, 2 * LANES)
    kx = kv[:, :LANES]
    v = kv[:, LANES:]
    k2 = kx * kx
    s_lo = jnp.sum(jnp.where(lo, k2, 0.0), axis=-1, keepdims=True)
    s_hi = jnp.sum(jnp.where(lo, 0.0, k2), axis=-1, keepdims=True)
    r = lax.rsqrt(jnp.where(lo, s_lo, s_hi) * (1.0 / HEAD_DIM) + EPS)
    k_ref[...] = rope(kx * r * kg_ref[...]).astype(BF16)
    vt = v.T
    ones = jnp.ones((HEAD_DIM, TM), F32)
    vt_ref[0] = jnp.concatenate([vt[:HEAD_DIM], ones], axis=0).astype(BF16)
    vt_ref[1] = jnp.concatenate([vt[HEAD_DIM:], ones], axis=0).astype(BF16)

    c0 = 10 * LANES
    c3 = proj(c0, 6 * LANES)
    cw = 2 * LANES
    u_ref[...] = c3[:, cw:2 * cw] * c3[:, 2 * cw:3 * cw]

    hg = proj(c0 + 6 * LANES, 10 * LANES)
    zf, zb, iv, hq, og = (hg[:, cw * i:cw * (i + 1)] for i in range(5))
    cbog_ref[:, :cw] = c3[:, :cw].astype(BF16)
    cbog_ref[:, cw:] = og.astype(BF16)
    qv_ref[:, :cw] = (hq * _sigmoid(hq)).astype(BF16)
    qv_ref[:, cw:] = iv.astype(BF16)
    for d, z in enumerate((zf, zb)):
        log_lb = gc_ref[0:1, cw * d:cw * (d + 1)]
        log1m_lb = gc_ref[1:2, cw * d:cw * (d + 1)]
        one_m_lb = gc_ref[2:3, cw * d:cw * (d + 1)]
        t = jnp.exp(-jnp.abs(z))
        log_sig = jnp.minimum(z, 0.0) - jnp.log1p(t)
        a2 = log1m_lb + log_sig
        log_f = jnp.maximum(log_lb, a2) + jnp.log1p(jnp.exp(-jnp.abs(log_lb - a2)))
        sig_neg = jnp.where(z >= 0, t, 1.0) / (1.0 + t)
        gg_ref[:, cw * d:cw * (d + 1)] = log_f
        kk_ref[:, cw * d:cw * (d + 1)] = (one_m_lb * sig_neg).astype(BF16)


def _inproj_call(x_all, mod, ng, w1, cos_t, sin_t, qg, kg, gc):
    b, t, d = x_all.shape
    nt = t // TM
    nw = w1.shape[1]
    tok = lambda w: pl.BlockSpec((None, TM, w), lambda i, j: (i, j, 0))
    const = lambda r, w: pl.BlockSpec((r, w), lambda i, j: (0, 0))
    return pl.pallas_call(
        _inproj_kernel,
        grid=(b, nt),
        in_specs=[
            tok(d),
            pl.BlockSpec((None, None, N_MOD, d), lambda i, j: (i, jnp.minimum(j, 1), 0, 0)),
            const(8, d),
            const(d, nw),
            pl.BlockSpec((TM, LANES), lambda i, j: (j, 0)),
            pl.BlockSpec((TM, LANES), lambda i, j: (j, 0)),
            const(1, LANES),
            const(1, LANES),
            const(8, 4 * LANES),
        ],
        out_specs=[
            pl.BlockSpec((None, 8, TM, LANES), lambda i, j: (i, 0, j, 0)),
            tok(LANES),
            pl.BlockSpec((None, 2, None, LANES, TM), lambda i, j: (i, 0, j, 0, 0)),
            tok(4 * LANES),
            tok(2 * LANES),
            tok(4 * LANES),
            tok(4 * LANES),
            tok(4 * LANES),
        ],
        out_shape=[
            jax.ShapeDtypeStruct((b, 8, t, LANES), BF16),
            jax.ShapeDtypeStruct((b, t, LANES), BF16),
            jax.ShapeDtypeStruct((b, 2, nt, LANES, TM), BF16),
            jax.ShapeDtypeStruct((b, t, 4 * LANES), BF16),
            jax.ShapeDtypeStruct((b, t, 2 * LANES), F32),
            jax.ShapeDtypeStruct((b, t, 4 * LANES), BF16),
            jax.ShapeDtypeStruct((b, t, 4 * LANES), BF16),
            jax.ShapeDtypeStruct((b, t, 4 * LANES), F32),
        ],
        compiler_params=_cparams("parallel", "parallel"),
        name="inproj",
    )(x_all, mod, ng, w1, cos_t, sin_t, qg, kg, gc)


def _attn_kernel(q_ref, k_ref, vt_ref, o_ref, m_ref, acc_ref, *, n_kv):
    qi = pl.program_id(1)
    nkv = jnp.where(qi == 0, 1, n_kv)
    m_ref[...] = jnp.full(m_ref.shape, NEG_BIG, F32)
    acc_ref[...] = jnp.zeros(acc_ref.shape, F32)

    def score(j, h):
        r0 = pl.multiple_of(j * TM, TM)
        return _dot_nt(k_ref[pl.ds(r0, TM), :], q_ref[h])

    def tile_heads(j, scores, j_next):
        for h in range(8):
            hn = h + ATTN_AHEAD
            if hn < 8:
                scores.append(score(j, hn))
            elif j_next is not None:
                scores.append(score(j_next, hn - 8))
            s = scores.pop(0)
            m_old = m_ref[h]
            m_new = jnp.maximum(m_old, jnp.max(s, axis=0, keepdims=True))
            p = jnp.exp2(s - m_new).astype(BF16)
            pv = _dot(vt_ref[h // 4, j], p)
            acc_ref[h] = jnp.exp2(m_old - m_new) * acc_ref[h] + pv
            m_ref[h] = m_new
        return scores

    def body(i, ahead):
        scores = list(ahead)
        for u in range(ATTN_UNROLL):
            scores = tile_heads(ATTN_UNROLL * i + u, scores, ATTN_UNROLL * i + u + 1)
        return tuple(scores)

    ahead = lax.fori_loop(0, nkv // ATTN_UNROLL, body, tuple(score(0, h) for h in range(ATTN_AHEAD)))
    tile_heads(nkv - 1, list(ahead), None)
    outs = [acc_ref[h, :HEAD_DIM, :] / acc_ref[h, HEAD_DIM:, :] for h in range(8)]
    o_ref[...] = jnp.concatenate(outs, axis=0).T.astype(BF16)


def _attn_call(q, k, vt):
    b, _, t, _ = q.shape
    nt = t // TM
    assert nt % ATTN_UNROLL == 1, "the kv loop takes ATTN_UNROLL tiles per trip and leaves one for after it"
    return pl.pallas_call(
        functools.partial(_attn_kernel, n_kv=nt),
        grid=(b, nt),
        in_specs=[
            pl.BlockSpec((None, 8, TM, LANES), lambda i, j: (i, 0, j, 0)),
            pl.BlockSpec((None, t, LANES), lambda i, j: (i, 0, 0)),
            pl.BlockSpec((None, 2, nt, LANES, TM), lambda i, j: (i, 0, 0, 0, 0)),
        ],
        out_specs=pl.BlockSpec((None, TM, 4 * LANES), lambda i, j: (i, j, 0)),
        out_shape=jax.ShapeDtypeStruct((b, t, 4 * LANES), BF16),
        scratch_shapes=[pltpu.VMEM((8, 1, TM), F32), pltpu.VMEM((8, LANES, TM), F32)],
        compiler_params=_cparams("parallel", "parallel"),
        name="attn",
    )(q, k, vt)


def _split3(g):
    hi = g.astype(BF16)
    r1 = g - hi.astype(F32)
    mid = r1.astype(BF16)
    lo = (r1 - mid.astype(F32)).astype(BF16)
    return hi, mid, lo


def _hgrn_kernel(qvf_ref, kf_ref, gf_ref, qvb_ref, kb_ref, gb_ref, tri_ref, trij_ref, j_ref, e_ref, bd_ref,
                 md_ref, mo_ref, of_ref, ob_ref, st_ref, b_s, k_s):
    hw = 2 * LANES
    nch = TM // HGRN_CHUNK

    @pl.when(pl.program_id(1) == 0)
    def _():
        st_ref[...] = jnp.zeros(st_ref.shape, F32)

    flip = j_ref[...]

    def cumsum(m_ref, g):
        m = m_ref[...]
        hi, mid, lo = _split3(g)
        return _dot(m, hi) + _dot(m, mid) + _dot(m, lo)

    qvf = qvf_ref[...]
    fl = _dot(flip, jnp.concatenate([qvb_ref[...], kb_ref[...]], axis=1))
    q = jnp.concatenate([qvf[:, :hw].astype(F32), fl[:, :hw]], axis=0)
    v = jnp.concatenate([qvf[:, hw:].astype(F32), fl[:, hw:2 * hw]], axis=0)
    k = jnp.concatenate([kf_ref[...].astype(F32), fl[:, 2 * hw:]], axis=0)
    b = jnp.concatenate([cumsum(tri_ref, gf_ref[...]), cumsum(trij_ref, gb_ref[...])], axis=0) * LOG2_E

    ngr = 2 * TM // HGRN_SUB
    nsub = HGRN_CHUNK // HGRN_SUB
    b_s[...] = b.reshape(ngr, HGRN_SUB, hw)
    k_s[...] = k.reshape(ngr, HGRN_SUB, hw)
    q3 = q.reshape(ngr, HGRN_SUB, hw)
    b3 = b.reshape(ngr, HGRN_SUB, hw)

    def body(s, a):
        bs = b_s[:, pl.ds(s, 1), :]
        ks = k_s[:, pl.ds(s, 1), :]
        x = q3 * jnp.exp2(jnp.minimum(b3 - bs, 0.0)) * ks
        return a + _dot(x.reshape(2 * TM, hw).astype(BF16), e_ref[s])

    a = lax.fori_loop(0, HGRN_SUB, body, jnp.zeros((2 * TM, hw), F32))

    bd = bd_ref[...]
    bd16 = bd.astype(BF16)
    md = md_ref[...]
    mo = mo_ref[...]
    for d in range(2):
        st = st_ref[d]
        outs = []
        for c in range(nch):
            r0 = TM * d + HGRN_CHUNK * c
            bc = b[r0:r0 + HGRN_CHUNK]
            qc = q[r0:r0 + HGRN_CHUNK]
            kc = k[r0:r0 + HGRN_CHUNK]
            vc = v[r0:r0 + HGRN_CHUNK]
            ac = a[r0:r0 + HGRN_CHUNK] * md
            r_end = [bc[HGRN_SUB * (j + 1) - 1:HGRN_SUB * (j + 1), :] for j in range(nsub)]
            r_own = jnp.concatenate([jnp.broadcast_to(r, (HGRN_SUB, hw)) for r in r_end], axis=0)
            kt_bd = jnp.concatenate([(kc * jnp.exp2(r_own - bc)).astype(BF16)] * 4, axis=0) * bd16
            q_rel = jnp.concatenate(
                [qc * jnp.exp2(jnp.minimum(bc - r_end[j], 0.0)) for j in range(nsub - 1)], axis=0).astype(BF16)
            p = _dot_nt(q_rel, kt_bd) * mo
            for j in range(nsub - 1):
                ac = ac + p[HGRN_CHUNK * j:HGRN_CHUNK * (j + 1)]
            bl = r_end[-1]
            qe = (qc * jnp.exp2(bc)).astype(BF16)
            ke = (kc * jnp.exp2(bl - bc)).astype(BF16)
            vcb = vc.astype(BF16)
            vbd = jnp.concatenate([vcb] * 4, axis=0) * bd16
            outs.append(_dot(ac.astype(BF16), vbd) + _dot_nt(qe, st.astype(BF16)))
            st = st * jnp.exp2(bl) + _dot_tn(vcb, ke) * bd
        st_ref[d] = st
        od = jnp.concatenate(outs, axis=0).astype(BF16)
        if d == 0:
            of_ref[...] = od
        else:
            ob_ref[...] = _dot(flip, od).astype(BF16)


def _hgrn_consts():
    hw = 2 * LANES
    r = jnp.arange(TM)
    same_chunk = (r[:, None] // HGRN_CHUNK) == (r[None, :] // HGRN_CHUNK)
    tri = (same_chunk & (r[None, :] <= r[:, None])).astype(F32)
    flip = (r[:, None] + r[None, :] == TM - 1).astype(F32)
    trij = tri @ flip
    c = jnp.arange(hw)
    s = jnp.arange(HGRN_SUB)
    e = ((c[None, :, None] // HEAD_DIM == c[None, None, :] // HGRN_CHUNK)
         & (c[None, None, :] % HGRN_SUB == s[:, None, None])).astype(BF16)
    bd = (c[:, None] // HEAD_DIM == c[None, :] // HEAD_DIM).astype(F32)
    tt = jnp.arange(HGRN_CHUNK)[:, None]
    ss = (c % HGRN_CHUNK)[None, :]
    md = ((ss // HGRN_SUB == tt // HGRN_SUB) & (ss <= tt)).astype(F32)
    mo = jnp.concatenate([((ss // HGRN_SUB == j) & (tt // HGRN_SUB > j)).astype(F32)
                          for j in range(HGRN_CHUNK // HGRN_SUB - 1)], axis=0)
    return tri.astype(BF16), trij.astype(BF16), flip.astype(BF16), e, bd, md, mo


def _hgrn_call(qv, kk, gg, consts):
    b, t, _ = qv.shape
    nt = t // TM
    hw = 2 * LANES
    tri, trij, flip, e, bd, md, mo = consts
    fwd = lambda i, j: j
    bwd = lambda i, j: jnp.where(j == 0, 0, nt - j)
    const2 = lambda a: pl.BlockSpec(a.shape, lambda i, j: (0,) * a.ndim)
    return pl.pallas_call(
        _hgrn_kernel,
        grid=(b, nt),
        in_specs=[
            pl.BlockSpec((None, TM, 2 * hw), lambda i, j: (i, fwd(i, j), 0)),
            pl.BlockSpec((None, TM, hw), lambda i, j: (i, fwd(i, j), 0)),
            pl.BlockSpec((None, TM, hw), lambda i, j: (i, fwd(i, j), 0)),
            pl.BlockSpec((None, TM, 2 * hw), lambda i, j: (i, bwd(i, j), 0)),
            pl.BlockSpec((None, TM, hw), lambda i, j: (i, bwd(i, j), 1)),
            pl.BlockSpec((None, TM, hw), lambda i, j: (i, bwd(i, j), 1)),
            const2(tri), const2(trij), const2(flip), const2(e), const2(bd), const2(md), const2(mo),
        ],
        out_specs=[
            pl.BlockSpec((None, TM, hw), lambda i, j: (i, fwd(i, j), 0)),
            pl.BlockSpec((None, TM, hw), lambda i, j: (i, bwd(i, j), 0)),
        ],
        out_shape=[jax.ShapeDtypeStruct((b, t, hw), BF16)] * 2,
        scratch_shapes=[
            pltpu.VMEM((2, hw, hw), F32),
            pltpu.VMEM((2 * TM // HGRN_SUB, HGRN_SUB, hw), F32),
            pltpu.VMEM((2 * TM // HGRN_SUB, HGRN_SUB, hw), F32),
        ],
        compiler_params=_cparams("parallel", "arbitrary"),
        name="hgrn",
    )(qv, kk, gg, qv, kk, gg, tri, trij, flip, e, bd, md, mo)


def _outproj_kernel(x_ref, mod_ref, ng_ref, attn_ref, cbog_ref, u_ref, up_ref, un_ref, of_ref, ob_ref,
                    hgg_ref, cw_ref, hn_ref, w_ref, o_ref, *, n_tiles):
    t = pl.program_id(1)
    cw = 2 * LANES
    u = u_ref[...]
    row = lax.broadcasted_iota(jnp.int32, (TM, cw), 0)
    prev_row = jnp.where(t >= 2, up_ref[7:8, :], 0.0)
    next_row = jnp.where((t >= 1) & (t < n_tiles - 1), un_ref[0:1, :], 0.0)
    u_prev = jnp.where(row == 0, prev_row, pltpu.roll(u, 1, 0))
    u_next = jnp.where(row == TM - 1, next_row, pltpu.roll(u, TM - 1, 0))
    cbog = cbog_ref[...].astype(F32)
    conv = cbog[:, :cw] * (cw_ref[0:1, :] * u_prev + cw_ref[1:2, :] * u + cw_ref[2:3, :] * u_next)

    o = of_ref[...].astype(F32) + ob_ref[...].astype(F32)
    ms = _dot((o * o).astype(BF16), hn_ref[...])
    og = cbog[:, cw:]
    hg = o * lax.rsqrt(ms + EPS) * hgg_ref[...] * (og * _sigmoid(og))

    y = (_dot(attn_ref[...], w_ref[0:2 * cw, :]) + _dot(conv.astype(BF16), w_ref[2 * cw:3 * cw, :])
         + _dot(hg.astype(BF16), w_ref[3 * cw:4 * cw, :]))
    o_ref[...] = x_ref[...] + mod_ref[2:3, :] * _rms(y, ng_ref[1:2, :])


def _outproj_call(x_all, mod, ng, attn, cbog, u, o_f, o_b, hgg, conv_w, hn, w_out):
    b, t, d = x_all.shape
    nt = t // TM
    cw = 2 * LANES
    nb8 = t // 8
    tok = lambda w: pl.BlockSpec((None, TM, w), lambda i, j: (i, j, 0))
    const = lambda r, w: pl.BlockSpec((r, w), lambda i, j: (0, 0))
    return pl.pallas_call(
        functools.partial(_outproj_kernel, n_tiles=nt),
        grid=(b, nt),
        in_specs=[
            tok(d),
            pl.BlockSpec((None, None, N_MOD, d), lambda i, j: (i, jnp.minimum(j, 1), 0, 0)),
            const(8, d),
            tok(2 * cw),
            tok(2 * cw),
            tok(cw),
            pl.BlockSpec((None, 8, cw), lambda i, j: (i, jnp.maximum(j * (TM // 8) - 1, 0), 0)),
            pl.BlockSpec((None, 8, cw), lambda i, j: (i, jnp.minimum((j + 1) * (TM // 8), nb8 - 1), 0)),
            tok(cw),
            tok(cw),
            const(1, cw),
            const(8, cw),
            const(cw, cw),
            const(d, d),
        ],
        out_specs=tok(d),
        out_shape=jax.ShapeDtypeStruct((b, t, d), F32),
        compiler_params=_cparams("parallel", "parallel"),
        name="outproj",
    )(x_all, mod, ng, attn, cbog, u, u, u, o_f, o_b, hgg, conv_w, hn, w_out)


def _mod_rows(modt_ref, i, rows):
    sub = rows // modt_ref.shape[0]
    return jnp.concatenate(
        [jnp.broadcast_to(modt_ref[s, i:i + 1, :], (sub, modt_ref.shape[2])) for s in range(modt_ref.shape[0])],
        axis=0)


def _store_split(ref, val):
    w = ref.shape[2]
    for c in range(ref.shape[0]):
        ref[c] = val[:, w * c:w * (c + 1)]


def _load_split(ref):
    return jnp.concatenate([ref[c] for c in range(ref.shape[0])], axis=1)


def _split_spec(rows):
    return lambda d: pl.BlockSpec((GATHER_SPLIT, rows, d // GATHER_SPLIT), lambda i, *_: (0, i, 0))


def _ffn_kernel(x_ref, modt_ref, ng_ref, wg_ref, wu_ref, wd_ref, o_ref, *, tf):
    x = x_ref[...]
    rows = x.shape[0]
    h = (_rms(x, ng_ref[2:3, :]) * (1.0 + _mod_rows(modt_ref, 4, rows)) + _mod_rows(modt_ref, 3, rows)).astype(BF16)
    acc = jnp.zeros(x.shape, F32)
    for f0 in range(0, wg_ref.shape[1], tf):
        g = _dot(h, wg_ref[:, f0:f0 + tf])
        up = _dot(h, wu_ref[:, f0:f0 + tf])
        acc = acc + _dot((g * _sigmoid(g) * up).astype(BF16), wd_ref[f0:f0 + tf, :])
    o_ref[...] = x + _mod_rows(modt_ref, 5, rows) * _rms(acc, ng_ref[3:4, :])


def _ffn_call(x_flat, modt, ng, wg, wu, wd):
    n, d = x_flat.shape
    dff = wg.shape[1]
    nsub = FFN_TM // TM
    resident = lambda a: pl.BlockSpec(a.shape, lambda i: (0, 0), pipeline_mode=pl.Buffered(1))
    return pl.pallas_call(
        functools.partial(_ffn_kernel, tf=256),
        grid=(n // FFN_TM,),
        in_specs=[
            pl.BlockSpec((FFN_TM, d), lambda i: (i, 0)),
            pl.BlockSpec((nsub, N_MOD, d), lambda i: (i, 0, 0)),
            pl.BlockSpec((8, d), lambda i: (0, 0)),
            resident(wg), resident(wu), resident(wd),
        ],
        out_specs=pl.BlockSpec((FFN_TM, d), lambda i: (i, 0)),
        out_shape=jax.ShapeDtypeStruct((n, d), F32),
        compiler_params=_cparams("parallel"),
        name="ffn",
    )(x_flat, modt, ng, wg, wu, wd)


def _router_kernel(x_ref, modt_ref, ng_ref, wr_ref, h_ref, r_ref):
    x = x_ref[...]
    rows = x.shape[0]
    h = _rms(x, ng_ref[2:3, :]) * (1.0 + _mod_rows(modt_ref, 4, rows)) + _mod_rows(modt_ref, 3, rows)
    _store_split(h_ref, h)
    h_hi = h.astype(BF16)
    h_lo = (h - h_hi.astype(F32)).astype(BF16)
    logits = _dot(h_hi, wr_ref[0]) + _dot(h_lo, wr_ref[0]) + _dot(h_hi, wr_ref[1])
    lane = lax.broadcasted_iota(jnp.int32, logits.shape, 1)
    lg = jnp.where(lane < N_EXPERTS, logits, NEG_BIG)
    m1 = jnp.max(lg, axis=-1, keepdims=True)
    i1 = jnp.min(jnp.where(lg == m1, lane, LANES), axis=-1, keepdims=True)
    lg2 = jnp.where(lane == i1, NEG_BIG, lg)
    m2 = jnp.max(lg2, axis=-1, keepdims=True)
    i2 = jnp.min(jnp.where(lg2 == m2, lane, LANES), axis=-1, keepdims=True)
    e2 = jnp.exp(m2 - m1)
    w1 = 1.0 / (1.0 + e2)
    w2 = e2 / (1.0 + e2)
    r_ref[...] = jnp.where(lane == 0, i1.astype(F32),
                           jnp.where(lane == 1, i2.astype(F32),
                                     jnp.where(lane == 2, w1, jnp.where(lane == 3, w2, 0.0))))


def _router_call(x_flat, modt, ng, wr):
    n, d = x_flat.shape
    nsub = FFN_TM // TM
    return pl.pallas_call(
        _router_kernel,
        grid=(n // FFN_TM,),
        in_specs=[
            pl.BlockSpec((FFN_TM, d), lambda i: (i, 0)),
            pl.BlockSpec((nsub, N_MOD, d), lambda i: (i, 0, 0)),
            pl.BlockSpec((8, d), lambda i: (0, 0)),
            pl.BlockSpec(wr.shape, lambda i: (0, 0, 0)),
        ],
        out_specs=[_split_spec(FFN_TM)(d), pl.BlockSpec((FFN_TM, LANES), lambda i: (i, 0))],
        out_shape=[jax.ShapeDtypeStruct((GATHER_SPLIT, n, d // GATHER_SPLIT), F32),
                   jax.ShapeDtypeStruct((n, LANES), F32)],
        compiler_params=_cparams("parallel"),
        name="router",
    )(x_flat, modt, ng, wr)


def _gather_flat(data, idx):
    m = idx.shape[0]
    w = data.shape[1]
    mesh = plsc.VectorSubcoreMesh(core_axis_name="core", subcore_axis_name="subcore")

    @functools.partial(pl.kernel, out_type=jax.ShapeDtypeStruct((m, w), data.dtype), mesh=mesh)
    def gather(x_hbm, i_hbm, o_hbm):
        def body(i_vmem, o_vmem):
            pltpu.sync_copy(x_hbm.at[i_vmem.at[0]], o_vmem)

        pltpu.emit_pipeline(
            body,
            grid=(m // GATHER_WIN,),
            in_specs=[pl.BlockSpec((1, GATHER_WIN), lambda i: (0, i))],
            out_specs=[pl.BlockSpec((GATHER_WIN, w), lambda i: (i, 0))],
            core_axis_name=("core", "subcore"),
            dimension_semantics=(pltpu.PARALLEL,),
        )(i_hbm, o_hbm)

    return gather(data, idx.reshape(1, m))


def _gather_rows(data, idx):
    s, r, w = data.shape
    m = idx.shape[0]
    idx_all = (idx[None, :] + (jnp.arange(s, dtype=jnp.int32) * r)[:, None]).reshape(s * m)
    return _gather_flat(data.reshape(s * r, w), idx_all).reshape(s, m, w)


def _expert_kernel(be_ref, nv_ref, xs_ref, wg_ref, wu_ref, wd_ref, o_ref, acc_ref, *, nf):
    i = pl.program_id(0)
    f = pl.program_id(1)
    valid = i < nv_ref[0]

    @pl.when(valid)
    def _():
        xb = _load_split(xs_ref).astype(BF16)
        g = _dot(xb, wg_ref[...])
        up = _dot(xb, wu_ref[...])
        part = _dot((g * _sigmoid(g) * up).astype(BF16), wd_ref[...])

        @pl.when(f == 0)
        def _():
            acc_ref[...] = part

        @pl.when(f > 0)
        def _():
            acc_ref[...] += part

    @pl.when(f == nf - 1)
    def _():
        _store_split(o_ref, jnp.where(valid, acc_ref[...], 0.0))


def _expert_call(block_expert, n_valid, xs, wg, wu, wd):
    n_slots = xs.shape[1]
    d = wg.shape[1]
    dff = wg.shape[2]
    nf = dff // MOE_TF
    nb = n_slots // MOE_BLK
    grid_spec = pltpu.PrefetchScalarGridSpec(
        num_scalar_prefetch=2,
        grid=(nb, nf),
        in_specs=[
            _split_spec(MOE_BLK)(d),
            pl.BlockSpec((None, d, MOE_TF), lambda i, f, be, nv: (be[i], 0, f)),
            pl.BlockSpec((None, d, MOE_TF), lambda i, f, be, nv: (be[i], 0, f)),
            pl.BlockSpec((None, MOE_TF, d), lambda i, f, be, nv: (be[i], f, 0)),
        ],
        out_specs=_split_spec(MOE_BLK)(d),
        scratch_shapes=[pltpu.VMEM((MOE_BLK, d), F32)],
    )
    return pl.pallas_call(
        functools.partial(_expert_kernel, nf=nf),
        grid_spec=grid_spec,
        out_shape=jax.ShapeDtypeStruct((GATHER_SPLIT, n_slots, d // GATHER_SPLIT), F32),
        compiler_params=_cparams("arbitrary", "arbitrary"),
        name="experts",
    )(block_expert, n_valid, xs, wg, wu, wd)


def _combine_kernel(x_ref, modt_ref, ng_ref, y1_ref, y2_ref, r_ref, o_ref):
    x = x_ref[...]
    rows = x.shape[0]
    f = r_ref[:, 2:3] * _load_split(y1_ref) + r_ref[:, 3:4] * _load_split(y2_ref)
    o_ref[...] = x + _mod_rows(modt_ref, 5, rows) * _rms(f, ng_ref[3:4, :])


def _combine_call(x_flat, modt, ng, y1, y2, rinfo):
    n, d = x_flat.shape
    nsub = FFN_TM // TM
    tok = lambda w: pl.BlockSpec((FFN_TM, w), lambda i: (i, 0))
    return pl.pallas_call(
        _combine_kernel,
        grid=(n // FFN_TM,),
        in_specs=[tok(d), pl.BlockSpec((nsub, N_MOD, d), lambda i: (i, 0, 0)), pl.BlockSpec((8, d), lambda i: (0, 0)),
                  _split_spec(FFN_TM)(d), _split_spec(FFN_TM)(d), tok(LANES)],
        out_specs=tok(d),
        out_shape=jax.ShapeDtypeStruct((n, d), F32),
        compiler_params=_cparams("parallel"),
        name="combine",
    )(x_flat, modt, ng, y1, y2, rinfo)


def _moe_layer(x_flat, modt, ng, w_router, wg, wu, wd):
    n, d = x_flat.shape
    wr = jnp.pad(w_router, ((0, 0), (0, LANES - N_EXPERTS)))
    wr_hi = wr.astype(BF16)
    wr_lo = (wr - wr_hi.astype(F32)).astype(BF16)
    h, rinfo = _router_call(x_flat, modt, ng, jnp.stack([wr_hi, wr_lo]))

    n_assign = 2 * n
    expert = rinfo[:, :2].astype(jnp.int32).reshape(n_assign)
    onehot = (expert[:, None] == jnp.arange(N_EXPERTS, dtype=jnp.int32)[None, :]).astype(jnp.int32)
    csum = jnp.cumsum(onehot, axis=0)
    rank = jnp.sum((csum - onehot) * onehot, axis=1)
    counts = csum[-1]
    padded = (counts + MOE_BLK - 1) // MOE_BLK * MOE_BLK
    pad_end = jnp.cumsum(padded)
    pad_start = pad_end - padded
    dest = jnp.sum(onehot * pad_start[None, :], axis=1) + rank
    nb = -(-n_assign // MOE_BLK) + N_EXPERTS
    row_quantum = 32 * GATHER_WIN // GATHER_SPLIT
    nb = -(-nb * MOE_BLK // row_quantum) * row_quantum // MOE_BLK
    n_slots = nb * MOE_BLK
    slot_tok = jnp.zeros((n_slots,), jnp.int32).at[dest].set(jnp.arange(n_assign, dtype=jnp.int32) // 2)
    block_expert = jnp.minimum(
        jnp.searchsorted(pad_end, jnp.arange(nb, dtype=jnp.int32) * MOE_BLK, side="right"), N_EXPERTS - 1
    ).astype(jnp.int32)
    n_valid = (pad_end[-1:] // MOE_BLK).astype(jnp.int32)

    xs = _gather_rows(h, slot_tok)
    ys = _expert_call(block_expert, n_valid, xs, wg, wu, wd)
    dest2 = dest.reshape(n, 2)
    y1 = _gather_rows(ys, dest2[:, 0])
    y2 = _gather_rows(ys, dest2[:, 1])
    return _combine_call(x_flat, modt, ng, y1, y2, rinfo)


def _rope_tables(n_ctx, length):
    rows = length // GRID_W
    row = jnp.repeat(jnp.arange(rows, dtype=F32), GRID_W)
    col = jnp.tile(jnp.arange(GRID_W, dtype=F32), rows)
    axis_dim = HEAD_DIM // 2
    inv_freq = ROPE_THETA ** (-jnp.arange(0, axis_dim, 2, dtype=F32) / axis_dim)
    ar = row[:, None] * inv_freq
    ac = col[:, None] * inv_freq
    cos = jnp.concatenate([jnp.cos(ar), jnp.cos(ar), jnp.cos(ac), jnp.cos(ac)], axis=1)
    sin = jnp.concatenate([-jnp.sin(ar), jnp.sin(ar), -jnp.sin(ac), jnp.sin(ac)], axis=1)
    cos = jnp.concatenate([jnp.ones((n_ctx, HEAD_DIM), F32), cos], axis=0)
    sin = jnp.concatenate([jnp.zeros((n_ctx, HEAD_DIM), F32), sin], axis=0)
    return jnp.tile(cos, (1, 2)), jnp.tile(sin, (1, 2))


def _inproj_weights(w_in_l):
    d = w_in_l.shape[0]
    aw = d // 2
    nh = aw // HEAD_DIM
    wq = w_in_l[:, :aw].reshape(d, nh, HEAD_DIM)
    z = jnp.zeros((d, nh // 2, HEAD_DIM), w_in_l.dtype)
    q_pad = jnp.concatenate([
        jnp.concatenate([wq[:, :nh // 2], z], axis=2),
        jnp.concatenate([z, wq[:, nh // 2:]], axis=2),
    ], axis=1).reshape(d, nh * LANES)
    return jnp.concatenate([q_pad, w_in_l[:, aw:]], axis=1).astype(BF16)


def kernel(x, c, ctx, c_ctx, ada_w, ada_b, norm_g, w_in, w_out, attn_q_g, attn_k_g, conv_w, hgrn_lb, hgrn_g,
           ffn_w_gate, ffn_w_up, ffn_w_down, moe_router, moe_w_gate, moe_w_up, moe_w_down):
    bsz, length, d = x.shape
    n_ctx = ctx.shape[1]
    depth = w_in.shape[0]
    t = n_ctx + length
    nt = t // TM
    assert n_ctx == TM and length % TM == 0 and d == 8 * LANES
    assert (bsz * t) % FFN_TM == 0 and (bsz * t) % (32 * GATHER_WIN // GATHER_SPLIT) == 0

    r_pad = -(-(bsz + 1) // 8) * 8
    c_all = jnp.concatenate([c, c_ctx[None, :], jnp.zeros((r_pad - bsz - 1, d), F32)], axis=0)
    mod_all = _ada_call(c_all, ada_w, ada_b).reshape(depth, r_pad, N_MOD, d)

    cos_t, sin_t = _rope_tables(n_ctx, length)
    lb_all = jnp.cumsum(jax.nn.softmax(hgrn_lb.astype(F32), axis=0), axis=0)
    lb_all = lb_all - lb_all[0]
    hconsts = _hgrn_consts()
    cw = 2 * LANES
    cidx = jnp.arange(cw)
    head_mean = ((cidx[:, None] // HEAD_DIM == cidx[None, :] // HEAD_DIM).astype(F32) / HEAD_DIM).astype(BF16)

    x_all = jnp.concatenate([ctx, x], axis=1)
    for l in range(depth):
        mod_lat = mod_all[l, :bsz]
        mod_ctx = jnp.broadcast_to(mod_all[l, bsz][None], (bsz, N_MOD, d))
        mod = jnp.stack([mod_ctx, mod_lat], axis=1)
        modt = jnp.concatenate([mod[:, :1], jnp.broadcast_to(mod[:, 1:], (bsz, nt - 1, N_MOD, d))], axis=1)
        modt = modt.reshape(bsz * nt, N_MOD, d)
        ng = jnp.pad(norm_g[l], ((0, 4), (0, 0)))
        lb = lb_all[l].reshape(1, 2 * cw)
        gc = jnp.concatenate([jnp.log(lb), jnp.log1p(-lb), 1.0 - lb, jnp.zeros((5, 2 * cw), F32)], axis=0)
        qg = jnp.tile(attn_q_g[l], 2)[None, :]
        kg = jnp.tile(attn_k_g[l], 2)[None, :]

        q, k, vt, cbog, u, qv, kk, gg = _inproj_call(x_all, mod, ng, _inproj_weights(w_in[l]), cos_t, sin_t, qg, kg, gc)
        attn = _attn_call(q, k, vt)
        o_f, o_b = _hgrn_call(qv, kk, gg, hconsts)
        x_all = _outproj_call(x_all, mod, ng, attn, cbog, u, o_f, o_b, hgrn_g[l][None, :],
                              jnp.pad(conv_w[l], ((0, 5), (0, 0))), head_mean, w_out[l].astype(BF16))

        x_flat = x_all.reshape(bsz * t, d)
        if l % 2 == 0:
            x_flat = _ffn_call(x_flat, modt, ng, ffn_w_gate[l // 2].astype(BF16), ffn_w_up[l // 2].astype(BF16),
                               ffn_w_down[l // 2].astype(BF16))
        else:
            x_flat = _moe_layer(x_flat, modt, ng, moe_router[l // 2], moe_w_gate[l // 2].astype(BF16),
                                moe_w_up[l // 2].astype(BF16), moe_w_down[l // 2].astype(BF16))
        x_all = x_flat.reshape(bsz, t, d)
    return x_all[:, n_ctx:, :]
```

```python
import functools

import jax
import jax.numpy as jnp
from jax import lax
from jax.experimental import pallas as pl
from jax.experimental.pallas import tpu as pltpu
from jax.experimental.pallas import tpu_sc as plsc

F32 = jnp.float32
BF16 = jnp.bfloat16

HEAD_DIM = 64
GRID_W = 64
ROPE_THETA = 10000.0
ATTN_SCALE = HEAD_DIM ** -0.5
LOG2_E = 1.4426950408889634
HGRN_CHUNK = 64
HGRN_SUB = 16
VT_ROWS = HEAD_DIM + 16
ATTN_AHEAD = 4
ATTN_UNROLL = 4
N_EXPERTS = 8
N_MOD = 6
EPS = 1e-6

LANES = 128
TM = 256
FFN_TM = 512
MOE_BLK = 512
MOE_TF = 1792
GATHER_WIN = 128
GATHER_SPLIT = 4
NEG_BIG = -1e30
VMEM_LIMIT = 56 * 1024 * 1024


def _cparams(*sem):
    return pltpu.CompilerParams(dimension_semantics=sem, vmem_limit_bytes=VMEM_LIMIT)


def _sigmoid(z):
    return 1.0 / (1.0 + jnp.exp(-z))


def _dot(a, b):
    return jnp.dot(a, b, preferred_element_type=F32)


def _dot_nt(a, b):
    return lax.dot_general(a, b, (((1,), (1,)), ((), ())), preferred_element_type=F32)


def _dot_tn(a, b):
    return lax.dot_general(a, b, (((0,), (0,)), ((), ())), preferred_element_type=F32)


def _rms(x, g):
    return x * lax.rsqrt(jnp.mean(x * x, axis=-1, keepdims=True) + EPS) * g


def _ada_kernel(c_ref, w_ref, b_ref, o_ref):
    c = c_ref[...]
    s = (c * _sigmoid(c)).astype(BF16)
    o_ref[...] = _dot(s, w_ref[...].astype(BF16)) + b_ref[...]


def _ada_call(c_all, ada_w, ada_b):
    depth, d, n = ada_w.shape
    r = c_all.shape[0]
    tn = 512
    return pl.pallas_call(
        _ada_kernel,
        grid=(depth, n // tn),
        in_specs=[
            pl.BlockSpec((r, d), lambda l, j: (0, 0)),
            pl.BlockSpec((None, d, tn), lambda l, j: (l, 0, j)),
            pl.BlockSpec((None, 1, tn), lambda l, j: (l, 0, j)),
        ],
        out_specs=pl.BlockSpec((None, r, tn), lambda l, j: (l, 0, j)),
        out_shape=jax.ShapeDtypeStruct((depth, r, n), F32),
        compiler_params=_cparams("parallel", "parallel"),
        name="ada",
    )(c_all, ada_w, ada_b.reshape(depth, 1, n))


def _inproj_kernel(x_ref, mod_ref, ng_ref, w_ref, cos_ref, sin_ref, qg_ref, kg_ref, gc_ref,
                   q_ref, k_ref, vt_ref, cbog_ref, u_ref, qv_ref, kk_ref, gg_ref):
    x = x_ref[...]
    h = _rms(x, ng_ref[0:1, :]) * (1.0 + mod_ref[1:2, :]) + mod_ref[0:1, :]
    hb = h.astype(BF16)

    def proj(c0, n):
        return _dot(hb, w_ref[:, c0:c0 + n])

    cos = cos_ref[...]
    sin = sin_ref[...]
    lane = lax.broadcasted_iota(jnp.int32, (TM, LANES), 1)
    first_half = (lane % 32) < 16
    lo = lane < HEAD_DIM

    def rope(v):
        partner = jnp.where(first_half, pltpu.roll(v, LANES - 16, 1), pltpu.roll(v, 16, 1))
        return v * cos + partner * sin

    qg = qg_ref[...]
    for hp in range(4):
        q2 = proj(2 * LANES * hp, 2 * LANES)
        for s in range(2):
            qh = q2[:, LANES * s:LANES * (s + 1)]
            r = lax.rsqrt(jnp.sum(qh * qh, axis=-1, keepdims=True) * (1.0 / HEAD_DIM) + EPS)
            q_ref[2 * hp + s] = (rope(qh * r * qg) * (ATTN_SCALE * LOG2_E)).astype(BF16)

    kv = proj(8 * LANES, 2 * LANES)
    kx = kv[:, :LANES]
    v = kv[:, LANES:]
    k2 = kx * kx
    s_lo = jnp.sum(jnp.where(lo, k2, 0.0), axis=-1, keepdims=True)
    s_hi = jnp.sum(jnp.where(lo, 0.0, k2), axis=-1, keepdims=True)
    r = lax.rsqrt(jnp.where(lo, s_lo, s_hi) * (1.0 / HEAD_DIM) + EPS)
    k_ref[...] = rope(kx * r * kg_ref[...]).astype(BF16)
    vt = v.T
    ones = jnp.ones((VT_ROWS - HEAD_DIM, TM), F32)
    vt_ref[0] = jnp.concatenate([vt[:HEAD_DIM], ones], axis=0).astype(BF16)
    vt_ref[1] = jnp.concatenate([vt[HEAD_DIM:], ones], axis=0).astype(BF16)

    c0 = 10 * LANES
    c3 = proj(c0, 6 * LANES)
    cw = 2 * LANES
    u_ref[...] = c3[:, cw:2 * cw] * c3[:, 2 * cw:3 * cw]

    hg = proj(c0 + 6 * LANES, 10 * LANES)
    zf, zb, iv, hq, og = (hg[:, cw * i:cw * (i + 1)] for i in range(5))
    cbog_ref[:, :cw] = c3[:, :cw].astype(BF16)
    cbog_ref[:, cw:] = og.astype(BF16)
    qv_ref[:, :cw] = (hq * _sigmoid(hq)).astype(BF16)
    qv_ref[:, cw:] = iv.astype(BF16)
    for d, z in enumerate((zf, zb)):
        log_lb = gc_ref[0:1, cw * d:cw * (d + 1)]
        log1m_lb = gc_ref[1:2, cw * d:cw * (d + 1)]
        one_m_lb = gc_ref[2:3, cw * d:cw * (d + 1)]
        t = jnp.exp(-jnp.abs(z))
        log_sig = jnp.minimum(z, 0.0) - jnp.log1p(t)
        a2 = log1m_lb + log_sig
        log_f = jnp.maximum(log_lb, a2) + jnp.log1p(jnp.exp(-jnp.abs(log_lb - a2)))
        sig_neg = jnp.where(z >= 0, t, 1.0) / (1.0 + t)
        gg_ref[:, cw * d:cw * (d + 1)] = log_f
        kk_ref[:, cw * d:cw * (d + 1)] = (one_m_lb * sig_neg).astype(BF16)


def _inproj_call(x_all, mod, ng, w1, cos_t, sin_t, qg, kg, gc):
    b, t, d = x_all.shape
    nt = t // TM
    nw = w1.shape[1]
    tok = lambda w: pl.BlockSpec((None, TM, w), lambda i, j: (i, j, 0))
    const = lambda r, w: pl.BlockSpec((r, w), lambda i, j: (0, 0))
    return pl.pallas_call(
        _inproj_kernel,
        grid=(b, nt),
        in_specs=[
            tok(d),
            pl.BlockSpec((None, None, N_MOD, d), lambda i, j: (i, jnp.minimum(j, 1), 0, 0)),
            const(8, d),
            const(d, nw),
            pl.BlockSpec((TM, LANES), lambda i, j: (j, 0)),
            pl.BlockSpec((TM, LANES), lambda i, j: (j, 0)),
            const(1, LANES),
            const(1, LANES),
            const(8, 4 * LANES),
        ],
        out_specs=[
            pl.BlockSpec((None, 8, TM, LANES), lambda i, j: (i, 0, j, 0)),
            tok(LANES),
            pl.BlockSpec((None, 2, None, VT_ROWS, TM), lambda i, j: (i, 0, j, 0, 0)),
            tok(4 * LANES),
            tok(2 * LANES),
            tok(4 * LANES),
            tok(4 * LANES),
            tok(4 * LANES),
        ],
        out_shape=[
            jax.ShapeDtypeStruct((b, 8, t, LANES), BF16),
            jax.ShapeDtypeStruct((b, t, LANES), BF16),
            jax.ShapeDtypeStruct((b, 2, nt, VT_ROWS, TM), BF16),
            jax.ShapeDtypeStruct((b, t, 4 * LANES), BF16),
            jax.ShapeDtypeStruct((b, t, 2 * LANES), F32),
            jax.ShapeDtypeStruct((b, t, 4 * LANES), BF16),
            jax.ShapeDtypeStruct((b, t, 4 * LANES), BF16),
            jax.ShapeDtypeStruct((b, t, 4 * LANES), F32),
        ],
        compiler_params=_cparams("parallel", "parallel"),
        name="inproj",
    )(x_all, mod, ng, w1, cos_t, sin_t, qg, kg, gc)


def _attn_kernel(q_ref, k_ref, vt_ref, o_ref, m_ref, acc_ref, *, n_kv):
    qi = pl.program_id(1)
    nkv = jnp.where(qi == 0, 1, n_kv)
    m_ref[...] = jnp.full(m_ref.shape, NEG_BIG, F32)
    acc_ref[...] = jnp.zeros(acc_ref.shape, F32)

    def score(j, h):
        r0 = pl.multiple_of(j * TM, TM)
        return _dot_nt(k_ref[pl.ds(r0, TM), :], q_ref[h])

    def tile_heads(j, scores, j_next):
        for h in range(8):
            hn = h + ATTN_AHEAD
            if hn < 8:
                scores.append(score(j, hn))
            elif j_next is not None:
                scores.append(score(j_next, hn - 8))
            s = scores.pop(0)
            m_old = m_ref[h]
            m_new = jnp.maximum(m_old, jnp.max(s, axis=0, keepdims=True))
            p = jnp.exp2(s - m_new).astype(BF16)
            pv = _dot(vt_ref[h // 4, j], p)
            acc_ref[h] = jnp.exp2(m_old - m_new) * acc_ref[h] + pv
            m_ref[h] = m_new
        return scores

    def body(i, ahead):
        scores = list(ahead)
        for u in range(ATTN_UNROLL):
            scores = tile_heads(ATTN_UNROLL * i + u, scores, ATTN_UNROLL * i + u + 1)
        return tuple(scores)

    ahead = lax.fori_loop(0, nkv // ATTN_UNROLL, body, tuple(score(0, h) for h in range(ATTN_AHEAD)))
    tile_heads(nkv - 1, list(ahead), None)
    outs = [acc_ref[h, :HEAD_DIM, :] / acc_ref[h, HEAD_DIM:HEAD_DIM + 1, :] for h in range(8)]
    o_ref[...] = jnp.concatenate(outs, axis=0).T.astype(BF16)


def _attn_call(q, k, vt):
    b, _, t, _ = q.shape
    nt = t // TM
    assert nt % ATTN_UNROLL == 1, "the kv loop takes ATTN_UNROLL tiles per trip and leaves one for after it"
    return pl.pallas_call(
        functools.partial(_attn_kernel, n_kv=nt),
        grid=(b, nt),
        in_specs=[
            pl.BlockSpec((None, 8, TM, LANES), lambda i, j: (i, 0, j, 0)),
            pl.BlockSpec((None, t, LANES), lambda i, j: (i, 0, 0)),
            pl.BlockSpec((None, 2, nt, VT_ROWS, TM), lambda i, j: (i, 0, 0, 0, 0)),
        ],
        out_specs=pl.BlockSpec((None, TM, 4 * LANES), lambda i, j: (i, j, 0)),
        out_shape=jax.ShapeDtypeStruct((b, t, 4 * LANES), BF16),
        scratch_shapes=[pltpu.VMEM((8, 1, TM), F32), pltpu.VMEM((8, VT_ROWS, TM), F32)],
        compiler_params=_cparams("parallel", "parallel"),
        name="attn",
    )(q, k, vt)


def _split3(g):
    hi = g.astype(BF16)
    r1 = g - hi.astype(F32)
    mid = r1.astype(BF16)
    lo = (r1 - mid.astype(F32)).astype(BF16)
    return hi, mid, lo


def _hgrn_kernel(qvf_ref, kf_ref, gf_ref, qvb_ref, kb_ref, gb_ref, tri_ref, trij_ref, j_ref, e_ref, bd_ref,
                 md_ref, mo_ref, of_ref, ob_ref, st_ref, x_s):
    hw = 2 * LANES
    nch = TM // HGRN_CHUNK

    @pl.when(pl.program_id(1) == 0)
    def _():
        st_ref[...] = jnp.zeros(st_ref.shape, F32)

    flip = j_ref[...]

    def cumsum(m_ref, g):
        m = m_ref[...]
        hi, mid, lo = _split3(g)
        return _dot(m, hi) + _dot(m, mid) + _dot(m, lo)

    qvf = qvf_ref[...]
    fl = _dot(flip, jnp.concatenate([qvb_ref[...], kb_ref[...]], axis=1))
    q = jnp.concatenate([qvf[:, :hw].astype(F32), fl[:, :hw]], axis=0)
    v = jnp.concatenate([qvf[:, hw:].astype(F32), fl[:, hw:2 * hw]], axis=0)
    k = jnp.concatenate([kf_ref[...].astype(F32), fl[:, 2 * hw:]], axis=0)
    b = jnp.concatenate([cumsum(tri_ref, gf_ref[...]), cumsum(trij_ref, gb_ref[...])], axis=0) * LOG2_E

    ngr = 2 * TM // HGRN_SUB
    nsub = HGRN_CHUNK // HGRN_SUB
    q3 = q.reshape(ngr, HGRN_SUB, hw)
    b3 = b.reshape(ngr, HGRN_SUB, hw)
    k3 = k.reshape(ngr, HGRN_SUB, hw)
    for s in range(HGRN_SUB):
        x = q3 * jnp.exp2(jnp.minimum(b3 - b3[:, s:s + 1, :], 0.0)) * k3[:, s:s + 1, :]
        x_s[:, hw * s:hw * (s + 1)] = x.reshape(2 * TM, hw).astype(BF16)
    a = _dot(x_s[...], e_ref[...])

    bd = bd_ref[...]
    bd16 = bd.astype(BF16)
    md = md_ref[...]
    mo = mo_ref[...]
    for d in range(2):
        st = st_ref[d]
        outs = []
        for c in range(nch):
            r0 = TM * d + HGRN_CHUNK * c
            bc = b[r0:r0 + HGRN_CHUNK]
            qc = q[r0:r0 + HGRN_CHUNK]
            kc = k[r0:r0 + HGRN_CHUNK]
            vc = v[r0:r0 + HGRN_CHUNK]
            ac = a[r0:r0 + HGRN_CHUNK] * md
            r_end = [bc[HGRN_SUB * (j + 1) - 1:HGRN_SUB * (j + 1), :] for j in range(nsub)]
            r_own = jnp.concatenate([jnp.broadcast_to(r, (HGRN_SUB, hw)) for r in r_end], axis=0)
            kt_bd = jnp.concatenate([(kc * jnp.exp2(r_own - bc)).astype(BF16)] * 4, axis=0) * bd16
            q_rel = jnp.concatenate(
                [qc * jnp.exp2(jnp.minimum(bc - r_end[j], 0.0)) for j in range(nsub - 1)], axis=0).astype(BF16)
            p = _dot_nt(q_rel, kt_bd) * mo
            for j in range(nsub - 1):
                ac = ac + p[HGRN_CHUNK * j:HGRN_CHUNK * (j + 1)]
            bl = r_end[-1]
            qe = (qc * jnp.exp2(bc)).astype(BF16)
            ke = (kc * jnp.exp2(bl - bc)).astype(BF16)
            vcb = vc.astype(BF16)
            vbd = jnp.concatenate([vcb] * 4, axis=0) * bd16
            outs.append(_dot(ac.astype(BF16), vbd) + _dot_nt(qe, st.astype(BF16)))
            st = st * jnp.exp2(bl) + _dot_tn(vcb, ke) * bd
        st_ref[d] = st
        od = jnp.concatenate(outs, axis=0).astype(BF16)
        if d == 0:
            of_ref[...] = od
        else:
            ob_ref[...] = _dot(flip, od).astype(BF16)


def _hgrn_consts():
    hw = 2 * LANES
    r = jnp.arange(TM)
    same_chunk = (r[:, None] // HGRN_CHUNK) == (r[None, :] // HGRN_CHUNK)
    tri = (same_chunk & (r[None, :] <= r[:, None])).astype(F32)
    flip = (r[:, None] + r[None, :] == TM - 1).astype(F32)
    trij = tri @ flip
    c = jnp.arange(hw)
    s = jnp.arange(HGRN_SUB)
    e = ((c[None, :, None] // HEAD_DIM == c[None, None, :] // HGRN_CHUNK)
         & (c[None, None, :] % HGRN_SUB == s[:, None, None])).astype(BF16).reshape(HGRN_SUB * hw, hw)
    bd = (c[:, None] // HEAD_DIM == c[None, :] // HEAD_DIM).astype(F32)
    tt = jnp.arange(HGRN_CHUNK)[:, None]
    ss = (c % HGRN_CHUNK)[None, :]
    md = ((ss // HGRN_SUB == tt // HGRN_SUB) & (ss <= tt)).astype(F32)
    mo = jnp.concatenate([((ss // HGRN_SUB == j) & (tt // HGRN_SUB > j)).astype(F32)
                          for j in range(HGRN_CHUNK // HGRN_SUB - 1)], axis=0)
    return tri.astype(BF16), trij.astype(BF16), flip.astype(BF16), e, bd, md, mo


def _hgrn_call(qv, kk, gg, consts):
    b, t, _ = qv.shape
    nt = t // TM
    hw = 2 * LANES
    tri, trij, flip, e, bd, md, mo = consts
    fwd = lambda i, j: j
    bwd = lambda i, j: jnp.where(j == 0, 0, nt - j)
    const2 = lambda a: pl.BlockSpec(a.shape, lambda i, j: (0,) * a.ndim)
    return pl.pallas_call(
        _hgrn_kernel,
        grid=(b, nt),
        in_specs=[
            pl.BlockSpec((None, TM, 2 * hw), lambda i, j: (i, fwd(i, j), 0)),
            pl.BlockSpec((None, TM, hw), lambda i, j: (i, fwd(i, j), 0)),
            pl.BlockSpec((None, TM, hw), lambda i, j: (i, fwd(i, j), 0)),
            pl.BlockSpec((None, TM, 2 * hw), lambda i, j: (i, bwd(i, j), 0)),
            pl.BlockSpec((None, TM, hw), lambda i, j: (i, bwd(i, j), 1)),
            pl.BlockSpec((None, TM, hw), lambda i, j: (i, bwd(i, j), 1)),
            const2(tri), const2(trij), const2(flip), const2(e), const2(bd), const2(md), const2(mo),
        ],
        out_specs=[
            pl.BlockSpec((None, TM, hw), lambda i, j: (i, fwd(i, j), 0)),
            pl.BlockSpec((None, TM, hw), lambda i, j: (i, bwd(i, j), 0)),
        ],
        out_shape=[jax.ShapeDtypeStruct((b, t, hw), BF16)] * 2,
        scratch_shapes=[
            pltpu.VMEM((2, hw, hw), F32),
            pltpu.VMEM((2 * TM, HGRN_SUB * hw), BF16),
        ],
        compiler_params=_cparams("parallel", "arbitrary"),
        name="hgrn",
    )(qv, kk, gg, qv, kk, gg, tri, trij, flip, e, bd, md, mo)


def _outproj_kernel(x_ref, mod_ref, ng_ref, attn_ref, cbog_ref, u_ref, up_ref, un_ref, of_ref, ob_ref,
                    hgg_ref, cw_ref, hn_ref, w_ref, o_ref, *, n_tiles):
    t = pl.program_id(1)
    cw = 2 * LANES
    u = u_ref[...]
    row = lax.broadcasted_iota(jnp.int32, (TM, cw), 0)
    prev_row = jnp.where(t >= 2, up_ref[7:8, :], 0.0)
    next_row = jnp.where((t >= 1) & (t < n_tiles - 1), un_ref[0:1, :], 0.0)
    u_prev = jnp.where(row == 0, prev_row, pltpu.roll(u, 1, 0))
    u_next = jnp.where(row == TM - 1, next_row, pltpu.roll(u, TM - 1, 0))
    cbog = cbog_ref[...].astype(F32)
    conv = cbog[:, :cw] * (cw_ref[0:1, :] * u_prev + cw_ref[1:2, :] * u + cw_ref[2:3, :] * u_next)

    o = of_ref[...].astype(F32) + ob_ref[...].astype(F32)
    ms = _dot((o * o).astype(BF16), hn_ref[...])
    og = cbog[:, cw:]
    hg = o * lax.rsqrt(ms + EPS) * hgg_ref[...] * (og * _sigmoid(og))

    y = (_dot(attn_ref[...], w_ref[0:2 * cw, :]) + _dot(conv.astype(BF16), w_ref[2 * cw:3 * cw, :])
         + _dot(hg.astype(BF16), w_ref[3 * cw:4 * cw, :]))
    o_ref[...] = x_ref[...] + mod_ref[2:3, :] * _rms(y, ng_ref[1:2, :])


def _outproj_call(x_all, mod, ng, attn, cbog, u, o_f, o_b, hgg, conv_w, hn, w_out):
    b, t, d = x_all.shape
    nt = t // TM
    cw = 2 * LANES
    nb8 = t // 8
    tok = lambda w: pl.BlockSpec((None, TM, w), lambda i, j: (i, j, 0))
    const = lambda r, w: pl.BlockSpec((r, w), lambda i, j: (0, 0))
    return pl.pallas_call(
        functools.partial(_outproj_kernel, n_tiles=nt),
        grid=(b, nt),
        in_specs=[
            tok(d),
            pl.BlockSpec((None, None, N_MOD, d), lambda i, j: (i, jnp.minimum(j, 1), 0, 0)),
            const(8, d),
            tok(2 * cw),
            tok(2 * cw),
            tok(cw),
            pl.BlockSpec((None, 8, cw), lambda i, j: (i, jnp.maximum(j * (TM // 8) - 1, 0), 0)),
            pl.BlockSpec((None, 8, cw), lambda i, j: (i, jnp.minimum((j + 1) * (TM // 8), nb8 - 1), 0)),
            tok(cw),
            tok(cw),
            const(1, cw),
            const(8, cw),
            const(cw, cw),
            const(d, d),
        ],
        out_specs=tok(d),
        out_shape=jax.ShapeDtypeStruct((b, t, d), F32),
        compiler_params=_cparams("parallel", "parallel"),
        name="outproj",
    )(x_all, mod, ng, attn, cbog, u, u, u, o_f, o_b, hgg, conv_w, hn, w_out)


def _mod_rows(modt_ref, i, rows):
    sub = rows // modt_ref.shape[0]
    return jnp.concatenate(
        [jnp.broadcast_to(modt_ref[s, i:i + 1, :], (sub, modt_ref.shape[2])) for s in range(modt_ref.shape[0])],
        axis=0)


def _store_split(ref, val):
    w = ref.shape[2]
    for c in range(ref.shape[0]):
        ref[c] = val[:, w * c:w * (c + 1)]


def _load_split(ref):
    return jnp.concatenate([ref[c] for c in range(ref.shape[0])], axis=1)


def _split_spec(rows):
    return lambda d: pl.BlockSpec((GATHER_SPLIT, rows, d // GATHER_SPLIT), lambda i, *_: (0, i, 0))


def _ffn_kernel(x_ref, modt_ref, ng_ref, wg_ref, wu_ref, wd_ref, o_ref, *, tf):
    x = x_ref[...]
    rows = x.shape[0]
    h = (_rms(x, ng_ref[2:3, :]) * (1.0 + _mod_rows(modt_ref, 4, rows)) + _mod_rows(modt_ref, 3, rows)).astype(BF16)
    acc = jnp.zeros(x.shape, F32)
    for f0 in range(0, wg_ref.shape[1], tf):
        g = _dot(h, wg_ref[:, f0:f0 + tf])
        up = _dot(h, wu_ref[:, f0:f0 + tf])
        acc = acc + _dot((g * _sigmoid(g) * up).astype(BF16), wd_ref[f0:f0 + tf, :])
    o_ref[...] = x + _mod_rows(modt_ref, 5, rows) * _rms(acc, ng_ref[3:4, :])


def _ffn_call(x_flat, modt, ng, wg, wu, wd):
    n, d = x_flat.shape
    dff = wg.shape[1]
    nsub = FFN_TM // TM
    resident = lambda a: pl.BlockSpec(a.shape, lambda i: (0, 0), pipeline_mode=pl.Buffered(1))
    return pl.pallas_call(
        functools.partial(_ffn_kernel, tf=256),
        grid=(n // FFN_TM,),
        in_specs=[
            pl.BlockSpec((FFN_TM, d), lambda i: (i, 0)),
            pl.BlockSpec((nsub, N_MOD, d), lambda i: (i, 0, 0)),
            pl.BlockSpec((8, d), lambda i: (0, 0)),
            resident(wg), resident(wu), resident(wd),
        ],
        out_specs=pl.BlockSpec((FFN_TM, d), lambda i: (i, 0)),
        out_shape=jax.ShapeDtypeStruct((n, d), F32),
        compiler_params=_cparams("parallel"),
        name="ffn",
    )(x_flat, modt, ng, wg, wu, wd)


def _router_kernel(x_ref, modt_ref, ng_ref, wr_ref, h_ref, r_ref):
    x = x_ref[...]
    rows = x.shape[0]
    h = _rms(x, ng_ref[2:3, :]) * (1.0 + _mod_rows(modt_ref, 4, rows)) + _mod_rows(modt_ref, 3, rows)
    _store_split(h_ref, h)
    h_hi = h.astype(BF16)
    h_lo = (h - h_hi.astype(F32)).astype(BF16)
    logits = _dot(h_hi, wr_ref[0]) + _dot(h_lo, wr_ref[0]) + _dot(h_hi, wr_ref[1])
    lane = lax.broadcasted_iota(jnp.int32, logits.shape, 1)
    lg = jnp.where(lane < N_EXPERTS, logits, NEG_BIG)
    m1 = jnp.max(lg, axis=-1, keepdims=True)
    i1 = jnp.min(jnp.where(lg == m1, lane, LANES), axis=-1, keepdims=True)
    lg2 = jnp.where(lane == i1, NEG_BIG, lg)
    m2 = jnp.max(lg2, axis=-1, keepdims=True)
    i2 = jnp.min(jnp.where(lg2 == m2, lane, LANES), axis=-1, keepdims=True)
    e2 = jnp.exp(m2 - m1)
    w1 = 1.0 / (1.0 + e2)
    w2 = e2 / (1.0 + e2)
    r_ref[...] = jnp.where(lane == 0, i1.astype(F32),
                           jnp.where(lane == 1, i2.astype(F32),
                                     jnp.where(lane == 2, w1, jnp.where(lane == 3, w2, 0.0))))


def _router_call(x_flat, modt, ng, wr):
    n, d = x_flat.shape
    nsub = FFN_TM // TM
    return pl.pallas_call(
        _router_kernel,
        grid=(n // FFN_TM,),
        in_specs=[
            pl.BlockSpec((FFN_TM, d), lambda i: (i, 0)),
            pl.BlockSpec((nsub, N_MOD, d), lambda i: (i, 0, 0)),
            pl.BlockSpec((8, d), lambda i: (0, 0)),
            pl.BlockSpec(wr.shape, lambda i: (0, 0, 0)),
        ],
        out_specs=[_split_spec(FFN_TM)(d), pl.BlockSpec((FFN_TM, LANES), lambda i: (i, 0))],
        out_shape=[jax.ShapeDtypeStruct((GATHER_SPLIT, n, d // GATHER_SPLIT), F32),
                   jax.ShapeDtypeStruct((n, LANES), F32)],
        compiler_params=_cparams("parallel"),
        name="router",
    )(x_flat, modt, ng, wr)


def _gather_flat(data, idx):
    m = idx.shape[0]
    w = data.shape[1]
    mesh = plsc.VectorSubcoreMesh(core_axis_name="core", subcore_axis_name="subcore")

    @functools.partial(pl.kernel, out_type=jax.ShapeDtypeStruct((m, w), data.dtype), mesh=mesh)
    def gather(x_hbm, i_hbm, o_hbm):
        def body(i_vmem, o_vmem):
            pltpu.sync_copy(x_hbm.at[i_vmem.at[0]], o_vmem)

        pltpu.emit_pipeline(
            body,
            grid=(m // GATHER_WIN,),
            in_specs=[pl.BlockSpec((1, GATHER_WIN), lambda i: (0, i))],
            out_specs=[pl.BlockSpec((GATHER_WIN, w), lambda i: (i, 0))],
            core_axis_name=("core", "subcore"),
            dimension_semantics=(pltpu.PARALLEL,),
        )(i_hbm, o_hbm)

    return gather(data, idx.reshape(1, m))


def _gather_rows(data, idx):
    s, r, w = data.shape
    m = idx.shape[0]
    idx_all = (idx[None, :] + (jnp.arange(s, dtype=jnp.int32) * r)[:, None]).reshape(s * m)
    return _gather_flat(data.reshape(s * r, w), idx_all).reshape(s, m, w)


def _expert_kernel(be_ref, nv_ref, xs_ref, wg_ref, wu_ref, wd_ref, o_ref, acc_ref, *, nf):
    i = pl.program_id(0)
    f = pl.program_id(1)
    valid = i < nv_ref[0]

    @pl.when(valid)
    def _():
        xb = _load_split(xs_ref).astype(BF16)
        g = _dot(xb, wg_ref[...])
        up = _dot(xb, wu_ref[...])
        part = _dot((g * _sigmoid(g) * up).astype(BF16), wd_ref[...])

        @pl.when(f == 0)
        def _():
            acc_ref[...] = part

        @pl.when(f > 0)
        def _():
            acc_ref[...] += part

    @pl.when(f == nf - 1)
    def _():
        _store_split(o_ref, jnp.where(valid, acc_ref[...], 0.0))


def _expert_call(block_expert, n_valid, xs, wg, wu, wd):
    n_slots = xs.shape[1]
    d = wg.shape[1]
    dff = wg.shape[2]
    nf = dff // MOE_TF
    nb = n_slots // MOE_BLK
    grid_spec = pltpu.PrefetchScalarGridSpec(
        num_scalar_prefetch=2,
        grid=(nb, nf),
        in_specs=[
            _split_spec(MOE_BLK)(d),
            pl.BlockSpec((None, d, MOE_TF), lambda i, f, be, nv: (be[i], 0, f)),
            pl.BlockSpec((None, d, MOE_TF), lambda i, f, be, nv: (be[i], 0, f)),
            pl.BlockSpec((None, MOE_TF, d), lambda i, f, be, nv: (be[i], f, 0)),
        ],
        out_specs=_split_spec(MOE_BLK)(d),
        scratch_shapes=[pltpu.VMEM((MOE_BLK, d), F32)],
    )
    return pl.pallas_call(
        functools.partial(_expert_kernel, nf=nf),
        grid_spec=grid_spec,
        out_shape=jax.ShapeDtypeStruct((GATHER_SPLIT, n_slots, d // GATHER_SPLIT), F32),
        compiler_params=_cparams("arbitrary", "arbitrary"),
        name="experts",
    )(block_expert, n_valid, xs, wg, wu, wd)


def _combine_kernel(x_ref, modt_ref, ng_ref, y1_ref, y2_ref, r_ref, o_ref):
    x = x_ref[...]
    rows = x.shape[0]
    f = r_ref[:, 2:3] * _load_split(y1_ref) + r_ref[:, 3:4] * _load_split(y2_ref)
    o_ref[...] = x + _mod_rows(modt_ref, 5, rows) * _rms(f, ng_ref[3:4, :])


def _combine_call(x_flat, modt, ng, y1, y2, rinfo):
    n, d = x_flat.shape
    nsub = FFN_TM // TM
    tok = lambda w: pl.BlockSpec((FFN_TM, w), lambda i: (i, 0))
    return pl.pallas_call(
        _combine_kernel,
        grid=(n // FFN_TM,),
        in_specs=[tok(d), pl.BlockSpec((nsub, N_MOD, d), lambda i: (i, 0, 0)), pl.BlockSpec((8, d), lambda i: (0, 0)),
                  _split_spec(FFN_TM)(d), _split_spec(FFN_TM)(d), tok(LANES)],
        out_specs=tok(d),
        out_shape=jax.ShapeDtypeStruct((n, d), F32),
        compiler_params=_cparams("parallel"),
        name="combine",
    )(x_flat, modt, ng, y1, y2, rinfo)


def _moe_layer(x_flat, modt, ng, w_router, wg, wu, wd):
    n, d = x_flat.shape
    wr = jnp.pad(w_router, ((0, 0), (0, LANES - N_EXPERTS)))
    wr_hi = wr.astype(BF16)
    wr_lo = (wr - wr_hi.astype(F32)).astype(BF16)
    h, rinfo = _router_call(x_flat, modt, ng, jnp.stack([wr_hi, wr_lo]))

    n_assign = 2 * n
    expert = rinfo[:, :2].astype(jnp.int32).reshape(n_assign)
    onehot = (expert[:, None] == jnp.arange(N_EXPERTS, dtype=jnp.int32)[None, :]).astype(jnp.int32)
    csum = jnp.cumsum(onehot, axis=0)
    rank = jnp.sum((csum - onehot) * onehot, axis=1)
    counts = csum[-1]
    padded = (counts + MOE_BLK - 1) // MOE_BLK * MOE_BLK
    pad_end = jnp.cumsum(padded)
    pad_start = pad_end - padded
    dest = jnp.sum(onehot * pad_start[None, :], axis=1) + rank
    nb = -(-n_assign // MOE_BLK) + N_EXPERTS
    row_quantum = 32 * GATHER_WIN // GATHER_SPLIT
    nb = -(-nb * MOE_BLK // row_quantum) * row_quantum // MOE_BLK
    n_slots = nb * MOE_BLK
    slot_tok = jnp.zeros((n_slots,), jnp.int32).at[dest].set(
        jnp.arange(n_assign, dtype=jnp.int32) // 2, unique_indices=True, mode="promise_in_bounds")
    block_expert = jnp.minimum(
        jnp.searchsorted(pad_end, jnp.arange(nb, dtype=jnp.int32) * MOE_BLK, side="right"), N_EXPERTS - 1
    ).astype(jnp.int32)
    n_valid = (pad_end[-1:] // MOE_BLK).astype(jnp.int32)

    xs = _gather_rows(h, slot_tok)
    ys = _expert_call(block_expert, n_valid, xs, wg, wu, wd)
    dest2 = dest.reshape(n, 2)
    y1 = _gather_rows(ys, dest2[:, 0])
    y2 = _gather_rows(ys, dest2[:, 1])
    return _combine_call(x_flat, modt, ng, y1, y2, rinfo)


def _rope_tables(n_ctx, length):
    rows = length // GRID_W
    row = jnp.repeat(jnp.arange(rows, dtype=F32), GRID_W)
    col = jnp.tile(jnp.arange(GRID_W, dtype=F32), rows)
    axis_dim = HEAD_DIM // 2
    inv_freq = ROPE_THETA ** (-jnp.arange(0, axis_dim, 2, dtype=F32) / axis_dim)
    ar = row[:, None] * inv_freq
    ac = col[:, None] * inv_freq
    cos = jnp.concatenate([jnp.cos(ar), jnp.cos(ar), jnp.cos(ac), jnp.cos(ac)], axis=1)
    sin = jnp.concatenate([-jnp.sin(ar), jnp.sin(ar), -jnp.sin(ac), jnp.sin(ac)], axis=1)
    cos = jnp.concatenate([jnp.ones((n_ctx, HEAD_DIM), F32), cos], axis=0)
    sin = jnp.concatenate([jnp.zeros((n_ctx, HEAD_DIM), F32), sin], axis=0)
    return jnp.tile(cos, (1, 2)), jnp.tile(sin, (1, 2))


def _inproj_weights(w_in_l):
    d = w_in_l.shape[0]
    aw = d // 2
    nh = aw // HEAD_DIM
    wq = w_in_l[:, :aw].reshape(d, nh, HEAD_DIM)
    z = jnp.zeros((d, nh // 2, HEAD_DIM), w_in_l.dtype)
    q_pad = jnp.concatenate([
        jnp.concatenate([wq[:, :nh // 2], z], axis=2),
        jnp.concatenate([z, wq[:, nh // 2:]], axis=2),
    ], axis=1).reshape(d, nh * LANES)
    return jnp.concatenate([q_pad, w_in_l[:, aw:]], axis=1).astype(BF16)


def kernel(x, c, ctx, c_ctx, ada_w, ada_b, norm_g, w_in, w_out, attn_q_g, attn_k_g, conv_w, hgrn_lb, hgrn_g,
           ffn_w_gate, ffn_w_up, ffn_w_down, moe_router, moe_w_gate, moe_w_up, moe_w_down):
    bsz, length, d = x.shape
    n_ctx = ctx.shape[1]
    depth = w_in.shape[0]
    t = n_ctx + length
    nt = t // TM
    assert n_ctx == TM and length % TM == 0 and d == 8 * LANES
    assert (bsz * t) % FFN_TM == 0 and (bsz * t) % (32 * GATHER_WIN // GATHER_SPLIT) == 0

    r_pad = -(-(bsz + 1) // 8) * 8
    c_all = jnp.concatenate([c, c_ctx[None, :], jnp.zeros((r_pad - bsz - 1, d), F32)], axis=0)
    mod_all = _ada_call(c_all, ada_w, ada_b).reshape(depth, r_pad, N_MOD, d)

    cos_t, sin_t = _rope_tables(n_ctx, length)
    lb_all = jnp.cumsum(jax.nn.softmax(hgrn_lb.astype(F32), axis=0), axis=0)
    lb_all = lb_all - lb_all[0]
    hconsts = _hgrn_consts()
    cw = 2 * LANES
    cidx = jnp.arange(cw)
    head_mean = ((cidx[:, None] // HEAD_DIM == cidx[None, :] // HEAD_DIM).astype(F32) / HEAD_DIM).astype(BF16)

    x_all = jnp.concatenate([ctx, x], axis=1)
    for l in range(depth):
        mod_lat = mod_all[l, :bsz]
        mod_ctx = jnp.broadcast_to(mod_all[l, bsz][None], (bsz, N_MOD, d))
        mod = jnp.stack([mod_ctx, mod_lat], axis=1)
        modt = jnp.concatenate([mod[:, :1], jnp.broadcast_to(mod[:, 1:], (bsz, nt - 1, N_MOD, d))], axis=1)
        modt = modt.reshape(bsz * nt, N_MOD, d)
        ng = jnp.pad(norm_g[l], ((0, 4), (0, 0)))
        lb = lb_all[l].reshape(1, 2 * cw)
        gc = jnp.concatenate([jnp.log(lb), jnp.log1p(-lb), 1.0 - lb, jnp.zeros((5, 2 * cw), F32)], axis=0)
        qg = jnp.tile(attn_q_g[l], 2)[None, :]
        kg = jnp.tile(attn_k_g[l], 2)[None, :]

        q, k, vt, cbog, u, qv, kk, gg = _inproj_call(x_all, mod, ng, _inproj_weights(w_in[l]), cos_t, sin_t, qg, kg, gc)
        attn = _attn_call(q, k, vt)
        o_f, o_b = _hgrn_call(qv, kk, gg, hconsts)
        x_all = _outproj_call(x_all, mod, ng, attn, cbog, u, o_f, o_b, hgrn_g[l][None, :],
                              jnp.pad(conv_w[l], ((0, 5), (0, 0))), head_mean, w_out[l].astype(BF16))

        x_flat = x_all.reshape(bsz * t, d)
        if l % 2 == 0:
            x_flat = _ffn_call(x_flat, modt, ng, ffn_w_gate[l // 2].astype(BF16), ffn_w_up[l // 2].astype(BF16),
                               ffn_w_down[l // 2].astype(BF16))
        else:
            x_flat = _moe_layer(x_flat, modt, ng, moe_router[l // 2], moe_w_gate[l // 2].astype(BF16),
                                moe_w_up[l // 2].astype(BF16), moe_w_down[l // 2].astype(BF16))
        x_all = x_flat.reshape(bsz, t, d)
    return x_all[:, n_ctx:, :]
```

```python
import functools

import jax
import jax.numpy as jnp
from jax import lax
from jax.experimental import pallas as pl
from jax.experimental.pallas import tpu as pltpu
from jax.experimental.pallas import tpu_sc as plsc

F32 = jnp.float32
BF16 = jnp.bfloat16

HEAD_DIM = 64
GRID_W = 64
ROPE_THETA = 10000.0
ATTN_SCALE = HEAD_DIM ** -0.5
LOG2_E = 1.4426950408889634
HGRN_CHUNK = 64
HGRN_SUB = 16
VT_ROWS = HEAD_DIM + 16
ATTN_AHEAD = 4
ATTN_UNROLL = 8
N_EXPERTS = 8
N_MOD = 6
EPS = 1e-6

LANES = 128
TM = 256
FFN_TM = 512
MOE_BLK = 512
MOE_TF = 1792
GATHER_WIN = 128
GATHER_SPLIT = 4
NEG_BIG = -1e30
VMEM_LIMIT = 56 * 1024 * 1024


def _cparams(*sem):
    return pltpu.CompilerParams(dimension_semantics=sem, vmem_limit_bytes=VMEM_LIMIT)


def _sigmoid(z):
    return 1.0 / (1.0 + jnp.exp(-z))


def _dot(a, b):
    return jnp.dot(a, b, preferred_element_type=F32)


def _dot_nt(a, b):
    return lax.dot_general(a, b, (((1,), (1,)), ((), ())), preferred_element_type=F32)


def _dot_tn(a, b):
    return lax.dot_general(a, b, (((0,), (0,)), ((), ())), preferred_element_type=F32)


def _rms(x, g):
    return x * lax.rsqrt(jnp.mean(x * x, axis=-1, keepdims=True) + EPS) * g


def _ada_kernel(c_ref, w_ref, b_ref, o_ref):
    c = c_ref[...]
    s = (c * _sigmoid(c)).astype(BF16)
    o_ref[...] = _dot(s, w_ref[...].astype(BF16)) + b_ref[...]


def _ada_call(c_all, ada_w, ada_b):
    depth, d, n = ada_w.shape
    r = c_all.shape[0]
    tn = 512
    return pl.pallas_call(
        _ada_kernel,
        grid=(depth, n // tn),
        in_specs=[
            pl.BlockSpec((r, d), lambda l, j: (0, 0)),
            pl.BlockSpec((None, d, tn), lambda l, j: (l, 0, j)),
            pl.BlockSpec((None, 1, tn), lambda l, j: (l, 0, j)),
        ],
        out_specs=pl.BlockSpec((None, r, tn), lambda l, j: (l, 0, j)),
        out_shape=jax.ShapeDtypeStruct((depth, r, n), F32),
        compiler_params=_cparams("parallel", "parallel"),
        name="ada",
    )(c_all, ada_w, ada_b.reshape(depth, 1, n))


def _inproj_kernel(x_ref, mod_ref, ng_ref, w_ref, cos_ref, sin_ref, qg_ref, kg_ref, gc_ref,
                   q_ref, k_ref, vt_ref, cbog_ref, u_ref, qv_ref, kk_ref, gg_ref):
    x = x_ref[...]
    h = _rms(x, ng_ref[0:1, :]) * (1.0 + mod_ref[1:2, :]) + mod_ref[0:1, :]
    hb = h.astype(BF16)

    def proj(c0, n):
        return _dot(hb, w_ref[:, c0:c0 + n])

    cos = cos_ref[...]
    sin = sin_ref[...]
    lane = lax.broadcasted_iota(jnp.int32, (TM, LANES), 1)
    first_half = (lane % 32) < 16
    lo = lane < HEAD_DIM

    def rope(v):
        partner = jnp.where(first_half, pltpu.roll(v, LANES - 16, 1), pltpu.roll(v, 16, 1))
        return v * cos + partner * sin

    qg = qg_ref[...]
    for hp in range(4):
        q2 = proj(2 * LANES * hp, 2 * LANES)
        for s in range(2):
            qh = q2[:, LANES * s:LANES * (s + 1)]
            r = lax.rsqrt(jnp.sum(qh * qh, axis=-1, keepdims=True) * (1.0 / HEAD_DIM) + EPS)
            q_ref[2 * hp + s] = (rope(qh * r * qg) * (ATTN_SCALE * LOG2_E)).astype(BF16)

    kv = proj(8 * LANES, 2 * LANES)
    kx = kv[:, :LANES]
    v = kv[:, LANES:]
    k2 = kx * kx
    s_lo = jnp.sum(jnp.where(lo, k2, 0.0), axis=-1, keepdims=True)
    s_hi = jnp.sum(jnp.where(lo, 0.0, k2), axis=-1, keepdims=True)
    r = lax.rsqrt(jnp.where(lo, s_lo, s_hi) * (1.0 / HEAD_DIM) + EPS)
    k_ref[...] = rope(kx * r * kg_ref[...]).astype(BF16)
    vt = v.T
    ones = jnp.ones((VT_ROWS - HEAD_DIM, TM), F32)
    vt_ref[0] = jnp.concatenate([vt[:HEAD_DIM], ones], axis=0).astype(BF16)
    vt_ref[1] = jnp.concatenate([vt[HEAD_DIM:], ones], axis=0).astype(BF16)

    c0 = 10 * LANES
    c3 = proj(c0, 6 * LANES)
    cw = 2 * LANES
    u_ref[...] = c3[:, cw:2 * cw] * c3[:, 2 * cw:3 * cw]

    hg = proj(c0 + 6 * LANES, 10 * LANES)
    zf, zb, iv, hq, og = (hg[:, cw * i:cw * (i + 1)] for i in range(5))
    cbog_ref[:, :cw] = c3[:, :cw].astype(BF16)
    cbog_ref[:, cw:] = og.astype(BF16)
    qv_ref[:, :cw] = (hq * _sigmoid(hq)).astype(BF16)
    qv_ref[:, cw:] = iv.astype(BF16)
    for d, z in enumerate((zf, zb)):
        log_lb = gc_ref[0:1, cw * d:cw * (d + 1)]
        log1m_lb = gc_ref[1:2, cw * d:cw * (d + 1)]
        one_m_lb = gc_ref[2:3, cw * d:cw * (d + 1)]
        t = jnp.exp(-jnp.abs(z))
        log_sig = jnp.minimum(z, 0.0) - jnp.log1p(t)
        a2 = log1m_lb + log_sig
        log_f = jnp.maximum(log_lb, a2) + jnp.log1p(jnp.exp(-jnp.abs(log_lb - a2)))
        sig_neg = jnp.where(z >= 0, t, 1.0) / (1.0 + t)
        gg_ref[:, cw * d:cw * (d + 1)] = log_f
        kk_ref[:, cw * d:cw * (d + 1)] = (one_m_lb * sig_neg).astype(BF16)


def _inproj_call(x_all, mod, ng, w1, cos_t, sin_t, qg, kg, gc):
    b, t, d = x_all.shape
    nt = t // TM
    nw = w1.shape[1]
    tok = lambda w: pl.BlockSpec((None, TM, w), lambda i, j: (i, j, 0))
    const = lambda r, w: pl.BlockSpec((r, w), lambda i, j: (0, 0))
    return pl.pallas_call(
        _inproj_kernel,
        grid=(b, nt),
        in_specs=[
            tok(d),
            pl.BlockSpec((None, None, N_MOD, d), lambda i, j: (i, jnp.minimum(j, 1), 0, 0)),
            const(8, d),
            const(d, nw),
            pl.BlockSpec((TM, LANES), lambda i, j: (j, 0)),
            pl.BlockSpec((TM, LANES), lambda i, j: (j, 0)),
            const(1, LANES),
            const(1, LANES),
            const(8, 4 * LANES),
        ],
        out_specs=[
            pl.BlockSpec((None, 8, TM, LANES), lambda i, j: (i, 0, j, 0)),
            tok(LANES),
            pl.BlockSpec((None, 2, None, VT_ROWS, TM), lambda i, j: (i, 0, j, 0, 0)),
            tok(4 * LANES),
            tok(2 * LANES),
            tok(4 * LANES),
            tok(4 * LANES),
            tok(4 * LANES),
        ],
        out_shape=[
            jax.ShapeDtypeStruct((b, 8, t, LANES), BF16),
            jax.ShapeDtypeStruct((b, t, LANES), BF16),
            jax.ShapeDtypeStruct((b, 2, nt, VT_ROWS, TM), BF16),
            jax.ShapeDtypeStruct((b, t, 4 * LANES), BF16),
            jax.ShapeDtypeStruct((b, t, 2 * LANES), F32),
            jax.ShapeDtypeStruct((b, t, 4 * LANES), BF16),
            jax.ShapeDtypeStruct((b, t, 4 * LANES), BF16),
            jax.ShapeDtypeStruct((b, t, 4 * LANES), F32),
        ],
        compiler_params=_cparams("parallel", "parallel"),
        name="inproj",
    )(x_all, mod, ng, w1, cos_t, sin_t, qg, kg, gc)


def _attn_kernel(q_ref, k_ref, vt_ref, o_ref, m_ref, acc_ref, *, n_kv):
    qi = pl.program_id(1)
    nkv = jnp.where(qi == 0, 1, n_kv)
    m_ref[...] = jnp.full(m_ref.shape, NEG_BIG, F32)
    acc_ref[...] = jnp.zeros(acc_ref.shape, F32)

    def score(j, h):
        r0 = pl.multiple_of(j * TM, TM)
        return _dot_nt(k_ref[pl.ds(r0, TM), :], q_ref[h])

    def tile_heads(j, scores, j_next):
        for h in range(8):
            hn = h + ATTN_AHEAD
            if hn < 8:
                scores.append(score(j, hn))
            elif j_next is not None:
                scores.append(score(j_next, hn - 8))
            s = scores.pop(0)
            m_old = m_ref[h]
            m_new = jnp.maximum(m_old, jnp.max(s, axis=0, keepdims=True))
            p = jnp.exp2(s - m_new).astype(BF16)
            pv = _dot(vt_ref[h // 4, j], p)
            acc_ref[h] = jnp.exp2(m_old - m_new) * acc_ref[h] + pv
            m_ref[h] = m_new
        return scores

    def body(i, ahead):
        scores = list(ahead)
        for u in range(ATTN_UNROLL):
            scores = tile_heads(ATTN_UNROLL * i + u, scores, ATTN_UNROLL * i + u + 1)
        return tuple(scores)

    ahead = lax.fori_loop(0, nkv // ATTN_UNROLL, body, tuple(score(0, h) for h in range(ATTN_AHEAD)))
    tile_heads(nkv - 1, list(ahead), None)
    outs = [acc_ref[h, :HEAD_DIM, :] / acc_ref[h, HEAD_DIM:HEAD_DIM + 1, :] for h in range(8)]
    o_ref[...] = jnp.concatenate(outs, axis=0).T.astype(BF16)


def _attn_call(q, k, vt):
    b, _, t, _ = q.shape
    nt = t // TM
    assert nt % ATTN_UNROLL == 1, "the kv loop takes ATTN_UNROLL tiles per trip and leaves one for after it"
    return pl.pallas_call(
        functools.partial(_attn_kernel, n_kv=nt),
        grid=(b, nt),
        in_specs=[
            pl.BlockSpec((None, 8, TM, LANES), lambda i, j: (i, 0, j, 0)),
            pl.BlockSpec((None, t, LANES), lambda i, j: (i, 0, 0)),
            pl.BlockSpec((None, 2, nt, VT_ROWS, TM), lambda i, j: (i, 0, 0, 0, 0)),
        ],
        out_specs=pl.BlockSpec((None, TM, 4 * LANES), lambda i, j: (i, j, 0)),
        out_shape=jax.ShapeDtypeStruct((b, t, 4 * LANES), BF16),
        scratch_shapes=[pltpu.VMEM((8, 1, TM), F32), pltpu.VMEM((8, VT_ROWS, TM), F32)],
        compiler_params=_cparams("parallel", "parallel"),
        name="attn",
    )(q, k, vt)


def _split3(g):
    hi = g.astype(BF16)
    r1 = g - hi.astype(F32)
    mid = r1.astype(BF16)
    lo = (r1 - mid.astype(F32)).astype(BF16)
    return hi, mid, lo


def _hgrn_kernel(qvf_ref, kf_ref, gf_ref, qvb_ref, kb_ref, gb_ref, tri_ref, trij_ref, j_ref, e_ref, bd_ref,
                 md_ref, mo_ref, of_ref, ob_ref, st_ref, x_s):
    hw = 2 * LANES
    nch = TM // HGRN_CHUNK

    @pl.when(pl.program_id(1) == 0)
    def _():
        st_ref[...] = jnp.zeros(st_ref.shape, F32)

    flip = j_ref[...]

    def cumsum(m_ref, g):
        m = m_ref[...]
        hi, mid, lo = _split3(g)
        return _dot(m, hi) + _dot(m, mid) + _dot(m, lo)

    qvf = qvf_ref[...]
    fl = _dot(flip, jnp.concatenate([qvb_ref[...], kb_ref[...]], axis=1))
    q = jnp.concatenate([qvf[:, :hw].astype(F32), fl[:, :hw]], axis=0)
    v = jnp.concatenate([qvf[:, hw:].astype(F32), fl[:, hw:2 * hw]], axis=0)
    k = jnp.concatenate([kf_ref[...].astype(F32), fl[:, 2 * hw:]], axis=0)
    b = jnp.concatenate([cumsum(tri_ref, gf_ref[...]), cumsum(trij_ref, gb_ref[...])], axis=0) * LOG2_E

    ngr = 2 * TM // HGRN_SUB
    nsub = HGRN_CHUNK // HGRN_SUB
    q3 = q.reshape(ngr, HGRN_SUB, hw)
    b3 = b.reshape(ngr, HGRN_SUB, hw)
    k3 = k.reshape(ngr, HGRN_SUB, hw)
    for s in range(HGRN_SUB):
        x = q3 * jnp.exp2(jnp.minimum(b3 - b3[:, s:s + 1, :], 0.0)) * k3[:, s:s + 1, :]
        x_s[:, hw * s:hw * (s + 1)] = x.reshape(2 * TM, hw).astype(BF16)
    a = _dot(x_s[...], e_ref[...])

    bd = bd_ref[...]
    bd16 = bd.astype(BF16)
    md = md_ref[...]
    mo = mo_ref[...]
    for d in range(2):
        st = st_ref[d]
        outs = []
        for c in range(nch):
            r0 = TM * d + HGRN_CHUNK * c
            bc = b[r0:r0 + HGRN_CHUNK]
            qc = q[r0:r0 + HGRN_CHUNK]
            kc = k[r0:r0 + HGRN_CHUNK]
            vc = v[r0:r0 + HGRN_CHUNK]
            ac = a[r0:r0 + HGRN_CHUNK] * md
            r_end = [bc[HGRN_SUB * (j + 1) - 1:HGRN_SUB * (j + 1), :] for j in range(nsub)]
            r_own = jnp.concatenate([jnp.broadcast_to(r, (HGRN_SUB, hw)) for r in r_end], axis=0)
            kt_bd = jnp.concatenate([(kc * jnp.exp2(r_own - bc)).astype(BF16)] * 4, axis=0) * bd16
            q_rel = jnp.concatenate(
                [qc * jnp.exp2(jnp.minimum(bc - r_end[j], 0.0)) for j in range(nsub - 1)], axis=0).astype(BF16)
            p = _dot_nt(q_rel, kt_bd) * mo
            for j in range(nsub - 1):
                ac = ac + p[HGRN_CHUNK * j:HGRN_CHUNK * (j + 1)]
            bl = r_end[-1]
            qe = (qc * jnp.exp2(bc)).astype(BF16)
            ke = (kc * jnp.exp2(bl - bc)).astype(BF16)
            vcb = vc.astype(BF16)
            vbd = jnp.concatenate([vcb] * 4, axis=0) * bd16
            outs.append(_dot(ac.astype(BF16), vbd) + _dot_nt(qe, st.astype(BF16)))
            st = st * jnp.exp2(bl) + _dot_tn(vcb, ke) * bd
        st_ref[d] = st
        od = jnp.concatenate(outs, axis=0).astype(BF16)
        if d == 0:
            of_ref[...] = od
        else:
            ob_ref[...] = _dot(flip, od).astype(BF16)


def _hgrn_consts():
    hw = 2 * LANES
    r = jnp.arange(TM)
    same_chunk = (r[:, None] // HGRN_CHUNK) == (r[None, :] // HGRN_CHUNK)
    tri = (same_chunk & (r[None, :] <= r[:, None])).astype(F32)
    flip = (r[:, None] + r[None, :] == TM - 1).astype(F32)
    trij = tri @ flip
    c = jnp.arange(hw)
    s = jnp.arange(HGRN_SUB)
    e = ((c[None, :, None] // HEAD_DIM == c[None, None, :] // HGRN_CHUNK)
         & (c[None, None, :] % HGRN_SUB == s[:, None, None])).astype(BF16).reshape(HGRN_SUB * hw, hw)
    bd = (c[:, None] // HEAD_DIM == c[None, :] // HEAD_DIM).astype(F32)
    tt = jnp.arange(HGRN_CHUNK)[:, None]
    ss = (c % HGRN_CHUNK)[None, :]
    md = ((ss // HGRN_SUB == tt // HGRN_SUB) & (ss <= tt)).astype(F32)
    mo = jnp.concatenate([((ss // HGRN_SUB == j) & (tt // HGRN_SUB > j)).astype(F32)
                          for j in range(HGRN_CHUNK // HGRN_SUB - 1)], axis=0)
    return tri.astype(BF16), trij.astype(BF16), flip.astype(BF16), e, bd, md, mo


def _hgrn_call(qv, kk, gg, consts):
    b, t, _ = qv.shape
    nt = t // TM
    hw = 2 * LANES
    tri, trij, flip, e, bd, md, mo = consts
    fwd = lambda i, j: j
    bwd = lambda i, j: jnp.where(j == 0, 0, nt - j)
    const2 = lambda a: pl.BlockSpec(a.shape, lambda i, j: (0,) * a.ndim)
    return pl.pallas_call(
        _hgrn_kernel,
        grid=(b, nt),
        in_specs=[
            pl.BlockSpec((None, TM, 2 * hw), lambda i, j: (i, fwd(i, j), 0)),
            pl.BlockSpec((None, TM, hw), lambda i, j: (i, fwd(i, j), 0)),
            pl.BlockSpec((None, TM, hw), lambda i, j: (i, fwd(i, j), 0)),
            pl.BlockSpec((None, TM, 2 * hw), lambda i, j: (i, bwd(i, j), 0)),
            pl.BlockSpec((None, TM, hw), lambda i, j: (i, bwd(i, j), 1)),
            pl.BlockSpec((None, TM, hw), lambda i, j: (i, bwd(i, j), 1)),
            const2(tri), const2(trij), const2(flip), const2(e), const2(bd), const2(md), const2(mo),
        ],
        out_specs=[
            pl.BlockSpec((None, TM, hw), lambda i, j: (i, fwd(i, j), 0)),
            pl.BlockSpec((None, TM, hw), lambda i, j: (i, bwd(i, j), 0)),
        ],
        out_shape=[jax.ShapeDtypeStruct((b, t, hw), BF16)] * 2,
        scratch_shapes=[
            pltpu.VMEM((2, hw, hw), F32),
            pltpu.VMEM((2 * TM, HGRN_SUB * hw), BF16),
        ],
        compiler_params=_cparams("parallel", "arbitrary"),
        name="hgrn",
    )(qv, kk, gg, qv, kk, gg, tri, trij, flip, e, bd, md, mo)


def _outproj_kernel(x_ref, mod_ref, ng_ref, attn_ref, cbog_ref, u_ref, up_ref, un_ref, of_ref, ob_ref,
                    hgg_ref, cw_ref, hn_ref, w_ref, o_ref, *, n_tiles):
    t = pl.program_id(1)
    cw = 2 * LANES
    u = u_ref[...]
    row = lax.broadcasted_iota(jnp.int32, (TM, cw), 0)
    prev_row = jnp.where(t >= 2, up_ref[7:8, :], 0.0)
    next_row = jnp.where((t >= 1) & (t < n_tiles - 1), un_ref[0:1, :], 0.0)
    u_prev = jnp.where(row == 0, prev_row, pltpu.roll(u, 1, 0))
    u_next = jnp.where(row == TM - 1, next_row, pltpu.roll(u, TM - 1, 0))
    cbog = cbog_ref[...].astype(F32)
    conv = cbog[:, :cw] * (cw_ref[0:1, :] * u_prev + cw_ref[1:2, :] * u + cw_ref[2:3, :] * u_next)

    o = of_ref[...].astype(F32) + ob_ref[...].astype(F32)
    ms = _dot((o * o).astype(BF16), hn_ref[...])
    og = cbog[:, cw:]
    hg = o * lax.rsqrt(ms + EPS) * hgg_ref[...] * (og * _sigmoid(og))

    y = (_dot(attn_ref[...], w_ref[0:2 * cw, :]) + _dot(conv.astype(BF16), w_ref[2 * cw:3 * cw, :])
         + _dot(hg.astype(BF16), w_ref[3 * cw:4 * cw, :]))
    o_ref[...] = x_ref[...] + mod_ref[2:3, :] * _rms(y, ng_ref[1:2, :])


def _outproj_call(x_all, mod, ng, attn, cbog, u, o_f, o_b, hgg, conv_w, hn, w_out):
    b, t, d = x_all.shape
    nt = t // TM
    cw = 2 * LANES
    nb8 = t // 8
    tok = lambda w: pl.BlockSpec((None, TM, w), lambda i, j: (i, j, 0))
    const = lambda r, w: pl.BlockSpec((r, w), lambda i, j: (0, 0))
    return pl.pallas_call(
        functools.partial(_outproj_kernel, n_tiles=nt),
        grid=(b, nt),
        in_specs=[
            tok(d),
            pl.BlockSpec((None, None, N_MOD, d), lambda i, j: (i, jnp.minimum(j, 1), 0, 0)),
            const(8, d),
            tok(2 * cw),
            tok(2 * cw),
            tok(cw),
            pl.BlockSpec((None, 8, cw), lambda i, j: (i, jnp.maximum(j * (TM // 8) - 1, 0), 0)),
            pl.BlockSpec((None, 8, cw), lambda i, j: (i, jnp.minimum((j + 1) * (TM // 8), nb8 - 1), 0)),
            tok(cw),
            tok(cw),
            const(1, cw),
            const(8, cw),
            const(cw, cw),
            const(d, d),
        ],
        out_specs=tok(d),
        out_shape=jax.ShapeDtypeStruct((b, t, d), F32),
        compiler_params=_cparams("parallel", "parallel"),
        name="outproj",
    )(x_all, mod, ng, attn, cbog, u, u, u, o_f, o_b, hgg, conv_w, hn, w_out)


def _mod_rows(modt_ref, i, rows):
    sub = rows // modt_ref.shape[0]
    return jnp.concatenate(
        [jnp.broadcast_to(modt_ref[s, i:i + 1, :], (sub, modt_ref.shape[2])) for s in range(modt_ref.shape[0])],
        axis=0)


def _store_split(ref, val):
    w = ref.shape[2]
    for c in range(ref.shape[0]):
        ref[c] = val[:, w * c:w * (c + 1)]


def _load_split(ref):
    return jnp.concatenate([ref[c] for c in range(ref.shape[0])], axis=1)


def _split_spec(rows):
    return lambda d: pl.BlockSpec((GATHER_SPLIT, rows, d // GATHER_SPLIT), lambda i, *_: (0, i, 0))


def _ffn_kernel(x_ref, modt_ref, ng_ref, wg_ref, wu_ref, wd_ref, o_ref, *, tf):
    x = x_ref[...]
    rows = x.shape[0]
    h = (_rms(x, ng_ref[2:3, :]) * (1.0 + _mod_rows(modt_ref, 4, rows)) + _mod_rows(modt_ref, 3, rows)).astype(BF16)
    acc = jnp.zeros(x.shape, F32)
    for f0 in range(0, wg_ref.shape[1], tf):
        g = _dot(h, wg_ref[:, f0:f0 + tf])
        up = _dot(h, wu_ref[:, f0:f0 + tf])
        acc = acc + _dot((g * _sigmoid(g) * up).astype(BF16), wd_ref[f0:f0 + tf, :])
    o_ref[...] = x + _mod_rows(modt_ref, 5, rows) * _rms(acc, ng_ref[3:4, :])


def _ffn_call(x_flat, modt, ng, wg, wu, wd):
    n, d = x_flat.shape
    dff = wg.shape[1]
    nsub = FFN_TM // TM
    resident = lambda a: pl.BlockSpec(a.shape, lambda i: (0, 0), pipeline_mode=pl.Buffered(1))
    return pl.pallas_call(
        functools.partial(_ffn_kernel, tf=256),
        grid=(n // FFN_TM,),
        in_specs=[
            pl.BlockSpec((FFN_TM, d), lambda i: (i, 0)),
            pl.BlockSpec((nsub, N_MOD, d), lambda i: (i, 0, 0)),
            pl.BlockSpec((8, d), lambda i: (0, 0)),
            resident(wg), resident(wu), resident(wd),
        ],
        out_specs=pl.BlockSpec((FFN_TM, d), lambda i: (i, 0)),
        out_shape=jax.ShapeDtypeStruct((n, d), F32),
        compiler_params=_cparams("parallel"),
        name="ffn",
    )(x_flat, modt, ng, wg, wu, wd)


def _router_kernel(x_ref, modt_ref, ng_ref, wr_ref, h_ref, r_ref):
    x = x_ref[...]
    rows = x.shape[0]
    h = _rms(x, ng_ref[2:3, :]) * (1.0 + _mod_rows(modt_ref, 4, rows)) + _mod_rows(modt_ref, 3, rows)
    _store_split(h_ref, h)
    h_hi = h.astype(BF16)
    h_lo = (h - h_hi.astype(F32)).astype(BF16)
    logits = _dot(h_hi, wr_ref[0]) + _dot(h_lo, wr_ref[0]) + _dot(h_hi, wr_ref[1])
    lane = lax.broadcasted_iota(jnp.int32, logits.shape, 1)
    lg = jnp.where(lane < N_EXPERTS, logits, NEG_BIG)
    m1 = jnp.max(lg, axis=-1, keepdims=True)
    i1 = jnp.min(jnp.where(lg == m1, lane, LANES), axis=-1, keepdims=True)
    lg2 = jnp.where(lane == i1, NEG_BIG, lg)
    m2 = jnp.max(lg2, axis=-1, keepdims=True)
    i2 = jnp.min(jnp.where(lg2 == m2, lane, LANES), axis=-1, keepdims=True)
    e2 = jnp.exp(m2 - m1)
    w1 = 1.0 / (1.0 + e2)
    w2 = e2 / (1.0 + e2)
    r_ref[...] = jnp.where(lane == 0, i1.astype(F32),
                           jnp.where(lane == 1, i2.astype(F32),
                                     jnp.where(lane == 2, w1, jnp.where(lane == 3, w2, 0.0))))


def _router_call(x_flat, modt, ng, wr):
    n, d = x_flat.shape
    nsub = FFN_TM // TM
    return pl.pallas_call(
        _router_kernel,
        grid=(n // FFN_TM,),
        in_specs=[
            pl.BlockSpec((FFN_TM, d), lambda i: (i, 0)),
            pl.BlockSpec((nsub, N_MOD, d), lambda i: (i, 0, 0)),
            pl.BlockSpec((8, d), lambda i: (0, 0)),
            pl.BlockSpec(wr.shape, lambda i: (0, 0, 0)),
        ],
        out_specs=[_split_spec(FFN_TM)(d), pl.BlockSpec((FFN_TM, LANES), lambda i: (i, 0))],
        out_shape=[jax.ShapeDtypeStruct((GATHER_SPLIT, n, d // GATHER_SPLIT), F32),
                   jax.ShapeDtypeStruct((n, LANES), F32)],
        compiler_params=_cparams("parallel"),
        name="router",
    )(x_flat, modt, ng, wr)


def _gather_flat(data, idx):
    m = idx.shape[0]
    w = data.shape[1]
    mesh = plsc.VectorSubcoreMesh(core_axis_name="core", subcore_axis_name="subcore")

    @functools.partial(pl.kernel, out_type=jax.ShapeDtypeStruct((m, w), data.dtype), mesh=mesh)
    def gather(x_hbm, i_hbm, o_hbm):
        def body(i_vmem, o_vmem):
            pltpu.sync_copy(x_hbm.at[i_vmem.at[0]], o_vmem)

        pltpu.emit_pipeline(
            body,
            grid=(m // GATHER_WIN,),
            in_specs=[pl.BlockSpec((1, GATHER_WIN), lambda i: (0, i))],
            out_specs=[pl.BlockSpec((GATHER_WIN, w), lambda i: (i, 0))],
            core_axis_name=("core", "subcore"),
            dimension_semantics=(pltpu.PARALLEL,),
        )(i_hbm, o_hbm)

    return gather(data, idx.reshape(1, m))


def _scatter_flat(rows, idx, n_out):
    m, w = rows.shape
    mesh = plsc.VectorSubcoreMesh(core_axis_name="core", subcore_axis_name="subcore")

    @functools.partial(pl.kernel, out_type=jax.ShapeDtypeStruct((n_out, w), rows.dtype), mesh=mesh)
    def scatter(x_hbm, i_hbm, o_hbm):
        def body(x_vmem, i_vmem):
            pltpu.sync_copy(x_vmem, o_hbm.at[i_vmem.at[0]])

        pltpu.emit_pipeline(
            body,
            grid=(m // GATHER_WIN,),
            in_specs=[pl.BlockSpec((GATHER_WIN, w), lambda i: (i, 0)),
                      pl.BlockSpec((1, GATHER_WIN), lambda i: (0, i))],
            out_specs=[],
            core_axis_name=("core", "subcore"),
            dimension_semantics=(pltpu.PARALLEL,),
        )(x_hbm, i_hbm)

    return scatter(rows, idx.reshape(1, m))


def _cast_kernel(x_ref, o_ref):
    o_ref[...] = x_ref[...].astype(o_ref.dtype)


def _cast_bf16(w):
    e, r, c = w.shape
    tr = 512
    spec = pl.BlockSpec((None, tr, c), lambda i, j: (i, j, 0))
    return pl.pallas_call(
        _cast_kernel,
        grid=(e, r // tr),
        in_specs=[spec],
        out_specs=spec,
        out_shape=jax.ShapeDtypeStruct(w.shape, BF16),
        compiler_params=_cparams("parallel", "parallel"),
        name="cast",
    )(w)


def _gather_rows(data, idx):
    s, r, w = data.shape
    m = idx.shape[0]
    idx_all = (idx[None, :] + (jnp.arange(s, dtype=jnp.int32) * r)[:, None]).reshape(s * m)
    return _gather_flat(data.reshape(s * r, w), idx_all).reshape(s, m, w)


def _expert_kernel(be_ref, nv_ref, xs_ref, wg_ref, wu_ref, wd_ref, o_ref, acc_ref, *, nf):
    i = pl.program_id(0)
    f = pl.program_id(1)
    valid = i < nv_ref[0]

    @pl.when(valid)
    def _():
        xb = _load_split(xs_ref).astype(BF16)
        g = _dot(xb, wg_ref[...])
        up = _dot(xb, wu_ref[...])
        part = _dot((g * _sigmoid(g) * up).astype(BF16), wd_ref[...])

        @pl.when(f == 0)
        def _():
            acc_ref[...] = part

        @pl.when(f > 0)
        def _():
            acc_ref[...] += part

    @pl.when(f == nf - 1)
    def _():
        _store_split(o_ref, jnp.where(valid, acc_ref[...], 0.0))


def _expert_call(block_expert, n_valid, xs, wg, wu, wd):
    n_slots = xs.shape[1]
    d = wg.shape[1]
    dff = wg.shape[2]
    nf = dff // MOE_TF
    nb = n_slots // MOE_BLK
    grid_spec = pltpu.PrefetchScalarGridSpec(
        num_scalar_prefetch=2,
        grid=(nb, nf),
        in_specs=[
            _split_spec(MOE_BLK)(d),
            pl.BlockSpec((None, d, MOE_TF), lambda i, f, be, nv: (be[i], 0, f)),
            pl.BlockSpec((None, d, MOE_TF), lambda i, f, be, nv: (be[i], 0, f)),
            pl.BlockSpec((None, MOE_TF, d), lambda i, f, be, nv: (be[i], f, 0)),
        ],
        out_specs=_split_spec(MOE_BLK)(d),
        scratch_shapes=[pltpu.VMEM((MOE_BLK, d), F32)],
    )
    return pl.pallas_call(
        functools.partial(_expert_kernel, nf=nf),
        grid_spec=grid_spec,
        out_shape=jax.ShapeDtypeStruct((GATHER_SPLIT, n_slots, d // GATHER_SPLIT), F32),
        compiler_params=_cparams("arbitrary", "arbitrary"),
        name="experts",
    )(block_expert, n_valid, xs, wg, wu, wd)


def _combine_kernel(x_ref, modt_ref, ng_ref, y1_ref, y2_ref, r_ref, o_ref):
    x = x_ref[...]
    rows = x.shape[0]
    f = r_ref[:, 2:3] * _load_split(y1_ref) + r_ref[:, 3:4] * _load_split(y2_ref)
    o_ref[...] = x + _mod_rows(modt_ref, 5, rows) * _rms(f, ng_ref[3:4, :])


def _combine_call(x_flat, modt, ng, y1, y2, rinfo):
    n, d = x_flat.shape
    nsub = FFN_TM // TM
    tok = lambda w: pl.BlockSpec((FFN_TM, w), lambda i: (i, 0))
    return pl.pallas_call(
        _combine_kernel,
        grid=(n // FFN_TM,),
        in_specs=[tok(d), pl.BlockSpec((nsub, N_MOD, d), lambda i: (i, 0, 0)), pl.BlockSpec((8, d), lambda i: (0, 0)),
                  _split_spec(FFN_TM)(d), _split_spec(FFN_TM)(d), tok(LANES)],
        out_specs=tok(d),
        out_shape=jax.ShapeDtypeStruct((n, d), F32),
        compiler_params=_cparams("parallel"),
        name="combine",
    )(x_flat, modt, ng, y1, y2, rinfo)


def _moe_layer(x_flat, modt, ng, w_router, wg, wu, wd):
    n, d = x_flat.shape
    wr = jnp.pad(w_router, ((0, 0), (0, LANES - N_EXPERTS)))
    wr_hi = wr.astype(BF16)
    wr_lo = (wr - wr_hi.astype(F32)).astype(BF16)
    h, rinfo = _router_call(x_flat, modt, ng, jnp.stack([wr_hi, wr_lo]))

    n_assign = 2 * n
    expert = rinfo[:, :2].astype(jnp.int32).reshape(n_assign)
    onehot = (expert[:, None] == jnp.arange(N_EXPERTS, dtype=jnp.int32)[None, :]).astype(jnp.int32)
    csum = jnp.cumsum(onehot, axis=0)
    rank = jnp.sum((csum - onehot) * onehot, axis=1)
    counts = csum[-1]
    padded = (counts + MOE_BLK - 1) // MOE_BLK * MOE_BLK
    pad_end = jnp.cumsum(padded)
    pad_start = pad_end - padded
    dest = jnp.sum(onehot * pad_start[None, :], axis=1) + rank
    nb = -(-n_assign // MOE_BLK) + N_EXPERTS
    row_quantum = 32 * GATHER_WIN // GATHER_SPLIT
    nb = -(-nb * MOE_BLK // row_quantum) * row_quantum // MOE_BLK
    n_slots = nb * MOE_BLK
    block_expert = jnp.minimum(
        jnp.searchsorted(pad_end, jnp.arange(nb, dtype=jnp.int32) * MOE_BLK, side="right"), N_EXPERTS - 1
    ).astype(jnp.int32)
    n_valid = (pad_end[-1:] // MOE_BLK).astype(jnp.int32)
    tok_rows = jnp.broadcast_to((jnp.arange(n_assign, dtype=jnp.int32) // 2)[:, None], (n_assign, LANES))
    slot = jnp.arange(n_slots, dtype=jnp.int32)
    slot_used = (slot - jnp.repeat(pad_start[block_expert], MOE_BLK)) < jnp.repeat(counts[block_expert], MOE_BLK)
    slot_tok = jnp.where(slot_used, _scatter_flat(tok_rows, dest, n_slots)[:, 0], 0)

    xs = _gather_rows(h, slot_tok)
    ys = _expert_call(block_expert, n_valid, xs, wg, wu, wd)
    dest2 = dest.reshape(n, 2)
    y1 = _gather_rows(ys, dest2[:, 0])
    y2 = _gather_rows(ys, dest2[:, 1])
    return _combine_call(x_flat, modt, ng, y1, y2, rinfo)


def _rope_tables(n_ctx, length):
    rows = length // GRID_W
    row = jnp.repeat(jnp.arange(rows, dtype=F32), GRID_W)
    col = jnp.tile(jnp.arange(GRID_W, dtype=F32), rows)
    axis_dim = HEAD_DIM // 2
    inv_freq = ROPE_THETA ** (-jnp.arange(0, axis_dim, 2, dtype=F32) / axis_dim)
    ar = row[:, None] * inv_freq
    ac = col[:, None] * inv_freq
    cos = jnp.concatenate([jnp.cos(ar), jnp.cos(ar), jnp.cos(ac), jnp.cos(ac)], axis=1)
    sin = jnp.concatenate([-jnp.sin(ar), jnp.sin(ar), -jnp.sin(ac), jnp.sin(ac)], axis=1)
    cos = jnp.concatenate([jnp.ones((n_ctx, HEAD_DIM), F32), cos], axis=0)
    sin = jnp.concatenate([jnp.zeros((n_ctx, HEAD_DIM), F32), sin], axis=0)
    return jnp.tile(cos, (1, 2)), jnp.tile(sin, (1, 2))


def _inproj_weights(w_in_l):
    d = w_in_l.shape[0]
    aw = d // 2
    nh = aw // HEAD_DIM
    wq = w_in_l[:, :aw].reshape(d, nh, HEAD_DIM)
    z = jnp.zeros((d, nh // 2, HEAD_DIM), w_in_l.dtype)
    q_pad = jnp.concatenate([
        jnp.concatenate([wq[:, :nh // 2], z], axis=2),
        jnp.concatenate([z, wq[:, nh // 2:]], axis=2),
    ], axis=1).reshape(d, nh * LANES)
    return jnp.concatenate([q_pad, w_in_l[:, aw:]], axis=1).astype(BF16)


def kernel(x, c, ctx, c_ctx, ada_w, ada_b, norm_g, w_in, w_out, attn_q_g, attn_k_g, conv_w, hgrn_lb, hgrn_g,
           ffn_w_gate, ffn_w_up, ffn_w_down, moe_router, moe_w_gate, moe_w_up, moe_w_down):
    bsz, length, d = x.shape
    n_ctx = ctx.shape[1]
    depth = w_in.shape[0]
    t = n_ctx + length
    nt = t // TM
    assert n_ctx == TM and length % TM == 0 and d == 8 * LANES
    for n_tok in (bsz * t, bsz * length):
        assert n_tok % FFN_TM == 0 and (2 * n_tok) % (32 * GATHER_WIN) == 0

    r_pad = -(-(bsz + 1) // 8) * 8
    c_all = jnp.concatenate([c, c_ctx[None, :], jnp.zeros((r_pad - bsz - 1, d), F32)], axis=0)
    mod_all = _ada_call(c_all, ada_w, ada_b).reshape(depth, r_pad, N_MOD, d)

    cos_t, sin_t = _rope_tables(n_ctx, length)
    lb_all = jnp.cumsum(jax.nn.softmax(hgrn_lb.astype(F32), axis=0), axis=0)
    lb_all = lb_all - lb_all[0]
    hconsts = _hgrn_consts()
    cw = 2 * LANES
    cidx = jnp.arange(cw)
    head_mean = ((cidx[:, None] // HEAD_DIM == cidx[None, :] // HEAD_DIM).astype(F32) / HEAD_DIM).astype(BF16)

    x_all = jnp.concatenate([ctx, x], axis=1)
    for l in range(depth):
        mod_lat = mod_all[l, :bsz]
        mod_ctx = jnp.broadcast_to(mod_all[l, bsz][None], (bsz, N_MOD, d))
        mod = jnp.stack([mod_ctx, mod_lat], axis=1)
        modt = jnp.concatenate([mod[:, :1], jnp.broadcast_to(mod[:, 1:], (bsz, nt - 1, N_MOD, d))], axis=1)
        modt = modt.reshape(bsz * nt, N_MOD, d)
        ng = jnp.pad(norm_g[l], ((0, 4), (0, 0)))
        lb = lb_all[l].reshape(1, 2 * cw)
        gc = jnp.concatenate([jnp.log(lb), jnp.log1p(-lb), 1.0 - lb, jnp.zeros((5, 2 * cw), F32)], axis=0)
        qg = jnp.tile(attn_q_g[l], 2)[None, :]
        kg = jnp.tile(attn_k_g[l], 2)[None, :]

        q, k, vt, cbog, u, qv, kk, gg = _inproj_call(x_all, mod, ng, _inproj_weights(w_in[l]), cos_t, sin_t, qg, kg, gc)
        attn = _attn_call(q, k, vt)
        o_f, o_b = _hgrn_call(qv, kk, gg, hconsts)
        x_all = _outproj_call(x_all, mod, ng, attn, cbog, u, o_f, o_b, hgrn_g[l][None, :],
                              jnp.pad(conv_w[l], ((0, 5), (0, 0))), head_mean, w_out[l].astype(BF16))

        last = l == depth - 1
        if last:
            x_flat = x_all[:, n_ctx:, :].reshape(bsz * length, d)
            modt = jnp.broadcast_to(mod[:, 1:], (bsz, length // TM, N_MOD, d)).reshape(bsz * (length // TM), N_MOD, d)
        else:
            x_flat = x_all.reshape(bsz * t, d)
        if l % 2 == 0:
            x_flat = _ffn_call(x_flat, modt, ng, ffn_w_gate[l // 2].astype(BF16), ffn_w_up[l // 2].astype(BF16),
                               ffn_w_down[l // 2].astype(BF16))
        else:
            x_flat = _moe_layer(x_flat, modt, ng, moe_router[l // 2], _cast_bf16(moe_w_gate[l // 2]),
                                _cast_bf16(moe_w_up[l // 2]), _cast_bf16(moe_w_down[l // 2]))
        if last:
            return x_flat.reshape(bsz, length, d)
        x_all = x_flat.reshape(bsz, t, d)
```

```python
import functools

import jax
import jax.numpy as jnp
from jax import lax
from jax.experimental import pallas as pl
from jax.experimental.pallas import tpu as pltpu
from jax.experimental.pallas import tpu_sc as plsc

F32 = jnp.float32
BF16 = jnp.bfloat16

HEAD_DIM = 64
GRID_W = 64
ROPE_THETA = 10000.0
ATTN_SCALE = HEAD_DIM ** -0.5
LOG2_E = 1.4426950408889634
HGRN_CHUNK = 64
HGRN_SUB = 16
VT_ROWS = HEAD_DIM + 16
ATTN_AHEAD = 4
N_EXPERTS = 8
N_MOD = 6
EPS = 1e-6

LANES = 128
TM = 256
FFN_TM = 512
MOE_BLK = 512
MOE_TF = 512
GATHER_WIN = 128
GATHER_SPLIT = 4
NEG_BIG = -1e30
VMEM_LIMIT = 56 * 1024 * 1024


def _cparams(*sem):
    return pltpu.CompilerParams(dimension_semantics=sem, vmem_limit_bytes=VMEM_LIMIT)


def _sigmoid(z):
    return 1.0 / (1.0 + jnp.exp(-z))


def _dot(a, b):
    return jnp.dot(a, b, preferred_element_type=F32)


def _dot_nt(a, b):
    return lax.dot_general(a, b, (((1,), (1,)), ((), ())), preferred_element_type=F32)


def _dot_tn(a, b):
    return lax.dot_general(a, b, (((0,), (0,)), ((), ())), preferred_element_type=F32)


def _rms(x, g):
    return x * lax.rsqrt(jnp.mean(x * x, axis=-1, keepdims=True) + EPS) * g


def _ada_kernel(c_ref, w_ref, b_ref, o_ref):
    c = c_ref[...]
    s = (c * _sigmoid(c)).astype(BF16)
    o_ref[...] = _dot(s, w_ref[...].astype(BF16)) + b_ref[...]


def _ada_call(c_all, ada_w, ada_b):
    depth, d, n = ada_w.shape
    r = c_all.shape[0]
    tn = 512
    return pl.pallas_call(
        _ada_kernel,
        grid=(depth, n // tn),
        in_specs=[
            pl.BlockSpec((r, d), lambda l, j: (0, 0)),
            pl.BlockSpec((None, d, tn), lambda l, j: (l, 0, j)),
            pl.BlockSpec((None, 1, tn), lambda l, j: (l, 0, j)),
        ],
        out_specs=pl.BlockSpec((None, r, tn), lambda l, j: (l, 0, j)),
        out_shape=jax.ShapeDtypeStruct((depth, r, n), F32),
        compiler_params=_cparams("parallel", "parallel"),
        name="ada",
    )(c_all, ada_w, ada_b.reshape(depth, 1, n))


def _inproj_kernel(x_ref, mod_ref, ng_ref, w_ref, cos_ref, sin_ref, qg_ref, kg_ref, gc_ref,
                   q_ref, k_ref, vt_ref, cbog_ref, u_ref, qv_ref, kk_ref, gg_ref):
    x = x_ref[...]
    h = _rms(x, ng_ref[0:1, :]) * (1.0 + mod_ref[1:2, :]) + mod_ref[0:1, :]
    hb = h.astype(BF16)

    def proj(c0, n):
        return _dot(hb, w_ref[:, c0:c0 + n])

    cos = cos_ref[...]
    sin = sin_ref[...]
    lane = lax.broadcasted_iota(jnp.int32, (TM, LANES), 1)
    first_half = (lane % 32) < 16
    lo = lane < HEAD_DIM

    def rope(v):
        partner = jnp.where(first_half, pltpu.roll(v, LANES - 16, 1), pltpu.roll(v, 16, 1))
        return v * cos + partner * sin

    qg = qg_ref[...]
    for hp in range(4):
        q2 = proj(2 * LANES * hp, 2 * LANES)
        for s in range(2):
            qh = q2[:, LANES * s:LANES * (s + 1)]
            r = lax.rsqrt(jnp.sum(qh * qh, axis=-1, keepdims=True) * (1.0 / HEAD_DIM) + EPS)
            q_ref[2 * hp + s] = (rope(qh * r * qg) * (ATTN_SCALE * LOG2_E)).astype(BF16)

    kv = proj(8 * LANES, 2 * LANES)
    kx = kv[:, :LANES]
    v = kv[:, LANES:]
    k2 = kx * kx
    s_lo = jnp.sum(jnp.where(lo, k2, 0.0), axis=-1, keepdims=True)
    s_hi = jnp.sum(jnp.where(lo, 0.0, k2), axis=-1, keepdims=True)
    r = lax.rsqrt(jnp.where(lo, s_lo, s_hi) * (1.0 / HEAD_DIM) + EPS)
    k_ref[...] = rope(kx * r * kg_ref[...]).astype(BF16)
    vt = v.T
    ones = jnp.ones((VT_ROWS - HEAD_DIM, TM), F32)
    vt_ref[0] = jnp.concatenate([vt[:HEAD_DIM], ones], axis=0).astype(BF16)
    vt_ref[1] = jnp.concatenate([vt[HEAD_DIM:], ones], axis=0).astype(BF16)

    c0 = 10 * LANES
    c3 = proj(c0, 6 * LANES)
    cw = 2 * LANES
    u_ref[...] = c3[:, cw:2 * cw] * c3[:, 2 * cw:3 * cw]

    hg = proj(c0 + 6 * LANES, 10 * LANES)
    zf, zb, iv, hq, og = (hg[:, cw * i:cw * (i + 1)] for i in range(5))
    cbog_ref[:, :cw] = c3[:, :cw].astype(BF16)
    cbog_ref[:, cw:] = og.astype(BF16)
    qv_ref[:, :cw] = (hq * _sigmoid(hq)).astype(BF16)
    qv_ref[:, cw:] = iv.astype(BF16)
    for d, z in enumerate((zf, zb)):
        log_lb = gc_ref[0:1, cw * d:cw * (d + 1)]
        log1m_lb = gc_ref[1:2, cw * d:cw * (d + 1)]
        one_m_lb = gc_ref[2:3, cw * d:cw * (d + 1)]
        t = jnp.exp(-jnp.abs(z))
        log_sig = jnp.minimum(z, 0.0) - jnp.log1p(t)
        a2 = log1m_lb + log_sig
        log_f = jnp.maximum(log_lb, a2) + jnp.log1p(jnp.exp(-jnp.abs(log_lb - a2)))
        sig_neg = jnp.where(z >= 0, t, 1.0) / (1.0 + t)
        gg_ref[:, cw * d:cw * (d + 1)] = log_f
        kk_ref[:, cw * d:cw * (d + 1)] = (one_m_lb * sig_neg).astype(BF16)


def _inproj_call(x_all, mod, ng, w1, cos_t, sin_t, qg, kg, gc):
    b, t, d = x_all.shape
    nt = t // TM
    nw = w1.shape[1]
    tok = lambda w: pl.BlockSpec((None, TM, w), lambda i, j: (i, j, 0))
    const = lambda r, w: pl.BlockSpec((r, w), lambda i, j: (0, 0))
    return pl.pallas_call(
        _inproj_kernel,
        grid=(b, nt),
        in_specs=[
            tok(d),
            pl.BlockSpec((None, None, N_MOD, d), lambda i, j: (i, jnp.minimum(j, 1), 0, 0)),
            const(8, d),
            const(d, nw),
            pl.BlockSpec((TM, LANES), lambda i, j: (j, 0)),
            pl.BlockSpec((TM, LANES), lambda i, j: (j, 0)),
            const(1, LANES),
            const(1, LANES),
            const(8, 4 * LANES),
        ],
        out_specs=[
            pl.BlockSpec((None, 8, TM, LANES), lambda i, j: (i, 0, j, 0)),
            tok(LANES),
            pl.BlockSpec((None, 2, None, VT_ROWS, TM), lambda i, j: (i, 0, j, 0, 0)),
            tok(4 * LANES),
            tok(2 * LANES),
            tok(4 * LANES),
            tok(4 * LANES),
            tok(4 * LANES),
        ],
        out_shape=[
            jax.ShapeDtypeStruct((b, 8, t, LANES), BF16),
            jax.ShapeDtypeStruct((b, t, LANES), BF16),
            jax.ShapeDtypeStruct((b, 2, nt, VT_ROWS, TM), BF16),
            jax.ShapeDtypeStruct((b, t, 4 * LANES), BF16),
            jax.ShapeDtypeStruct((b, t, 2 * LANES), F32),
            jax.ShapeDtypeStruct((b, t, 4 * LANES), BF16),
            jax.ShapeDtypeStruct((b, t, 4 * LANES), BF16),
            jax.ShapeDtypeStruct((b, t, 4 * LANES), F32),
        ],
        compiler_params=_cparams("parallel", "parallel"),
        name="inproj",
    )(x_all, mod, ng, w1, cos_t, sin_t, qg, kg, gc)


def _attn_kernel(q_ref, k_ref, vt_ref, o_ref, m_ref, acc_ref, *, n_kv):
    def score(j, h):
        return _dot_nt(k_ref[TM * j:TM * (j + 1), :], q_ref[h])

    def attend(n_tiles):
        steps = [(j, h) for j in range(n_tiles) for h in range(8)]
        scores = [score(j, h) for j, h in steps[:ATTN_AHEAD]]
        for i, (j, h) in enumerate(steps):
            if i + ATTN_AHEAD < len(steps):
                scores.append(score(*steps[i + ATTN_AHEAD]))
            s = scores.pop(0)
            m_tile = jnp.max(s, axis=0, keepdims=True)
            if j == 0:
                m_new = m_tile
                acc_ref[h] = _dot(vt_ref[h // 4, j], jnp.exp2(s - m_new).astype(BF16))
            else:
                m_old = m_ref[h]
                m_new = jnp.maximum(m_old, m_tile)
                pv = _dot(vt_ref[h // 4, j], jnp.exp2(s - m_new).astype(BF16))
                acc_ref[h] = jnp.exp2(m_old - m_new) * acc_ref[h] + pv
            m_ref[h] = m_new
        outs = [acc_ref[h, :HEAD_DIM, :] / acc_ref[h, HEAD_DIM:HEAD_DIM + 1, :] for h in range(8)]
        o_ref[...] = jnp.concatenate(outs, axis=0).T.astype(BF16)

    @pl.when(pl.program_id(1) == 0)
    def _():
        attend(1)

    @pl.when(pl.program_id(1) > 0)
    def _():
        attend(n_kv)


def _attn_call(q, k, vt):
    b, _, t, _ = q.shape
    nt = t // TM
    return pl.pallas_call(
        functools.partial(_attn_kernel, n_kv=nt),
        grid=(b, nt),
        in_specs=[
            pl.BlockSpec((None, 8, TM, LANES), lambda i, j: (i, 0, j, 0)),
            pl.BlockSpec((None, t, LANES), lambda i, j: (i, 0, 0)),
            pl.BlockSpec((None, 2, nt, VT_ROWS, TM), lambda i, j: (i, 0, 0, 0, 0)),
        ],
        out_specs=pl.BlockSpec((None, TM, 4 * LANES), lambda i, j: (i, j, 0)),
        out_shape=jax.ShapeDtypeStruct((b, t, 4 * LANES), BF16),
        scratch_shapes=[pltpu.VMEM((8, 1, TM), F32), pltpu.VMEM((8, VT_ROWS, TM), F32)],
        compiler_params=_cparams("parallel", "parallel"),
        name="attn",
    )(q, k, vt)


def _split3(g):
    hi = g.astype(BF16)
    r1 = g - hi.astype(F32)
    mid = r1.astype(BF16)
    lo = (r1 - mid.astype(F32)).astype(BF16)
    return hi, mid, lo


def _hgrn_kernel(qvf_ref, kf_ref, gf_ref, qvb_ref, kb_ref, gb_ref, tri_ref, trij_ref, j_ref, e_ref, bd_ref,
                 md_ref, mo_ref, of_ref, ob_ref, st_ref, x_s):
    hw = 2 * LANES
    nch = TM // HGRN_CHUNK

    @pl.when(pl.program_id(1) == 0)
    def _():
        st_ref[...] = jnp.zeros(st_ref.shape, F32)

    flip = j_ref[...]

    def cumsum(m_ref, g):
        m = m_ref[...]
        hi, mid, lo = _split3(g)
        return _dot(m, hi) + _dot(m, mid) + _dot(m, lo)

    qvf = qvf_ref[...]
    fl = _dot(flip, jnp.concatenate([qvb_ref[...], kb_ref[...]], axis=1))
    q = jnp.concatenate([qvf[:, :hw].astype(F32), fl[:, :hw]], axis=0)
    v = jnp.concatenate([qvf[:, hw:].astype(F32), fl[:, hw:2 * hw]], axis=0)
    k = jnp.concatenate([kf_ref[...].astype(F32), fl[:, 2 * hw:]], axis=0)
    b = jnp.concatenate([cumsum(tri_ref, gf_ref[...]), cumsum(trij_ref, gb_ref[...])], axis=0) * LOG2_E

    ngr = 2 * TM // HGRN_SUB
    nsub = HGRN_CHUNK // HGRN_SUB
    q3 = q.reshape(ngr, HGRN_SUB, hw)
    b3 = b.reshape(ngr, HGRN_SUB, hw)
    k3 = k.reshape(ngr, HGRN_SUB, hw)
    for s in range(HGRN_SUB):
        x = q3 * jnp.exp2(jnp.minimum(b3 - b3[:, s:s + 1, :], 0.0)) * k3[:, s:s + 1, :]
        x_s[:, hw * s:hw * (s + 1)] = x.reshape(2 * TM, hw).astype(BF16)
    a = _dot(x_s[...], e_ref[...])

    bd = bd_ref[...]
    bd16 = bd.astype(BF16)
    md = md_ref[...]
    mo = mo_ref[...]
    for d in range(2):
        st = st_ref[d]
        outs = []
        for c in range(nch):
            r0 = TM * d + HGRN_CHUNK * c
            bc = b[r0:r0 + HGRN_CHUNK]
            qc = q[r0:r0 + HGRN_CHUNK]
            kc = k[r0:r0 + HGRN_CHUNK]
            vc = v[r0:r0 + HGRN_CHUNK]
            ac = a[r0:r0 + HGRN_CHUNK] * md
            r_end = [bc[HGRN_SUB * (j + 1) - 1:HGRN_SUB * (j + 1), :] for j in range(nsub)]
            r_own = jnp.concatenate([jnp.broadcast_to(r, (HGRN_SUB, hw)) for r in r_end], axis=0)
            kt_bd = jnp.concatenate([(kc * jnp.exp2(r_own - bc)).astype(BF16)] * 4, axis=0) * bd16
            q_rel = jnp.concatenate(
                [qc * jnp.exp2(jnp.minimum(bc - r_end[j], 0.0)) for j in range(nsub - 1)], axis=0).astype(BF16)
            p = _dot_nt(q_rel, kt_bd) * mo
            for j in range(nsub - 1):
                ac = ac + p[HGRN_CHUNK * j:HGRN_CHUNK * (j + 1)]
            bl = r_end[-1]
            qe = (qc * jnp.exp2(bc)).astype(BF16)
            ke = (kc * jnp.exp2(bl - bc)).astype(BF16)
            vcb = vc.astype(BF16)
            vbd = jnp.concatenate([vcb] * 4, axis=0) * bd16
            outs.append(_dot(ac.astype(BF16), vbd) + _dot_nt(qe, st.astype(BF16)))
            st = st * jnp.exp2(bl) + _dot_tn(vcb, ke) * bd
        st_ref[d] = st
        od = jnp.concatenate(outs, axis=0).astype(BF16)
        if d == 0:
            of_ref[...] = od
        else:
            ob_ref[...] = _dot(flip, od).astype(BF16)


def _hgrn_consts():
    hw = 2 * LANES
    r = jnp.arange(TM)
    same_chunk = (r[:, None] // HGRN_CHUNK) == (r[None, :] // HGRN_CHUNK)
    tri = (same_chunk & (r[None, :] <= r[:, None])).astype(F32)
    flip = (r[:, None] + r[None, :] == TM - 1).astype(F32)
    trij = tri @ flip
    c = jnp.arange(hw)
    s = jnp.arange(HGRN_SUB)
    e = ((c[None, :, None] // HEAD_DIM == c[None, None, :] // HGRN_CHUNK)
         & (c[None, None, :] % HGRN_SUB == s[:, None, None])).astype(BF16).reshape(HGRN_SUB * hw, hw)
    bd = (c[:, None] // HEAD_DIM == c[None, :] // HEAD_DIM).astype(F32)
    tt = jnp.arange(HGRN_CHUNK)[:, None]
    ss = (c % HGRN_CHUNK)[None, :]
    md = ((ss // HGRN_SUB == tt // HGRN_SUB) & (ss <= tt)).astype(F32)
    mo = jnp.concatenate([((ss // HGRN_SUB == j) & (tt // HGRN_SUB > j)).astype(F32)
                          for j in range(HGRN_CHUNK // HGRN_SUB - 1)], axis=0)
    return tri.astype(BF16), trij.astype(BF16), flip.astype(BF16), e, bd, md, mo


def _hgrn_call(qv, kk, gg, consts):
    b, t, _ = qv.shape
    nt = t // TM
    hw = 2 * LANES
    tri, trij, flip, e, bd, md, mo = consts
    fwd = lambda i, j: j
    bwd = lambda i, j: jnp.where(j == 0, 0, nt - j)
    const2 = lambda a: pl.BlockSpec(a.shape, lambda i, j: (0,) * a.ndim)
    return pl.pallas_call(
        _hgrn_kernel,
        grid=(b, nt),
        in_specs=[
            pl.BlockSpec((None, TM, 2 * hw), lambda i, j: (i, fwd(i, j), 0)),
            pl.BlockSpec((None, TM, hw), lambda i, j: (i, fwd(i, j), 0)),
            pl.BlockSpec((None, TM, hw), lambda i, j: (i, fwd(i, j), 0)),
            pl.BlockSpec((None, TM, 2 * hw), lambda i, j: (i, bwd(i, j), 0)),
            pl.BlockSpec((None, TM, hw), lambda i, j: (i, bwd(i, j), 1)),
            pl.BlockSpec((None, TM, hw), lambda i, j: (i, bwd(i, j), 1)),
            const2(tri), const2(trij), const2(flip), const2(e), const2(bd), const2(md), const2(mo),
        ],
        out_specs=[
            pl.BlockSpec((None, TM, hw), lambda i, j: (i, fwd(i, j), 0)),
            pl.BlockSpec((None, TM, hw), lambda i, j: (i, bwd(i, j), 0)),
        ],
        out_shape=[jax.ShapeDtypeStruct((b, t, hw), BF16)] * 2,
        scratch_shapes=[
            pltpu.VMEM((2, hw, hw), F32),
            pltpu.VMEM((2 * TM, HGRN_SUB * hw), BF16),
        ],
        compiler_params=_cparams("parallel", "arbitrary"),
        name="hgrn",
    )(qv, kk, gg, qv, kk, gg, tri, trij, flip, e, bd, md, mo)


def _outproj_kernel(x_ref, mod_ref, ng_ref, attn_ref, cbog_ref, u_ref, up_ref, un_ref, of_ref, ob_ref,
                    hgg_ref, cw_ref, hn_ref, w_ref, o_ref, *, n_tiles):
    t = pl.program_id(1)
    cw = 2 * LANES
    u = u_ref[...]
    row = lax.broadcasted_iota(jnp.int32, (TM, cw), 0)
    prev_row = jnp.where(t >= 2, up_ref[7:8, :], 0.0)
    next_row = jnp.where((t >= 1) & (t < n_tiles - 1), un_ref[0:1, :], 0.0)
    u_prev = jnp.where(row == 0, prev_row, pltpu.roll(u, 1, 0))
    u_next = jnp.where(row == TM - 1, next_row, pltpu.roll(u, TM - 1, 0))
    cbog = cbog_ref[...].astype(F32)
    conv = cbog[:, :cw] * (cw_ref[0:1, :] * u_prev + cw_ref[1:2, :] * u + cw_ref[2:3, :] * u_next)

    o = of_ref[...].astype(F32) + ob_ref[...].astype(F32)
    ms = _dot((o * o).astype(BF16), hn_ref[...])
    og = cbog[:, cw:]
    hg = o * lax.rsqrt(ms + EPS) * hgg_ref[...] * (og * _sigmoid(og))

    y = (_dot(attn_ref[...], w_ref[0:2 * cw, :]) + _dot(conv.astype(BF16), w_ref[2 * cw:3 * cw, :])
         + _dot(hg.astype(BF16), w_ref[3 * cw:4 * cw, :]))
    o_ref[...] = x_ref[...] + mod_ref[2:3, :] * _rms(y, ng_ref[1:2, :])


def _outproj_call(x_all, mod, ng, attn, cbog, u, o_f, o_b, hgg, conv_w, hn, w_out):
    b, t, d = x_all.shape
    nt = t // TM
    cw = 2 * LANES
    nb8 = t // 8
    tok = lambda w: pl.BlockSpec((None, TM, w), lambda i, j: (i, j, 0))
    const = lambda r, w: pl.BlockSpec((r, w), lambda i, j: (0, 0))
    return pl.pallas_call(
        functools.partial(_outproj_kernel, n_tiles=nt),
        grid=(b, nt),
        in_specs=[
            tok(d),
            pl.BlockSpec((None, None, N_MOD, d), lambda i, j: (i, jnp.minimum(j, 1), 0, 0)),
            const(8, d),
            tok(2 * cw),
            tok(2 * cw),
            tok(cw),
            pl.BlockSpec((None, 8, cw), lambda i, j: (i, jnp.maximum(j * (TM // 8) - 1, 0), 0)),
            pl.BlockSpec((None, 8, cw), lambda i, j: (i, jnp.minimum((j + 1) * (TM // 8), nb8 - 1), 0)),
            tok(cw),
            tok(cw),
            const(1, cw),
            const(8, cw),
            const(cw, cw),
            const(d, d),
        ],
        out_specs=tok(d),
        out_shape=jax.ShapeDtypeStruct((b, t, d), F32),
        compiler_params=_cparams("parallel", "parallel"),
        name="outproj",
    )(x_all, mod, ng, attn, cbog, u, u, u, o_f, o_b, hgg, conv_w, hn, w_out)


def _mod_rows(modt_ref, i, rows):
    sub = rows // modt_ref.shape[0]
    return jnp.concatenate(
        [jnp.broadcast_to(modt_ref[s, i:i + 1, :], (sub, modt_ref.shape[2])) for s in range(modt_ref.shape[0])],
        axis=0)


def _store_split(ref, val):
    w = ref.shape[2]
    for c in range(ref.shape[0]):
        ref[c] = val[:, w * c:w * (c + 1)]


def _load_split(ref):
    return jnp.concatenate([ref[c] for c in range(ref.shape[0])], axis=1)


def _split_spec(rows):
    return lambda d: pl.BlockSpec((GATHER_SPLIT, rows, d // GATHER_SPLIT), lambda i, *_: (0, i, 0))


def _ffn_kernel(x_ref, modt_ref, ng_ref, wg_ref, wu_ref, wd_ref, o_ref, *, tf):
    x = x_ref[...]
    rows = x.shape[0]
    h = (_rms(x, ng_ref[2:3, :]) * (1.0 + _mod_rows(modt_ref, 4, rows)) + _mod_rows(modt_ref, 3, rows)).astype(BF16)
    acc = jnp.zeros(x.shape, F32)
    for f0 in range(0, wg_ref.shape[1], tf):
        g = _dot(h, wg_ref[:, f0:f0 + tf])
        up = _dot(h, wu_ref[:, f0:f0 + tf])
        acc = acc + _dot((g * _sigmoid(g) * up).astype(BF16), wd_ref[f0:f0 + tf, :])
    o_ref[...] = x + _mod_rows(modt_ref, 5, rows) * _rms(acc, ng_ref[3:4, :])


def _ffn_call(x_flat, modt, ng, wg, wu, wd):
    n, d = x_flat.shape
    dff = wg.shape[1]
    nsub = FFN_TM // TM
    resident = lambda a: pl.BlockSpec(a.shape, lambda i: (0, 0), pipeline_mode=pl.Buffered(1))
    return pl.pallas_call(
        functools.partial(_ffn_kernel, tf=256),
        grid=(n // FFN_TM,),
        in_specs=[
            pl.BlockSpec((FFN_TM, d), lambda i: (i, 0)),
            pl.BlockSpec((nsub, N_MOD, d), lambda i: (i, 0, 0)),
            pl.BlockSpec((8, d), lambda i: (0, 0)),
            resident(wg), resident(wu), resident(wd),
        ],
        out_specs=pl.BlockSpec((FFN_TM, d), lambda i: (i, 0)),
        out_shape=jax.ShapeDtypeStruct((n, d), F32),
        compiler_params=_cparams("parallel"),
        name="ffn",
    )(x_flat, modt, ng, wg, wu, wd)


def _router_kernel(x_ref, modt_ref, ng_ref, wr_ref, h_ref, r_ref):
    x = x_ref[...]
    rows = x.shape[0]
    h = _rms(x, ng_ref[2:3, :]) * (1.0 + _mod_rows(modt_ref, 4, rows)) + _mod_rows(modt_ref, 3, rows)
    _store_split(h_ref, h)
    h_hi = h.astype(BF16)
    h_lo = (h - h_hi.astype(F32)).astype(BF16)
    logits = _dot(h_hi, wr_ref[0]) + _dot(h_lo, wr_ref[0]) + _dot(h_hi, wr_ref[1])
    lane = lax.broadcasted_iota(jnp.int32, logits.shape, 1)
    lg = jnp.where(lane < N_EXPERTS, logits, NEG_BIG)
    m1 = jnp.max(lg, axis=-1, keepdims=True)
    i1 = jnp.min(jnp.where(lg == m1, lane, LANES), axis=-1, keepdims=True)
    lg2 = jnp.where(lane == i1, NEG_BIG, lg)
    m2 = jnp.max(lg2, axis=-1, keepdims=True)
    i2 = jnp.min(jnp.where(lg2 == m2, lane, LANES), axis=-1, keepdims=True)
    e2 = jnp.exp(m2 - m1)
    w1 = 1.0 / (1.0 + e2)
    w2 = e2 / (1.0 + e2)
    r_ref[...] = jnp.where(lane == 0, i1.astype(F32),
                           jnp.where(lane == 1, i2.astype(F32),
                                     jnp.where(lane == 2, w1, jnp.where(lane == 3, w2, 0.0))))


def _router_call(x_flat, modt, ng, wr):
    n, d = x_flat.shape
    nsub = FFN_TM // TM
    return pl.pallas_call(
        _router_kernel,
        grid=(n // FFN_TM,),
        in_specs=[
            pl.BlockSpec((FFN_TM, d), lambda i: (i, 0)),
            pl.BlockSpec((nsub, N_MOD, d), lambda i: (i, 0, 0)),
            pl.BlockSpec((8, d), lambda i: (0, 0)),
            pl.BlockSpec(wr.shape, lambda i: (0, 0, 0)),
        ],
        out_specs=[_split_spec(FFN_TM)(d), pl.BlockSpec((FFN_TM, LANES), lambda i: (i, 0))],
        out_shape=[jax.ShapeDtypeStruct((GATHER_SPLIT, n, d // GATHER_SPLIT), F32),
                   jax.ShapeDtypeStruct((n, LANES), F32)],
        compiler_params=_cparams("parallel"),
        name="router",
    )(x_flat, modt, ng, wr)


def _gather_flat(data, idx):
    m = idx.shape[0]
    w = data.shape[1]
    mesh = plsc.VectorSubcoreMesh(core_axis_name="core", subcore_axis_name="subcore")

    @functools.partial(pl.kernel, out_type=jax.ShapeDtypeStruct((m, w), data.dtype), mesh=mesh)
    def gather(x_hbm, i_hbm, o_hbm):
        def body(i_vmem, o_vmem):
            pltpu.sync_copy(x_hbm.at[i_vmem.at[0]], o_vmem)

        pltpu.emit_pipeline(
            body,
            grid=(m // GATHER_WIN,),
            in_specs=[pl.BlockSpec((1, GATHER_WIN), lambda i: (0, i))],
            out_specs=[pl.BlockSpec((GATHER_WIN, w), lambda i: (i, 0))],
            core_axis_name=("core", "subcore"),
            dimension_semantics=(pltpu.PARALLEL,),
        )(i_hbm, o_hbm)

    return gather(data, idx.reshape(1, m))


def _scatter_flat(rows, idx, n_out):
    m, w = rows.shape
    mesh = plsc.VectorSubcoreMesh(core_axis_name="core", subcore_axis_name="subcore")

    @functools.partial(pl.kernel, out_type=jax.ShapeDtypeStruct((n_out, w), rows.dtype), mesh=mesh)
    def scatter(x_hbm, i_hbm, o_hbm):
        def body(x_vmem, i_vmem):
            pltpu.sync_copy(x_vmem, o_hbm.at[i_vmem.at[0]])

        pltpu.emit_pipeline(
            body,
            grid=(m // GATHER_WIN,),
            in_specs=[pl.BlockSpec((GATHER_WIN, w), lambda i: (i, 0)),
                      pl.BlockSpec((1, GATHER_WIN), lambda i: (0, i))],
            out_specs=[],
            core_axis_name=("core", "subcore"),
            dimension_semantics=(pltpu.PARALLEL,),
        )(x_hbm, i_hbm)

    return scatter(rows, idx.reshape(1, m))


def _col0_kernel(x_ref, o_ref):
    o_ref[...] = x_ref[...].T[0:1, :]


def _first_column(rows):
    n = rows.shape[0]
    blk = 1024
    return pl.pallas_call(
        _col0_kernel,
        grid=(n // blk,),
        in_specs=[pl.BlockSpec((blk, LANES), lambda i: (i, 0))],
        out_specs=pl.BlockSpec((1, blk), lambda i: (0, i)),
        out_shape=jax.ShapeDtypeStruct((1, n), rows.dtype),
        compiler_params=_cparams("parallel"),
        name="col0",
    )(rows)


def _cast_kernel(x_ref, o_ref):
    o_ref[...] = x_ref[...].astype(o_ref.dtype)


def _cast_bf16(w):
    e, r, c = w.shape
    tr = 256
    spec = pl.BlockSpec((None, tr, c), lambda i, j: (i, j, 0))
    return pl.pallas_call(
        _cast_kernel,
        grid=(e, r // tr),
        in_specs=[spec],
        out_specs=spec,
        out_shape=jax.ShapeDtypeStruct(w.shape, BF16),
        compiler_params=_cparams("parallel", "parallel"),
        name="cast",
    )(w)


def _gather_rows(data, idx):
    s, r, w = data.shape
    m = idx.shape[0]
    idx_all = (idx[None, :] + (jnp.arange(s, dtype=jnp.int32) * r)[:, None]).reshape(s * m)
    return _gather_flat(data.reshape(s * r, w), idx_all).reshape(s, m, w)


def _expert_kernel(be_ref, nv_ref, xs_ref, wg_ref, wu_ref, wd_ref, o_ref):
    valid = pl.program_id(0) < nv_ref[0]

    @pl.when(valid)
    def _():
        xb = _load_split(xs_ref).astype(BF16)
        acc = jnp.zeros((xb.shape[0], wd_ref.shape[1]), F32)
        for f0 in range(0, wg_ref.shape[1], MOE_TF):
            g = _dot(xb, wg_ref[:, f0:f0 + MOE_TF])
            up = _dot(xb, wu_ref[:, f0:f0 + MOE_TF])
            acc = acc + _dot((g * _sigmoid(g) * up).astype(BF16), wd_ref[f0:f0 + MOE_TF, :])
        _store_split(o_ref, acc)

    @pl.when(jnp.logical_not(valid))
    def _():
        o_ref[...] = jnp.zeros(o_ref.shape, F32)


def _expert_call(block_expert, n_valid, xs, wg, wu, wd):
    n_slots = xs.shape[1]
    d = wg.shape[1]
    dff = wg.shape[2]
    nb = n_slots // MOE_BLK
    resident = lambda r, c: pl.BlockSpec((None, r, c), lambda i, be, nv: (be[i], 0, 0), pipeline_mode=pl.Buffered(1))
    grid_spec = pltpu.PrefetchScalarGridSpec(
        num_scalar_prefetch=2,
        grid=(nb,),
        in_specs=[_split_spec(MOE_BLK)(d), resident(d, dff), resident(d, dff), resident(dff, d)],
        out_specs=_split_spec(MOE_BLK)(d),
    )
    return pl.pallas_call(
        _expert_kernel,
        grid_spec=grid_spec,
        out_shape=jax.ShapeDtypeStruct((GATHER_SPLIT, n_slots, d // GATHER_SPLIT), F32),
        compiler_params=_cparams("arbitrary"),
        name="experts",
    )(block_expert, n_valid, xs, wg, wu, wd)


def _combine_kernel(x_ref, modt_ref, ng_ref, y1_ref, y2_ref, r_ref, o_ref):
    x = x_ref[...]
    rows = x.shape[0]
    f = r_ref[:, 2:3] * _load_split(y1_ref) + r_ref[:, 3:4] * _load_split(y2_ref)
    o_ref[...] = x + _mod_rows(modt_ref, 5, rows) * _rms(f, ng_ref[3:4, :])


def _combine_call(x_flat, modt, ng, y1, y2, rinfo):
    n, d = x_flat.shape
    nsub = FFN_TM // TM
    tok = lambda w: pl.BlockSpec((FFN_TM, w), lambda i: (i, 0))
    return pl.pallas_call(
        _combine_kernel,
        grid=(n // FFN_TM,),
        in_specs=[tok(d), pl.BlockSpec((nsub, N_MOD, d), lambda i: (i, 0, 0)), pl.BlockSpec((8, d), lambda i: (0, 0)),
                  _split_spec(FFN_TM)(d), _split_spec(FFN_TM)(d), tok(LANES)],
        out_specs=tok(d),
        out_shape=jax.ShapeDtypeStruct((n, d), F32),
        compiler_params=_cparams("parallel"),
        name="combine",
    )(x_flat, modt, ng, y1, y2, rinfo)


def _moe_layer(x_flat, modt, ng, w_router, wg, wu, wd):
    n, d = x_flat.shape
    wr = jnp.pad(w_router, ((0, 0), (0, LANES - N_EXPERTS)))
    wr_hi = wr.astype(BF16)
    wr_lo = (wr - wr_hi.astype(F32)).astype(BF16)
    h, rinfo = _router_call(x_flat, modt, ng, jnp.stack([wr_hi, wr_lo]))

    n_assign = 2 * n
    expert = rinfo[:, :2].astype(jnp.int32).reshape(n_assign)
    onehot = (expert[:, None] == jnp.arange(N_EXPERTS, dtype=jnp.int32)[None, :]).astype(jnp.int32)
    csum = jnp.cumsum(onehot, axis=0)
    rank = jnp.sum((csum - onehot) * onehot, axis=1)
    counts = csum[-1]
    padded = (counts + MOE_BLK - 1) // MOE_BLK * MOE_BLK
    pad_end = jnp.cumsum(padded)
    pad_start = pad_end - padded
    dest = jnp.sum(onehot * pad_start[None, :], axis=1) + rank
    nb = -(-n_assign // MOE_BLK) + N_EXPERTS
    row_quantum = 32 * GATHER_WIN // GATHER_SPLIT
    nb = -(-nb * MOE_BLK // row_quantum) * row_quantum // MOE_BLK
    n_slots = nb * MOE_BLK
    block_expert = jnp.minimum(
        jnp.searchsorted(pad_end, jnp.arange(nb, dtype=jnp.int32) * MOE_BLK, side="right"), N_EXPERTS - 1
    ).astype(jnp.int32)
    n_valid = (pad_end[-1:] // MOE_BLK).astype(jnp.int32)
    tok_rows = jnp.broadcast_to((jnp.arange(n_assign, dtype=jnp.int32) // 2)[:, None], (n_assign, LANES))
    slot = jnp.arange(n_slots, dtype=jnp.int32)
    slot_used = (slot - jnp.repeat(pad_start[block_expert], MOE_BLK)) < jnp.repeat(counts[block_expert], MOE_BLK)
    slot_tok = jnp.where(slot_used, _first_column(_scatter_flat(tok_rows, dest, n_slots)).reshape(n_slots), 0)

    xs = _gather_rows(h, slot_tok)
    ys = _expert_call(block_expert, n_valid, xs, wg, wu, wd)
    dest2 = dest.reshape(n, 2)
    y1 = _gather_rows(ys, dest2[:, 0])
    y2 = _gather_rows(ys, dest2[:, 1])
    return _combine_call(x_flat, modt, ng, y1, y2, rinfo)


def _rope_tables(n_ctx, length):
    rows = length // GRID_W
    row = jnp.repeat(jnp.arange(rows, dtype=F32), GRID_W)
    col = jnp.tile(jnp.arange(GRID_W, dtype=F32), rows)
    axis_dim = HEAD_DIM // 2
    inv_freq = ROPE_THETA ** (-jnp.arange(0, axis_dim, 2, dtype=F32) / axis_dim)
    ar = row[:, None] * inv_freq
    ac = col[:, None] * inv_freq
    cos = jnp.concatenate([jnp.cos(ar), jnp.cos(ar), jnp.cos(ac), jnp.cos(ac)], axis=1)
    sin = jnp.concatenate([-jnp.sin(ar), jnp.sin(ar), -jnp.sin(ac), jnp.sin(ac)], axis=1)
    cos = jnp.concatenate([jnp.ones((n_ctx, HEAD_DIM), F32), cos], axis=0)
    sin = jnp.concatenate([jnp.zeros((n_ctx, HEAD_DIM), F32), sin], axis=0)
    return jnp.tile(cos, (1, 2)), jnp.tile(sin, (1, 2))


def _inproj_weights(w_in_l):
    d = w_in_l.shape[0]
    aw = d // 2
    nh = aw // HEAD_DIM
    wq = w_in_l[:, :aw].reshape(d, nh, HEAD_DIM)
    z = jnp.zeros((d, nh // 2, HEAD_DIM), w_in_l.dtype)
    q_pad = jnp.concatenate([
        jnp.concatenate([wq[:, :nh // 2], z], axis=2),
        jnp.concatenate([z, wq[:, nh // 2:]], axis=2),
    ], axis=1).reshape(d, nh * LANES)
    return jnp.concatenate([q_pad, w_in_l[:, aw:]], axis=1).astype(BF16)


def kernel(x, c, ctx, c_ctx, ada_w, ada_b, norm_g, w_in, w_out, attn_q_g, attn_k_g, conv_w, hgrn_lb, hgrn_g,
           ffn_w_gate, ffn_w_up, ffn_w_down, moe_router, moe_w_gate, moe_w_up, moe_w_down):
    bsz, length, d = x.shape
    n_ctx = ctx.shape[1]
    depth = w_in.shape[0]
    t = n_ctx + length
    nt = t // TM
    assert n_ctx == TM and length % TM == 0 and d == 8 * LANES
    for n_tok in (bsz * t, bsz * length):
        assert n_tok % FFN_TM == 0 and (2 * n_tok) % (32 * GATHER_WIN) == 0

    r_pad = -(-(bsz + 1) // 8) * 8
    c_all = jnp.concatenate([c, c_ctx[None, :], jnp.zeros((r_pad - bsz - 1, d), F32)], axis=0)
    mod_all = _ada_call(c_all, ada_w, ada_b).reshape(depth, r_pad, N_MOD, d)

    cos_t, sin_t = _rope_tables(n_ctx, length)
    lb_all = jnp.cumsum(jax.nn.softmax(hgrn_lb.astype(F32), axis=0), axis=0)
    lb_all = lb_all - lb_all[0]
    hconsts = _hgrn_consts()
    cw = 2 * LANES
    cidx = jnp.arange(cw)
    head_mean = ((cidx[:, None] // HEAD_DIM == cidx[None, :] // HEAD_DIM).astype(F32) / HEAD_DIM).astype(BF16)

    x_all = jnp.concatenate([ctx, x], axis=1)
    for l in range(depth):
        mod_lat = mod_all[l, :bsz]
        mod_ctx = jnp.broadcast_to(mod_all[l, bsz][None], (bsz, N_MOD, d))
        mod = jnp.stack([mod_ctx, mod_lat], axis=1)
        modt = jnp.concatenate([mod[:, :1], jnp.broadcast_to(mod[:, 1:], (bsz, nt - 1, N_MOD, d))], axis=1)
        modt = modt.reshape(bsz * nt, N_MOD, d)
        ng = jnp.pad(norm_g[l], ((0, 4), (0, 0)))
        lb = lb_all[l].reshape(1, 2 * cw)
        gc = jnp.concatenate([jnp.log(lb), jnp.log1p(-lb), 1.0 - lb, jnp.zeros((5, 2 * cw), F32)], axis=0)
        qg = jnp.tile(attn_q_g[l], 2)[None, :]
        kg = jnp.tile(attn_k_g[l], 2)[None, :]

        q, k, vt, cbog, u, qv, kk, gg = _inproj_call(x_all, mod, ng, _inproj_weights(w_in[l]), cos_t, sin_t, qg, kg, gc)
        attn = _attn_call(q, k, vt)
        o_f, o_b = _hgrn_call(qv, kk, gg, hconsts)
        x_all = _outproj_call(x_all, mod, ng, attn, cbog, u, o_f, o_b, hgrn_g[l][None, :],
                              jnp.pad(conv_w[l], ((0, 5), (0, 0))), head_mean, w_out[l].astype(BF16))

        last = l == depth - 1
        if last:
            x_flat = x_all[:, n_ctx:, :].reshape(bsz * length, d)
            modt = jnp.broadcast_to(mod[:, 1:], (bsz, length // TM, N_MOD, d)).reshape(bsz * (length // TM), N_MOD, d)
        else:
            x_flat = x_all.reshape(bsz * t, d)
        if l % 2 == 0:
            x_flat = _ffn_call(x_flat, modt, ng, ffn_w_gate[l // 2].astype(BF16), ffn_w_up[l // 2].astype(BF16),
                               ffn_w_down[l // 2].astype(BF16))
        else:
            x_flat = _moe_layer(x_flat, modt, ng, moe_router[l // 2], _cast_bf16(moe_w_gate[l // 2]),
                                _cast_bf16(moe_w_up[l // 2]), _cast_bf16(moe_w_down[l // 2]))
        if last:
            return x_flat.reshape(bsz, length, d)
        x_all = x_flat.reshape(bsz, t, d)
```

```python
import functools

import jax
import jax.numpy as jnp
from jax import lax
from jax.experimental import pallas as pl
from jax.experimental.pallas import tpu as pltpu
from jax.experimental.pallas import tpu_sc as plsc

F32 = jnp.float32
BF16 = jnp.bfloat16

HEAD_DIM = 64
GRID_W = 64
ROPE_THETA = 10000.0
ATTN_SCALE = HEAD_DIM ** -0.5
LOG2_E = 1.4426950408889634
HGRN_CHUNK = 64
HGRN_SUB = 16
VT_ROWS = HEAD_DIM + 16
ATTN_AHEAD = 4
N_EXPERTS = 8
N_MOD = 6
EPS = 1e-6

LANES = 128
TM = 256
FFN_TM = 512
MOE_BLK = 512
MOE_TF = 512
GATHER_WIN = 128
GATHER_SPLIT = 4
NEG_BIG = -1e30
VMEM_LIMIT = 56 * 1024 * 1024


def _cparams(*sem):
    return pltpu.CompilerParams(dimension_semantics=sem, vmem_limit_bytes=VMEM_LIMIT)


def _sigmoid(z):
    return 1.0 / (1.0 + jnp.exp(-z))


def _dot(a, b):
    return jnp.dot(a, b, preferred_element_type=F32)


def _dot_nt(a, b):
    return lax.dot_general(a, b, (((1,), (1,)), ((), ())), preferred_element_type=F32)


def _dot_tn(a, b):
    return lax.dot_general(a, b, (((0,), (0,)), ((), ())), preferred_element_type=F32)


def _rms(x, g):
    return x * lax.rsqrt(jnp.mean(x * x, axis=-1, keepdims=True) + EPS) * g


def _ada_kernel(c_ref, w_ref, b_ref, o_ref):
    c = c_ref[...]
    s = (c * _sigmoid(c)).astype(BF16)
    o_ref[...] = _dot(s, w_ref[...].astype(BF16)) + b_ref[...]


def _ada_call(c_all, ada_w, ada_b):
    depth, d, n = ada_w.shape
    r = c_all.shape[0]
    tn = 512
    return pl.pallas_call(
        _ada_kernel,
        grid=(depth, n // tn),
        in_specs=[
            pl.BlockSpec((r, d), lambda l, j: (0, 0)),
            pl.BlockSpec((None, d, tn), lambda l, j: (l, 0, j)),
            pl.BlockSpec((None, 1, tn), lambda l, j: (l, 0, j)),
        ],
        out_specs=pl.BlockSpec((None, r, tn), lambda l, j: (l, 0, j)),
        out_shape=jax.ShapeDtypeStruct((depth, r, n), F32),
        compiler_params=_cparams("parallel", "parallel"),
        name="ada",
    )(c_all, ada_w, ada_b.reshape(depth, 1, n))


def _inproj_kernel(x_ref, mod_ref, ng_ref, w_ref, cos_ref, sin_ref, qg_ref, kg_ref, gc_ref,
                   q_ref, k_ref, vt_ref, cbog_ref, u_ref, qv_ref, kk_ref, gg_ref):
    x = x_ref[...]
    h = _rms(x, ng_ref[0:1, :]) * (1.0 + mod_ref[1:2, :]) + mod_ref[0:1, :]
    hb = h.astype(BF16)

    def proj(c0, n):
        return _dot(hb, w_ref[:, c0:c0 + n])

    cos = cos_ref[...]
    sin = sin_ref[...]
    lane = lax.broadcasted_iota(jnp.int32, (TM, LANES), 1)
    first_half = (lane % 32) < 16
    lo = lane < HEAD_DIM

    def rope(v):
        partner = jnp.where(first_half, pltpu.roll(v, LANES - 16, 1), pltpu.roll(v, 16, 1))
        return v * cos + partner * sin

    qg = qg_ref[...]
    for hp in range(4):
        q2 = proj(2 * LANES * hp, 2 * LANES)
        for s in range(2):
            qh = q2[:, LANES * s:LANES * (s + 1)]
            r = lax.rsqrt(jnp.sum(qh * qh, axis=-1, keepdims=True) * (1.0 / HEAD_DIM) + EPS)
            q_ref[2 * hp + s] = (rope(qh * r * qg) * (ATTN_SCALE * LOG2_E)).astype(BF16)

    kv = proj(8 * LANES, 2 * LANES)
    kx = kv[:, :LANES]
    v = kv[:, LANES:]
    k2 = kx * kx
    s_lo = jnp.sum(jnp.where(lo, k2, 0.0), axis=-1, keepdims=True)
    s_hi = jnp.sum(jnp.where(lo, 0.0, k2), axis=-1, keepdims=True)
    r = lax.rsqrt(jnp.where(lo, s_lo, s_hi) * (1.0 / HEAD_DIM) + EPS)
    k_ref[...] = rope(kx * r * kg_ref[...]).astype(BF16)
    vt = v.T
    ones = jnp.ones((VT_ROWS - HEAD_DIM, TM), F32)
    vt_ref[0] = jnp.concatenate([vt[:HEAD_DIM], ones], axis=0).astype(BF16)
    vt_ref[1] = jnp.concatenate([vt[HEAD_DIM:], ones], axis=0).astype(BF16)

    c0 = 10 * LANES
    c3 = proj(c0, 6 * LANES)
    cw = 2 * LANES
    u_ref[...] = c3[:, cw:2 * cw] * c3[:, 2 * cw:3 * cw]

    hg = proj(c0 + 6 * LANES, 10 * LANES)
    zf, zb, iv, hq, og = (hg[:, cw * i:cw * (i + 1)] for i in range(5))
    cbog_ref[:, :cw] = c3[:, :cw].astype(BF16)
    cbog_ref[:, cw:] = og.astype(BF16)
    qv_ref[:, :cw] = (hq * _sigmoid(hq)).astype(BF16)
    qv_ref[:, cw:] = iv.astype(BF16)
    for d, z in enumerate((zf, zb)):
        log_lb = gc_ref[0:1, cw * d:cw * (d + 1)]
        log1m_lb = gc_ref[1:2, cw * d:cw * (d + 1)]
        one_m_lb = gc_ref[2:3, cw * d:cw * (d + 1)]
        t = jnp.exp(-jnp.abs(z))
        log_sig = jnp.minimum(z, 0.0) - jnp.log1p(t)
        a2 = log1m_lb + log_sig
        log_f = jnp.maximum(log_lb, a2) + jnp.log1p(jnp.exp(-jnp.abs(log_lb - a2)))
        sig_neg = jnp.where(z >= 0, t, 1.0) / (1.0 + t)
        gg_ref[:, cw * d:cw * (d + 1)] = log_f
        kk_ref[:, cw * d:cw * (d + 1)] = (one_m_lb * sig_neg).astype(BF16)


def _inproj_call(x_all, mod, ng, w1, cos_t, sin_t, qg, kg, gc):
    b, t, d = x_all.shape
    nt = t // TM
    nw = w1.shape[1]
    tok = lambda w: pl.BlockSpec((None, TM, w), lambda i, j: (i, j, 0))
    const = lambda r, w: pl.BlockSpec((r, w), lambda i, j: (0, 0))
    return pl.pallas_call(
        _inproj_kernel,
        grid=(b, nt),
        in_specs=[
            tok(d),
            pl.BlockSpec((None, None, N_MOD, d), lambda i, j: (i, jnp.minimum(j, 1), 0, 0)),
            const(8, d),
            const(d, nw),
            pl.BlockSpec((TM, LANES), lambda i, j: (j, 0)),
            pl.BlockSpec((TM, LANES), lambda i, j: (j, 0)),
            const(1, LANES),
            const(1, LANES),
            const(8, 4 * LANES),
        ],
        out_specs=[
            pl.BlockSpec((None, 8, TM, LANES), lambda i, j: (i, 0, j, 0)),
            tok(LANES),
            pl.BlockSpec((None, 2, None, VT_ROWS, TM), lambda i, j: (i, 0, j, 0, 0)),
            tok(4 * LANES),
            tok(2 * LANES),
            tok(4 * LANES),
            tok(4 * LANES),
            tok(4 * LANES),
        ],
        out_shape=[
            jax.ShapeDtypeStruct((b, 8, t, LANES), BF16),
            jax.ShapeDtypeStruct((b, t, LANES), BF16),
            jax.ShapeDtypeStruct((b, 2, nt, VT_ROWS, TM), BF16),
            jax.ShapeDtypeStruct((b, t, 4 * LANES), BF16),
            jax.ShapeDtypeStruct((b, t, 2 * LANES), F32),
            jax.ShapeDtypeStruct((b, t, 4 * LANES), BF16),
            jax.ShapeDtypeStruct((b, t, 4 * LANES), BF16),
            jax.ShapeDtypeStruct((b, t, 4 * LANES), F32),
        ],
        compiler_params=_cparams("parallel", "parallel"),
        name="inproj",
    )(x_all, mod, ng, w1, cos_t, sin_t, qg, kg, gc)


def _attn_kernel(q_ref, k_ref, vt_ref, o_ref, m_ref, acc_ref, *, n_kv):
    def score(j, h):
        return _dot_nt(k_ref[TM * j:TM * (j + 1), :], q_ref[h])

    def attend(n_tiles):
        steps = [(j, h) for j in range(n_tiles) for h in range(8)]
        scores = [score(j, h) for j, h in steps[:ATTN_AHEAD]]
        for i, (j, h) in enumerate(steps):
            if i + ATTN_AHEAD < len(steps):
                scores.append(score(*steps[i + ATTN_AHEAD]))
            s = scores.pop(0)
            m_tile = jnp.max(s, axis=0, keepdims=True)
            if j == 0:
                m_new = m_tile
                acc_ref[h] = _dot(vt_ref[h // 4, j], jnp.exp2(s - m_new).astype(BF16))
            else:
                m_old = m_ref[h]
                m_new = jnp.maximum(m_old, m_tile)
                pv = _dot(vt_ref[h // 4, j], jnp.exp2(s - m_new).astype(BF16))
                acc_ref[h] = jnp.exp2(m_old - m_new) * acc_ref[h] + pv
            m_ref[h] = m_new
        outs = [acc_ref[h, :HEAD_DIM, :] / acc_ref[h, HEAD_DIM:HEAD_DIM + 1, :] for h in range(8)]
        o_ref[...] = jnp.concatenate(outs, axis=0).T.astype(BF16)

    @pl.when(pl.program_id(1) == 0)
    def _():
        attend(1)

    @pl.when(pl.program_id(1) > 0)
    def _():
        attend(n_kv)


def _attn_call(q, k, vt):
    b, _, t, _ = q.shape
    nt = t // TM
    return pl.pallas_call(
        functools.partial(_attn_kernel, n_kv=nt),
        grid=(b, nt),
        in_specs=[
            pl.BlockSpec((None, 8, TM, LANES), lambda i, j: (i, 0, j, 0)),
            pl.BlockSpec((None, t, LANES), lambda i, j: (i, 0, 0)),
            pl.BlockSpec((None, 2, nt, VT_ROWS, TM), lambda i, j: (i, 0, 0, 0, 0)),
        ],
        out_specs=pl.BlockSpec((None, TM, 4 * LANES), lambda i, j: (i, j, 0)),
        out_shape=jax.ShapeDtypeStruct((b, t, 4 * LANES), BF16),
        scratch_shapes=[pltpu.VMEM((8, 1, TM), F32), pltpu.VMEM((8, VT_ROWS, TM), F32)],
        compiler_params=_cparams("parallel", "parallel"),
        name="attn",
    )(q, k, vt)


def _split3(g):
    hi = g.astype(BF16)
    r1 = g - hi.astype(F32)
    mid = r1.astype(BF16)
    lo = (r1 - mid.astype(F32)).astype(BF16)
    return hi, mid, lo


def _hgrn_kernel(qvf_ref, kf_ref, gf_ref, qvb_ref, kb_ref, gb_ref, tri_ref, trij_ref, j_ref, e_ref, bd_ref,
                 md_ref, mo_ref, of_ref, ob_ref, st_ref, x_s):
    hw = 2 * LANES
    nch = TM // HGRN_CHUNK

    @pl.when(pl.program_id(1) == 0)
    def _():
        st_ref[...] = jnp.zeros(st_ref.shape, F32)

    flip = j_ref[...]

    def cumsum(m_ref, g):
        m = m_ref[...]
        hi, mid, lo = _split3(g)
        return _dot(m, hi) + _dot(m, mid) + _dot(m, lo)

    qvf = qvf_ref[...]
    fl = _dot(flip, jnp.concatenate([qvb_ref[...], kb_ref[...]], axis=1))
    q = jnp.concatenate([qvf[:, :hw].astype(F32), fl[:, :hw]], axis=0)
    v = jnp.concatenate([qvf[:, hw:].astype(F32), fl[:, hw:2 * hw]], axis=0)
    k = jnp.concatenate([kf_ref[...].astype(F32), fl[:, 2 * hw:]], axis=0)
    b = jnp.concatenate([cumsum(tri_ref, gf_ref[...]), cumsum(trij_ref, gb_ref[...])], axis=0) * LOG2_E

    ngr = 2 * TM // HGRN_SUB
    nsub = HGRN_CHUNK // HGRN_SUB
    q3 = q.reshape(ngr, HGRN_SUB, hw)
    b3 = b.reshape(ngr, HGRN_SUB, hw)
    k3 = k.reshape(ngr, HGRN_SUB, hw)
    for s in range(HGRN_SUB):
        x = q3 * jnp.exp2(jnp.minimum(b3 - b3[:, s:s + 1, :], 0.0)) * k3[:, s:s + 1, :]
        x_s[:, hw * s:hw * (s + 1)] = x.reshape(2 * TM, hw).astype(BF16)
    a = _dot(x_s[...], e_ref[...])

    bd = bd_ref[...]
    bd16 = bd.astype(BF16)
    md = md_ref[...]
    mo = mo_ref[...]
    for d in range(2):
        st = st_ref[d]
        outs = []
        for c in range(nch):
            r0 = TM * d + HGRN_CHUNK * c
            bc = b[r0:r0 + HGRN_CHUNK]
            qc = q[r0:r0 + HGRN_CHUNK]
            kc = k[r0:r0 + HGRN_CHUNK]
            vc = v[r0:r0 + HGRN_CHUNK]
            ac = a[r0:r0 + HGRN_CHUNK] * md
            r_end = [bc[HGRN_SUB * (j + 1) - 1:HGRN_SUB * (j + 1), :] for j in range(nsub)]
            r_own = jnp.concatenate([jnp.broadcast_to(r, (HGRN_SUB, hw)) for r in r_end], axis=0)
            kt_bd = jnp.concatenate([(kc * jnp.exp2(r_own - bc)).astype(BF16)] * 4, axis=0) * bd16
            q_rel = jnp.concatenate(
                [qc * jnp.exp2(jnp.minimum(bc - r_end[j], 0.0)) for j in range(nsub - 1)], axis=0).astype(BF16)
            p = _dot_nt(q_rel, kt_bd) * mo
            for j in range(nsub - 1):
                ac = ac + p[HGRN_CHUNK * j:HGRN_CHUNK * (j + 1)]
            bl = r_end[-1]
            qe = (qc * jnp.exp2(bc)).astype(BF16)
            ke = (kc * jnp.exp2(bl - bc)).astype(BF16)
            vcb = vc.astype(BF16)
            vbd = jnp.concatenate([vcb] * 4, axis=0) * bd16
            outs.append(_dot(ac.astype(BF16), vbd) + _dot_nt(qe, st.astype(BF16)))
            st = st * jnp.exp2(bl) + _dot_tn(vcb, ke) * bd
        st_ref[d] = st
        od = jnp.concatenate(outs, axis=0).astype(BF16)
        if d == 0:
            of_ref[...] = od
        else:
            ob_ref[...] = _dot(flip, od).astype(BF16)


def _hgrn_consts():
    hw = 2 * LANES
    r = jnp.arange(TM)
    same_chunk = (r[:, None] // HGRN_CHUNK) == (r[None, :] // HGRN_CHUNK)
    tri = (same_chunk & (r[None, :] <= r[:, None])).astype(F32)
    flip = (r[:, None] + r[None, :] == TM - 1).astype(F32)
    trij = tri @ flip
    c = jnp.arange(hw)
    s = jnp.arange(HGRN_SUB)
    e = ((c[None, :, None] // HEAD_DIM == c[None, None, :] // HGRN_CHUNK)
         & (c[None, None, :] % HGRN_SUB == s[:, None, None])).astype(BF16).reshape(HGRN_SUB * hw, hw)
    bd = (c[:, None] // HEAD_DIM == c[None, :] // HEAD_DIM).astype(F32)
    tt = jnp.arange(HGRN_CHUNK)[:, None]
    ss = (c % HGRN_CHUNK)[None, :]
    md = ((ss // HGRN_SUB == tt // HGRN_SUB) & (ss <= tt)).astype(F32)
    mo = jnp.concatenate([((ss // HGRN_SUB == j) & (tt // HGRN_SUB > j)).astype(F32)
                          for j in range(HGRN_CHUNK // HGRN_SUB - 1)], axis=0)
    return tri.astype(BF16), trij.astype(BF16), flip.astype(BF16), e, bd, md, mo


def _hgrn_call(qv, kk, gg, consts):
    b, t, _ = qv.shape
    nt = t // TM
    hw = 2 * LANES
    tri, trij, flip, e, bd, md, mo = consts
    fwd = lambda i, j: j
    bwd = lambda i, j: jnp.where(j == 0, 0, nt - j)
    const2 = lambda a: pl.BlockSpec(a.shape, lambda i, j: (0,) * a.ndim)
    return pl.pallas_call(
        _hgrn_kernel,
        grid=(b, nt),
        in_specs=[
            pl.BlockSpec((None, TM, 2 * hw), lambda i, j: (i, fwd(i, j), 0)),
            pl.BlockSpec((None, TM, hw), lambda i, j: (i, fwd(i, j), 0)),
            pl.BlockSpec((None, TM, hw), lambda i, j: (i, fwd(i, j), 0)),
            pl.BlockSpec((None, TM, 2 * hw), lambda i, j: (i, bwd(i, j), 0)),
            pl.BlockSpec((None, TM, hw), lambda i, j: (i, bwd(i, j), 1)),
            pl.BlockSpec((None, TM, hw), lambda i, j: (i, bwd(i, j), 1)),
            const2(tri), const2(trij), const2(flip), const2(e), const2(bd), const2(md), const2(mo),
        ],
        out_specs=[
            pl.BlockSpec((None, TM, hw), lambda i, j: (i, fwd(i, j), 0)),
            pl.BlockSpec((None, TM, hw), lambda i, j: (i, bwd(i, j), 0)),
        ],
        out_shape=[jax.ShapeDtypeStruct((b, t, hw), BF16)] * 2,
        scratch_shapes=[
            pltpu.VMEM((2, hw, hw), F32),
            pltpu.VMEM((2 * TM, HGRN_SUB * hw), BF16),
        ],
        compiler_params=_cparams("parallel", "arbitrary"),
        name="hgrn",
    )(qv, kk, gg, qv, kk, gg, tri, trij, flip, e, bd, md, mo)


def _outproj_kernel(x_ref, mod_ref, ng_ref, attn_ref, cbog_ref, u_ref, up_ref, un_ref, of_ref, ob_ref,
                    hgg_ref, cw_ref, hn_ref, w_ref, o_ref, *, n_tiles):
    t = pl.program_id(1)
    cw = 2 * LANES
    u = u_ref[...]
    row = lax.broadcasted_iota(jnp.int32, (TM, cw), 0)
    prev_row = jnp.where(t >= 2, up_ref[7:8, :], 0.0)
    next_row = jnp.where((t >= 1) & (t < n_tiles - 1), un_ref[0:1, :], 0.0)
    u_prev = jnp.where(row == 0, prev_row, pltpu.roll(u, 1, 0))
    u_next = jnp.where(row == TM - 1, next_row, pltpu.roll(u, TM - 1, 0))
    cbog = cbog_ref[...].astype(F32)
    conv = cbog[:, :cw] * (cw_ref[0:1, :] * u_prev + cw_ref[1:2, :] * u + cw_ref[2:3, :] * u_next)

    o = of_ref[...].astype(F32) + ob_ref[...].astype(F32)
    ms = _dot((o * o).astype(BF16), hn_ref[...])
    og = cbog[:, cw:]
    hg = o * lax.rsqrt(ms + EPS) * hgg_ref[...] * (og * _sigmoid(og))

    y = (_dot(attn_ref[...], w_ref[0:2 * cw, :]) + _dot(conv.astype(BF16), w_ref[2 * cw:3 * cw, :])
         + _dot(hg.astype(BF16), w_ref[3 * cw:4 * cw, :]))
    o_ref[...] = x_ref[...] + mod_ref[2:3, :] * _rms(y, ng_ref[1:2, :])


def _outproj_call(x_all, mod, ng, attn, cbog, u, o_f, o_b, hgg, conv_w, hn, w_out):
    b, t, d = x_all.shape
    nt = t // TM
    cw = 2 * LANES
    nb8 = t // 8
    tok = lambda w: pl.BlockSpec((None, TM, w), lambda i, j: (i, j, 0))
    const = lambda r, w: pl.BlockSpec((r, w), lambda i, j: (0, 0))
    return pl.pallas_call(
        functools.partial(_outproj_kernel, n_tiles=nt),
        grid=(b, nt),
        in_specs=[
            tok(d),
            pl.BlockSpec((None, None, N_MOD, d), lambda i, j: (i, jnp.minimum(j, 1), 0, 0)),
            const(8, d),
            tok(2 * cw),
            tok(2 * cw),
            tok(cw),
            pl.BlockSpec((None, 8, cw), lambda i, j: (i, jnp.maximum(j * (TM // 8) - 1, 0), 0)),
            pl.BlockSpec((None, 8, cw), lambda i, j: (i, jnp.minimum((j + 1) * (TM // 8), nb8 - 1), 0)),
            tok(cw),
            tok(cw),
            const(1, cw),
            const(8, cw),
            const(cw, cw),
            const(d, d),
        ],
        out_specs=tok(d),
        out_shape=jax.ShapeDtypeStruct((b, t, d), F32),
        compiler_params=_cparams("parallel", "parallel"),
        name="outproj",
    )(x_all, mod, ng, attn, cbog, u, u, u, o_f, o_b, hgg, conv_w, hn, w_out)


def _mod_rows(modt_ref, i, rows):
    sub = rows // modt_ref.shape[0]
    return jnp.concatenate(
        [jnp.broadcast_to(modt_ref[s, i:i + 1, :], (sub, modt_ref.shape[2])) for s in range(modt_ref.shape[0])],
        axis=0)


def _store_split(ref, val):
    w = ref.shape[2]
    for c in range(ref.shape[0]):
        ref[c] = val[:, w * c:w * (c + 1)]


def _load_split(ref):
    return jnp.concatenate([ref[c] for c in range(ref.shape[0])], axis=1)


def _split_spec(rows):
    return lambda d: pl.BlockSpec((GATHER_SPLIT, rows, d // GATHER_SPLIT), lambda i, *_: (0, i, 0))


def _ffn_kernel(x_ref, modt_ref, ng_ref, wg_ref, wu_ref, wd_ref, o_ref, *, tf):
    x = x_ref[...]
    rows = x.shape[0]
    h = (_rms(x, ng_ref[2:3, :]) * (1.0 + _mod_rows(modt_ref, 4, rows)) + _mod_rows(modt_ref, 3, rows)).astype(BF16)
    acc = jnp.zeros(x.shape, F32)
    for f0 in range(0, wg_ref.shape[1], tf):
        g = _dot(h, wg_ref[:, f0:f0 + tf])
        up = _dot(h, wu_ref[:, f0:f0 + tf])
        acc = acc + _dot((g * _sigmoid(g) * up).astype(BF16), wd_ref[f0:f0 + tf, :])
    o_ref[...] = x + _mod_rows(modt_ref, 5, rows) * _rms(acc, ng_ref[3:4, :])


def _ffn_call(x_flat, modt, ng, wg, wu, wd):
    n, d = x_flat.shape
    dff = wg.shape[1]
    nsub = FFN_TM // TM
    resident = lambda a: pl.BlockSpec(a.shape, lambda i: (0, 0), pipeline_mode=pl.Buffered(1))
    return pl.pallas_call(
        functools.partial(_ffn_kernel, tf=256),
        grid=(n // FFN_TM,),
        in_specs=[
            pl.BlockSpec((FFN_TM, d), lambda i: (i, 0)),
            pl.BlockSpec((nsub, N_MOD, d), lambda i: (i, 0, 0)),
            pl.BlockSpec((8, d), lambda i: (0, 0)),
            resident(wg), resident(wu), resident(wd),
        ],
        out_specs=pl.BlockSpec((FFN_TM, d), lambda i: (i, 0)),
        out_shape=jax.ShapeDtypeStruct((n, d), F32),
        compiler_params=_cparams("parallel"),
        name="ffn",
    )(x_flat, modt, ng, wg, wu, wd)


def _router_kernel(x_ref, modt_ref, ng_ref, wr_ref, h_ref, r_ref, rt_ref):
    x = x_ref[...]
    rows = x.shape[0]
    h = _rms(x, ng_ref[2:3, :]) * (1.0 + _mod_rows(modt_ref, 4, rows)) + _mod_rows(modt_ref, 3, rows)
    _store_split(h_ref, h)
    h_hi = h.astype(BF16)
    h_lo = (h - h_hi.astype(F32)).astype(BF16)
    logits = _dot(h_hi, wr_ref[0]) + _dot(h_lo, wr_ref[0]) + _dot(h_hi, wr_ref[1])
    lane = lax.broadcasted_iota(jnp.int32, logits.shape, 1)
    lg = jnp.where(lane < N_EXPERTS, logits, NEG_BIG)
    m1 = jnp.max(lg, axis=-1, keepdims=True)
    i1 = jnp.min(jnp.where(lg == m1, lane, LANES), axis=-1, keepdims=True)
    lg2 = jnp.where(lane == i1, NEG_BIG, lg)
    m2 = jnp.max(lg2, axis=-1, keepdims=True)
    i2 = jnp.min(jnp.where(lg2 == m2, lane, LANES), axis=-1, keepdims=True)
    e2 = jnp.exp(m2 - m1)
    w1 = 1.0 / (1.0 + e2)
    w2 = e2 / (1.0 + e2)
    r = jnp.where(lane == 0, i1.astype(F32),
                  jnp.where(lane == 1, i2.astype(F32), jnp.where(lane == 2, w1, jnp.where(lane == 3, w2, 0.0))))
    r_ref[...] = r
    rt_ref[...] = r.T[0:8, :]


def _router_call(x_flat, modt, ng, wr):
    n, d = x_flat.shape
    nsub = FFN_TM // TM
    return pl.pallas_call(
        _router_kernel,
        grid=(n // FFN_TM,),
        in_specs=[
            pl.BlockSpec((FFN_TM, d), lambda i: (i, 0)),
            pl.BlockSpec((nsub, N_MOD, d), lambda i: (i, 0, 0)),
            pl.BlockSpec((8, d), lambda i: (0, 0)),
            pl.BlockSpec(wr.shape, lambda i: (0, 0, 0)),
        ],
        out_specs=[_split_spec(FFN_TM)(d), pl.BlockSpec((FFN_TM, LANES), lambda i: (i, 0)),
                   pl.BlockSpec((8, FFN_TM), lambda i: (0, i))],
        out_shape=[jax.ShapeDtypeStruct((GATHER_SPLIT, n, d // GATHER_SPLIT), F32),
                   jax.ShapeDtypeStruct((n, LANES), F32), jax.ShapeDtypeStruct((8, n), F32)],
        compiler_params=_cparams("parallel"),
        name="router",
    )(x_flat, modt, ng, wr)


def _gather_flat(data, idx):
    m = idx.shape[0]
    w = data.shape[1]
    mesh = plsc.VectorSubcoreMesh(core_axis_name="core", subcore_axis_name="subcore")

    @functools.partial(pl.kernel, out_type=jax.ShapeDtypeStruct((m, w), data.dtype), mesh=mesh)
    def gather(x_hbm, i_hbm, o_hbm):
        def body(i_vmem, o_vmem):
            pltpu.sync_copy(x_hbm.at[i_vmem.at[0]], o_vmem)

        pltpu.emit_pipeline(
            body,
            grid=(m // GATHER_WIN,),
            in_specs=[pl.BlockSpec((1, GATHER_WIN), lambda i: (0, i))],
            out_specs=[pl.BlockSpec((GATHER_WIN, w), lambda i: (i, 0))],
            core_axis_name=("core", "subcore"),
            dimension_semantics=(pltpu.PARALLEL,),
        )(i_hbm, o_hbm)

    return gather(data, idx.reshape(1, m))


def _scatter_flat(rows, idx, n_out):
    m, w = rows.shape
    mesh = plsc.VectorSubcoreMesh(core_axis_name="core", subcore_axis_name="subcore")

    @functools.partial(pl.kernel, out_type=jax.ShapeDtypeStruct((n_out, w), rows.dtype), mesh=mesh)
    def scatter(x_hbm, i_hbm, o_hbm):
        def body(x_vmem, i_vmem):
            pltpu.sync_copy(x_vmem, o_hbm.at[i_vmem.at[0]])

        pltpu.emit_pipeline(
            body,
            grid=(m // GATHER_WIN,),
            in_specs=[pl.BlockSpec((GATHER_WIN, w), lambda i: (i, 0)),
                      pl.BlockSpec((1, GATHER_WIN), lambda i: (0, i))],
            out_specs=[],
            core_axis_name=("core", "subcore"),
            dimension_semantics=(pltpu.PARALLEL,),
        )(x_hbm, i_hbm)

    return scatter(rows, idx.reshape(1, m))


def _col0_kernel(x_ref, o_ref):
    o_ref[...] = x_ref[...].T[0:1, :]


def _first_column(rows):
    n = rows.shape[0]
    blk = 1024
    return pl.pallas_call(
        _col0_kernel,
        grid=(n // blk,),
        in_specs=[pl.BlockSpec((blk, LANES), lambda i: (i, 0))],
        out_specs=pl.BlockSpec((1, blk), lambda i: (0, i)),
        out_shape=jax.ShapeDtypeStruct((1, n), rows.dtype),
        compiler_params=_cparams("parallel"),
        name="col0",
    )(rows)


CAST_STREAMS = 4


def _cast_kernel(*refs):
    o_ref = refs[-1]
    cw = refs[0].shape[1]
    for s, x_ref in enumerate(refs[:-1]):
        o_ref[:, cw * s:cw * (s + 1)] = x_ref[...].astype(o_ref.dtype)


def _cast_bf16(w):
    e, r, c = w.shape
    tr = 256
    cw = c // CAST_STREAMS
    chunk = lambda s: pl.BlockSpec((None, tr, cw), lambda i, j: (i, j, s))
    return pl.pallas_call(
        _cast_kernel,
        grid=(e, r // tr),
        in_specs=[chunk(s) for s in range(CAST_STREAMS)],
        out_specs=pl.BlockSpec((None, tr, c), lambda i, j: (i, j, 0)),
        out_shape=jax.ShapeDtypeStruct(w.shape, BF16),
        compiler_params=_cparams("parallel", "parallel"),
        name="cast",
    )(*([w] * CAST_STREAMS))


def _gather_rows(data, idx):
    s, r, w = data.shape
    m = idx.shape[0]
    idx_all = (idx[None, :] + (jnp.arange(s, dtype=jnp.int32) * r)[:, None]).reshape(s * m)
    return _gather_flat(data.reshape(s * r, w), idx_all).reshape(s, m, w)


def _expert_kernel(be_ref, nv_ref, xs_ref, wg_ref, wu_ref, wd_ref, o_ref):
    valid = pl.program_id(0) < nv_ref[0]

    @pl.when(valid)
    def _():
        xb = _load_split(xs_ref).astype(BF16)
        acc = jnp.zeros((xb.shape[0], wd_ref.shape[1]), F32)
        for f0 in range(0, wg_ref.shape[1], MOE_TF):
            g = _dot(xb, wg_ref[:, f0:f0 + MOE_TF])
            up = _dot(xb, wu_ref[:, f0:f0 + MOE_TF])
            acc = acc + _dot((g * _sigmoid(g) * up).astype(BF16), wd_ref[f0:f0 + MOE_TF, :])
        _store_split(o_ref, acc)

    @pl.when(jnp.logical_not(valid))
    def _():
        o_ref[...] = jnp.zeros(o_ref.shape, F32)


def _expert_call(block_expert, n_valid, xs, wg, wu, wd):
    n_slots = xs.shape[1]
    d = wg.shape[1]
    dff = wg.shape[2]
    nb = n_slots // MOE_BLK
    resident = lambda r, c: pl.BlockSpec((None, r, c), lambda i, be, nv: (be[i], 0, 0), pipeline_mode=pl.Buffered(1))
    grid_spec = pltpu.PrefetchScalarGridSpec(
        num_scalar_prefetch=2,
        grid=(nb,),
        in_specs=[_split_spec(MOE_BLK)(d), resident(d, dff), resident(d, dff), resident(dff, d)],
        out_specs=_split_spec(MOE_BLK)(d),
    )
    return pl.pallas_call(
        _expert_kernel,
        grid_spec=grid_spec,
        out_shape=jax.ShapeDtypeStruct((GATHER_SPLIT, n_slots, d // GATHER_SPLIT), F32),
        compiler_params=_cparams("arbitrary"),
        name="experts",
    )(block_expert, n_valid, xs, wg, wu, wd)


def _combine_kernel(x_ref, modt_ref, ng_ref, y1_ref, y2_ref, r_ref, o_ref):
    x = x_ref[...]
    rows = x.shape[0]
    f = r_ref[:, 2:3] * _load_split(y1_ref) + r_ref[:, 3:4] * _load_split(y2_ref)
    o_ref[...] = x + _mod_rows(modt_ref, 5, rows) * _rms(f, ng_ref[3:4, :])


def _combine_call(x_flat, modt, ng, y12, rinfo):
    n, d = x_flat.shape
    nsub = FFN_TM // TM
    nblk = n // FFN_TM
    tok = lambda w: pl.BlockSpec((FFN_TM, w), lambda i: (i, 0))
    second = pl.BlockSpec((GATHER_SPLIT, FFN_TM, d // GATHER_SPLIT), lambda i: (0, i + nblk, 0))
    return pl.pallas_call(
        _combine_kernel,
        grid=(nblk,),
        in_specs=[tok(d), pl.BlockSpec((nsub, N_MOD, d), lambda i: (i, 0, 0)), pl.BlockSpec((8, d), lambda i: (0, 0)),
                  _split_spec(FFN_TM)(d), second, tok(LANES)],
        out_specs=tok(d),
        out_shape=jax.ShapeDtypeStruct((n, d), F32),
        compiler_params=_cparams("parallel"),
        name="combine",
    )(x_flat, modt, ng, y12, y12, rinfo)


def _moe_layer(x_flat, modt, ng, w_router, wg, wu, wd):
    n, d = x_flat.shape
    wr = jnp.pad(w_router, ((0, 0), (0, LANES - N_EXPERTS)))
    wr_hi = wr.astype(BF16)
    wr_lo = (wr - wr_hi.astype(F32)).astype(BF16)
    h, rinfo, rinfo_t = _router_call(x_flat, modt, ng, jnp.stack([wr_hi, wr_lo]))

    n_assign = 2 * n
    expert = jnp.concatenate([rinfo_t[0], rinfo_t[1]]).astype(jnp.int32)
    onehot = (expert[None, :] == jnp.arange(N_EXPERTS, dtype=jnp.int32)[:, None]).astype(jnp.int32)
    csum = jnp.cumsum(onehot, axis=1)
    rank = jnp.sum((csum - onehot) * onehot, axis=0)
    counts = csum[:, -1]
    padded = (counts + MOE_BLK - 1) // MOE_BLK * MOE_BLK
    pad_end = jnp.cumsum(padded)
    pad_start = pad_end - padded
    dest = jnp.sum(onehot * pad_start[:, None], axis=0) + rank
    nb = -(-n_assign // MOE_BLK) + N_EXPERTS
    row_quantum = 32 * GATHER_WIN // GATHER_SPLIT
    nb = -(-nb * MOE_BLK // row_quantum) * row_quantum // MOE_BLK
    n_slots = nb * MOE_BLK
    block_expert = jnp.minimum(
        jnp.searchsorted(pad_end, jnp.arange(nb, dtype=jnp.int32) * MOE_BLK, side="right"), N_EXPERTS - 1
    ).astype(jnp.int32)
    n_valid = (pad_end[-1:] // MOE_BLK).astype(jnp.int32)
    tok_rows = jnp.broadcast_to((jnp.arange(n_assign, dtype=jnp.int32) % n)[:, None], (n_assign, LANES))
    slot = jnp.arange(n_slots, dtype=jnp.int32)
    slot_used = (slot - jnp.repeat(pad_start[block_expert], MOE_BLK)) < jnp.repeat(counts[block_expert], MOE_BLK)
    slot_tok = jnp.where(slot_used, _first_column(_scatter_flat(tok_rows, dest, n_slots)).reshape(n_slots), 0)

    xs = _gather_rows(h, slot_tok)
    ys = _expert_call(block_expert, n_valid, xs, wg, wu, wd)
    y12 = _gather_rows(ys, dest)
    return _combine_call(x_flat, modt, ng, y12, rinfo)


def _rope_tables(n_ctx, length):
    rows = length // GRID_W
    row = jnp.repeat(jnp.arange(rows, dtype=F32), GRID_W)
    col = jnp.tile(jnp.arange(GRID_W, dtype=F32), rows)
    axis_dim = HEAD_DIM // 2
    inv_freq = ROPE_THETA ** (-jnp.arange(0, axis_dim, 2, dtype=F32) / axis_dim)
    ar = row[:, None] * inv_freq
    ac = col[:, None] * inv_freq
    cos = jnp.concatenate([jnp.cos(ar), jnp.cos(ar), jnp.cos(ac), jnp.cos(ac)], axis=1)
    sin = jnp.concatenate([-jnp.sin(ar), jnp.sin(ar), -jnp.sin(ac), jnp.sin(ac)], axis=1)
    cos = jnp.concatenate([jnp.ones((n_ctx, HEAD_DIM), F32), cos], axis=0)
    sin = jnp.concatenate([jnp.zeros((n_ctx, HEAD_DIM), F32), sin], axis=0)
    return jnp.tile(cos, (1, 2)), jnp.tile(sin, (1, 2))


def _inproj_weights(w_in_l):
    d = w_in_l.shape[0]
    aw = d // 2
    nh = aw // HEAD_DIM
    wq = w_in_l[:, :aw].reshape(d, nh, HEAD_DIM)
    z = jnp.zeros((d, nh // 2, HEAD_DIM), w_in_l.dtype)
    q_pad = jnp.concatenate([
        jnp.concatenate([wq[:, :nh // 2], z], axis=2),
        jnp.concatenate([z, wq[:, nh // 2:]], axis=2),
    ], axis=1).reshape(d, nh * LANES)
    return jnp.concatenate([q_pad, w_in_l[:, aw:]], axis=1).astype(BF16)


def kernel(x, c, ctx, c_ctx, ada_w, ada_b, norm_g, w_in, w_out, attn_q_g, attn_k_g, conv_w, hgrn_lb, hgrn_g,
           ffn_w_gate, ffn_w_up, ffn_w_down, moe_router, moe_w_gate, moe_w_up, moe_w_down):
    bsz, length, d = x.shape
    n_ctx = ctx.shape[1]
    depth = w_in.shape[0]
    t = n_ctx + length
    nt = t // TM
    assert n_ctx == TM and length % TM == 0 and d == 8 * LANES
    for n_tok in (bsz * t, bsz * length):
        assert n_tok % FFN_TM == 0 and (2 * n_tok) % (32 * GATHER_WIN) == 0

    r_pad = -(-(bsz + 1) // 8) * 8
    c_all = jnp.concatenate([c, c_ctx[None, :], jnp.zeros((r_pad - bsz - 1, d), F32)], axis=0)
    mod_all = _ada_call(c_all, ada_w, ada_b).reshape(depth, r_pad, N_MOD, d)

    cos_t, sin_t = _rope_tables(n_ctx, length)
    lb_all = jnp.cumsum(jax.nn.softmax(hgrn_lb.astype(F32), axis=0), axis=0)
    lb_all = lb_all - lb_all[0]
    hconsts = _hgrn_consts()
    cw = 2 * LANES
    cidx = jnp.arange(cw)
    head_mean = ((cidx[:, None] // HEAD_DIM == cidx[None, :] // HEAD_DIM).astype(F32) / HEAD_DIM).astype(BF16)

    x_all = jnp.concatenate([ctx, x], axis=1)
    for l in range(depth):
        mod_lat = mod_all[l, :bsz]
        mod_ctx = jnp.broadcast_to(mod_all[l, bsz][None], (bsz, N_MOD, d))
        mod = jnp.stack([mod_ctx, mod_lat], axis=1)
        modt = jnp.concatenate([mod[:, :1], jnp.broadcast_to(mod[:, 1:], (bsz, nt - 1, N_MOD, d))], axis=1)
        modt = modt.reshape(bsz * nt, N_MOD, d)
        ng = jnp.pad(norm_g[l], ((0, 4), (0, 0)))
        lb = lb_all[l].reshape(1, 2 * cw)
        gc = jnp.concatenate([jnp.log(lb), jnp.log1p(-lb), 1.0 - lb, jnp.zeros((5, 2 * cw), F32)], axis=0)
        qg = jnp.tile(attn_q_g[l], 2)[None, :]
        kg = jnp.tile(attn_k_g[l], 2)[None, :]

        q, k, vt, cbog, u, qv, kk, gg = _inproj_call(x_all, mod, ng, _inproj_weights(w_in[l]), cos_t, sin_t, qg, kg, gc)
        attn = _attn_call(q, k, vt)
        o_f, o_b = _hgrn_call(qv, kk, gg, hconsts)
        x_all = _outproj_call(x_all, mod, ng, attn, cbog, u, o_f, o_b, hgrn_g[l][None, :],
                              jnp.pad(conv_w[l], ((0, 5), (0, 0))), head_mean, w_out[l].astype(BF16))

        last = l == depth - 1
        if last:
            x_flat = x_all[:, n_ctx:, :].reshape(bsz * length, d)
            modt = jnp.broadcast_to(mod[:, 1:], (bsz, length // TM, N_MOD, d)).reshape(bsz * (length // TM), N_MOD, d)
        else:
            x_flat = x_all.reshape(bsz * t, d)
        if l % 2 == 0:
            x_flat = _ffn_call(x_flat, modt, ng, ffn_w_gate[l // 2].astype(BF16), ffn_w_up[l // 2].astype(BF16),
                               ffn_w_down[l // 2].astype(BF16))
        else:
            x_flat = _moe_layer(x_flat, modt, ng, moe_router[l // 2], _cast_bf16(moe_w_gate[l // 2]),
                                _cast_bf16(moe_w_up[l // 2]), _cast_bf16(moe_w_down[l // 2]))
        if last:
            return x_flat.reshape(bsz, length, d)
        x_all = x_flat.reshape(bsz, t, d)
```

```python
import functools

import jax
import jax.numpy as jnp
from jax import lax
from jax.experimental import pallas as pl
from jax.experimental.pallas import tpu as pltpu
from jax.experimental.pallas import tpu_sc as plsc

F32 = jnp.float32
BF16 = jnp.bfloat16

HEAD_DIM = 64
GRID_W = 64
ROPE_THETA = 10000.0
ATTN_SCALE = HEAD_DIM ** -0.5
LOG2_E = 1.4426950408889634
HGRN_CHUNK = 64
HGRN_SUB = 16
VT_ROWS = HEAD_DIM + 16
ATTN_AHEAD = 4
N_EXPERTS = 8
N_MOD = 6
EPS = 1e-6

LANES = 128
TM = 256
FFN_TM = 512
FFN_TF = 256
N_MIX_REFS = 11
CAST_STREAMS = 4
MOE_BLK = 512
MOE_TF = 512
GATHER_WIN = 128
GATHER_SPLIT = 4
NEG_BIG = -1e30
VMEM_LIMIT = 56 * 1024 * 1024


def _cparams(*sem):
    return pltpu.CompilerParams(dimension_semantics=sem, vmem_limit_bytes=VMEM_LIMIT)


def _sigmoid(z):
    return 1.0 / (1.0 + jnp.exp(-z))


def _dot(a, b):
    return jnp.dot(a, b, preferred_element_type=F32)


def _dot_nt(a, b):
    return lax.dot_general(a, b, (((1,), (1,)), ((), ())), preferred_element_type=F32)


def _dot_tn(a, b):
    return lax.dot_general(a, b, (((0,), (0,)), ((), ())), preferred_element_type=F32)


def _rms(x, g):
    return x * lax.rsqrt(jnp.mean(x * x, axis=-1, keepdims=True) + EPS) * g


def _ada_kernel(c_ref, w_ref, b_ref, o_ref):
    c = c_ref[...]
    s = (c * _sigmoid(c)).astype(BF16)
    o_ref[...] = _dot(s, w_ref[...].astype(BF16)) + b_ref[...]


def _ada_call(c_all, ada_w, ada_b):
    depth, d, n = ada_w.shape
    r = c_all.shape[0]
    tn = 512
    return pl.pallas_call(
        _ada_kernel,
        grid=(depth, n // tn),
        in_specs=[
            pl.BlockSpec((r, d), lambda l, j: (0, 0)),
            pl.BlockSpec((None, d, tn), lambda l, j: (l, 0, j)),
            pl.BlockSpec((None, 1, tn), lambda l, j: (l, 0, j)),
        ],
        out_specs=pl.BlockSpec((None, r, tn), lambda l, j: (l, 0, j)),
        out_shape=jax.ShapeDtypeStruct((depth, r, n), F32),
        compiler_params=_cparams("parallel", "parallel"),
        name="ada",
    )(c_all, ada_w, ada_b.reshape(depth, 1, n))


def _inproj_kernel(x_ref, mod_ref, ng_ref, w_ref, cos_ref, sin_ref, qg_ref, kg_ref, gc_ref,
                   q_ref, k_ref, vt_ref, cbog_ref, u_ref, qv_ref, kk_ref, gg_ref):
    x = x_ref[...]
    h = _rms(x, ng_ref[0:1, :]) * (1.0 + mod_ref[1:2, :]) + mod_ref[0:1, :]
    hb = h.astype(BF16)

    def proj(c0, n):
        return _dot(hb, w_ref[:, c0:c0 + n])

    cos = cos_ref[...]
    sin = sin_ref[...]
    lane = lax.broadcasted_iota(jnp.int32, (TM, LANES), 1)
    first_half = (lane % 32) < 16
    lo = lane < HEAD_DIM

    def rope(v):
        partner = jnp.where(first_half, pltpu.roll(v, LANES - 16, 1), pltpu.roll(v, 16, 1))
        return v * cos + partner * sin

    qg = qg_ref[...]
    for hp in range(4):
        q2 = proj(2 * LANES * hp, 2 * LANES)
        for s in range(2):
            qh = q2[:, LANES * s:LANES * (s + 1)]
            r = lax.rsqrt(jnp.sum(qh * qh, axis=-1, keepdims=True) * (1.0 / HEAD_DIM) + EPS)
            q_ref[2 * hp + s] = (rope(qh * r * qg) * (ATTN_SCALE * LOG2_E)).astype(BF16)

    kv = proj(8 * LANES, 2 * LANES)
    kx = kv[:, :LANES]
    v = kv[:, LANES:]
    k2 = kx * kx
    s_lo = jnp.sum(jnp.where(lo, k2, 0.0), axis=-1, keepdims=True)
    s_hi = jnp.sum(jnp.where(lo, 0.0, k2), axis=-1, keepdims=True)
    r = lax.rsqrt(jnp.where(lo, s_lo, s_hi) * (1.0 / HEAD_DIM) + EPS)
    k_ref[...] = rope(kx * r * kg_ref[...]).astype(BF16)
    vt = v.T
    ones = jnp.ones((VT_ROWS - HEAD_DIM, TM), F32)
    vt_ref[0] = jnp.concatenate([vt[:HEAD_DIM], ones], axis=0).astype(BF16)
    vt_ref[1] = jnp.concatenate([vt[HEAD_DIM:], ones], axis=0).astype(BF16)

    c0 = 10 * LANES
    c3 = proj(c0, 6 * LANES)
    cw = 2 * LANES
    u_ref[...] = c3[:, cw:2 * cw] * c3[:, 2 * cw:3 * cw]

    hg = proj(c0 + 6 * LANES, 10 * LANES)
    zf, zb, iv, hq, og = (hg[:, cw * i:cw * (i + 1)] for i in range(5))
    cbog_ref[:, :cw] = c3[:, :cw].astype(BF16)
    cbog_ref[:, cw:] = og.astype(BF16)
    qv_ref[:, :cw] = (hq * _sigmoid(hq)).astype(BF16)
    qv_ref[:, cw:] = iv.astype(BF16)
    for d, z in enumerate((zf, zb)):
        log_lb = gc_ref[0:1, cw * d:cw * (d + 1)]
        log1m_lb = gc_ref[1:2, cw * d:cw * (d + 1)]
        one_m_lb = gc_ref[2:3, cw * d:cw * (d + 1)]
        t = jnp.exp(-jnp.abs(z))
        log_sig = jnp.minimum(z, 0.0) - jnp.log1p(t)
        a2 = log1m_lb + log_sig
        log_f = jnp.maximum(log_lb, a2) + jnp.log1p(jnp.exp(-jnp.abs(log_lb - a2)))
        sig_neg = jnp.where(z >= 0, t, 1.0) / (1.0 + t)
        gg_ref[:, cw * d:cw * (d + 1)] = log_f
        kk_ref[:, cw * d:cw * (d + 1)] = (one_m_lb * sig_neg).astype(BF16)


def _inproj_call(x_all, mod, ng, w1, cos_t, sin_t, qg, kg, gc):
    b, t, d = x_all.shape
    nt = t // TM
    nw = w1.shape[1]
    tok = lambda w: pl.BlockSpec((None, TM, w), lambda i, j: (i, j, 0))
    const = lambda r, w: pl.BlockSpec((r, w), lambda i, j: (0, 0))
    return pl.pallas_call(
        _inproj_kernel,
        grid=(b, nt),
        in_specs=[
            tok(d),
            pl.BlockSpec((None, None, N_MOD, d), lambda i, j: (i, jnp.minimum(j, 1), 0, 0)),
            const(8, d),
            const(d, nw),
            pl.BlockSpec((TM, LANES), lambda i, j: (j, 0)),
            pl.BlockSpec((TM, LANES), lambda i, j: (j, 0)),
            const(1, LANES),
            const(1, LANES),
            const(8, 4 * LANES),
        ],
        out_specs=[
            pl.BlockSpec((None, 8, TM, LANES), lambda i, j: (i, 0, j, 0)),
            tok(LANES),
            pl.BlockSpec((None, 2, None, VT_ROWS, TM), lambda i, j: (i, 0, j, 0, 0)),
            tok(4 * LANES),
            tok(2 * LANES),
            tok(4 * LANES),
            tok(4 * LANES),
            tok(4 * LANES),
        ],
        out_shape=[
            jax.ShapeDtypeStruct((b, 8, t, LANES), BF16),
            jax.ShapeDtypeStruct((b, t, LANES), BF16),
            jax.ShapeDtypeStruct((b, 2, nt, VT_ROWS, TM), BF16),
            jax.ShapeDtypeStruct((b, t, 4 * LANES), BF16),
            jax.ShapeDtypeStruct((b, t, 2 * LANES), F32),
            jax.ShapeDtypeStruct((b, t, 4 * LANES), BF16),
            jax.ShapeDtypeStruct((b, t, 4 * LANES), BF16),
            jax.ShapeDtypeStruct((b, t, 4 * LANES), F32),
        ],
        compiler_params=_cparams("parallel", "parallel"),
        name="inproj",
    )(x_all, mod, ng, w1, cos_t, sin_t, qg, kg, gc)


def _attn_kernel(q_ref, k_ref, vt_ref, o_ref, m_ref, acc_ref, *, n_kv):
    def score(j, h):
        return _dot_nt(k_ref[TM * j:TM * (j + 1), :], q_ref[h])

    def attend(n_tiles):
        steps = [(j, h) for j in range(n_tiles) for h in range(8)]
        scores = [score(j, h) for j, h in steps[:ATTN_AHEAD]]
        for i, (j, h) in enumerate(steps):
            if i + ATTN_AHEAD < len(steps):
                scores.append(score(*steps[i + ATTN_AHEAD]))
            s = scores.pop(0)
            m_tile = jnp.max(s, axis=0, keepdims=True)
            if j == 0:
                m_new = m_tile
                acc_ref[h] = _dot(vt_ref[h // 4, j], jnp.exp2(s - m_new).astype(BF16))
            else:
                m_old = m_ref[h]
                m_new = jnp.maximum(m_old, m_tile)
                pv = _dot(vt_ref[h // 4, j], jnp.exp2(s - m_new).astype(BF16))
                acc_ref[h] = jnp.exp2(m_old - m_new) * acc_ref[h] + pv
            m_ref[h] = m_new
        outs = [acc_ref[h, :HEAD_DIM, :] / acc_ref[h, HEAD_DIM:HEAD_DIM + 1, :] for h in range(8)]
        o_ref[...] = jnp.concatenate(outs, axis=0).T.astype(BF16)

    @pl.when(pl.program_id(1) == 0)
    def _():
        attend(1)

    @pl.when(pl.program_id(1) > 0)
    def _():
        attend(n_kv)


def _attn_call(q, k, vt):
    b, _, t, _ = q.shape
    nt = t // TM
    return pl.pallas_call(
        functools.partial(_attn_kernel, n_kv=nt),
        grid=(b, nt),
        in_specs=[
            pl.BlockSpec((None, 8, TM, LANES), lambda i, j: (i, 0, j, 0)),
            pl.BlockSpec((None, t, LANES), lambda i, j: (i, 0, 0)),
            pl.BlockSpec((None, 2, nt, VT_ROWS, TM), lambda i, j: (i, 0, 0, 0, 0)),
        ],
        out_specs=pl.BlockSpec((None, TM, 4 * LANES), lambda i, j: (i, j, 0)),
        out_shape=jax.ShapeDtypeStruct((b, t, 4 * LANES), BF16),
        scratch_shapes=[pltpu.VMEM((8, 1, TM), F32), pltpu.VMEM((8, VT_ROWS, TM), F32)],
        compiler_params=_cparams("parallel", "parallel"),
        name="attn",
    )(q, k, vt)


def _split3(g):
    hi = g.astype(BF16)
    r1 = g - hi.astype(F32)
    mid = r1.astype(BF16)
    lo = (r1 - mid.astype(F32)).astype(BF16)
    return hi, mid, lo


def _hgrn_kernel(qvf_ref, kf_ref, gf_ref, qvb_ref, kb_ref, gb_ref, tri_ref, trij_ref, j_ref, e_ref, bd_ref,
                 md_ref, mo_ref, of_ref, ob_ref, st_ref, x_s):
    hw = 2 * LANES
    nch = TM // HGRN_CHUNK

    @pl.when(pl.program_id(1) == 0)
    def _():
        st_ref[...] = jnp.zeros(st_ref.shape, F32)

    flip = j_ref[...]

    def cumsum(m_ref, g):
        m = m_ref[...]
        hi, mid, lo = _split3(g)
        return _dot(m, hi) + _dot(m, mid) + _dot(m, lo)

    qvf = qvf_ref[...]
    fl = _dot(flip, jnp.concatenate([qvb_ref[...], kb_ref[...]], axis=1))
    q = jnp.concatenate([qvf[:, :hw].astype(F32), fl[:, :hw]], axis=0)
    v = jnp.concatenate([qvf[:, hw:].astype(F32), fl[:, hw:2 * hw]], axis=0)
    k = jnp.concatenate([kf_ref[...].astype(F32), fl[:, 2 * hw:]], axis=0)
    b = jnp.concatenate([cumsum(tri_ref, gf_ref[...]), cumsum(trij_ref, gb_ref[...])], axis=0) * LOG2_E

    ngr = 2 * TM // HGRN_SUB
    nsub = HGRN_CHUNK // HGRN_SUB
    q3 = q.reshape(ngr, HGRN_SUB, hw)
    b3 = b.reshape(ngr, HGRN_SUB, hw)
    k3 = k.reshape(ngr, HGRN_SUB, hw)
    for s in range(HGRN_SUB):
        x = q3 * jnp.exp2(jnp.minimum(b3 - b3[:, s:s + 1, :], 0.0)) * k3[:, s:s + 1, :]
        x_s[:, hw * s:hw * (s + 1)] = x.reshape(2 * TM, hw).astype(BF16)
    a = _dot(x_s[...], e_ref[...])

    bd = bd_ref[...]
    bd16 = bd.astype(BF16)
    md = md_ref[...]
    mo = mo_ref[...]
    for d in range(2):
        st = st_ref[d]
        outs = []
        for c in range(nch):
            r0 = TM * d + HGRN_CHUNK * c
            bc = b[r0:r0 + HGRN_CHUNK]
            qc = q[r0:r0 + HGRN_CHUNK]
            kc = k[r0:r0 + HGRN_CHUNK]
            vc = v[r0:r0 + HGRN_CHUNK]
            ac = a[r0:r0 + HGRN_CHUNK] * md
            r_end = [bc[HGRN_SUB * (j + 1) - 1:HGRN_SUB * (j + 1), :] for j in range(nsub)]
            r_own = jnp.concatenate([jnp.broadcast_to(r, (HGRN_SUB, hw)) for r in r_end], axis=0)
            kt_bd = jnp.concatenate([(kc * jnp.exp2(r_own - bc)).astype(BF16)] * 4, axis=0) * bd16
            q_rel = jnp.concatenate(
                [qc * jnp.exp2(jnp.minimum(bc - r_end[j], 0.0)) for j in range(nsub - 1)], axis=0).astype(BF16)
            p = _dot_nt(q_rel, kt_bd) * mo
            for j in range(nsub - 1):
                ac = ac + p[HGRN_CHUNK * j:HGRN_CHUNK * (j + 1)]
            bl = r_end[-1]
            qe = (qc * jnp.exp2(bc)).astype(BF16)
            ke = (kc * jnp.exp2(bl - bc)).astype(BF16)
            vcb = vc.astype(BF16)
            vbd = jnp.concatenate([vcb] * 4, axis=0) * bd16
            outs.append(_dot(ac.astype(BF16), vbd) + _dot_nt(qe, st.astype(BF16)))
            st = st * jnp.exp2(bl) + _dot_tn(vcb, ke) * bd
        st_ref[d] = st
        od = jnp.concatenate(outs, axis=0).astype(BF16)
        if d == 0:
            of_ref[...] = od
        else:
            ob_ref[...] = _dot(flip, od).astype(BF16)


def _hgrn_consts():
    hw = 2 * LANES
    r = jnp.arange(TM)
    same_chunk = (r[:, None] // HGRN_CHUNK) == (r[None, :] // HGRN_CHUNK)
    tri = (same_chunk & (r[None, :] <= r[:, None])).astype(F32)
    flip = (r[:, None] + r[None, :] == TM - 1).astype(F32)
    trij = tri @ flip
    c = jnp.arange(hw)
    s = jnp.arange(HGRN_SUB)
    e = ((c[None, :, None] // HEAD_DIM == c[None, None, :] // HGRN_CHUNK)
         & (c[None, None, :] % HGRN_SUB == s[:, None, None])).astype(BF16).reshape(HGRN_SUB * hw, hw)
    bd = (c[:, None] // HEAD_DIM == c[None, :] // HEAD_DIM).astype(F32)
    tt = jnp.arange(HGRN_CHUNK)[:, None]
    ss = (c % HGRN_CHUNK)[None, :]
    md = ((ss // HGRN_SUB == tt // HGRN_SUB) & (ss <= tt)).astype(F32)
    mo = jnp.concatenate([((ss // HGRN_SUB == j) & (tt // HGRN_SUB > j)).astype(F32)
                          for j in range(HGRN_CHUNK // HGRN_SUB - 1)], axis=0)
    return tri.astype(BF16), trij.astype(BF16), flip.astype(BF16), e, bd, md, mo


def _hgrn_call(qv, kk, gg, consts):
    b, t, _ = qv.shape
    nt = t // TM
    hw = 2 * LANES
    tri, trij, flip, e, bd, md, mo = consts
    fwd = lambda i, j: j
    bwd = lambda i, j: jnp.where(j == 0, 0, nt - j)
    const2 = lambda a: pl.BlockSpec(a.shape, lambda i, j: (0,) * a.ndim)
    return pl.pallas_call(
        _hgrn_kernel,
        grid=(b, nt),
        in_specs=[
            pl.BlockSpec((None, TM, 2 * hw), lambda i, j: (i, fwd(i, j), 0)),
            pl.BlockSpec((None, TM, hw), lambda i, j: (i, fwd(i, j), 0)),
            pl.BlockSpec((None, TM, hw), lambda i, j: (i, fwd(i, j), 0)),
            pl.BlockSpec((None, TM, 2 * hw), lambda i, j: (i, bwd(i, j), 0)),
            pl.BlockSpec((None, TM, hw), lambda i, j: (i, bwd(i, j), 1)),
            pl.BlockSpec((None, TM, hw), lambda i, j: (i, bwd(i, j), 1)),
            const2(tri), const2(trij), const2(flip), const2(e), const2(bd), const2(md), const2(mo),
        ],
        out_specs=[
            pl.BlockSpec((None, TM, hw), lambda i, j: (i, fwd(i, j), 0)),
            pl.BlockSpec((None, TM, hw), lambda i, j: (i, bwd(i, j), 0)),
        ],
        out_shape=[jax.ShapeDtypeStruct((b, t, hw), BF16)] * 2,
        scratch_shapes=[
            pltpu.VMEM((2, hw, hw), F32),
            pltpu.VMEM((2 * TM, HGRN_SUB * hw), BF16),
        ],
        compiler_params=_cparams("parallel", "arbitrary"),
        name="hgrn",
    )(qv, kk, gg, qv, kk, gg, tri, trij, flip, e, bd, md, mo)


def _mix_out(x_ref, modt_ref, ng_ref, attn_ref, cbog_ref, u_ref, up_ref, un_ref, of_ref, ob_ref,
             hgg_ref, cw_ref, hn_ref, w_ref, *, n_tiles):
    rows = x_ref.shape[0]
    nsub = rows // TM
    cw = 2 * LANES
    u = u_ref[...]
    row = lax.broadcasted_iota(jnp.int32, (rows, cw), 0)
    u_prev = jnp.where(row == 0, up_ref[7:8, :], pltpu.roll(u, 1, 0))
    u_next = jnp.where(row == rows - 1, un_ref[0:1, :], pltpu.roll(u, rows - 1, 0))
    for s in range(nsub):
        t = (pl.program_id(0) * nsub + s) % n_tiles
        u_prev = jnp.where(t >= 2, u_prev, jnp.where(row == s * TM, 0.0, u_prev))
        u_next = jnp.where((t >= 1) & (t < n_tiles - 1), u_next, jnp.where(row == s * TM + TM - 1, 0.0, u_next))
    cbog = cbog_ref[...].astype(F32)
    conv = cbog[:, :cw] * (cw_ref[0:1, :] * u_prev + cw_ref[1:2, :] * u + cw_ref[2:3, :] * u_next)

    o = of_ref[...].astype(F32) + ob_ref[...].astype(F32)
    ms = _dot((o * o).astype(BF16), hn_ref[...])
    og = cbog[:, cw:]
    hg = o * lax.rsqrt(ms + EPS) * hgg_ref[...] * (og * _sigmoid(og))

    y = (_dot(attn_ref[...], w_ref[0:2 * cw, :]) + _dot(conv.astype(BF16), w_ref[2 * cw:3 * cw, :])
         + _dot(hg.astype(BF16), w_ref[3 * cw:4 * cw, :]))
    return x_ref[...] + _mod_rows(modt_ref, 2, rows) * _rms(y, ng_ref[1:2, :])


def _mod_rows(modt_ref, i, rows):
    sub = rows // modt_ref.shape[0]
    return jnp.concatenate(
        [jnp.broadcast_to(modt_ref[s, i:i + 1, :], (sub, modt_ref.shape[2])) for s in range(modt_ref.shape[0])],
        axis=0)


def _store_split(ref, val):
    w = ref.shape[2]
    for c in range(ref.shape[0]):
        ref[c] = val[:, w * c:w * (c + 1)]


def _load_split(ref):
    return jnp.concatenate([ref[c] for c in range(ref.shape[0])], axis=1)


def _split_spec(rows):
    return lambda d: pl.BlockSpec((GATHER_SPLIT, rows, d // GATHER_SPLIT), lambda i, *_: (0, i, 0))


def _ffn_body(x, modt_ref, ng_ref, wg_ref, wu_ref, wd_ref):
    rows = x.shape[0]
    h = (_rms(x, ng_ref[2:3, :]) * (1.0 + _mod_rows(modt_ref, 4, rows)) + _mod_rows(modt_ref, 3, rows)).astype(BF16)
    acc = jnp.zeros(x.shape, F32)
    for f0 in range(0, wg_ref.shape[1], FFN_TF):
        g = _dot(h, wg_ref[:, f0:f0 + FFN_TF])
        up = _dot(h, wu_ref[:, f0:f0 + FFN_TF])
        acc = acc + _dot((g * _sigmoid(g) * up).astype(BF16), wd_ref[f0:f0 + FFN_TF, :])
    return x + _mod_rows(modt_ref, 5, rows) * _rms(acc, ng_ref[3:4, :])


def _router_body(x, modt_ref, ng_ref, wr_ref, h_ref, r_ref, rt_ref):
    rows = x.shape[0]
    h = _rms(x, ng_ref[2:3, :]) * (1.0 + _mod_rows(modt_ref, 4, rows)) + _mod_rows(modt_ref, 3, rows)
    _store_split(h_ref, h)
    h_hi = h.astype(BF16)
    h_lo = (h - h_hi.astype(F32)).astype(BF16)
    logits = _dot(h_hi, wr_ref[0]) + _dot(h_lo, wr_ref[0]) + _dot(h_hi, wr_ref[1])
    lane = lax.broadcasted_iota(jnp.int32, logits.shape, 1)
    lg = jnp.where(lane < N_EXPERTS, logits, NEG_BIG)
    m1 = jnp.max(lg, axis=-1, keepdims=True)
    i1 = jnp.min(jnp.where(lg == m1, lane, LANES), axis=-1, keepdims=True)
    lg2 = jnp.where(lane == i1, NEG_BIG, lg)
    m2 = jnp.max(lg2, axis=-1, keepdims=True)
    i2 = jnp.min(jnp.where(lg2 == m2, lane, LANES), axis=-1, keepdims=True)
    e2 = jnp.exp(m2 - m1)
    w1 = 1.0 / (1.0 + e2)
    w2 = e2 / (1.0 + e2)
    r = jnp.where(lane == 0, i1.astype(F32),
                  jnp.where(lane == 1, i2.astype(F32), jnp.where(lane == 2, w1, jnp.where(lane == 3, w2, 0.0))))
    r_ref[...] = r
    rt_ref[...] = r.T[0:8, :]


def _mixer_kernel(*refs, fused, mode, n_tiles):
    x_ref, modt_ref, ng_ref = refs[:3]
    pos = 3
    if fused:
        x = _mix_out(x_ref, modt_ref, ng_ref, *refs[pos:pos + N_MIX_REFS], n_tiles=n_tiles)
        pos += N_MIX_REFS
    else:
        x = x_ref[...]
    if mode == "ffn":
        wg_ref, wu_ref, wd_ref, o_ref = refs[pos:]
        o_ref[...] = _ffn_body(x, modt_ref, ng_ref, wg_ref, wu_ref, wd_ref)
    elif mode == "router":
        wr_ref, *outs = refs[pos:]
        _router_body(x, modt_ref, ng_ref, wr_ref, *outs)
    else:
        refs[pos][...] = x


def _mixer_call(x_flat, modt, ng, mix, mode, weights, n_tiles=0):
    n, d = x_flat.shape
    nsub = FFN_TM // TM
    cw = 2 * LANES
    tok = lambda w: pl.BlockSpec((FFN_TM, w), lambda i: (i, 0))
    resident = lambda a: pl.BlockSpec(a.shape, lambda i: (0,) * a.ndim, pipeline_mode=pl.Buffered(1))
    stream = jax.ShapeDtypeStruct((n, d), F32)
    args = [x_flat, modt, ng]
    in_specs = [tok(d), pl.BlockSpec((nsub, N_MOD, d), lambda i: (i, 0, 0)), pl.BlockSpec((8, d), lambda i: (0, 0))]
    fused = mix is not None
    if fused:
        attn, cbog, u, o_f, o_b, hgg, conv_w, hn, w_out = mix
        nb8 = n // 8
        args += [attn, cbog, u, u, u, o_f, o_b, hgg, conv_w, hn, w_out]
        in_specs += [
            tok(2 * cw), tok(2 * cw), tok(cw),
            pl.BlockSpec((8, cw), lambda i: (jnp.maximum(i * (FFN_TM // 8) - 1, 0), 0)),
            pl.BlockSpec((8, cw), lambda i: (jnp.minimum((i + 1) * (FFN_TM // 8), nb8 - 1), 0)),
            tok(cw), tok(cw), resident(hgg), resident(conv_w), resident(hn), resident(w_out),
        ]
        assert len(in_specs) == 3 + N_MIX_REFS
    args += list(weights)
    in_specs += [resident(w) for w in weights]
    if mode == "router":
        out_specs = [_split_spec(FFN_TM)(d), tok(LANES), pl.BlockSpec((8, FFN_TM), lambda i: (0, i))]
        out_shape = [jax.ShapeDtypeStruct((GATHER_SPLIT, n, d // GATHER_SPLIT), F32),
                     jax.ShapeDtypeStruct((n, LANES), F32), jax.ShapeDtypeStruct((8, n), F32)]
    else:
        out_specs, out_shape = tok(d), stream
    return pl.pallas_call(
        functools.partial(_mixer_kernel, fused=fused, mode=mode, n_tiles=n_tiles),
        grid=(n // FFN_TM,),
        in_specs=in_specs,
        out_specs=out_specs,
        out_shape=out_shape,
        compiler_params=_cparams("parallel"),
        name=("mix_" if fused else "") + mode,
    )(*args)


def _gather_flat(data, idx):
    m = idx.shape[0]
    w = data.shape[1]
    mesh = plsc.VectorSubcoreMesh(core_axis_name="core", subcore_axis_name="subcore")

    @functools.partial(pl.kernel, out_type=jax.ShapeDtypeStruct((m, w), data.dtype), mesh=mesh)
    def gather(x_hbm, i_hbm, o_hbm):
        def body(i_vmem, o_vmem):
            pltpu.sync_copy(x_hbm.at[i_vmem.at[0]], o_vmem)

        pltpu.emit_pipeline(
            body,
            grid=(m // GATHER_WIN,),
            in_specs=[pl.BlockSpec((1, GATHER_WIN), lambda i: (0, i))],
            out_specs=[pl.BlockSpec((GATHER_WIN, w), lambda i: (i, 0))],
            core_axis_name=("core", "subcore"),
            dimension_semantics=(pltpu.PARALLEL,),
        )(i_hbm, o_hbm)

    return gather(data, idx.reshape(1, m))


def _scatter_flat(rows, idx, n_out):
    m, w = rows.shape
    mesh = plsc.VectorSubcoreMesh(core_axis_name="core", subcore_axis_name="subcore")

    @functools.partial(pl.kernel, out_type=jax.ShapeDtypeStruct((n_out, w), rows.dtype), mesh=mesh)
    def scatter(x_hbm, i_hbm, o_hbm):
        def body(x_vmem, i_vmem):
            pltpu.sync_copy(x_vmem, o_hbm.at[i_vmem.at[0]])

        pltpu.emit_pipeline(
            body,
            grid=(m // GATHER_WIN,),
            in_specs=[pl.BlockSpec((GATHER_WIN, w), lambda i: (i, 0)),
                      pl.BlockSpec((1, GATHER_WIN), lambda i: (0, i))],
            out_specs=[],
            core_axis_name=("core", "subcore"),
            dimension_semantics=(pltpu.PARALLEL,),
        )(x_hbm, i_hbm)

    return scatter(rows, idx.reshape(1, m))


def _col0_kernel(x_ref, o_ref):
    o_ref[...] = x_ref[...].T[0:1, :]


def _first_column(rows):
    n = rows.shape[0]
    blk = 1024
    return pl.pallas_call(
        _col0_kernel,
        grid=(n // blk,),
        in_specs=[pl.BlockSpec((blk, LANES), lambda i: (i, 0))],
        out_specs=pl.BlockSpec((1, blk), lambda i: (0, i)),
        out_shape=jax.ShapeDtypeStruct((1, n), rows.dtype),
        compiler_params=_cparams("parallel"),
        name="col0",
    )(rows)


def _cast_kernel(*refs):
    o_ref = refs[-1]
    cw = refs[0].shape[1]
    for s, x_ref in enumerate(refs[:-1]):
        o_ref[:, cw * s:cw * (s + 1)] = x_ref[...].astype(o_ref.dtype)


def _cast_bf16(w, layer):
    _, e, r, c = w.shape
    tr = 256
    cw = c // CAST_STREAMS
    chunk = lambda s: pl.BlockSpec((None, None, tr, cw), lambda i, j: (layer, i, j, s))
    return pl.pallas_call(
        _cast_kernel,
        grid=(e, r // tr),
        in_specs=[chunk(s) for s in range(CAST_STREAMS)],
        out_specs=pl.BlockSpec((None, tr, c), lambda i, j: (i, j, 0)),
        out_shape=jax.ShapeDtypeStruct((e, r, c), BF16),
        compiler_params=_cparams("parallel", "parallel"),
        name="cast",
    )(*([w] * CAST_STREAMS))


def _gather_rows(data, idx):
    s, r, w = data.shape
    m = idx.shape[0]
    idx_all = (idx[None, :] + (jnp.arange(s, dtype=jnp.int32) * r)[:, None]).reshape(s * m)
    return _gather_flat(data.reshape(s * r, w), idx_all).reshape(s, m, w)


def _expert_kernel(be_ref, nv_ref, xs_ref, wg_ref, wu_ref, wd_ref, o_ref):
    valid = pl.program_id(0) < nv_ref[0]

    @pl.when(valid)
    def _():
        xb = _load_split(xs_ref).astype(BF16)
        acc = jnp.zeros((xb.shape[0], wd_ref.shape[1]), F32)
        for f0 in range(0, wg_ref.shape[1], MOE_TF):
            g = _dot(xb, wg_ref[:, f0:f0 + MOE_TF])
            up = _dot(xb, wu_ref[:, f0:f0 + MOE_TF])
            acc = acc + _dot((g * _sigmoid(g) * up).astype(BF16), wd_ref[f0:f0 + MOE_TF, :])
        _store_split(o_ref, acc)

    @pl.when(jnp.logical_not(valid))
    def _():
        o_ref[...] = jnp.zeros(o_ref.shape, F32)


def _expert_call(block_expert, n_valid, xs, wg, wu, wd):
    n_slots = xs.shape[1]
    d = wg.shape[1]
    dff = wg.shape[2]
    nb = n_slots // MOE_BLK
    resident = lambda r, c: pl.BlockSpec((None, r, c), lambda i, be, nv: (be[i], 0, 0), pipeline_mode=pl.Buffered(1))
    grid_spec = pltpu.PrefetchScalarGridSpec(
        num_scalar_prefetch=2,
        grid=(nb,),
        in_specs=[_split_spec(MOE_BLK)(d), resident(d, dff), resident(d, dff), resident(dff, d)],
        out_specs=_split_spec(MOE_BLK)(d),
    )
    return pl.pallas_call(
        _expert_kernel,
        grid_spec=grid_spec,
        out_shape=jax.ShapeDtypeStruct((GATHER_SPLIT, n_slots, d // GATHER_SPLIT), F32),
        compiler_params=_cparams("arbitrary"),
        name="experts",
    )(block_expert, n_valid, xs, wg, wu, wd)


def _combine_kernel(x_ref, modt_ref, ng_ref, y1_ref, y2_ref, r_ref, o_ref):
    x = x_ref[...]
    rows = x.shape[0]
    f = r_ref[:, 2:3] * _load_split(y1_ref) + r_ref[:, 3:4] * _load_split(y2_ref)
    o_ref[...] = x + _mod_rows(modt_ref, 5, rows) * _rms(f, ng_ref[3:4, :])


def _combine_call(x_flat, modt, ng, y12, rinfo):
    n, d = x_flat.shape
    nsub = FFN_TM // TM
    nblk = n // FFN_TM
    tok = lambda w: pl.BlockSpec((FFN_TM, w), lambda i: (i, 0))
    second = pl.BlockSpec((GATHER_SPLIT, FFN_TM, d // GATHER_SPLIT), lambda i: (0, i + nblk, 0))
    return pl.pallas_call(
        _combine_kernel,
        grid=(nblk,),
        in_specs=[tok(d), pl.BlockSpec((nsub, N_MOD, d), lambda i: (i, 0, 0)), pl.BlockSpec((8, d), lambda i: (0, 0)),
                  _split_spec(FFN_TM)(d), second, tok(LANES)],
        out_specs=tok(d),
        out_shape=jax.ShapeDtypeStruct((n, d), F32),
        compiler_params=_cparams("parallel"),
        name="combine",
    )(x_flat, modt, ng, y12, y12, rinfo)


def _router_weights(w_router):
    wr = jnp.pad(w_router, ((0, 0), (0, LANES - N_EXPERTS)))
    wr_hi = wr.astype(BF16)
    wr_lo = (wr - wr_hi.astype(F32)).astype(BF16)
    return jnp.stack([wr_hi, wr_lo])


def _moe_layer(x_flat, h, rinfo, rinfo_t, modt, ng, wg, wu, wd):
    n, d = x_flat.shape
    n_assign = 2 * n
    expert = jnp.concatenate([rinfo_t[0], rinfo_t[1]]).astype(jnp.int32)
    onehot = (expert[None, :] == jnp.arange(N_EXPERTS, dtype=jnp.int32)[:, None]).astype(jnp.int32)
    csum = jnp.cumsum(onehot, axis=1)
    rank = jnp.sum((csum - onehot) * onehot, axis=0)
    counts = csum[:, -1]
    padded = (counts + MOE_BLK - 1) // MOE_BLK * MOE_BLK
    pad_end = jnp.cumsum(padded)
    pad_start = pad_end - padded
    dest = jnp.sum(onehot * pad_start[:, None], axis=0) + rank
    nb = -(-n_assign // MOE_BLK) + N_EXPERTS
    row_quantum = 32 * GATHER_WIN // GATHER_SPLIT
    nb = -(-nb * MOE_BLK // row_quantum) * row_quantum // MOE_BLK
    n_slots = nb * MOE_BLK
    block_expert = jnp.minimum(
        jnp.searchsorted(pad_end, jnp.arange(nb, dtype=jnp.int32) * MOE_BLK, side="right"), N_EXPERTS - 1
    ).astype(jnp.int32)
    n_valid = (pad_end[-1:] // MOE_BLK).astype(jnp.int32)
    tok_rows = jnp.broadcast_to((jnp.arange(n_assign, dtype=jnp.int32) % n)[:, None], (n_assign, LANES))
    slot = jnp.arange(n_slots, dtype=jnp.int32)
    slot_used = (slot - jnp.repeat(pad_start[block_expert], MOE_BLK)) < jnp.repeat(counts[block_expert], MOE_BLK)
    slot_tok = jnp.where(slot_used, _first_column(_scatter_flat(tok_rows, dest, n_slots)).reshape(n_slots), 0)

    xs = _gather_rows(h, slot_tok)
    ys = _expert_call(block_expert, n_valid, xs, wg, wu, wd)
    y12 = _gather_rows(ys, dest)
    return _combine_call(x_flat, modt, ng, y12, rinfo)


def _rope_tables(n_ctx, length):
    rows = length // GRID_W
    row = jnp.repeat(jnp.arange(rows, dtype=F32), GRID_W)
    col = jnp.tile(jnp.arange(GRID_W, dtype=F32), rows)
    axis_dim = HEAD_DIM // 2
    inv_freq = ROPE_THETA ** (-jnp.arange(0, axis_dim, 2, dtype=F32) / axis_dim)
    ar = row[:, None] * inv_freq
    ac = col[:, None] * inv_freq
    cos = jnp.concatenate([jnp.cos(ar), jnp.cos(ar), jnp.cos(ac), jnp.cos(ac)], axis=1)
    sin = jnp.concatenate([-jnp.sin(ar), jnp.sin(ar), -jnp.sin(ac), jnp.sin(ac)], axis=1)
    cos = jnp.concatenate([jnp.ones((n_ctx, HEAD_DIM), F32), cos], axis=0)
    sin = jnp.concatenate([jnp.zeros((n_ctx, HEAD_DIM), F32), sin], axis=0)
    return jnp.tile(cos, (1, 2)), jnp.tile(sin, (1, 2))


def _inproj_weights(w_in_l):
    d = w_in_l.shape[0]
    aw = d // 2
    nh = aw // HEAD_DIM
    wq = w_in_l[:, :aw].reshape(d, nh, HEAD_DIM)
    z = jnp.zeros((d, nh // 2, HEAD_DIM), w_in_l.dtype)
    q_pad = jnp.concatenate([
        jnp.concatenate([wq[:, :nh // 2], z], axis=2),
        jnp.concatenate([z, wq[:, nh // 2:]], axis=2),
    ], axis=1).reshape(d, nh * LANES)
    return jnp.concatenate([q_pad, w_in_l[:, aw:]], axis=1).astype(BF16)


def kernel(x, c, ctx, c_ctx, ada_w, ada_b, norm_g, w_in, w_out, attn_q_g, attn_k_g, conv_w, hgrn_lb, hgrn_g,
           ffn_w_gate, ffn_w_up, ffn_w_down, moe_router, moe_w_gate, moe_w_up, moe_w_down):
    bsz, length, d = x.shape
    n_ctx = ctx.shape[1]
    depth = w_in.shape[0]
    t = n_ctx + length
    nt = t // TM
    assert n_ctx == TM and length % TM == 0 and d == 8 * LANES
    for n_tok in (bsz * t, bsz * length):
        assert n_tok % FFN_TM == 0 and (2 * n_tok) % (32 * GATHER_WIN) == 0

    r_pad = -(-(bsz + 1) // 8) * 8
    c_all = jnp.concatenate([c, c_ctx[None, :], jnp.zeros((r_pad - bsz - 1, d), F32)], axis=0)
    mod_all = _ada_call(c_all, ada_w, ada_b).reshape(depth, r_pad, N_MOD, d)

    cos_t, sin_t = _rope_tables(n_ctx, length)
    lb_all = jnp.cumsum(jax.nn.softmax(hgrn_lb.astype(F32), axis=0), axis=0)
    lb_all = lb_all - lb_all[0]
    hconsts = _hgrn_consts()
    cw = 2 * LANES
    cidx = jnp.arange(cw)
    head_mean = ((cidx[:, None] // HEAD_DIM == cidx[None, :] // HEAD_DIM).astype(F32) / HEAD_DIM).astype(BF16)

    x_all = jnp.concatenate([ctx, x], axis=1)
    for l in range(depth):
        mod_lat = mod_all[l, :bsz]
        mod_ctx = jnp.broadcast_to(mod_all[l, bsz][None], (bsz, N_MOD, d))
        mod = jnp.stack([mod_ctx, mod_lat], axis=1)
        modt = jnp.concatenate([mod[:, :1], jnp.broadcast_to(mod[:, 1:], (bsz, nt - 1, N_MOD, d))], axis=1)
        modt = modt.reshape(bsz * nt, N_MOD, d)
        ng = jnp.pad(norm_g[l], ((0, 4), (0, 0)))
        lb = lb_all[l].reshape(1, 2 * cw)
        gc = jnp.concatenate([jnp.log(lb), jnp.log1p(-lb), 1.0 - lb, jnp.zeros((5, 2 * cw), F32)], axis=0)
        qg = jnp.tile(attn_q_g[l], 2)[None, :]
        kg = jnp.tile(attn_k_g[l], 2)[None, :]

        q, k, vt, cbog, u, qv, kk, gg = _inproj_call(x_all, mod, ng, _inproj_weights(w_in[l]), cos_t, sin_t, qg, kg, gc)
        attn = _attn_call(q, k, vt)
        o_f, o_b = _hgrn_call(qv, kk, gg, hconsts)
        n_all = bsz * t
        flat = lambda a: a.reshape(n_all, a.shape[-1])
        mix = (flat(attn), flat(cbog), flat(u), flat(o_f), flat(o_b), hgrn_g[l][None, :],
               jnp.pad(conv_w[l], ((0, 5), (0, 0))), head_mean, w_out[l].astype(BF16))
        x_flat = flat(x_all)
        if l % 2 == 0:
            weights = (ffn_w_gate[l // 2].astype(BF16), ffn_w_up[l // 2].astype(BF16), ffn_w_down[l // 2].astype(BF16))
        else:
            weights = (_router_weights(moe_router[l // 2]),)
            experts = (_cast_bf16(moe_w_gate, l // 2), _cast_bf16(moe_w_up, l // 2), _cast_bf16(moe_w_down, l // 2))
        mode = "ffn" if l % 2 == 0 else "router"

        if l < depth - 1:
            if mode == "ffn":
                x_flat = _mixer_call(x_flat, modt, ng, mix, mode, weights, nt)
            else:
                x_flat = _mixer_call(x_flat, modt, ng, mix, "plain", (), nt)
                h, rinfo, rinfo_t = _mixer_call(x_flat, modt, ng, None, mode, weights)
                x_flat = _moe_layer(x_flat, h, rinfo, rinfo_t, modt, ng, *experts)
            x_all = x_flat.reshape(bsz, t, d)
            continue

        x_all = _mixer_call(x_flat, modt, ng, mix, "plain", (), nt).reshape(bsz, t, d)
        x_flat = x_all[:, n_ctx:, :].reshape(bsz * length, d)
        modt = jnp.broadcast_to(mod[:, 1:], (bsz, length // TM, N_MOD, d)).reshape(bsz * (length // TM), N_MOD, d)
        if mode == "ffn":
            x_flat = _mixer_call(x_flat, modt, ng, None, mode, weights)
        else:
            h, rinfo, rinfo_t = _mixer_call(x_flat, modt, ng, None, mode, weights)
            x_flat = _moe_layer(x_flat, h, rinfo, rinfo_t, modt, ng, *experts)
        return x_flat.reshape(bsz, length, d)
```

```python
import functools

import jax
import jax.numpy as jnp
from jax import lax
from jax.experimental import pallas as pl
from jax.experimental.pallas import tpu as pltpu
from jax.experimental.pallas import tpu_sc as plsc

F32 = jnp.float32
BF16 = jnp.bfloat16

HEAD_DIM = 64
GRID_W = 64
ROPE_THETA = 10000.0
ATTN_SCALE = HEAD_DIM ** -0.5
LOG2_E = 1.4426950408889634
HGRN_CHUNK = 64
HGRN_SUB = 16
VT_ROWS = HEAD_DIM + 16
ATTN_AHEAD = 6
N_EXPERTS = 8
N_MOD = 6
EPS = 1e-6

LANES = 128
TM = 256
FFN_TM = 512
FFN_TF = 256
N_MIX_REFS = 11
CAST_STREAMS = 4
MOE_BLK = 512
MOE_TF = 512
GATHER_WIN = 128
GATHER_SPLIT = 4
NEG_BIG = -1e30
VMEM_LIMIT = 56 * 1024 * 1024


def _cparams(*sem):
    return pltpu.CompilerParams(dimension_semantics=sem, vmem_limit_bytes=VMEM_LIMIT)


def _sigmoid(z):
    return 1.0 / (1.0 + jnp.exp(-z))


def _dot(a, b):
    return jnp.dot(a, b, preferred_element_type=F32)


def _dot_nt(a, b):
    return lax.dot_general(a, b, (((1,), (1,)), ((), ())), preferred_element_type=F32)


def _dot_tn(a, b):
    return lax.dot_general(a, b, (((0,), (0,)), ((), ())), preferred_element_type=F32)


def _rms(x, g):
    return x * lax.rsqrt(jnp.mean(x * x, axis=-1, keepdims=True) + EPS) * g


def _ada_kernel(c_ref, w_ref, b_ref, o_ref):
    c = c_ref[...]
    s = (c * _sigmoid(c)).astype(BF16)
    o_ref[...] = _dot(s, w_ref[...].astype(BF16)) + b_ref[...]


def _ada_call(c_all, ada_w, ada_b):
    depth, d, n = ada_w.shape
    r = c_all.shape[0]
    tn = 512
    return pl.pallas_call(
        _ada_kernel,
        grid=(depth, n // tn),
        in_specs=[
            pl.BlockSpec((r, d), lambda l, j: (0, 0)),
            pl.BlockSpec((None, d, tn), lambda l, j: (l, 0, j)),
            pl.BlockSpec((None, 1, tn), lambda l, j: (l, 0, j)),
        ],
        out_specs=pl.BlockSpec((None, r, tn), lambda l, j: (l, 0, j)),
        out_shape=jax.ShapeDtypeStruct((depth, r, n), F32),
        compiler_params=_cparams("parallel", "parallel"),
        name="ada",
    )(c_all, ada_w, ada_b.reshape(depth, 1, n))


def _inproj_kernel(x_ref, modt_ref, ng_ref, w_ref, cos_ref, sin_ref, qg_ref, kg_ref, gc_ref,
                   q_ref, k_ref, vt_ref, cbog_ref, u_ref, qv_ref, kk_ref, gg_ref):
    x = x_ref[...]
    rows = x.shape[0]
    h = _rms(x, ng_ref[0:1, :]) * (1.0 + _mod_rows(modt_ref, 1, rows)) + _mod_rows(modt_ref, 0, rows)
    hb = h.astype(BF16)

    def proj(c0, n):
        return _dot(hb, w_ref[:, c0:c0 + n])

    cos = cos_ref[...]
    sin = sin_ref[...]
    lane = lax.broadcasted_iota(jnp.int32, (rows, LANES), 1)
    first_half = (lane % 32) < 16
    lo = lane < HEAD_DIM

    def norm_rope(blk, g):
        b2 = blk * blk
        s_lo = jnp.sum(jnp.where(lo, b2, 0.0), axis=-1, keepdims=True)
        s_hi = jnp.sum(jnp.where(lo, 0.0, b2), axis=-1, keepdims=True)
        v = blk * lax.rsqrt(jnp.where(lo, s_lo, s_hi) * (1.0 / HEAD_DIM) + EPS) * g
        partner = jnp.where(first_half, pltpu.roll(v, LANES - 16, 1), pltpu.roll(v, 16, 1))
        return v * cos + partner * sin

    qa = proj(0, 4 * LANES)
    for j in range(4):
        qn = norm_rope(qa[:, LANES * j:LANES * (j + 1)], qg_ref[...]) * (ATTN_SCALE * LOG2_E)
        sw = pltpu.roll(qn, HEAD_DIM, 1)
        if j < 2:
            q_ref[2 * j] = jnp.where(lo, qn, 0.0).astype(BF16)
            q_ref[2 * j + 1] = jnp.where(lo, sw, 0.0).astype(BF16)
        else:
            q_ref[2 * j] = jnp.where(lo, 0.0, sw).astype(BF16)
            q_ref[2 * j + 1] = jnp.where(lo, 0.0, qn).astype(BF16)

    kv = proj(4 * LANES, 2 * LANES)
    k_ref[...] = norm_rope(kv[:, :LANES], kg_ref[...]).astype(BF16)
    vt = kv[:, LANES:].T
    ones = jnp.ones((VT_ROWS - HEAD_DIM, TM), F32)
    for s in range(rows // TM):
        for g in range(2):
            vt_ref[g, s] = jnp.concatenate(
                [vt[HEAD_DIM * g:HEAD_DIM * (g + 1), TM * s:TM * (s + 1)], ones], axis=0).astype(BF16)

    c0 = 6 * LANES
    c3 = proj(c0, 6 * LANES)
    cw = 2 * LANES
    u_ref[...] = c3[:, cw:2 * cw] * c3[:, 2 * cw:3 * cw]

    hg = proj(c0 + 6 * LANES, 10 * LANES)
    zf, zb, iv, hq, og = (hg[:, cw * i:cw * (i + 1)] for i in range(5))
    cbog_ref[:, :cw] = c3[:, :cw].astype(BF16)
    cbog_ref[:, cw:] = og.astype(BF16)
    qv_ref[:, :cw] = (hq * _sigmoid(hq)).astype(BF16)
    qv_ref[:, cw:] = iv.astype(BF16)
    for d, z in enumerate((zf, zb)):
        log_lb = gc_ref[0:1, cw * d:cw * (d + 1)]
        log1m_lb = gc_ref[1:2, cw * d:cw * (d + 1)]
        one_m_lb = gc_ref[2:3, cw * d:cw * (d + 1)]
        t = jnp.exp(-jnp.abs(z))
        log_sig = jnp.minimum(z, 0.0) - jnp.log1p(t)
        a2 = log1m_lb + log_sig
        log_f = jnp.maximum(log_lb, a2) + jnp.log1p(jnp.exp(-jnp.abs(log_lb - a2)))
        sig_neg = jnp.where(z >= 0, t, 1.0) / (1.0 + t)
        gg_ref[:, cw * d:cw * (d + 1)] = log_f
        kk_ref[:, cw * d:cw * (d + 1)] = (one_m_lb * sig_neg).astype(BF16)


def _inproj_call(x_flat, modt, ng, w1, cos_f, sin_f, qg, kg, gc):
    n, d = x_flat.shape
    nsub = FFN_TM // TM
    tok = lambda w: pl.BlockSpec((FFN_TM, w), lambda i: (i, 0))
    const = lambda a: pl.BlockSpec(a.shape, lambda i: (0,) * a.ndim, pipeline_mode=pl.Buffered(1))
    return pl.pallas_call(
        _inproj_kernel,
        grid=(n // FFN_TM,),
        in_specs=[
            tok(d),
            pl.BlockSpec((nsub, N_MOD, d), lambda i: (i, 0, 0)),
            const(ng), const(w1), tok(LANES), tok(LANES), const(qg), const(kg), const(gc),
        ],
        out_specs=[
            pl.BlockSpec((8, FFN_TM, LANES), lambda i: (0, i, 0)),
            tok(LANES),
            pl.BlockSpec((2, nsub, VT_ROWS, TM), lambda i: (0, i, 0, 0)),
            tok(4 * LANES),
            tok(2 * LANES),
            tok(4 * LANES),
            tok(4 * LANES),
            tok(4 * LANES),
        ],
        out_shape=[
            jax.ShapeDtypeStruct((8, n, LANES), BF16),
            jax.ShapeDtypeStruct((n, LANES), BF16),
            jax.ShapeDtypeStruct((2, n // TM, VT_ROWS, TM), BF16),
            jax.ShapeDtypeStruct((n, 4 * LANES), BF16),
            jax.ShapeDtypeStruct((n, 2 * LANES), F32),
            jax.ShapeDtypeStruct((n, 4 * LANES), BF16),
            jax.ShapeDtypeStruct((n, 4 * LANES), BF16),
            jax.ShapeDtypeStruct((n, 4 * LANES), F32),
        ],
        compiler_params=_cparams("parallel"),
        name="inproj",
    )(x_flat, modt, ng, w1, cos_f, sin_f, qg, kg, gc)


def _attn_kernel(q_ref, k_ref, vt_ref, o_ref, m_ref, acc_ref, *, n_kv):
    def score(j, h):
        return _dot_nt(k_ref[TM * j:TM * (j + 1), :], q_ref[h])

    def attend(n_tiles):
        steps = [(j, h) for j in range(n_tiles) for h in range(8)]
        scores = [score(j, h) for j, h in steps[:ATTN_AHEAD]]
        for i, (j, h) in enumerate(steps):
            if i + ATTN_AHEAD < len(steps):
                scores.append(score(*steps[i + ATTN_AHEAD]))
            s = scores.pop(0)
            m_tile = jnp.max(s, axis=0, keepdims=True)
            if j == 0:
                m_new = m_tile
                acc_ref[h] = _dot(vt_ref[h // 4, j], jnp.exp2(s - m_new).astype(BF16))
            else:
                m_old = m_ref[h]
                m_new = jnp.maximum(m_old, m_tile)
                pv = _dot(vt_ref[h // 4, j], jnp.exp2(s - m_new).astype(BF16))
                acc_ref[h] = jnp.exp2(m_old - m_new) * acc_ref[h] + pv
            m_ref[h] = m_new
        outs = [acc_ref[h, :HEAD_DIM, :] / acc_ref[h, HEAD_DIM:HEAD_DIM + 1, :] for h in range(8)]
        o_ref[...] = jnp.concatenate(outs, axis=0).T.astype(BF16)

    @pl.when(pl.program_id(1) == 0)
    def _():
        attend(1)

    @pl.when(pl.program_id(1) > 0)
    def _():
        attend(n_kv)


def _attn_call(q, k, vt, b):
    n = k.shape[0]
    t = n // b
    nt = t // TM
    return pl.pallas_call(
        functools.partial(_attn_kernel, n_kv=nt),
        grid=(b, nt),
        in_specs=[
            pl.BlockSpec((8, TM, LANES), lambda i, j: (0, i * nt + j, 0)),
            pl.BlockSpec((t, LANES), lambda i, j: (i, 0)),
            pl.BlockSpec((2, nt, VT_ROWS, TM), lambda i, j: (0, i, 0, 0)),
        ],
        out_specs=pl.BlockSpec((TM, 4 * LANES), lambda i, j: (i * nt + j, 0)),
        out_shape=jax.ShapeDtypeStruct((n, 4 * LANES), BF16),
        scratch_shapes=[pltpu.VMEM((8, 1, TM), F32), pltpu.VMEM((8, VT_ROWS, TM), F32)],
        compiler_params=_cparams("parallel", "parallel"),
        name="attn",
    )(q, k, vt)


def _split3(g):
    hi = g.astype(BF16)
    r1 = g - hi.astype(F32)
    mid = r1.astype(BF16)
    lo = (r1 - mid.astype(F32)).astype(BF16)
    return hi, mid, lo


def _hgrn_kernel(qvf_ref, kf_ref, gf_ref, qvb_ref, kb_ref, gb_ref, tri_ref, trij_ref, j_ref, e_ref, bd_ref,
                 md_ref, mo_ref, of_ref, ob_ref, st_ref, x_s):
    hw = 2 * LANES
    nch = TM // HGRN_CHUNK

    @pl.when(pl.program_id(1) == 0)
    def _():
        st_ref[...] = jnp.zeros(st_ref.shape, F32)

    flip = j_ref[...]

    def cumsum(m_ref, g):
        m = m_ref[...]
        hi, mid, lo = _split3(g)
        return _dot(m, hi) + _dot(m, mid) + _dot(m, lo)

    qvf = qvf_ref[...]
    fl = _dot(flip, jnp.concatenate([qvb_ref[...], kb_ref[...]], axis=1))
    q = jnp.concatenate([qvf[:, :hw].astype(F32), fl[:, :hw]], axis=0)
    v = jnp.concatenate([qvf[:, hw:].astype(F32), fl[:, hw:2 * hw]], axis=0)
    k = jnp.concatenate([kf_ref[...].astype(F32), fl[:, 2 * hw:]], axis=0)
    b = jnp.concatenate([cumsum(tri_ref, gf_ref[...]), cumsum(trij_ref, gb_ref[...])], axis=0) * LOG2_E

    ngr = 2 * TM // HGRN_SUB
    nsub = HGRN_CHUNK // HGRN_SUB
    q3 = q.reshape(ngr, HGRN_SUB, hw)
    b3 = b.reshape(ngr, HGRN_SUB, hw)
    k3 = k.reshape(ngr, HGRN_SUB, hw)
    for s in range(HGRN_SUB):
        x = q3 * jnp.exp2(jnp.minimum(b3 - b3[:, s:s + 1, :], 0.0)) * k3[:, s:s + 1, :]
        x_s[:, hw * s:hw * (s + 1)] = x.reshape(2 * TM, hw).astype(BF16)
    a = _dot(x_s[...], e_ref[...])

    bd = bd_ref[...]
    bd16 = bd.astype(BF16)
    md = md_ref[...]
    mo = mo_ref[...]
    for d in range(2):
        st = st_ref[d]
        outs = []
        for c in range(nch):
            r0 = TM * d + HGRN_CHUNK * c
            bc = b[r0:r0 + HGRN_CHUNK]
            qc = q[r0:r0 + HGRN_CHUNK]
            kc = k[r0:r0 + HGRN_CHUNK]
            vc = v[r0:r0 + HGRN_CHUNK]
            ac = a[r0:r0 + HGRN_CHUNK] * md
            r_end = [bc[HGRN_SUB * (j + 1) - 1:HGRN_SUB * (j + 1), :] for j in range(nsub)]
            r_own = jnp.concatenate([jnp.broadcast_to(r, (HGRN_SUB, hw)) for r in r_end], axis=0)
            kt_bd = jnp.concatenate([(kc * jnp.exp2(r_own - bc)).astype(BF16)] * 4, axis=0) * bd16
            q_rel = jnp.concatenate(
                [qc * jnp.exp2(jnp.minimum(bc - r_end[j], 0.0)) for j in range(nsub - 1)], axis=0).astype(BF16)
            p = _dot_nt(q_rel, kt_bd) * mo
            for j in range(nsub - 1):
                ac = ac + p[HGRN_CHUNK * j:HGRN_CHUNK * (j + 1)]
            bl = r_end[-1]
            qe = (qc * jnp.exp2(bc)).astype(BF16)
            ke = (kc * jnp.exp2(bl - bc)).astype(BF16)
            vcb = vc.astype(BF16)
            vbd = jnp.concatenate([vcb] * 4, axis=0) * bd16
            outs.append(_dot(ac.astype(BF16), vbd) + _dot_nt(qe, st.astype(BF16)))
            st = st * jnp.exp2(bl) + _dot_tn(vcb, ke) * bd
        st_ref[d] = st
        od = jnp.concatenate(outs, axis=0).astype(BF16)
        if d == 0:
            of_ref[...] = od
        else:
            ob_ref[...] = _dot(flip, od).astype(BF16)


def _hgrn_consts():
    hw = 2 * LANES
    r = jnp.arange(TM)
    same_chunk = (r[:, None] // HGRN_CHUNK) == (r[None, :] // HGRN_CHUNK)
    tri = (same_chunk & (r[None, :] <= r[:, None])).astype(F32)
    flip = (r[:, None] + r[None, :] == TM - 1).astype(F32)
    trij = tri @ flip
    c = jnp.arange(hw)
    s = jnp.arange(HGRN_SUB)
    e = ((c[None, :, None] // HEAD_DIM == c[None, None, :] // HGRN_CHUNK)
         & (c[None, None, :] % HGRN_SUB == s[:, None, None])).astype(BF16).reshape(HGRN_SUB * hw, hw)
    bd = (c[:, None] // HEAD_DIM == c[None, :] // HEAD_DIM).astype(F32)
    tt = jnp.arange(HGRN_CHUNK)[:, None]
    ss = (c % HGRN_CHUNK)[None, :]
    md = ((ss // HGRN_SUB == tt // HGRN_SUB) & (ss <= tt)).astype(F32)
    mo = jnp.concatenate([((ss // HGRN_SUB == j) & (tt // HGRN_SUB > j)).astype(F32)
                          for j in range(HGRN_CHUNK // HGRN_SUB - 1)], axis=0)
    return tri.astype(BF16), trij.astype(BF16), flip.astype(BF16), e, bd, md, mo


def _hgrn_call(qv, kk, gg, consts, b):
    n = qv.shape[0]
    nt = n // b // TM
    hw = 2 * LANES
    tri, trij, flip, e, bd, md, mo = consts
    fwd = lambda i, j: i * nt + j
    bwd = lambda i, j: i * nt + jnp.where(j == 0, 0, nt - j)
    const2 = lambda a: pl.BlockSpec(a.shape, lambda i, j: (0,) * a.ndim)
    return pl.pallas_call(
        _hgrn_kernel,
        grid=(b, nt),
        in_specs=[
            pl.BlockSpec((TM, 2 * hw), lambda i, j: (fwd(i, j), 0)),
            pl.BlockSpec((TM, hw), lambda i, j: (fwd(i, j), 0)),
            pl.BlockSpec((TM, hw), lambda i, j: (fwd(i, j), 0)),
            pl.BlockSpec((TM, 2 * hw), lambda i, j: (bwd(i, j), 0)),
            pl.BlockSpec((TM, hw), lambda i, j: (bwd(i, j), 1)),
            pl.BlockSpec((TM, hw), lambda i, j: (bwd(i, j), 1)),
            const2(tri), const2(trij), const2(flip), const2(e), const2(bd), const2(md), const2(mo),
        ],
        out_specs=[
            pl.BlockSpec((TM, hw), lambda i, j: (fwd(i, j), 0)),
            pl.BlockSpec((TM, hw), lambda i, j: (bwd(i, j), 0)),
        ],
        out_shape=[jax.ShapeDtypeStruct((n, hw), BF16)] * 2,
        scratch_shapes=[
            pltpu.VMEM((2, hw, hw), F32),
            pltpu.VMEM((2 * TM, HGRN_SUB * hw), BF16),
        ],
        compiler_params=_cparams("parallel", "arbitrary"),
        name="hgrn",
    )(qv, kk, gg, qv, kk, gg, tri, trij, flip, e, bd, md, mo)


def _mix_out(x_ref, modt_ref, ng_ref, attn_ref, cbog_ref, u_ref, up_ref, un_ref, of_ref, ob_ref,
             hgg_ref, cw_ref, hn_ref, w_ref, *, n_tiles):
    rows = x_ref.shape[0]
    nsub = rows // TM
    cw = 2 * LANES
    u = u_ref[...]
    row = lax.broadcasted_iota(jnp.int32, (rows, cw), 0)
    u_prev = jnp.where(row == 0, up_ref[7:8, :], pltpu.roll(u, 1, 0))
    u_next = jnp.where(row == rows - 1, un_ref[0:1, :], pltpu.roll(u, rows - 1, 0))
    for s in range(nsub):
        t = (pl.program_id(0) * nsub + s) % n_tiles
        u_prev = jnp.where(t >= 2, u_prev, jnp.where(row == s * TM, 0.0, u_prev))
        u_next = jnp.where((t >= 1) & (t < n_tiles - 1), u_next, jnp.where(row == s * TM + TM - 1, 0.0, u_next))
    cbog = cbog_ref[...].astype(F32)
    conv = cbog[:, :cw] * (cw_ref[0:1, :] * u_prev + cw_ref[1:2, :] * u + cw_ref[2:3, :] * u_next)

    o = of_ref[...].astype(F32) + ob_ref[...].astype(F32)
    ms = _dot((o * o).astype(BF16), hn_ref[...])
    og = cbog[:, cw:]
    hg = o * lax.rsqrt(ms + EPS) * hgg_ref[...] * (og * _sigmoid(og))

    y = (_dot(attn_ref[...], w_ref[0:2 * cw, :]) + _dot(conv.astype(BF16), w_ref[2 * cw:3 * cw, :])
         + _dot(hg.astype(BF16), w_ref[3 * cw:4 * cw, :]))
    return x_ref[...] + _mod_rows(modt_ref, 2, rows) * _rms(y, ng_ref[1:2, :])


def _mod_rows(modt_ref, i, rows):
    sub = rows // modt_ref.shape[0]
    return jnp.concatenate(
        [jnp.broadcast_to(modt_ref[s, i:i + 1, :], (sub, modt_ref.shape[2])) for s in range(modt_ref.shape[0])],
        axis=0)


def _store_split(ref, val):
    w = ref.shape[2]
    for c in range(ref.shape[0]):
        ref[c] = val[:, w * c:w * (c + 1)]


def _load_split(ref):
    return jnp.concatenate([ref[c] for c in range(ref.shape[0])], axis=1)


def _split_spec(rows):
    return lambda d: pl.BlockSpec((GATHER_SPLIT, rows, d // GATHER_SPLIT), lambda i, *_: (0, i, 0))


def _ffn_body(x, modt_ref, ng_ref, wg_ref, wu_ref, wd_ref):
    rows = x.shape[0]
    h = (_rms(x, ng_ref[2:3, :]) * (1.0 + _mod_rows(modt_ref, 4, rows)) + _mod_rows(modt_ref, 3, rows)).astype(BF16)
    acc = jnp.zeros(x.shape, F32)
    for f0 in range(0, wg_ref.shape[1], FFN_TF):
        g = _dot(h, wg_ref[:, f0:f0 + FFN_TF])
        up = _dot(h, wu_ref[:, f0:f0 + FFN_TF])
        acc = acc + _dot((g * _sigmoid(g) * up).astype(BF16), wd_ref[f0:f0 + FFN_TF, :])
    return x + _mod_rows(modt_ref, 5, rows) * _rms(acc, ng_ref[3:4, :])


def _router_body(x, modt_ref, ng_ref, wr_ref, h_ref, r_ref, rt_ref):
    rows = x.shape[0]
    h = _rms(x, ng_ref[2:3, :]) * (1.0 + _mod_rows(modt_ref, 4, rows)) + _mod_rows(modt_ref, 3, rows)
    _store_split(h_ref, h)
    h_hi = h.astype(BF16)
    h_lo = (h - h_hi.astype(F32)).astype(BF16)
    logits = _dot(h_hi, wr_ref[0]) + _dot(h_lo, wr_ref[0]) + _dot(h_hi, wr_ref[1])
    lane = lax.broadcasted_iota(jnp.int32, logits.shape, 1)
    lg = jnp.where(lane < N_EXPERTS, logits, NEG_BIG)
    m1 = jnp.max(lg, axis=-1, keepdims=True)
    i1 = jnp.min(jnp.where(lg == m1, lane, LANES), axis=-1, keepdims=True)
    lg2 = jnp.where(lane == i1, NEG_BIG, lg)
    m2 = jnp.max(lg2, axis=-1, keepdims=True)
    i2 = jnp.min(jnp.where(lg2 == m2, lane, LANES), axis=-1, keepdims=True)
    e2 = jnp.exp(m2 - m1)
    w1 = 1.0 / (1.0 + e2)
    w2 = e2 / (1.0 + e2)
    r = jnp.where(lane == 0, i1.astype(F32),
                  jnp.where(lane == 1, i2.astype(F32), jnp.where(lane == 2, w1, jnp.where(lane == 3, w2, 0.0))))
    r_ref[...] = r
    rt_ref[...] = r.T[0:8, :]


def _mixer_kernel(*refs, fused, mode, n_tiles):
    x_ref, modt_ref, ng_ref = refs[:3]
    pos = 3
    if fused:
        x = _mix_out(x_ref, modt_ref, ng_ref, *refs[pos:pos + N_MIX_REFS], n_tiles=n_tiles)
        pos += N_MIX_REFS
    else:
        x = x_ref[...]
    if mode == "ffn":
        wg_ref, wu_ref, wd_ref, o_ref = refs[pos:]
        o_ref[...] = _ffn_body(x, modt_ref, ng_ref, wg_ref, wu_ref, wd_ref)
    elif mode == "router":
        wr_ref, *outs = refs[pos:]
        _router_body(x, modt_ref, ng_ref, wr_ref, *outs)
    else:
        refs[pos][...] = x


def _mixer_call(x_flat, modt, ng, mix, mode, weights, n_tiles=0):
    n, d = x_flat.shape
    nsub = FFN_TM // TM
    cw = 2 * LANES
    tok = lambda w: pl.BlockSpec((FFN_TM, w), lambda i: (i, 0))
    resident = lambda a: pl.BlockSpec(a.shape, lambda i: (0,) * a.ndim, pipeline_mode=pl.Buffered(1))
    stream = jax.ShapeDtypeStruct((n, d), F32)
    args = [x_flat, modt, ng]
    in_specs = [tok(d), pl.BlockSpec((nsub, N_MOD, d), lambda i: (i, 0, 0)), pl.BlockSpec((8, d), lambda i: (0, 0))]
    fused = mix is not None
    if fused:
        attn, cbog, u, o_f, o_b, hgg, conv_w, hn, w_out = mix
        nb8 = n // 8
        args += [attn, cbog, u, u, u, o_f, o_b, hgg, conv_w, hn, w_out]
        in_specs += [
            tok(2 * cw), tok(2 * cw), tok(cw),
            pl.BlockSpec((8, cw), lambda i: (jnp.maximum(i * (FFN_TM // 8) - 1, 0), 0)),
            pl.BlockSpec((8, cw), lambda i: (jnp.minimum((i + 1) * (FFN_TM // 8), nb8 - 1), 0)),
            tok(cw), tok(cw), resident(hgg), resident(conv_w), resident(hn), resident(w_out),
        ]
        assert len(in_specs) == 3 + N_MIX_REFS
    args += list(weights)
    in_specs += [resident(w) for w in weights]
    if mode == "router":
        out_specs = [_split_spec(FFN_TM)(d), tok(LANES), pl.BlockSpec((8, FFN_TM), lambda i: (0, i))]
        out_shape = [jax.ShapeDtypeStruct((GATHER_SPLIT, n, d // GATHER_SPLIT), F32),
                     jax.ShapeDtypeStruct((n, LANES), F32), jax.ShapeDtypeStruct((8, n), F32)]
    else:
        out_specs, out_shape = tok(d), stream
    return pl.pallas_call(
        functools.partial(_mixer_kernel, fused=fused, mode=mode, n_tiles=n_tiles),
        grid=(n // FFN_TM,),
        in_specs=in_specs,
        out_specs=out_specs,
        out_shape=out_shape,
        compiler_params=_cparams("parallel"),
        name=("mix_" if fused else "") + mode,
    )(*args)


def _gather_flat(data, idx):
    m = idx.shape[0]
    w = data.shape[1]
    mesh = plsc.VectorSubcoreMesh(core_axis_name="core", subcore_axis_name="subcore")

    @functools.partial(pl.kernel, out_type=jax.ShapeDtypeStruct((m, w), data.dtype), mesh=mesh)
    def gather(x_hbm, i_hbm, o_hbm):
        def body(i_vmem, o_vmem):
            pltpu.sync_copy(x_hbm.at[i_vmem.at[0]], o_vmem)

        pltpu.emit_pipeline(
            body,
            grid=(m // GATHER_WIN,),
            in_specs=[pl.BlockSpec((1, GATHER_WIN), lambda i: (0, i))],
            out_specs=[pl.BlockSpec((GATHER_WIN, w), lambda i: (i, 0))],
            core_axis_name=("core", "subcore"),
            dimension_semantics=(pltpu.PARALLEL,),
        )(i_hbm, o_hbm)

    return gather(data, idx.reshape(1, m))


def _scatter_flat(rows, idx, n_out):
    m, w = rows.shape
    mesh = plsc.VectorSubcoreMesh(core_axis_name="core", subcore_axis_name="subcore")

    @functools.partial(pl.kernel, out_type=jax.ShapeDtypeStruct((n_out, w), rows.dtype), mesh=mesh)
    def scatter(x_hbm, i_hbm, o_hbm):
        def body(x_vmem, i_vmem):
            pltpu.sync_copy(x_vmem, o_hbm.at[i_vmem.at[0]])

        pltpu.emit_pipeline(
            body,
            grid=(m // GATHER_WIN,),
            in_specs=[pl.BlockSpec((GATHER_WIN, w), lambda i: (i, 0)),
                      pl.BlockSpec((1, GATHER_WIN), lambda i: (0, i))],
            out_specs=[],
            core_axis_name=("core", "subcore"),
            dimension_semantics=(pltpu.PARALLEL,),
        )(x_hbm, i_hbm)

    return scatter(rows, idx.reshape(1, m))


def _col0_kernel(x_ref, o_ref):
    o_ref[...] = x_ref[...].T[0:1, :]


def _first_column(rows):
    n = rows.shape[0]
    blk = 1024
    return pl.pallas_call(
        _col0_kernel,
        grid=(n // blk,),
        in_specs=[pl.BlockSpec((blk, LANES), lambda i: (i, 0))],
        out_specs=pl.BlockSpec((1, blk), lambda i: (0, i)),
        out_shape=jax.ShapeDtypeStruct((1, n), rows.dtype),
        compiler_params=_cparams("parallel"),
        name="col0",
    )(rows)


def _cast_kernel(*refs):
    o_ref = refs[-1]
    cw = refs[0].shape[1]
    for s, x_ref in enumerate(refs[:-1]):
        o_ref[:, cw * s:cw * (s + 1)] = x_ref[...].astype(o_ref.dtype)


def _cast_bf16(w, layer):
    _, e, r, c = w.shape
    tr = 256
    cw = c // CAST_STREAMS
    chunk = lambda s: pl.BlockSpec((None, None, tr, cw), lambda i, j: (layer, i, j, s))
    return pl.pallas_call(
        _cast_kernel,
        grid=(e, r // tr),
        in_specs=[chunk(s) for s in range(CAST_STREAMS)],
        out_specs=pl.BlockSpec((None, tr, c), lambda i, j: (i, j, 0)),
        out_shape=jax.ShapeDtypeStruct((e, r, c), BF16),
        compiler_params=_cparams("parallel", "parallel"),
        name="cast",
    )(*([w] * CAST_STREAMS))


def _gather_rows(data, idx):
    s, r, w = data.shape
    m = idx.shape[0]
    idx_all = (idx[None, :] + (jnp.arange(s, dtype=jnp.int32) * r)[:, None]).reshape(s * m)
    return _gather_flat(data.reshape(s * r, w), idx_all).reshape(s, m, w)


def _expert_kernel(be_ref, nv_ref, xs_ref, wg_ref, wu_ref, wd_ref, o_ref):
    valid = pl.program_id(0) < nv_ref[0]

    @pl.when(valid)
    def _():
        xb = _load_split(xs_ref).astype(BF16)
        acc = jnp.zeros((xb.shape[0], wd_ref.shape[1]), F32)
        for f0 in range(0, wg_ref.shape[1], MOE_TF):
            g = _dot(xb, wg_ref[:, f0:f0 + MOE_TF])
            up = _dot(xb, wu_ref[:, f0:f0 + MOE_TF])
            acc = acc + _dot((g * _sigmoid(g) * up).astype(BF16), wd_ref[f0:f0 + MOE_TF, :])
        _store_split(o_ref, acc)

    @pl.when(jnp.logical_not(valid))
    def _():
        o_ref[...] = jnp.zeros(o_ref.shape, F32)


def _expert_call(block_expert, n_valid, xs, wg, wu, wd):
    n_slots = xs.shape[1]
    d = wg.shape[1]
    dff = wg.shape[2]
    nb = n_slots // MOE_BLK
    resident = lambda r, c: pl.BlockSpec((None, r, c), lambda i, be, nv: (be[i], 0, 0), pipeline_mode=pl.Buffered(1))
    grid_spec = pltpu.PrefetchScalarGridSpec(
        num_scalar_prefetch=2,
        grid=(nb,),
        in_specs=[_split_spec(MOE_BLK)(d), resident(d, dff), resident(d, dff), resident(dff, d)],
        out_specs=_split_spec(MOE_BLK)(d),
    )
    return pl.pallas_call(
        _expert_kernel,
        grid_spec=grid_spec,
        out_shape=jax.ShapeDtypeStruct((GATHER_SPLIT, n_slots, d // GATHER_SPLIT), F32),
        compiler_params=_cparams("arbitrary"),
        name="experts",
    )(block_expert, n_valid, xs, wg, wu, wd)


def _combine_kernel(x_ref, modt_ref, ng_ref, y1_ref, y2_ref, r_ref, o_ref):
    x = x_ref[...]
    rows = x.shape[0]
    f = r_ref[:, 2:3] * _load_split(y1_ref) + r_ref[:, 3:4] * _load_split(y2_ref)
    o_ref[...] = x + _mod_rows(modt_ref, 5, rows) * _rms(f, ng_ref[3:4, :])


def _combine_call(x_flat, modt, ng, y12, rinfo):
    n, d = x_flat.shape
    nsub = FFN_TM // TM
    nblk = n // FFN_TM
    tok = lambda w: pl.BlockSpec((FFN_TM, w), lambda i: (i, 0))
    second = pl.BlockSpec((GATHER_SPLIT, FFN_TM, d // GATHER_SPLIT), lambda i: (0, i + nblk, 0))
    return pl.pallas_call(
        _combine_kernel,
        grid=(nblk,),
        in_specs=[tok(d), pl.BlockSpec((nsub, N_MOD, d), lambda i: (i, 0, 0)), pl.BlockSpec((8, d), lambda i: (0, 0)),
                  _split_spec(FFN_TM)(d), second, tok(LANES)],
        out_specs=tok(d),
        out_shape=jax.ShapeDtypeStruct((n, d), F32),
        compiler_params=_cparams("parallel"),
        name="combine",
    )(x_flat, modt, ng, y12, y12, rinfo)


def _router_weights(w_router):
    wr = jnp.pad(w_router, ((0, 0), (0, LANES - N_EXPERTS)))
    wr_hi = wr.astype(BF16)
    wr_lo = (wr - wr_hi.astype(F32)).astype(BF16)
    return jnp.stack([wr_hi, wr_lo])


def _moe_layer(x_flat, h, rinfo, rinfo_t, modt, ng, wg, wu, wd):
    n, d = x_flat.shape
    n_assign = 2 * n
    expert = jnp.concatenate([rinfo_t[0], rinfo_t[1]]).astype(jnp.int32)
    onehot = (expert[None, :] == jnp.arange(N_EXPERTS, dtype=jnp.int32)[:, None]).astype(jnp.int32)
    csum = jnp.cumsum(onehot, axis=1)
    rank = jnp.sum((csum - onehot) * onehot, axis=0)
    counts = csum[:, -1]
    padded = (counts + MOE_BLK - 1) // MOE_BLK * MOE_BLK
    pad_end = jnp.cumsum(padded)
    pad_start = pad_end - padded
    dest = jnp.sum(onehot * pad_start[:, None], axis=0) + rank
    nb = -(-n_assign // MOE_BLK) + N_EXPERTS
    row_quantum = 32 * GATHER_WIN // GATHER_SPLIT
    nb = -(-nb * MOE_BLK // row_quantum) * row_quantum // MOE_BLK
    n_slots = nb * MOE_BLK
    block_expert = jnp.minimum(
        jnp.searchsorted(pad_end, jnp.arange(nb, dtype=jnp.int32) * MOE_BLK, side="right"), N_EXPERTS - 1
    ).astype(jnp.int32)
    n_valid = (pad_end[-1:] // MOE_BLK).astype(jnp.int32)
    tok_rows = jnp.broadcast_to((jnp.arange(n_assign, dtype=jnp.int32) % n)[:, None], (n_assign, LANES))
    slot = jnp.arange(n_slots, dtype=jnp.int32)
    slot_used = (slot - jnp.repeat(pad_start[block_expert], MOE_BLK)) < jnp.repeat(counts[block_expert], MOE_BLK)
    slot_tok = jnp.where(slot_used, _first_column(_scatter_flat(tok_rows, dest, n_slots)).reshape(n_slots), 0)

    xs = _gather_rows(h, slot_tok)
    ys = _expert_call(block_expert, n_valid, xs, wg, wu, wd)
    y12 = _gather_rows(ys, dest)
    return _combine_call(x_flat, modt, ng, y12, rinfo)


def _rope_tables(n_ctx, length):
    rows = length // GRID_W
    row = jnp.repeat(jnp.arange(rows, dtype=F32), GRID_W)
    col = jnp.tile(jnp.arange(GRID_W, dtype=F32), rows)
    axis_dim = HEAD_DIM // 2
    inv_freq = ROPE_THETA ** (-jnp.arange(0, axis_dim, 2, dtype=F32) / axis_dim)
    ar = row[:, None] * inv_freq
    ac = col[:, None] * inv_freq
    cos = jnp.concatenate([jnp.cos(ar), jnp.cos(ar), jnp.cos(ac), jnp.cos(ac)], axis=1)
    sin = jnp.concatenate([-jnp.sin(ar), jnp.sin(ar), -jnp.sin(ac), jnp.sin(ac)], axis=1)
    cos = jnp.concatenate([jnp.ones((n_ctx, HEAD_DIM), F32), cos], axis=0)
    sin = jnp.concatenate([jnp.zeros((n_ctx, HEAD_DIM), F32), sin], axis=0)
    return jnp.tile(cos, (1, 2)), jnp.tile(sin, (1, 2))


def kernel(x, c, ctx, c_ctx, ada_w, ada_b, norm_g, w_in, w_out, attn_q_g, attn_k_g, conv_w, hgrn_lb, hgrn_g,
           ffn_w_gate, ffn_w_up, ffn_w_down, moe_router, moe_w_gate, moe_w_up, moe_w_down):
    bsz, length, d = x.shape
    n_ctx = ctx.shape[1]
    depth = w_in.shape[0]
    t = n_ctx + length
    nt = t // TM
    assert n_ctx == TM and length % TM == 0 and d == 8 * LANES
    for n_tok in (bsz * t, bsz * length):
        assert n_tok % FFN_TM == 0 and (2 * n_tok) % (32 * GATHER_WIN) == 0

    r_pad = -(-(bsz + 1) // 8) * 8
    c_all = jnp.concatenate([c, c_ctx[None, :], jnp.zeros((r_pad - bsz - 1, d), F32)], axis=0)
    mod_all = _ada_call(c_all, ada_w, ada_b).reshape(depth, r_pad, N_MOD, d)

    cos_t, sin_t = _rope_tables(n_ctx, length)
    lb_all = jnp.cumsum(jax.nn.softmax(hgrn_lb.astype(F32), axis=0), axis=0)
    lb_all = lb_all - lb_all[0]
    hconsts = _hgrn_consts()
    cw = 2 * LANES
    cidx = jnp.arange(cw)
    head_mean = ((cidx[:, None] // HEAD_DIM == cidx[None, :] // HEAD_DIM).astype(F32) / HEAD_DIM).astype(BF16)

    cos_f = jnp.tile(cos_t, (bsz, 1))
    sin_f = jnp.tile(sin_t, (bsz, 1))
    x_flat = jnp.concatenate([ctx, x], axis=1).reshape(bsz * t, d)
    for l in range(depth):
        mod_lat = mod_all[l, :bsz]
        mod_ctx = jnp.broadcast_to(mod_all[l, bsz][None], (bsz, N_MOD, d))
        mod = jnp.stack([mod_ctx, mod_lat], axis=1)
        modt = jnp.concatenate([mod[:, :1], jnp.broadcast_to(mod[:, 1:], (bsz, nt - 1, N_MOD, d))], axis=1)
        modt = modt.reshape(bsz * nt, N_MOD, d)
        ng = jnp.pad(norm_g[l], ((0, 4), (0, 0)))
        lb = lb_all[l].reshape(1, 2 * cw)
        gc = jnp.concatenate([jnp.log(lb), jnp.log1p(-lb), 1.0 - lb, jnp.zeros((5, 2 * cw), F32)], axis=0)
        qg = jnp.tile(attn_q_g[l], 2)[None, :]
        kg = jnp.tile(attn_k_g[l], 2)[None, :]

        q, k, vt, cbog, u, qv, kk, gg = _inproj_call(x_flat, modt, ng, w_in[l].astype(BF16), cos_f, sin_f, qg, kg, gc)
        attn = _attn_call(q, k, vt, bsz)
        o_f, o_b = _hgrn_call(qv, kk, gg, hconsts, bsz)
        mix = (attn, cbog, u, o_f, o_b, hgrn_g[l][None, :],
               jnp.pad(conv_w[l], ((0, 5), (0, 0))), head_mean, w_out[l].astype(BF16))
        if l % 2 == 0:
            weights = (ffn_w_gate[l // 2].astype(BF16), ffn_w_up[l // 2].astype(BF16), ffn_w_down[l // 2].astype(BF16))
        else:
            weights = (_router_weights(moe_router[l // 2]),)
            experts = (_cast_bf16(moe_w_gate, l // 2), _cast_bf16(moe_w_up, l // 2), _cast_bf16(moe_w_down, l // 2))
        mode = "ffn" if l % 2 == 0 else "router"

        if l < depth - 1:
            if mode == "ffn":
                x_flat = _mixer_call(x_flat, modt, ng, mix, mode, weights, nt)
            else:
                x_flat = _mixer_call(x_flat, modt, ng, mix, "plain", (), nt)
                h, rinfo, rinfo_t = _mixer_call(x_flat, modt, ng, None, mode, weights)
                x_flat = _moe_layer(x_flat, h, rinfo, rinfo_t, modt, ng, *experts)
            continue

        x_all = _mixer_call(x_flat, modt, ng, mix, "plain", (), nt).reshape(bsz, t, d)
        x_flat = x_all[:, n_ctx:, :].reshape(bsz * length, d)
        modt = jnp.broadcast_to(mod[:, 1:], (bsz, length // TM, N_MOD, d)).reshape(bsz * (length // TM), N_MOD, d)
        if mode == "ffn":
            x_flat = _mixer_call(x_flat, modt, ng, None, mode, weights)
        else:
            h, rinfo, rinfo_t = _mixer_call(x_flat, modt, ng, None, mode, weights)
            x_flat = _moe_layer(x_flat, h, rinfo, rinfo_t, modt, ng, *experts)
        return x_flat.reshape(bsz, length, d)
```

```python
import functools

import jax
import jax.numpy as jnp
from jax import lax
from jax.experimental import pallas as pl
from jax.experimental.pallas import tpu as pltpu
from jax.experimental.pallas import tpu_sc as plsc

F32 = jnp.float32
BF16 = jnp.bfloat16

HEAD_DIM = 64
GRID_W = 64
ROPE_THETA = 10000.0
ATTN_SCALE = HEAD_DIM ** -0.5
LOG2_E = 1.4426950408889634
HGRN_CHUNK = 64
HGRN_SUB = 16
VT_ROWS = HEAD_DIM + 16
ATTN_AHEAD = 6
N_EXPERTS = 8
N_MOD = 6
EPS = 1e-6

LANES = 128
TM = 256
FFN_TM = 512
FFN_TF = 256
N_MIX_REFS = 11
CAST_STREAMS = 4
MOE_BLK = 512
MOE_TF = 512
GATHER_WIN = 128
GATHER_SPLIT = 2
NEG_BIG = -1e30
VMEM_LIMIT = 56 * 1024 * 1024


def _cparams(*sem):
    return pltpu.CompilerParams(dimension_semantics=sem, vmem_limit_bytes=VMEM_LIMIT)


def _sigmoid(z):
    return 1.0 / (1.0 + jnp.exp(-z))


def _dot(a, b):
    return jnp.dot(a, b, preferred_element_type=F32)


def _dot_nt(a, b):
    return lax.dot_general(a, b, (((1,), (1,)), ((), ())), preferred_element_type=F32)


def _dot_tn(a, b):
    return lax.dot_general(a, b, (((0,), (0,)), ((), ())), preferred_element_type=F32)


def _rms(x, g):
    return x * lax.rsqrt(jnp.mean(x * x, axis=-1, keepdims=True) + EPS) * g


def _ada_kernel(c_ref, w_ref, b_ref, o_ref):
    c = c_ref[...]
    s = (c * _sigmoid(c)).astype(BF16)
    o_ref[...] = _dot(s, w_ref[...].astype(BF16)) + b_ref[...]


def _ada_call(c_all, ada_w, ada_b):
    depth, d, n = ada_w.shape
    r = c_all.shape[0]
    tn = 512
    return pl.pallas_call(
        _ada_kernel,
        grid=(depth, n // tn),
        in_specs=[
            pl.BlockSpec((r, d), lambda l, j: (0, 0)),
            pl.BlockSpec((None, d, tn), lambda l, j: (l, 0, j)),
            pl.BlockSpec((None, 1, tn), lambda l, j: (l, 0, j)),
        ],
        out_specs=pl.BlockSpec((None, r, tn), lambda l, j: (l, 0, j)),
        out_shape=jax.ShapeDtypeStruct((depth, r, n), F32),
        compiler_params=_cparams("parallel", "parallel"),
        name="ada",
    )(c_all, ada_w, ada_b.reshape(depth, 1, n))


def _inproj_kernel(x_ref, modt_ref, ng_ref, w_ref, cos_ref, sin_ref, qg_ref, kg_ref, gc_ref,
                   q_ref, k_ref, vt_ref, cbog_ref, u_ref, qv_ref, kk_ref, gg_ref):
    x = x_ref[...]
    rows = x.shape[0]
    h = _rms(x, ng_ref[0:1, :]) * (1.0 + _mod_rows(modt_ref, 1, rows)) + _mod_rows(modt_ref, 0, rows)
    hb = h.astype(BF16)

    def proj(c0, n):
        return _dot(hb, w_ref[:, c0:c0 + n])

    cos = cos_ref[...]
    sin = sin_ref[...]
    lane = lax.broadcasted_iota(jnp.int32, (rows, LANES), 1)
    first_half = (lane % 32) < 16
    lo = lane < HEAD_DIM

    def norm_rope(blk, g):
        b2 = blk * blk
        s_lo = jnp.sum(jnp.where(lo, b2, 0.0), axis=-1, keepdims=True)
        s_hi = jnp.sum(jnp.where(lo, 0.0, b2), axis=-1, keepdims=True)
        v = blk * lax.rsqrt(jnp.where(lo, s_lo, s_hi) * (1.0 / HEAD_DIM) + EPS) * g
        partner = jnp.where(first_half, pltpu.roll(v, LANES - 16, 1), pltpu.roll(v, 16, 1))
        return v * cos + partner * sin

    qa = proj(0, 4 * LANES)
    for j in range(4):
        qn = norm_rope(qa[:, LANES * j:LANES * (j + 1)], qg_ref[...]) * (ATTN_SCALE * LOG2_E)
        sw = pltpu.roll(qn, HEAD_DIM, 1)
        if j < 2:
            q_ref[2 * j] = jnp.where(lo, qn, 0.0).astype(BF16)
            q_ref[2 * j + 1] = jnp.where(lo, sw, 0.0).astype(BF16)
        else:
            q_ref[2 * j] = jnp.where(lo, 0.0, sw).astype(BF16)
            q_ref[2 * j + 1] = jnp.where(lo, 0.0, qn).astype(BF16)

    kv = proj(4 * LANES, 2 * LANES)
    k_ref[...] = norm_rope(kv[:, :LANES], kg_ref[...]).astype(BF16)
    vt = kv[:, LANES:].T
    ones = jnp.ones((VT_ROWS - HEAD_DIM, TM), F32)
    for s in range(rows // TM):
        for g in range(2):
            vt_ref[g, s] = jnp.concatenate(
                [vt[HEAD_DIM * g:HEAD_DIM * (g + 1), TM * s:TM * (s + 1)], ones], axis=0).astype(BF16)

    c0 = 6 * LANES
    c3 = proj(c0, 6 * LANES)
    cw = 2 * LANES
    u_ref[...] = c3[:, cw:2 * cw] * c3[:, 2 * cw:3 * cw]

    hg = proj(c0 + 6 * LANES, 10 * LANES)
    zf, zb, iv, hq, og = (hg[:, cw * i:cw * (i + 1)] for i in range(5))
    cbog_ref[:, :cw] = c3[:, :cw].astype(BF16)
    cbog_ref[:, cw:] = og.astype(BF16)
    qv_ref[:, :cw] = (hq * _sigmoid(hq)).astype(BF16)
    qv_ref[:, cw:] = iv.astype(BF16)
    for d, z in enumerate((zf, zb)):
        log_lb = gc_ref[0:1, cw * d:cw * (d + 1)]
        log1m_lb = gc_ref[1:2, cw * d:cw * (d + 1)]
        one_m_lb = gc_ref[2:3, cw * d:cw * (d + 1)]
        t = jnp.exp(-jnp.abs(z))
        log_sig = jnp.minimum(z, 0.0) - jnp.log1p(t)
        a2 = log1m_lb + log_sig
        log_f = jnp.maximum(log_lb, a2) + jnp.log1p(jnp.exp(-jnp.abs(log_lb - a2)))
        sig_neg = jnp.where(z >= 0, t, 1.0) / (1.0 + t)
        gg_ref[:, cw * d:cw * (d + 1)] = log_f
        kk_ref[:, cw * d:cw * (d + 1)] = (one_m_lb * sig_neg).astype(BF16)


def _inproj_call(x_flat, modt, ng, w1, cos_f, sin_f, qg, kg, gc):
    n, d = x_flat.shape
    nsub = FFN_TM // TM
    tok = lambda w: pl.BlockSpec((FFN_TM, w), lambda i: (i, 0))
    const = lambda a: pl.BlockSpec(a.shape, lambda i: (0,) * a.ndim, pipeline_mode=pl.Buffered(1))
    return pl.pallas_call(
        _inproj_kernel,
        grid=(n // FFN_TM,),
        in_specs=[
            tok(d),
            pl.BlockSpec((nsub, N_MOD, d), lambda i: (i, 0, 0)),
            const(ng), const(w1), tok(LANES), tok(LANES), const(qg), const(kg), const(gc),
        ],
        out_specs=[
            pl.BlockSpec((8, FFN_TM, LANES), lambda i: (0, i, 0)),
            tok(LANES),
            pl.BlockSpec((2, nsub, VT_ROWS, TM), lambda i: (0, i, 0, 0)),
            tok(4 * LANES),
            tok(2 * LANES),
            tok(4 * LANES),
            tok(4 * LANES),
            tok(4 * LANES),
        ],
        out_shape=[
            jax.ShapeDtypeStruct((8, n, LANES), BF16),
            jax.ShapeDtypeStruct((n, LANES), BF16),
            jax.ShapeDtypeStruct((2, n // TM, VT_ROWS, TM), BF16),
            jax.ShapeDtypeStruct((n, 4 * LANES), BF16),
            jax.ShapeDtypeStruct((n, 2 * LANES), F32),
            jax.ShapeDtypeStruct((n, 4 * LANES), BF16),
            jax.ShapeDtypeStruct((n, 4 * LANES), BF16),
            jax.ShapeDtypeStruct((n, 4 * LANES), F32),
        ],
        compiler_params=_cparams("parallel"),
        name="inproj",
    )(x_flat, modt, ng, w1, cos_f, sin_f, qg, kg, gc)


def _attn_kernel(q_ref, k_ref, vt_ref, o_ref, m_ref, acc_ref, *, n_kv):
    def score(j, h):
        return _dot_nt(k_ref[TM * j:TM * (j + 1), :], q_ref[h])

    def attend(n_tiles):
        steps = [(j, h) for j in range(n_tiles) for h in range(8)]
        scores = [score(j, h) for j, h in steps[:ATTN_AHEAD]]
        for i, (j, h) in enumerate(steps):
            if i + ATTN_AHEAD < len(steps):
                scores.append(score(*steps[i + ATTN_AHEAD]))
            s = scores.pop(0)
            m_tile = jnp.max(s, axis=0, keepdims=True)
            if j == 0:
                m_new = m_tile
                acc_ref[h] = _dot(vt_ref[h // 4, j], jnp.exp2(s - m_new).astype(BF16))
            else:
                m_old = m_ref[h]
                m_new = jnp.maximum(m_old, m_tile)
                pv = _dot(vt_ref[h // 4, j], jnp.exp2(s - m_new).astype(BF16))
                acc_ref[h] = jnp.exp2(m_old - m_new) * acc_ref[h] + pv
            m_ref[h] = m_new
        outs = [acc_ref[h, :HEAD_DIM, :] / acc_ref[h, HEAD_DIM:HEAD_DIM + 1, :] for h in range(8)]
        o_ref[...] = jnp.concatenate(outs, axis=0).T.astype(BF16)

    @pl.when(pl.program_id(1) == 0)
    def _():
        attend(1)

    @pl.when(pl.program_id(1) > 0)
    def _():
        attend(n_kv)


def _attn_call(q, k, vt, b):
    n = k.shape[0]
    t = n // b
    nt = t // TM
    return pl.pallas_call(
        functools.partial(_attn_kernel, n_kv=nt),
        grid=(b, nt),
        in_specs=[
            pl.BlockSpec((8, TM, LANES), lambda i, j: (0, i * nt + j, 0)),
            pl.BlockSpec((t, LANES), lambda i, j: (i, 0)),
            pl.BlockSpec((2, nt, VT_ROWS, TM), lambda i, j: (0, i, 0, 0)),
        ],
        out_specs=pl.BlockSpec((TM, 4 * LANES), lambda i, j: (i * nt + j, 0)),
        out_shape=jax.ShapeDtypeStruct((n, 4 * LANES), BF16),
        scratch_shapes=[pltpu.VMEM((8, 1, TM), F32), pltpu.VMEM((8, VT_ROWS, TM), F32)],
        compiler_params=_cparams("parallel", "parallel"),
        name="attn",
    )(q, k, vt)


def _split3(g):
    hi = g.astype(BF16)
    r1 = g - hi.astype(F32)
    mid = r1.astype(BF16)
    lo = (r1 - mid.astype(F32)).astype(BF16)
    return hi, mid, lo


def _hgrn_kernel(qvf_ref, kf_ref, gf_ref, qvb_ref, kb_ref, gb_ref, tri_ref, trij_ref, j_ref, e_ref, bd_ref,
                 md_ref, mo_ref, of_ref, ob_ref, st_ref, x_s):
    hw = 2 * LANES
    nch = TM // HGRN_CHUNK

    @pl.when(pl.program_id(1) == 0)
    def _():
        st_ref[...] = jnp.zeros(st_ref.shape, F32)

    flip = j_ref[...]

    def cumsum(m_ref, g):
        m = m_ref[...]
        hi, mid, lo = _split3(g)
        return _dot(m, hi) + _dot(m, mid) + _dot(m, lo)

    qvf = qvf_ref[...]
    fl = _dot(flip, jnp.concatenate([qvb_ref[...], kb_ref[...]], axis=1))
    q = jnp.concatenate([qvf[:, :hw].astype(F32), fl[:, :hw]], axis=0)
    v = jnp.concatenate([qvf[:, hw:].astype(F32), fl[:, hw:2 * hw]], axis=0)
    k = jnp.concatenate([kf_ref[...].astype(F32), fl[:, 2 * hw:]], axis=0)
    b = jnp.concatenate([cumsum(tri_ref, gf_ref[...]), cumsum(trij_ref, gb_ref[...])], axis=0) * LOG2_E

    ngr = 2 * TM // HGRN_SUB
    nsub = HGRN_CHUNK // HGRN_SUB
    q3 = q.reshape(ngr, HGRN_SUB, hw)
    b3 = b.reshape(ngr, HGRN_SUB, hw)
    k3 = k.reshape(ngr, HGRN_SUB, hw)
    for s in range(HGRN_SUB):
        x = q3 * jnp.exp2(jnp.minimum(b3 - b3[:, s:s + 1, :], 0.0)) * k3[:, s:s + 1, :]
        x_s[:, hw * s:hw * (s + 1)] = x.reshape(2 * TM, hw).astype(BF16)
    a = _dot(x_s[...], e_ref[...])

    bd = bd_ref[...]
    bd16 = bd.astype(BF16)
    md = md_ref[...]
    mo = mo_ref[...]
    for d in range(2):
        st = st_ref[d]
        outs = []
        for c in range(nch):
            r0 = TM * d + HGRN_CHUNK * c
            bc = b[r0:r0 + HGRN_CHUNK]
            qc = q[r0:r0 + HGRN_CHUNK]
            kc = k[r0:r0 + HGRN_CHUNK]
            vc = v[r0:r0 + HGRN_CHUNK]
            ac = a[r0:r0 + HGRN_CHUNK] * md
            r_end = [bc[HGRN_SUB * (j + 1) - 1:HGRN_SUB * (j + 1), :] for j in range(nsub)]
            r_own = jnp.concatenate([jnp.broadcast_to(r, (HGRN_SUB, hw)) for r in r_end], axis=0)
            kt_bd = jnp.concatenate([(kc * jnp.exp2(r_own - bc)).astype(BF16)] * 4, axis=0) * bd16
            q_rel = jnp.concatenate(
                [qc * jnp.exp2(jnp.minimum(bc - r_end[j], 0.0)) for j in range(nsub - 1)], axis=0).astype(BF16)
            p = _dot_nt(q_rel, kt_bd) * mo
            for j in range(nsub - 1):
                ac = ac + p[HGRN_CHUNK * j:HGRN_CHUNK * (j + 1)]
            bl = r_end[-1]
            qe = (qc * jnp.exp2(bc)).astype(BF16)
            ke = (kc * jnp.exp2(bl - bc)).astype(BF16)
            vcb = vc.astype(BF16)
            vbd = jnp.concatenate([vcb] * 4, axis=0) * bd16
            outs.append(_dot(ac.astype(BF16), vbd) + _dot_nt(qe, st.astype(BF16)))
            st = st * jnp.exp2(bl) + _dot_tn(vcb, ke) * bd
        st_ref[d] = st
        od = jnp.concatenate(outs, axis=0).astype(BF16)
        if d == 0:
            of_ref[...] = od
        else:
            ob_ref[...] = _dot(flip, od).astype(BF16)


def _hgrn_consts():
    hw = 2 * LANES
    r = jnp.arange(TM)
    same_chunk = (r[:, None] // HGRN_CHUNK) == (r[None, :] // HGRN_CHUNK)
    tri = (same_chunk & (r[None, :] <= r[:, None])).astype(F32)
    flip = (r[:, None] + r[None, :] == TM - 1).astype(F32)
    trij = tri @ flip
    c = jnp.arange(hw)
    s = jnp.arange(HGRN_SUB)
    e = ((c[None, :, None] // HEAD_DIM == c[None, None, :] // HGRN_CHUNK)
         & (c[None, None, :] % HGRN_SUB == s[:, None, None])).astype(BF16).reshape(HGRN_SUB * hw, hw)
    bd = (c[:, None] // HEAD_DIM == c[None, :] // HEAD_DIM).astype(F32)
    tt = jnp.arange(HGRN_CHUNK)[:, None]
    ss = (c % HGRN_CHUNK)[None, :]
    md = ((ss // HGRN_SUB == tt // HGRN_SUB) & (ss <= tt)).astype(F32)
    mo = jnp.concatenate([((ss // HGRN_SUB == j) & (tt // HGRN_SUB > j)).astype(F32)
                          for j in range(HGRN_CHUNK // HGRN_SUB - 1)], axis=0)
    return tri.astype(BF16), trij.astype(BF16), flip.astype(BF16), e, bd, md, mo


def _hgrn_call(qv, kk, gg, consts, b):
    n = qv.shape[0]
    nt = n // b // TM
    hw = 2 * LANES
    tri, trij, flip, e, bd, md, mo = consts
    fwd = lambda i, j: i * nt + j
    bwd = lambda i, j: i * nt + jnp.where(j == 0, 0, nt - j)
    const2 = lambda a: pl.BlockSpec(a.shape, lambda i, j: (0,) * a.ndim)
    return pl.pallas_call(
        _hgrn_kernel,
        grid=(b, nt),
        in_specs=[
            pl.BlockSpec((TM, 2 * hw), lambda i, j: (fwd(i, j), 0)),
            pl.BlockSpec((TM, hw), lambda i, j: (fwd(i, j), 0)),
            pl.BlockSpec((TM, hw), lambda i, j: (fwd(i, j), 0)),
            pl.BlockSpec((TM, 2 * hw), lambda i, j: (bwd(i, j), 0)),
            pl.BlockSpec((TM, hw), lambda i, j: (bwd(i, j), 1)),
            pl.BlockSpec((TM, hw), lambda i, j: (bwd(i, j), 1)),
            const2(tri), const2(trij), const2(flip), const2(e), const2(bd), const2(md), const2(mo),
        ],
        out_specs=[
            pl.BlockSpec((TM, hw), lambda i, j: (fwd(i, j), 0)),
            pl.BlockSpec((TM, hw), lambda i, j: (bwd(i, j), 0)),
        ],
        out_shape=[jax.ShapeDtypeStruct((n, hw), BF16)] * 2,
        scratch_shapes=[
            pltpu.VMEM((2, hw, hw), F32),
            pltpu.VMEM((2 * TM, HGRN_SUB * hw), BF16),
        ],
        compiler_params=_cparams("parallel", "arbitrary"),
        name="hgrn",
    )(qv, kk, gg, qv, kk, gg, tri, trij, flip, e, bd, md, mo)


def _mix_out(x_ref, modt_ref, ng_ref, attn_ref, cbog_ref, u_ref, up_ref, un_ref, of_ref, ob_ref,
             hgg_ref, cw_ref, hn_ref, w_ref, *, n_tiles):
    rows = x_ref.shape[0]
    nsub = rows // TM
    cw = 2 * LANES
    u = u_ref[...]
    row = lax.broadcasted_iota(jnp.int32, (rows, cw), 0)
    u_prev = jnp.where(row == 0, up_ref[7:8, :], pltpu.roll(u, 1, 0))
    u_next = jnp.where(row == rows - 1, un_ref[0:1, :], pltpu.roll(u, rows - 1, 0))
    for s in range(nsub):
        t = (pl.program_id(0) * nsub + s) % n_tiles
        u_prev = jnp.where(t >= 2, u_prev, jnp.where(row == s * TM, 0.0, u_prev))
        u_next = jnp.where((t >= 1) & (t < n_tiles - 1), u_next, jnp.where(row == s * TM + TM - 1, 0.0, u_next))
    cbog = cbog_ref[...].astype(F32)
    conv = cbog[:, :cw] * (cw_ref[0:1, :] * u_prev + cw_ref[1:2, :] * u + cw_ref[2:3, :] * u_next)

    o = of_ref[...].astype(F32) + ob_ref[...].astype(F32)
    ms = _dot((o * o).astype(BF16), hn_ref[...])
    og = cbog[:, cw:]
    hg = o * lax.rsqrt(ms + EPS) * hgg_ref[...] * (og * _sigmoid(og))

    y = (_dot(attn_ref[...], w_ref[0:2 * cw, :]) + _dot(conv.astype(BF16), w_ref[2 * cw:3 * cw, :])
         + _dot(hg.astype(BF16), w_ref[3 * cw:4 * cw, :]))
    return x_ref[...] + _mod_rows(modt_ref, 2, rows) * _rms(y, ng_ref[1:2, :])


def _mod_rows(modt_ref, i, rows):
    sub = rows // modt_ref.shape[0]
    return jnp.concatenate(
        [jnp.broadcast_to(modt_ref[s, i:i + 1, :], (sub, modt_ref.shape[2])) for s in range(modt_ref.shape[0])],
        axis=0)


def _store_split(ref, val):
    w = ref.shape[2]
    for c in range(ref.shape[0]):
        lo = val[:, 2 * w * c:2 * w * c + w].astype(BF16).astype(F32)
        hi = val[:, 2 * w * c + w:2 * w * (c + 1)].astype(BF16).astype(F32)
        ref[c] = lax.bitcast_convert_type(hi, jnp.uint32) | (lax.bitcast_convert_type(lo, jnp.uint32) >> 16)


def _load_split(ref):
    cols = []
    for c in range(ref.shape[0]):
        word = ref[c]
        cols.append(lax.bitcast_convert_type(word << 16, F32))
        cols.append(lax.bitcast_convert_type(word & jnp.uint32(0xFFFF0000), F32))
    return jnp.concatenate(cols, axis=1)


def _split_shape(n, d):
    return jax.ShapeDtypeStruct((GATHER_SPLIT, n, d // GATHER_SPLIT // 2), jnp.uint32)


def _split_spec(rows):
    return lambda d: pl.BlockSpec((GATHER_SPLIT, rows, d // GATHER_SPLIT // 2), lambda i, *_: (0, i, 0))


def _ffn_body(x, modt_ref, ng_ref, wg_ref, wu_ref, wd_ref):
    rows = x.shape[0]
    h = (_rms(x, ng_ref[2:3, :]) * (1.0 + _mod_rows(modt_ref, 4, rows)) + _mod_rows(modt_ref, 3, rows)).astype(BF16)
    acc = jnp.zeros(x.shape, F32)
    for f0 in range(0, wg_ref.shape[1], FFN_TF):
        g = _dot(h, wg_ref[:, f0:f0 + FFN_TF])
        up = _dot(h, wu_ref[:, f0:f0 + FFN_TF])
        acc = acc + _dot((g * _sigmoid(g) * up).astype(BF16), wd_ref[f0:f0 + FFN_TF, :])
    return x + _mod_rows(modt_ref, 5, rows) * _rms(acc, ng_ref[3:4, :])


def _router_body(x, modt_ref, ng_ref, wr_ref, h_ref, r_ref, rt_ref):
    rows = x.shape[0]
    h = _rms(x, ng_ref[2:3, :]) * (1.0 + _mod_rows(modt_ref, 4, rows)) + _mod_rows(modt_ref, 3, rows)
    _store_split(h_ref, h)
    h_hi = h.astype(BF16)
    h_lo = (h - h_hi.astype(F32)).astype(BF16)
    logits = _dot(h_hi, wr_ref[0]) + _dot(h_lo, wr_ref[0]) + _dot(h_hi, wr_ref[1])
    lane = lax.broadcasted_iota(jnp.int32, logits.shape, 1)
    lg = jnp.where(lane < N_EXPERTS, logits, NEG_BIG)
    m1 = jnp.max(lg, axis=-1, keepdims=True)
    i1 = jnp.min(jnp.where(lg == m1, lane, LANES), axis=-1, keepdims=True)
    lg2 = jnp.where(lane == i1, NEG_BIG, lg)
    m2 = jnp.max(lg2, axis=-1, keepdims=True)
    i2 = jnp.min(jnp.where(lg2 == m2, lane, LANES), axis=-1, keepdims=True)
    e2 = jnp.exp(m2 - m1)
    w1 = 1.0 / (1.0 + e2)
    w2 = e2 / (1.0 + e2)
    r = jnp.where(lane == 0, i1.astype(F32),
                  jnp.where(lane == 1, i2.astype(F32), jnp.where(lane == 2, w1, jnp.where(lane == 3, w2, 0.0))))
    r_ref[...] = r
    rt_ref[...] = r.T[0:8, :]


def _mixer_kernel(*refs, fused, mode, n_tiles):
    x_ref, modt_ref, ng_ref = refs[:3]
    pos = 3
    if fused:
        x = _mix_out(x_ref, modt_ref, ng_ref, *refs[pos:pos + N_MIX_REFS], n_tiles=n_tiles)
        pos += N_MIX_REFS
    else:
        x = x_ref[...]
    if mode == "ffn":
        wg_ref, wu_ref, wd_ref, o_ref = refs[pos:]
        o_ref[...] = _ffn_body(x, modt_ref, ng_ref, wg_ref, wu_ref, wd_ref)
    elif mode == "router":
        wr_ref, *outs = refs[pos:]
        _router_body(x, modt_ref, ng_ref, wr_ref, *outs)
    else:
        refs[pos][...] = x


def _mixer_call(x_flat, modt, ng, mix, mode, weights, n_tiles=0):
    n, d = x_flat.shape
    nsub = FFN_TM // TM
    cw = 2 * LANES
    tok = lambda w: pl.BlockSpec((FFN_TM, w), lambda i: (i, 0))
    resident = lambda a: pl.BlockSpec(a.shape, lambda i: (0,) * a.ndim, pipeline_mode=pl.Buffered(1))
    stream = jax.ShapeDtypeStruct((n, d), F32)
    args = [x_flat, modt, ng]
    in_specs = [tok(d), pl.BlockSpec((nsub, N_MOD, d), lambda i: (i, 0, 0)), pl.BlockSpec((8, d), lambda i: (0, 0))]
    fused = mix is not None
    if fused:
        attn, cbog, u, o_f, o_b, hgg, conv_w, hn, w_out = mix
        nb8 = n // 8
        args += [attn, cbog, u, u, u, o_f, o_b, hgg, conv_w, hn, w_out]
        in_specs += [
            tok(2 * cw), tok(2 * cw), tok(cw),
            pl.BlockSpec((8, cw), lambda i: (jnp.maximum(i * (FFN_TM // 8) - 1, 0), 0)),
            pl.BlockSpec((8, cw), lambda i: (jnp.minimum((i + 1) * (FFN_TM // 8), nb8 - 1), 0)),
            tok(cw), tok(cw), resident(hgg), resident(conv_w), resident(hn), resident(w_out),
        ]
        assert len(in_specs) == 3 + N_MIX_REFS
    args += list(weights)
    in_specs += [resident(w) for w in weights]
    if mode == "router":
        out_specs = [_split_spec(FFN_TM)(d), tok(LANES), pl.BlockSpec((8, FFN_TM), lambda i: (0, i))]
        out_shape = [_split_shape(n, d),
                     jax.ShapeDtypeStruct((n, LANES), F32), jax.ShapeDtypeStruct((8, n), F32)]
    else:
        out_specs, out_shape = tok(d), stream
    return pl.pallas_call(
        functools.partial(_mixer_kernel, fused=fused, mode=mode, n_tiles=n_tiles),
        grid=(n // FFN_TM,),
        in_specs=in_specs,
        out_specs=out_specs,
        out_shape=out_shape,
        compiler_params=_cparams("parallel"),
        name=("mix_" if fused else "") + mode,
    )(*args)


def _gather_flat(data, idx):
    m = idx.shape[0]
    w = data.shape[1]
    mesh = plsc.VectorSubcoreMesh(core_axis_name="core", subcore_axis_name="subcore")

    @functools.partial(pl.kernel, out_type=jax.ShapeDtypeStruct((m, w), data.dtype), mesh=mesh)
    def gather(x_hbm, i_hbm, o_hbm):
        def body(i_vmem, o_vmem):
            pltpu.sync_copy(x_hbm.at[i_vmem.at[0]], o_vmem)

        pltpu.emit_pipeline(
            body,
            grid=(m // GATHER_WIN,),
            in_specs=[pl.BlockSpec((1, GATHER_WIN), lambda i: (0, i))],
            out_specs=[pl.BlockSpec((GATHER_WIN, w), lambda i: (i, 0))],
            core_axis_name=("core", "subcore"),
            dimension_semantics=(pltpu.PARALLEL,),
        )(i_hbm, o_hbm)

    return gather(data, idx.reshape(1, m))


def _scatter_flat(rows, idx, n_out):
    m, w = rows.shape
    mesh = plsc.VectorSubcoreMesh(core_axis_name="core", subcore_axis_name="subcore")

    @functools.partial(pl.kernel, out_type=jax.ShapeDtypeStruct((n_out, w), rows.dtype), mesh=mesh)
    def scatter(x_hbm, i_hbm, o_hbm):
        def body(x_vmem, i_vmem):
            pltpu.sync_copy(x_vmem, o_hbm.at[i_vmem.at[0]])

        pltpu.emit_pipeline(
            body,
            grid=(m // GATHER_WIN,),
            in_specs=[pl.BlockSpec((GATHER_WIN, w), lambda i: (i, 0)),
                      pl.BlockSpec((1, GATHER_WIN), lambda i: (0, i))],
            out_specs=[],
            core_axis_name=("core", "subcore"),
            dimension_semantics=(pltpu.PARALLEL,),
        )(x_hbm, i_hbm)

    return scatter(rows, idx.reshape(1, m))


def _col0_kernel(x_ref, o_ref):
    o_ref[...] = x_ref[...].T[0:1, :]


def _first_column(rows):
    n = rows.shape[0]
    blk = 1024
    return pl.pallas_call(
        _col0_kernel,
        grid=(n // blk,),
        in_specs=[pl.BlockSpec((blk, LANES), lambda i: (i, 0))],
        out_specs=pl.BlockSpec((1, blk), lambda i: (0, i)),
        out_shape=jax.ShapeDtypeStruct((1, n), rows.dtype),
        compiler_params=_cparams("parallel"),
        name="col0",
    )(rows)


def _cast_kernel(*refs):
    o_ref = refs[-1]
    cw = refs[0].shape[1]
    for s, x_ref in enumerate(refs[:-1]):
        o_ref[:, cw * s:cw * (s + 1)] = x_ref[...].astype(o_ref.dtype)


def _cast_bf16(w, layer):
    _, e, r, c = w.shape
    tr = 256
    cw = c // CAST_STREAMS
    chunk = lambda s: pl.BlockSpec((None, None, tr, cw), lambda i, j: (layer, i, j, s))
    return pl.pallas_call(
        _cast_kernel,
        grid=(e, r // tr),
        in_specs=[chunk(s) for s in range(CAST_STREAMS)],
        out_specs=pl.BlockSpec((None, tr, c), lambda i, j: (i, j, 0)),
        out_shape=jax.ShapeDtypeStruct((e, r, c), BF16),
        compiler_params=_cparams("parallel", "parallel"),
        name="cast",
    )(*([w] * CAST_STREAMS))


def _gather_rows(data, idx):
    s, r, w = data.shape
    m = idx.shape[0]
    idx_all = (idx[None, :] + (jnp.arange(s, dtype=jnp.int32) * r)[:, None]).reshape(s * m)
    return _gather_flat(data.reshape(s * r, w), idx_all).reshape(s, m, w)


def _expert_kernel(be_ref, nv_ref, xs_ref, wg_ref, wu_ref, wd_ref, o_ref):
    valid = pl.program_id(0) < nv_ref[0]

    @pl.when(valid)
    def _():
        xb = _load_split(xs_ref).astype(BF16)
        acc = jnp.zeros((xb.shape[0], wd_ref.shape[1]), F32)
        for f0 in range(0, wg_ref.shape[1], MOE_TF):
            g = _dot(xb, wg_ref[:, f0:f0 + MOE_TF])
            up = _dot(xb, wu_ref[:, f0:f0 + MOE_TF])
            acc = acc + _dot((g * _sigmoid(g) * up).astype(BF16), wd_ref[f0:f0 + MOE_TF, :])
        _store_split(o_ref, acc)

    @pl.when(jnp.logical_not(valid))
    def _():
        o_ref[...] = jnp.zeros(o_ref.shape, o_ref.dtype)


def _expert_call(block_expert, n_valid, xs, wg, wu, wd):
    n_slots = xs.shape[1]
    d = wg.shape[1]
    dff = wg.shape[2]
    nb = n_slots // MOE_BLK
    resident = lambda r, c: pl.BlockSpec((None, r, c), lambda i, be, nv: (be[i], 0, 0), pipeline_mode=pl.Buffered(1))
    grid_spec = pltpu.PrefetchScalarGridSpec(
        num_scalar_prefetch=2,
        grid=(nb,),
        in_specs=[_split_spec(MOE_BLK)(d), resident(d, dff), resident(d, dff), resident(dff, d)],
        out_specs=_split_spec(MOE_BLK)(d),
    )
    return pl.pallas_call(
        _expert_kernel,
        grid_spec=grid_spec,
        out_shape=_split_shape(n_slots, d),
        compiler_params=_cparams("arbitrary"),
        name="experts",
    )(block_expert, n_valid, xs, wg, wu, wd)


def _combine_kernel(x_ref, modt_ref, ng_ref, y1_ref, y2_ref, r_ref, o_ref):
    x = x_ref[...]
    rows = x.shape[0]
    f = r_ref[:, 2:3] * _load_split(y1_ref) + r_ref[:, 3:4] * _load_split(y2_ref)
    o_ref[...] = x + _mod_rows(modt_ref, 5, rows) * _rms(f, ng_ref[3:4, :])


def _combine_call(x_flat, modt, ng, y12, rinfo):
    n, d = x_flat.shape
    nsub = FFN_TM // TM
    nblk = n // FFN_TM
    tok = lambda w: pl.BlockSpec((FFN_TM, w), lambda i: (i, 0))
    second = pl.BlockSpec((GATHER_SPLIT, FFN_TM, d // GATHER_SPLIT // 2), lambda i: (0, i + nblk, 0))
    return pl.pallas_call(
        _combine_kernel,
        grid=(nblk,),
        in_specs=[tok(d), pl.BlockSpec((nsub, N_MOD, d), lambda i: (i, 0, 0)), pl.BlockSpec((8, d), lambda i: (0, 0)),
                  _split_spec(FFN_TM)(d), second, tok(LANES)],
        out_specs=tok(d),
        out_shape=jax.ShapeDtypeStruct((n, d), F32),
        compiler_params=_cparams("parallel"),
        name="combine",
    )(x_flat, modt, ng, y12, y12, rinfo)


def _router_weights(w_router):
    wr = jnp.pad(w_router, ((0, 0), (0, LANES - N_EXPERTS)))
    wr_hi = wr.astype(BF16)
    wr_lo = (wr - wr_hi.astype(F32)).astype(BF16)
    return jnp.stack([wr_hi, wr_lo])


def _moe_layer(x_flat, h, rinfo, rinfo_t, modt, ng, wg, wu, wd):
    n, d = x_flat.shape
    n_assign = 2 * n
    expert = jnp.concatenate([rinfo_t[0], rinfo_t[1]]).astype(jnp.int32)
    onehot = (expert[None, :] == jnp.arange(N_EXPERTS, dtype=jnp.int32)[:, None]).astype(jnp.int32)
    csum = jnp.cumsum(onehot, axis=1)
    rank = jnp.sum((csum - onehot) * onehot, axis=0)
    counts = csum[:, -1]
    padded = (counts + MOE_BLK - 1) // MOE_BLK * MOE_BLK
    pad_end = jnp.cumsum(padded)
    pad_start = pad_end - padded
    dest = jnp.sum(onehot * pad_start[:, None], axis=0) + rank
    nb = -(-n_assign // MOE_BLK) + N_EXPERTS
    row_quantum = 32 * GATHER_WIN // GATHER_SPLIT
    nb = -(-nb * MOE_BLK // row_quantum) * row_quantum // MOE_BLK
    n_slots = nb * MOE_BLK
    block_expert = jnp.minimum(
        jnp.searchsorted(pad_end, jnp.arange(nb, dtype=jnp.int32) * MOE_BLK, side="right"), N_EXPERTS - 1
    ).astype(jnp.int32)
    n_valid = (pad_end[-1:] // MOE_BLK).astype(jnp.int32)
    tok_rows = jnp.broadcast_to((jnp.arange(n_assign, dtype=jnp.int32) % n)[:, None], (n_assign, LANES))
    slot = jnp.arange(n_slots, dtype=jnp.int32)
    slot_used = (slot - jnp.repeat(pad_start[block_expert], MOE_BLK)) < jnp.repeat(counts[block_expert], MOE_BLK)
    slot_tok = jnp.where(slot_used, _first_column(_scatter_flat(tok_rows, dest, n_slots)).reshape(n_slots), 0)

    xs = _gather_rows(h, slot_tok)
    ys = _expert_call(block_expert, n_valid, xs, wg, wu, wd)
    y12 = _gather_rows(ys, dest)
    return _combine_call(x_flat, modt, ng, y12, rinfo)


def _rope_tables(n_ctx, length):
    rows = length // GRID_W
    row = jnp.repeat(jnp.arange(rows, dtype=F32), GRID_W)
    col = jnp.tile(jnp.arange(GRID_W, dtype=F32), rows)
    axis_dim = HEAD_DIM // 2
    inv_freq = ROPE_THETA ** (-jnp.arange(0, axis_dim, 2, dtype=F32) / axis_dim)
    ar = row[:, None] * inv_freq
    ac = col[:, None] * inv_freq
    cos = jnp.concatenate([jnp.cos(ar), jnp.cos(ar), jnp.cos(ac), jnp.cos(ac)], axis=1)
    sin = jnp.concatenate([-jnp.sin(ar), jnp.sin(ar), -jnp.sin(ac), jnp.sin(ac)], axis=1)
    cos = jnp.concatenate([jnp.ones((n_ctx, HEAD_DIM), F32), cos], axis=0)
    sin = jnp.concatenate([jnp.zeros((n_ctx, HEAD_DIM), F32), sin], axis=0)
    return jnp.tile(cos, (1, 2)), jnp.tile(sin, (1, 2))


def kernel(x, c, ctx, c_ctx, ada_w, ada_b, norm_g, w_in, w_out, attn_q_g, attn_k_g, conv_w, hgrn_lb, hgrn_g,
           ffn_w_gate, ffn_w_up, ffn_w_down, moe_router, moe_w_gate, moe_w_up, moe_w_down):
    bsz, length, d = x.shape
    n_ctx = ctx.shape[1]
    depth = w_in.shape[0]
    t = n_ctx + length
    nt = t // TM
    assert n_ctx == TM and length % TM == 0 and d == 8 * LANES
    for n_tok in (bsz * t, bsz * length):
        assert n_tok % FFN_TM == 0 and (2 * n_tok) % (32 * GATHER_WIN) == 0

    r_pad = -(-(bsz + 1) // 8) * 8
    c_all = jnp.concatenate([c, c_ctx[None, :], jnp.zeros((r_pad - bsz - 1, d), F32)], axis=0)
    mod_all = _ada_call(c_all, ada_w, ada_b).reshape(depth, r_pad, N_MOD, d)

    cos_t, sin_t = _rope_tables(n_ctx, length)
    lb_all = jnp.cumsum(jax.nn.softmax(hgrn_lb.astype(F32), axis=0), axis=0)
    lb_all = lb_all - lb_all[0]
    hconsts = _hgrn_consts()
    cw = 2 * LANES
    cidx = jnp.arange(cw)
    head_mean = ((cidx[:, None] // HEAD_DIM == cidx[None, :] // HEAD_DIM).astype(F32) / HEAD_DIM).astype(BF16)

    cos_f = jnp.tile(cos_t, (bsz, 1))
    sin_f = jnp.tile(sin_t, (bsz, 1))
    x_flat = jnp.concatenate([ctx, x], axis=1).reshape(bsz * t, d)
    for l in range(depth):
        mod_lat = mod_all[l, :bsz]
        mod_ctx = jnp.broadcast_to(mod_all[l, bsz][None], (bsz, N_MOD, d))
        mod = jnp.stack([mod_ctx, mod_lat], axis=1)
        modt = jnp.concatenate([mod[:, :1], jnp.broadcast_to(mod[:, 1:], (bsz, nt - 1, N_MOD, d))], axis=1)
        modt = modt.reshape(bsz * nt, N_MOD, d)
        ng = jnp.pad(norm_g[l], ((0, 4), (0, 0)))
        lb = lb_all[l].reshape(1, 2 * cw)
        gc = jnp.concatenate([jnp.log(lb), jnp.log1p(-lb), 1.0 - lb, jnp.zeros((5, 2 * cw), F32)], axis=0)
        qg = jnp.tile(attn_q_g[l], 2)[None, :]
        kg = jnp.tile(attn_k_g[l], 2)[None, :]

        q, k, vt, cbog, u, qv, kk, gg = _inproj_call(x_flat, modt, ng, w_in[l].astype(BF16), cos_f, sin_f, qg, kg, gc)
        attn = _attn_call(q, k, vt, bsz)
        o_f, o_b = _hgrn_call(qv, kk, gg, hconsts, bsz)
        mix = (attn, cbog, u, o_f, o_b, hgrn_g[l][None, :],
               jnp.pad(conv_w[l], ((0, 5), (0, 0))), head_mean, w_out[l].astype(BF16))
        if l % 2 == 0:
            weights = (ffn_w_gate[l // 2].astype(BF16), ffn_w_up[l // 2].astype(BF16), ffn_w_down[l // 2].astype(BF16))
        else:
            weights = (_router_weights(moe_router[l // 2]),)
            experts = (_cast_bf16(moe_w_gate, l // 2), _cast_bf16(moe_w_up, l // 2), _cast_bf16(moe_w_down, l // 2))
        mode = "ffn" if l % 2 == 0 else "router"

        if l < depth - 1:
            if mode == "ffn":
                x_flat = _mixer_call(x_flat, modt, ng, mix, mode, weights, nt)
            else:
                x_flat = _mixer_call(x_flat, modt, ng, mix, "plain", (), nt)
                h, rinfo, rinfo_t = _mixer_call(x_flat, modt, ng, None, mode, weights)
                x_flat = _moe_layer(x_flat, h, rinfo, rinfo_t, modt, ng, *experts)
            continue

        x_all = _mixer_call(x_flat, modt, ng, mix, "plain", (), nt).reshape(bsz, t, d)
        x_flat = x_all[:, n_ctx:, :].reshape(bsz * length, d)
        modt = jnp.broadcast_to(mod[:, 1:], (bsz, length // TM, N_MOD, d)).reshape(bsz * (length // TM), N_MOD, d)
        if mode == "ffn":
            x_flat = _mixer_call(x_flat, modt, ng, None, mode, weights)
        else:
            h, rinfo, rinfo_t = _mixer_call(x_flat, modt, ng, None, mode, weights)
            x_flat = _moe_layer(x_flat, h, rinfo, rinfo_t, modt, ng, *experts)
        return x_flat.reshape(bsz, length, d)
```

```python
import functools

import jax
import jax.numpy as jnp
from jax import lax
from jax.experimental import pallas as pl
from jax.experimental.pallas import tpu as pltpu
from jax.experimental.pallas import tpu_sc as plsc

F32 = jnp.float32
BF16 = jnp.bfloat16

HEAD_DIM = 64
GRID_W = 64
ROPE_THETA = 10000.0
ATTN_SCALE = HEAD_DIM ** -0.5
LOG2_E = 1.4426950408889634
HGRN_CHUNK = 64
HGRN_SUB = 8
VT_ROWS = HEAD_DIM + 16
ATTN_AHEAD = 6
N_EXPERTS = 8
N_MOD = 6
EPS = 1e-6

LANES = 128
TM = 256
FFN_TM = 512
FFN_TF = 512
N_MIX_REFS = 11
CAST_STREAMS = 4
MOE_BLK = 512
MOE_TF = 512
GATHER_WIN = 128
GATHER_SPLIT = 2
NEG_BIG = -1e30
VMEM_LIMIT = 56 * 1024 * 1024


def _cparams(*sem):
    return pltpu.CompilerParams(dimension_semantics=sem, vmem_limit_bytes=VMEM_LIMIT)


def _sigmoid(z):
    return 1.0 / (1.0 + jnp.exp(-z))


def _dot(a, b):
    return jnp.dot(a, b, preferred_element_type=F32)


def _dot_nt(a, b):
    return lax.dot_general(a, b, (((1,), (1,)), ((), ())), preferred_element_type=F32)


def _dot_tn(a, b):
    return lax.dot_general(a, b, (((0,), (0,)), ((), ())), preferred_element_type=F32)


def _rms(x, g):
    return x * lax.rsqrt(jnp.mean(x * x, axis=-1, keepdims=True) + EPS) * g


def _ada_kernel(c_ref, w_ref, b_ref, o_ref):
    c = c_ref[...]
    s = (c * _sigmoid(c)).astype(BF16)
    o_ref[...] = _dot(s, w_ref[...].astype(BF16)) + b_ref[...]


def _ada_call(c_all, ada_w, ada_b):
    depth, d, n = ada_w.shape
    r = c_all.shape[0]
    tn = 512
    return pl.pallas_call(
        _ada_kernel,
        grid=(depth, n // tn),
        in_specs=[
            pl.BlockSpec((r, d), lambda l, j: (0, 0)),
            pl.BlockSpec((None, d, tn), lambda l, j: (l, 0, j)),
            pl.BlockSpec((None, 1, tn), lambda l, j: (l, 0, j)),
        ],
        out_specs=pl.BlockSpec((None, r, tn), lambda l, j: (l, 0, j)),
        out_shape=jax.ShapeDtypeStruct((depth, r, n), F32),
        compiler_params=_cparams("parallel", "parallel"),
        name="ada",
    )(c_all, ada_w, ada_b.reshape(depth, 1, n))


def _inproj_kernel(x_ref, modt_ref, ng_ref, w_ref, cos_ref, sin_ref, qg_ref, kg_ref, gc_ref,
                   q_ref, k_ref, vt_ref, cbog_ref, u_ref, qv_ref, kk_ref, gg_ref):
    x = x_ref[...]
    rows = x.shape[0]
    h = _rms(x, ng_ref[0:1, :]) * (1.0 + _mod_rows(modt_ref, 1, rows)) + _mod_rows(modt_ref, 0, rows)
    hb = h.astype(BF16)

    def proj(c0, n):
        return _dot(hb, w_ref[:, c0:c0 + n])

    cos = cos_ref[...]
    sin = sin_ref[...]
    lane = lax.broadcasted_iota(jnp.int32, (rows, LANES), 1)
    first_half = (lane % 32) < 16
    lo = lane < HEAD_DIM

    def norm_rope(blk, g):
        b2 = blk * blk
        s_lo = jnp.sum(jnp.where(lo, b2, 0.0), axis=-1, keepdims=True)
        s_hi = jnp.sum(jnp.where(lo, 0.0, b2), axis=-1, keepdims=True)
        v = blk * lax.rsqrt(jnp.where(lo, s_lo, s_hi) * (1.0 / HEAD_DIM) + EPS) * g
        partner = jnp.where(first_half, pltpu.roll(v, LANES - 16, 1), pltpu.roll(v, 16, 1))
        return v * cos + partner * sin

    qa = proj(0, 4 * LANES)
    for j in range(4):
        qn = norm_rope(qa[:, LANES * j:LANES * (j + 1)], qg_ref[...]) * (ATTN_SCALE * LOG2_E)
        sw = pltpu.roll(qn, HEAD_DIM, 1)
        if j < 2:
            q_ref[2 * j] = jnp.where(lo, qn, 0.0).astype(BF16)
            q_ref[2 * j + 1] = jnp.where(lo, sw, 0.0).astype(BF16)
        else:
            q_ref[2 * j] = jnp.where(lo, 0.0, sw).astype(BF16)
            q_ref[2 * j + 1] = jnp.where(lo, 0.0, qn).astype(BF16)

    kv = proj(4 * LANES, 2 * LANES)
    k_ref[...] = norm_rope(kv[:, :LANES], kg_ref[...]).astype(BF16)
    vt = kv[:, LANES:].T
    ones = jnp.ones((VT_ROWS - HEAD_DIM, TM), F32)
    for s in range(rows // TM):
        for g in range(2):
            vt_ref[g, s] = jnp.concatenate(
                [vt[HEAD_DIM * g:HEAD_DIM * (g + 1), TM * s:TM * (s + 1)], ones], axis=0).astype(BF16)

    c0 = 6 * LANES
    c3 = proj(c0, 6 * LANES)
    cw = 2 * LANES
    u_ref[...] = c3[:, cw:2 * cw] * c3[:, 2 * cw:3 * cw]

    hg = proj(c0 + 6 * LANES, 10 * LANES)
    zf, zb, iv, hq, og = (hg[:, cw * i:cw * (i + 1)] for i in range(5))
    cbog_ref[:, :cw] = c3[:, :cw].astype(BF16)
    cbog_ref[:, cw:] = og.astype(BF16)
    qv_ref[:, :cw] = (hq * _sigmoid(hq)).astype(BF16)
    qv_ref[:, cw:] = iv.astype(BF16)
    for d, z in enumerate((zf, zb)):
        log_lb = gc_ref[0:1, cw * d:cw * (d + 1)]
        log1m_lb = gc_ref[1:2, cw * d:cw * (d + 1)]
        one_m_lb = gc_ref[2:3, cw * d:cw * (d + 1)]
        t = jnp.exp(-jnp.abs(z))
        log_sig = jnp.minimum(z, 0.0) - jnp.log1p(t)
        a2 = log1m_lb + log_sig
        log_f = jnp.maximum(log_lb, a2) + jnp.log1p(jnp.exp(-jnp.abs(log_lb - a2)))
        sig_neg = jnp.where(z >= 0, t, 1.0) / (1.0 + t)
        gg_ref[:, cw * d:cw * (d + 1)] = log_f
        kk_ref[:, cw * d:cw * (d + 1)] = (one_m_lb * sig_neg).astype(BF16)


def _inproj_call(x_flat, modt, ng, w1, cos_f, sin_f, qg, kg, gc):
    n, d = x_flat.shape
    nsub = FFN_TM // TM
    tok = lambda w: pl.BlockSpec((FFN_TM, w), lambda i: (i, 0))
    const = lambda a: pl.BlockSpec(a.shape, lambda i: (0,) * a.ndim, pipeline_mode=pl.Buffered(1))
    return pl.pallas_call(
        _inproj_kernel,
        grid=(n // FFN_TM,),
        in_specs=[
            tok(d),
            pl.BlockSpec((nsub, N_MOD, d), lambda i: (i, 0, 0)),
            const(ng), const(w1), tok(LANES), tok(LANES), const(qg), const(kg), const(gc),
        ],
        out_specs=[
            pl.BlockSpec((8, FFN_TM, LANES), lambda i: (0, i, 0)),
            tok(LANES),
            pl.BlockSpec((2, nsub, VT_ROWS, TM), lambda i: (0, i, 0, 0)),
            tok(4 * LANES),
            tok(2 * LANES),
            tok(4 * LANES),
            tok(4 * LANES),
            tok(4 * LANES),
        ],
        out_shape=[
            jax.ShapeDtypeStruct((8, n, LANES), BF16),
            jax.ShapeDtypeStruct((n, LANES), BF16),
            jax.ShapeDtypeStruct((2, n // TM, VT_ROWS, TM), BF16),
            jax.ShapeDtypeStruct((n, 4 * LANES), BF16),
            jax.ShapeDtypeStruct((n, 2 * LANES), F32),
            jax.ShapeDtypeStruct((n, 4 * LANES), BF16),
            jax.ShapeDtypeStruct((n, 4 * LANES), BF16),
            jax.ShapeDtypeStruct((n, 4 * LANES), F32),
        ],
        compiler_params=_cparams("parallel"),
        name="inproj",
    )(x_flat, modt, ng, w1, cos_f, sin_f, qg, kg, gc)


def _attn_kernel(q_ref, k_ref, vt_ref, o_ref, m_ref, acc_ref, *, n_kv):
    def score(j, h):
        return _dot_nt(k_ref[TM * j:TM * (j + 1), :], q_ref[h])

    def attend(n_tiles):
        steps = [(j, h) for j in range(n_tiles) for h in range(8)]
        scores = [score(j, h) for j, h in steps[:ATTN_AHEAD]]
        for i, (j, h) in enumerate(steps):
            if i + ATTN_AHEAD < len(steps):
                scores.append(score(*steps[i + ATTN_AHEAD]))
            s = scores.pop(0)
            m_tile = jnp.max(s, axis=0, keepdims=True)
            if j == 0:
                m_new = m_tile
                acc_ref[h] = _dot(vt_ref[h // 4, j], jnp.exp2(s - m_new).astype(BF16))
            else:
                m_old = m_ref[h]
                m_new = jnp.maximum(m_old, m_tile)
                pv = _dot(vt_ref[h // 4, j], jnp.exp2(s - m_new).astype(BF16))
                acc_ref[h] = jnp.exp2(m_old - m_new) * acc_ref[h] + pv
            m_ref[h] = m_new
        outs = [acc_ref[h, :HEAD_DIM, :] / acc_ref[h, HEAD_DIM:HEAD_DIM + 1, :] for h in range(8)]
        o_ref[...] = jnp.concatenate(outs, axis=0).T.astype(BF16)

    @pl.when(pl.program_id(1) == 0)
    def _():
        attend(1)

    @pl.when(pl.program_id(1) > 0)
    def _():
        attend(n_kv)


def _attn_call(q, k, vt, b):
    n = k.shape[0]
    t = n // b
    nt = t // TM
    return pl.pallas_call(
        functools.partial(_attn_kernel, n_kv=nt),
        grid=(b, nt),
        in_specs=[
            pl.BlockSpec((8, TM, LANES), lambda i, j: (0, i * nt + j, 0)),
            pl.BlockSpec((t, LANES), lambda i, j: (i, 0)),
            pl.BlockSpec((2, nt, VT_ROWS, TM), lambda i, j: (0, i, 0, 0)),
        ],
        out_specs=pl.BlockSpec((TM, 4 * LANES), lambda i, j: (i * nt + j, 0)),
        out_shape=jax.ShapeDtypeStruct((n, 4 * LANES), BF16),
        scratch_shapes=[pltpu.VMEM((8, 1, TM), F32), pltpu.VMEM((8, VT_ROWS, TM), F32)],
        compiler_params=_cparams("parallel", "parallel"),
        name="attn",
    )(q, k, vt)


def _split3(g):
    hi = g.astype(BF16)
    r1 = g - hi.astype(F32)
    mid = r1.astype(BF16)
    lo = (r1 - mid.astype(F32)).astype(BF16)
    return hi, mid, lo


def _hgrn_kernel(qvf_ref, kf_ref, gf_ref, qvb_ref, kb_ref, gb_ref, tri_ref, trij_ref, j_ref, e_ref, bd_ref,
                 md_ref, mo_ref, of_ref, ob_ref, st_ref, x_s):
    hw = 2 * LANES
    nch = TM // HGRN_CHUNK

    @pl.when(pl.program_id(1) == 0)
    def _():
        st_ref[...] = jnp.zeros(st_ref.shape, F32)

    flip = j_ref[...]

    def cumsum(m_ref, g):
        m = m_ref[...]
        hi, mid, lo = _split3(g)
        return _dot(m, hi) + _dot(m, mid) + _dot(m, lo)

    qvf = qvf_ref[...]
    fl = _dot(flip, jnp.concatenate([qvb_ref[...], kb_ref[...]], axis=1))
    q = jnp.concatenate([qvf[:, :hw].astype(F32), fl[:, :hw]], axis=0)
    v = jnp.concatenate([qvf[:, hw:].astype(F32), fl[:, hw:2 * hw]], axis=0)
    k = jnp.concatenate([kf_ref[...].astype(F32), fl[:, 2 * hw:]], axis=0)
    b = jnp.concatenate([cumsum(tri_ref, gf_ref[...]), cumsum(trij_ref, gb_ref[...])], axis=0) * LOG2_E

    ngr = 2 * TM // HGRN_SUB
    nsub = HGRN_CHUNK // HGRN_SUB
    q3 = q.reshape(ngr, HGRN_SUB, hw)
    b3 = b.reshape(ngr, HGRN_SUB, hw)
    k3 = k.reshape(ngr, HGRN_SUB, hw)
    for s in range(HGRN_SUB):
        x = q3 * jnp.exp2(jnp.minimum(b3 - b3[:, s:s + 1, :], 0.0)) * k3[:, s:s + 1, :]
        x_s[:, hw * s:hw * (s + 1)] = x.reshape(2 * TM, hw).astype(BF16)
    a = _dot(x_s[...], e_ref[...])

    bd = bd_ref[...]
    bd16 = bd.astype(BF16)
    md = md_ref[...]
    mo = mo_ref[...]
    for d in range(2):
        st = st_ref[d]
        outs = []
        for c in range(nch):
            r0 = TM * d + HGRN_CHUNK * c
            bc = b[r0:r0 + HGRN_CHUNK]
            qc = q[r0:r0 + HGRN_CHUNK]
            kc = k[r0:r0 + HGRN_CHUNK]
            vc = v[r0:r0 + HGRN_CHUNK]
            ac = a[r0:r0 + HGRN_CHUNK] * md
            r_end = [bc[HGRN_SUB * (j + 1) - 1:HGRN_SUB * (j + 1), :] for j in range(nsub)]
            r_own = jnp.concatenate([jnp.broadcast_to(r, (HGRN_SUB, hw)) for r in r_end], axis=0)
            kt_bd = jnp.concatenate([(kc * jnp.exp2(r_own - bc)).astype(BF16)] * 4, axis=0) * bd16
            q_rel = jnp.concatenate(
                [qc * jnp.exp2(jnp.minimum(bc - r_end[j], 0.0)) for j in range(nsub - 1)], axis=0).astype(BF16)
            p = _dot_nt(q_rel, kt_bd) * mo
            for j in range(nsub - 1):
                ac = ac + p[HGRN_CHUNK * j:HGRN_CHUNK * (j + 1)]
            bl = r_end[-1]
            qe = (qc * jnp.exp2(bc)).astype(BF16)
            ke = (kc * jnp.exp2(bl - bc)).astype(BF16)
            vcb = vc.astype(BF16)
            vbd = jnp.concatenate([vcb] * 4, axis=0) * bd16
            outs.append(_dot(ac.astype(BF16), vbd) + _dot_nt(qe, st.astype(BF16)))
            st = st * jnp.exp2(bl) + _dot_tn(vcb, ke) * bd
        st_ref[d] = st
        od = jnp.concatenate(outs, axis=0).astype(BF16)
        if d == 0:
            of_ref[...] = od
        else:
            ob_ref[...] = _dot(flip, od).astype(BF16)


def _hgrn_consts():
    hw = 2 * LANES
    r = jnp.arange(TM)
    same_chunk = (r[:, None] // HGRN_CHUNK) == (r[None, :] // HGRN_CHUNK)
    tri = (same_chunk & (r[None, :] <= r[:, None])).astype(F32)
    flip = (r[:, None] + r[None, :] == TM - 1).astype(F32)
    trij = tri @ flip
    c = jnp.arange(hw)
    s = jnp.arange(HGRN_SUB)
    e = ((c[None, :, None] // HEAD_DIM == c[None, None, :] // HGRN_CHUNK)
         & (c[None, None, :] % HGRN_SUB == s[:, None, None])).astype(BF16).reshape(HGRN_SUB * hw, hw)
    bd = (c[:, None] // HEAD_DIM == c[None, :] // HEAD_DIM).astype(F32)
    tt = jnp.arange(HGRN_CHUNK)[:, None]
    ss = (c % HGRN_CHUNK)[None, :]
    md = ((ss // HGRN_SUB == tt // HGRN_SUB) & (ss <= tt)).astype(F32)
    mo = jnp.concatenate([((ss // HGRN_SUB == j) & (tt // HGRN_SUB > j)).astype(F32)
                          for j in range(HGRN_CHUNK // HGRN_SUB - 1)], axis=0)
    return tri.astype(BF16), trij.astype(BF16), flip.astype(BF16), e, bd, md, mo


def _hgrn_call(qv, kk, gg, consts, b):
    n = qv.shape[0]
    nt = n // b // TM
    hw = 2 * LANES
    tri, trij, flip, e, bd, md, mo = consts
    fwd = lambda i, j: i * nt + j
    bwd = lambda i, j: i * nt + jnp.where(j == 0, 0, nt - j)
    const2 = lambda a: pl.BlockSpec(a.shape, lambda i, j: (0,) * a.ndim)
    return pl.pallas_call(
        _hgrn_kernel,
        grid=(b, nt),
        in_specs=[
            pl.BlockSpec((TM, 2 * hw), lambda i, j: (fwd(i, j), 0)),
            pl.BlockSpec((TM, hw), lambda i, j: (fwd(i, j), 0)),
            pl.BlockSpec((TM, hw), lambda i, j: (fwd(i, j), 0)),
            pl.BlockSpec((TM, 2 * hw), lambda i, j: (bwd(i, j), 0)),
            pl.BlockSpec((TM, hw), lambda i, j: (bwd(i, j), 1)),
            pl.BlockSpec((TM, hw), lambda i, j: (bwd(i, j), 1)),
            const2(tri), const2(trij), const2(flip), const2(e), const2(bd), const2(md), const2(mo),
        ],
        out_specs=[
            pl.BlockSpec((TM, hw), lambda i, j: (fwd(i, j), 0)),
            pl.BlockSpec((TM, hw), lambda i, j: (bwd(i, j), 0)),
        ],
        out_shape=[jax.ShapeDtypeStruct((n, hw), BF16)] * 2,
        scratch_shapes=[
            pltpu.VMEM((2, hw, hw), F32),
            pltpu.VMEM((2 * TM, HGRN_SUB * hw), BF16),
        ],
        compiler_params=_cparams("parallel", "arbitrary"),
        name="hgrn",
    )(qv, kk, gg, qv, kk, gg, tri, trij, flip, e, bd, md, mo)


def _mix_out(x_ref, modt_ref, ng_ref, attn_ref, cbog_ref, u_ref, up_ref, un_ref, of_ref, ob_ref,
             hgg_ref, cw_ref, hn_ref, w_ref, *, n_tiles):
    rows = x_ref.shape[0]
    nsub = rows // TM
    cw = 2 * LANES
    u = u_ref[...]
    row = lax.broadcasted_iota(jnp.int32, (rows, cw), 0)
    u_prev = jnp.where(row == 0, up_ref[7:8, :], pltpu.roll(u, 1, 0))
    u_next = jnp.where(row == rows - 1, un_ref[0:1, :], pltpu.roll(u, rows - 1, 0))
    for s in range(nsub):
        t = (pl.program_id(0) * nsub + s) % n_tiles
        u_prev = jnp.where(t >= 2, u_prev, jnp.where(row == s * TM, 0.0, u_prev))
        u_next = jnp.where((t >= 1) & (t < n_tiles - 1), u_next, jnp.where(row == s * TM + TM - 1, 0.0, u_next))
    cbog = cbog_ref[...].astype(F32)
    conv = cbog[:, :cw] * (cw_ref[0:1, :] * u_prev + cw_ref[1:2, :] * u + cw_ref[2:3, :] * u_next)

    o = of_ref[...].astype(F32) + ob_ref[...].astype(F32)
    ms = _dot((o * o).astype(BF16), hn_ref[...])
    og = cbog[:, cw:]
    hg = o * lax.rsqrt(ms + EPS) * hgg_ref[...] * (og * _sigmoid(og))

    y = (_dot(attn_ref[...], w_ref[0:2 * cw, :]) + _dot(conv.astype(BF16), w_ref[2 * cw:3 * cw, :])
         + _dot(hg.astype(BF16), w_ref[3 * cw:4 * cw, :]))
    return x_ref[...] + _mod_rows(modt_ref, 2, rows) * _rms(y, ng_ref[1:2, :])


def _mod_rows(modt_ref, i, rows):
    sub = rows // modt_ref.shape[0]
    return jnp.concatenate(
        [jnp.broadcast_to(modt_ref[s, i:i + 1, :], (sub, modt_ref.shape[2])) for s in range(modt_ref.shape[0])],
        axis=0)


def _store_split(ref, val):
    w = ref.shape[2]
    for c in range(ref.shape[0]):
        lo = val[:, 2 * w * c:2 * w * c + w].astype(BF16).astype(F32)
        hi = val[:, 2 * w * c + w:2 * w * (c + 1)].astype(BF16).astype(F32)
        ref[c] = lax.bitcast_convert_type(hi, jnp.uint32) | (lax.bitcast_convert_type(lo, jnp.uint32) >> 16)


def _load_split(ref):
    cols = []
    for c in range(ref.shape[0]):
        word = ref[c]
        cols.append(lax.bitcast_convert_type(word << 16, F32))
        cols.append(lax.bitcast_convert_type(word & jnp.uint32(0xFFFF0000), F32))
    return jnp.concatenate(cols, axis=1)


def _split_shape(n, d):
    return jax.ShapeDtypeStruct((GATHER_SPLIT, n, d // GATHER_SPLIT // 2), jnp.uint32)


def _split_spec(rows):
    return lambda d: pl.BlockSpec((GATHER_SPLIT, rows, d // GATHER_SPLIT // 2), lambda i, *_: (0, i, 0))


def _ffn_body(x, modt_ref, ng_ref, wg_ref, wu_ref, wd_ref):
    rows = x.shape[0]
    h = (_rms(x, ng_ref[2:3, :]) * (1.0 + _mod_rows(modt_ref, 4, rows)) + _mod_rows(modt_ref, 3, rows)).astype(BF16)
    acc = jnp.zeros(x.shape, F32)
    dff = wg_ref.shape[1]
    for f0 in range(0, dff, FFN_TF):
        f1 = min(f0 + FFN_TF, dff)
        g = _dot(h, wg_ref[:, f0:f1])
        up = _dot(h, wu_ref[:, f0:f1])
        acc = acc + _dot((g * _sigmoid(g) * up).astype(BF16), wd_ref[f0:f1, :])
    return x + _mod_rows(modt_ref, 5, rows) * _rms(acc, ng_ref[3:4, :])


def _router_body(x, modt_ref, ng_ref, wr_ref, h_ref, r_ref, rt_ref):
    rows = x.shape[0]
    h = _rms(x, ng_ref[2:3, :]) * (1.0 + _mod_rows(modt_ref, 4, rows)) + _mod_rows(modt_ref, 3, rows)
    _store_split(h_ref, h)
    h_hi = h.astype(BF16)
    h_lo = (h - h_hi.astype(F32)).astype(BF16)
    logits = _dot(h_hi, wr_ref[0]) + _dot(h_lo, wr_ref[0]) + _dot(h_hi, wr_ref[1])
    lane = lax.broadcasted_iota(jnp.int32, logits.shape, 1)
    lg = jnp.where(lane < N_EXPERTS, logits, NEG_BIG)
    m1 = jnp.max(lg, axis=-1, keepdims=True)
    i1 = jnp.min(jnp.where(lg == m1, lane, LANES), axis=-1, keepdims=True)
    lg2 = jnp.where(lane == i1, NEG_BIG, lg)
    m2 = jnp.max(lg2, axis=-1, keepdims=True)
    i2 = jnp.min(jnp.where(lg2 == m2, lane, LANES), axis=-1, keepdims=True)
    e2 = jnp.exp(m2 - m1)
    w1 = 1.0 / (1.0 + e2)
    w2 = e2 / (1.0 + e2)
    r = jnp.where(lane == 0, i1.astype(F32),
                  jnp.where(lane == 1, i2.astype(F32), jnp.where(lane == 2, w1, jnp.where(lane == 3, w2, 0.0))))
    r_ref[...] = r
    rt_ref[...] = r.T[0:8, :]


def _mixer_kernel(*refs, fused, mode, n_tiles):
    x_ref, modt_ref, ng_ref = refs[:3]
    pos = 3
    if fused:
        x = _mix_out(x_ref, modt_ref, ng_ref, *refs[pos:pos + N_MIX_REFS], n_tiles=n_tiles)
        pos += N_MIX_REFS
    else:
        x = x_ref[...]
    if mode == "ffn":
        wg_ref, wu_ref, wd_ref, o_ref = refs[pos:]
        o_ref[...] = _ffn_body(x, modt_ref, ng_ref, wg_ref, wu_ref, wd_ref)
    elif mode == "router":
        wr_ref, *outs = refs[pos:]
        _router_body(x, modt_ref, ng_ref, wr_ref, *outs)
    else:
        refs[pos][...] = x


def _mixer_call(x_flat, modt, ng, mix, mode, weights, n_tiles=0):
    n, d = x_flat.shape
    nsub = FFN_TM // TM
    cw = 2 * LANES
    tok = lambda w: pl.BlockSpec((FFN_TM, w), lambda i: (i, 0))
    resident = lambda a: pl.BlockSpec(a.shape, lambda i: (0,) * a.ndim, pipeline_mode=pl.Buffered(1))
    stream = jax.ShapeDtypeStruct((n, d), F32)
    args = [x_flat, modt, ng]
    in_specs = [tok(d), pl.BlockSpec((nsub, N_MOD, d), lambda i: (i, 0, 0)), pl.BlockSpec((8, d), lambda i: (0, 0))]
    fused = mix is not None
    if fused:
        attn, cbog, u, o_f, o_b, hgg, conv_w, hn, w_out = mix
        nb8 = n // 8
        args += [attn, cbog, u, u, u, o_f, o_b, hgg, conv_w, hn, w_out]
        in_specs += [
            tok(2 * cw), tok(2 * cw), tok(cw),
            pl.BlockSpec((8, cw), lambda i: (jnp.maximum(i * (FFN_TM // 8) - 1, 0), 0)),
            pl.BlockSpec((8, cw), lambda i: (jnp.minimum((i + 1) * (FFN_TM // 8), nb8 - 1), 0)),
            tok(cw), tok(cw), resident(hgg), resident(conv_w), resident(hn), resident(w_out),
        ]
        assert len(in_specs) == 3 + N_MIX_REFS
    args += list(weights)
    in_specs += [resident(w) for w in weights]
    if mode == "router":
        out_specs = [_split_spec(FFN_TM)(d), tok(LANES), pl.BlockSpec((8, FFN_TM), lambda i: (0, i))]
        out_shape = [_split_shape(n, d),
                     jax.ShapeDtypeStruct((n, LANES), F32), jax.ShapeDtypeStruct((8, n), F32)]
    else:
        out_specs, out_shape = tok(d), stream
    return pl.pallas_call(
        functools.partial(_mixer_kernel, fused=fused, mode=mode, n_tiles=n_tiles),
        grid=(n // FFN_TM,),
        in_specs=in_specs,
        out_specs=out_specs,
        out_shape=out_shape,
        compiler_params=_cparams("parallel"),
        name=("mix_" if fused else "") + mode,
    )(*args)


def _gather_flat(data, idx):
    m = idx.shape[0]
    w = data.shape[1]
    mesh = plsc.VectorSubcoreMesh(core_axis_name="core", subcore_axis_name="subcore")

    @functools.partial(pl.kernel, out_type=jax.ShapeDtypeStruct((m, w), data.dtype), mesh=mesh)
    def gather(x_hbm, i_hbm, o_hbm):
        def body(i_vmem, o_vmem):
            pltpu.sync_copy(x_hbm.at[i_vmem.at[0]], o_vmem)

        pltpu.emit_pipeline(
            body,
            grid=(m // GATHER_WIN,),
            in_specs=[pl.BlockSpec((1, GATHER_WIN), lambda i: (0, i))],
            out_specs=[pl.BlockSpec((GATHER_WIN, w), lambda i: (i, 0))],
            core_axis_name=("core", "subcore"),
            dimension_semantics=(pltpu.PARALLEL,),
        )(i_hbm, o_hbm)

    return gather(data, idx.reshape(1, m))


def _scatter_flat(rows, idx, n_out):
    m, w = rows.shape
    mesh = plsc.VectorSubcoreMesh(core_axis_name="core", subcore_axis_name="subcore")

    @functools.partial(pl.kernel, out_type=jax.ShapeDtypeStruct((n_out, w), rows.dtype), mesh=mesh)
    def scatter(x_hbm, i_hbm, o_hbm):
        def body(x_vmem, i_vmem):
            pltpu.sync_copy(x_vmem, o_hbm.at[i_vmem.at[0]])

        pltpu.emit_pipeline(
            body,
            grid=(m // GATHER_WIN,),
            in_specs=[pl.BlockSpec((GATHER_WIN, w), lambda i: (i, 0)),
                      pl.BlockSpec((1, GATHER_WIN), lambda i: (0, i))],
            out_specs=[],
            core_axis_name=("core", "subcore"),
            dimension_semantics=(pltpu.PARALLEL,),
        )(x_hbm, i_hbm)

    return scatter(rows, idx.reshape(1, m))


def _col0_kernel(x_ref, o_ref):
    o_ref[...] = x_ref[...].T[0:1, :]


def _first_column(rows):
    n = rows.shape[0]
    blk = 1024
    return pl.pallas_call(
        _col0_kernel,
        grid=(n // blk,),
        in_specs=[pl.BlockSpec((blk, LANES), lambda i: (i, 0))],
        out_specs=pl.BlockSpec((1, blk), lambda i: (0, i)),
        out_shape=jax.ShapeDtypeStruct((1, n), rows.dtype),
        compiler_params=_cparams("parallel"),
        name="col0",
    )(rows)


def _cast_kernel(*refs):
    o_ref = refs[-1]
    cw = refs[0].shape[1]
    for s, x_ref in enumerate(refs[:-1]):
        o_ref[:, cw * s:cw * (s + 1)] = x_ref[...].astype(o_ref.dtype)


def _cast_bf16(w, layer):
    _, e, r, c = w.shape
    tr = 256
    cw = c // CAST_STREAMS
    chunk = lambda s: pl.BlockSpec((None, None, tr, cw), lambda i, j: (layer, i, j, s))
    return pl.pallas_call(
        _cast_kernel,
        grid=(e, r // tr),
        in_specs=[chunk(s) for s in range(CAST_STREAMS)],
        out_specs=pl.BlockSpec((None, tr, c), lambda i, j: (i, j, 0)),
        out_shape=jax.ShapeDtypeStruct((e, r, c), BF16),
        compiler_params=_cparams("parallel", "parallel"),
        name="cast",
    )(*([w] * CAST_STREAMS))


def _gather_rows(data, idx):
    s, r, w = data.shape
    m = idx.shape[0]
    idx_all = (idx[None, :] + (jnp.arange(s, dtype=jnp.int32) * r)[:, None]).reshape(s * m)
    return _gather_flat(data.reshape(s * r, w), idx_all).reshape(s, m, w)


def _expert_kernel(be_ref, nv_ref, xs_ref, wg_ref, wu_ref, wd_ref, o_ref):
    valid = pl.program_id(0) < nv_ref[0]

    @pl.when(valid)
    def _():
        xb = _load_split(xs_ref).astype(BF16)
        acc = jnp.zeros((xb.shape[0], wd_ref.shape[1]), F32)
        for f0 in range(0, wg_ref.shape[1], MOE_TF):
            g = _dot(xb, wg_ref[:, f0:f0 + MOE_TF])
            up = _dot(xb, wu_ref[:, f0:f0 + MOE_TF])
            acc = acc + _dot((g * _sigmoid(g) * up).astype(BF16), wd_ref[f0:f0 + MOE_TF, :])
        _store_split(o_ref, acc)

    @pl.when(jnp.logical_not(valid))
    def _():
        o_ref[...] = jnp.zeros(o_ref.shape, o_ref.dtype)


def _expert_call(block_expert, n_valid, xs, wg, wu, wd):
    n_slots = xs.shape[1]
    d = wg.shape[1]
    dff = wg.shape[2]
    nb = n_slots // MOE_BLK
    resident = lambda r, c: pl.BlockSpec((None, r, c), lambda i, be, nv: (be[i], 0, 0), pipeline_mode=pl.Buffered(1))
    grid_spec = pltpu.PrefetchScalarGridSpec(
        num_scalar_prefetch=2,
        grid=(nb,),
        in_specs=[_split_spec(MOE_BLK)(d), resident(d, dff), resident(d, dff), resident(dff, d)],
        out_specs=_split_spec(MOE_BLK)(d),
    )
    return pl.pallas_call(
        _expert_kernel,
        grid_spec=grid_spec,
        out_shape=_split_shape(n_slots, d),
        compiler_params=_cparams("arbitrary"),
        name="experts",
    )(block_expert, n_valid, xs, wg, wu, wd)


def _combine_kernel(x_ref, modt_ref, ng_ref, y1_ref, y2_ref, r_ref, o_ref):
    x = x_ref[...]
    rows = x.shape[0]
    f = r_ref[:, 2:3] * _load_split(y1_ref) + r_ref[:, 3:4] * _load_split(y2_ref)
    o_ref[...] = x + _mod_rows(modt_ref, 5, rows) * _rms(f, ng_ref[3:4, :])


def _combine_call(x_flat, modt, ng, y12, rinfo):
    n, d = x_flat.shape
    nsub = FFN_TM // TM
    nblk = n // FFN_TM
    tok = lambda w: pl.BlockSpec((FFN_TM, w), lambda i: (i, 0))
    second = pl.BlockSpec((GATHER_SPLIT, FFN_TM, d // GATHER_SPLIT // 2), lambda i: (0, i + nblk, 0))
    return pl.pallas_call(
        _combine_kernel,
        grid=(nblk,),
        in_specs=[tok(d), pl.BlockSpec((nsub, N_MOD, d), lambda i: (i, 0, 0)), pl.BlockSpec((8, d), lambda i: (0, 0)),
                  _split_spec(FFN_TM)(d), second, tok(LANES)],
        out_specs=tok(d),
        out_shape=jax.ShapeDtypeStruct((n, d), F32),
        compiler_params=_cparams("parallel"),
        name="combine",
    )(x_flat, modt, ng, y12, y12, rinfo)


def _router_weights(w_router):
    wr = jnp.pad(w_router, ((0, 0), (0, LANES - N_EXPERTS)))
    wr_hi = wr.astype(BF16)
    wr_lo = (wr - wr_hi.astype(F32)).astype(BF16)
    return jnp.stack([wr_hi, wr_lo])


def _moe_layer(x_flat, h, rinfo, rinfo_t, modt, ng, wg, wu, wd):
    n, d = x_flat.shape
    n_assign = 2 * n
    expert = jnp.concatenate([rinfo_t[0], rinfo_t[1]]).astype(jnp.int32)
    onehot = (expert[None, :] == jnp.arange(N_EXPERTS, dtype=jnp.int32)[:, None]).astype(jnp.int32)
    csum = jnp.cumsum(onehot, axis=1)
    rank = jnp.sum((csum - onehot) * onehot, axis=0)
    counts = csum[:, -1]
    padded = (counts + MOE_BLK - 1) // MOE_BLK * MOE_BLK
    pad_end = jnp.cumsum(padded)
    pad_start = pad_end - padded
    dest = jnp.sum(onehot * pad_start[:, None], axis=0) + rank
    nb = -(-n_assign // MOE_BLK) + N_EXPERTS
    row_quantum = 32 * GATHER_WIN // GATHER_SPLIT
    nb = -(-nb * MOE_BLK // row_quantum) * row_quantum // MOE_BLK
    n_slots = nb * MOE_BLK
    block_expert = jnp.minimum(
        jnp.searchsorted(pad_end, jnp.arange(nb, dtype=jnp.int32) * MOE_BLK, side="right"), N_EXPERTS - 1
    ).astype(jnp.int32)
    n_valid = (pad_end[-1:] // MOE_BLK).astype(jnp.int32)
    tok_rows = jnp.broadcast_to((jnp.arange(n_assign, dtype=jnp.int32) % n)[:, None], (n_assign, LANES))
    slot = jnp.arange(n_slots, dtype=jnp.int32)
    slot_used = (slot - jnp.repeat(pad_start[block_expert], MOE_BLK)) < jnp.repeat(counts[block_expert], MOE_BLK)
    slot_tok = jnp.where(slot_used, _first_column(_scatter_flat(tok_rows, dest, n_slots)).reshape(n_slots), 0)

    xs = _gather_rows(h, slot_tok)
    ys = _expert_call(block_expert, n_valid, xs, wg, wu, wd)
    y12 = _gather_rows(ys, dest)
    return _combine_call(x_flat, modt, ng, y12, rinfo)


def _rope_tables(n_ctx, length):
    rows = length // GRID_W
    row = jnp.repeat(jnp.arange(rows, dtype=F32), GRID_W)
    col = jnp.tile(jnp.arange(GRID_W, dtype=F32), rows)
    axis_dim = HEAD_DIM // 2
    inv_freq = ROPE_THETA ** (-jnp.arange(0, axis_dim, 2, dtype=F32) / axis_dim)
    ar = row[:, None] * inv_freq
    ac = col[:, None] * inv_freq
    cos = jnp.concatenate([jnp.cos(ar), jnp.cos(ar), jnp.cos(ac), jnp.cos(ac)], axis=1)
    sin = jnp.concatenate([-jnp.sin(ar), jnp.sin(ar), -jnp.sin(ac), jnp.sin(ac)], axis=1)
    cos = jnp.concatenate([jnp.ones((n_ctx, HEAD_DIM), F32), cos], axis=0)
    sin = jnp.concatenate([jnp.zeros((n_ctx, HEAD_DIM), F32), sin], axis=0)
    return jnp.tile(cos, (1, 2)), jnp.tile(sin, (1, 2))


def kernel(x, c, ctx, c_ctx, ada_w, ada_b, norm_g, w_in, w_out, attn_q_g, attn_k_g, conv_w, hgrn_lb, hgrn_g,
           ffn_w_gate, ffn_w_up, ffn_w_down, moe_router, moe_w_gate, moe_w_up, moe_w_down):
    bsz, length, d = x.shape
    n_ctx = ctx.shape[1]
    depth = w_in.shape[0]
    t = n_ctx + length
    nt = t // TM
    assert n_ctx == TM and length % TM == 0 and d == 8 * LANES
    for n_tok in (bsz * t, bsz * length):
        assert n_tok % FFN_TM == 0 and (2 * n_tok) % (32 * GATHER_WIN) == 0

    r_pad = -(-(bsz + 1) // 8) * 8
    c_all = jnp.concatenate([c, c_ctx[None, :], jnp.zeros((r_pad - bsz - 1, d), F32)], axis=0)
    mod_all = _ada_call(c_all, ada_w, ada_b).reshape(depth, r_pad, N_MOD, d)

    cos_t, sin_t = _rope_tables(n_ctx, length)
    lb_all = jnp.cumsum(jax.nn.softmax(hgrn_lb.astype(F32), axis=0), axis=0)
    lb_all = lb_all - lb_all[0]
    hconsts = _hgrn_consts()
    cw = 2 * LANES
    cidx = jnp.arange(cw)
    head_mean = ((cidx[:, None] // HEAD_DIM == cidx[None, :] // HEAD_DIM).astype(F32) / HEAD_DIM).astype(BF16)

    cos_f = jnp.tile(cos_t, (bsz, 1))
    sin_f = jnp.tile(sin_t, (bsz, 1))
    x_flat = jnp.concatenate([ctx, x], axis=1).reshape(bsz * t, d)
    for l in range(depth):
        mod_lat = mod_all[l, :bsz]
        mod_ctx = jnp.broadcast_to(mod_all[l, bsz][None], (bsz, N_MOD, d))
        mod = jnp.stack([mod_ctx, mod_lat], axis=1)
        modt = jnp.concatenate([mod[:, :1], jnp.broadcast_to(mod[:, 1:], (bsz, nt - 1, N_MOD, d))], axis=1)
        modt = modt.reshape(bsz * nt, N_MOD, d)
        ng = jnp.pad(norm_g[l], ((0, 4), (0, 0)))
        lb = lb_all[l].reshape(1, 2 * cw)
        gc = jnp.concatenate([jnp.log(lb), jnp.log1p(-lb), 1.0 - lb, jnp.zeros((5, 2 * cw), F32)], axis=0)
        qg = jnp.tile(attn_q_g[l], 2)[None, :]
        kg = jnp.tile(attn_k_g[l], 2)[None, :]

        q, k, vt, cbog, u, qv, kk, gg = _inproj_call(x_flat, modt, ng, w_in[l].astype(BF16), cos_f, sin_f, qg, kg, gc)
        attn = _attn_call(q, k, vt, bsz)
        o_f, o_b = _hgrn_call(qv, kk, gg, hconsts, bsz)
        mix = (attn, cbog, u, o_f, o_b, hgrn_g[l][None, :],
               jnp.pad(conv_w[l], ((0, 5), (0, 0))), head_mean, w_out[l].astype(BF16))
        if l % 2 == 0:
            weights = (ffn_w_gate[l // 2].astype(BF16), ffn_w_up[l // 2].astype(BF16), ffn_w_down[l // 2].astype(BF16))
        else:
            weights = (_router_weights(moe_router[l // 2]),)
            experts = (_cast_bf16(moe_w_gate, l // 2), _cast_bf16(moe_w_up, l // 2), _cast_bf16(moe_w_down, l // 2))
        mode = "ffn" if l % 2 == 0 else "router"

        if l < depth - 1:
            if mode == "ffn":
                x_flat = _mixer_call(x_flat, modt, ng, mix, mode, weights, nt)
            else:
                x_flat = _mixer_call(x_flat, modt, ng, mix, "plain", (), nt)
                h, rinfo, rinfo_t = _mixer_call(x_flat, modt, ng, None, mode, weights)
                x_flat = _moe_layer(x_flat, h, rinfo, rinfo_t, modt, ng, *experts)
            continue

        x_all = _mixer_call(x_flat, modt, ng, mix, "plain", (), nt).reshape(bsz, t, d)
        x_flat = x_all[:, n_ctx:, :].reshape(bsz * length, d)
        modt = jnp.broadcast_to(mod[:, 1:], (bsz, length // TM, N_MOD, d)).reshape(bsz * (length // TM), N_MOD, d)
        if mode == "ffn":
            x_flat = _mixer_call(x_flat, modt, ng, None, mode, weights)
        else:
            h, rinfo, rinfo_t = _mixer_call(x_flat, modt, ng, None, mode, weights)
            x_flat = _moe_layer(x_flat, h, rinfo, rinfo_t, modt, ng, *experts)
        return x_flat.reshape(bsz, length, d)
```

```python
import functools

import jax
import jax.numpy as jnp
from jax import lax
from jax.experimental import pallas as pl
from jax.experimental.pallas import tpu as pltpu
from jax.experimental.pallas import tpu_sc as plsc

F32 = jnp.float32
BF16 = jnp.bfloat16

HEAD_DIM = 64
GRID_W = 64
ROPE_THETA = 10000.0
ATTN_SCALE = HEAD_DIM ** -0.5
LOG2_E = 1.4426950408889634
HGRN_CHUNK = 64
HGRN_SUB = 8
VT_ROWS = HEAD_DIM + 16
ATTN_AHEAD = 6
N_EXPERTS = 8
N_MOD = 6
EPS = 1e-6

LANES = 128
TM = 256
FFN_TM = 512
FFN_TF = 512
N_MIX_REFS = 11
CAST_STREAMS = 4
MOE_BLK = 512
MOE_TF = 512
GATHER_WIN = 128
GATHER_SPLIT = 2
NEG_BIG = -1e30
VMEM_LIMIT = 56 * 1024 * 1024


def _cparams(*sem):
    return pltpu.CompilerParams(dimension_semantics=sem, vmem_limit_bytes=VMEM_LIMIT)


def _sigmoid(z):
    return 1.0 / (1.0 + jnp.exp(-z))


def _dot(a, b):
    return jnp.dot(a, b, preferred_element_type=F32)


def _dot_nt(a, b):
    return lax.dot_general(a, b, (((1,), (1,)), ((), ())), preferred_element_type=F32)


def _dot_tn(a, b):
    return lax.dot_general(a, b, (((0,), (0,)), ((), ())), preferred_element_type=F32)


def _rms(x, g):
    return x * lax.rsqrt(jnp.mean(x * x, axis=-1, keepdims=True) + EPS) * g


def _ada_kernel(c_ref, w_ref, b_ref, o_ref):
    c = c_ref[...]
    s = (c * _sigmoid(c)).astype(BF16)
    o_ref[...] = _dot(s, w_ref[...].astype(BF16)) + b_ref[...]


def _ada_call(c_all, ada_w, ada_b):
    depth, d, n = ada_w.shape
    r = c_all.shape[0]
    tn = 512
    return pl.pallas_call(
        _ada_kernel,
        grid=(depth, n // tn),
        in_specs=[
            pl.BlockSpec((r, d), lambda l, j: (0, 0)),
            pl.BlockSpec((None, d, tn), lambda l, j: (l, 0, j)),
            pl.BlockSpec((None, 1, tn), lambda l, j: (l, 0, j)),
        ],
        out_specs=pl.BlockSpec((None, r, tn), lambda l, j: (l, 0, j)),
        out_shape=jax.ShapeDtypeStruct((depth, r, n), F32),
        compiler_params=_cparams("parallel", "parallel"),
        name="ada",
    )(c_all, ada_w, ada_b.reshape(depth, 1, n))


def _inproj_kernel(x_ref, modt_ref, ng_ref, w_ref, cos_ref, sin_ref, qg_ref, kg_ref, gc_ref,
                   q_ref, k_ref, vt_ref, cbog_ref, u_ref, qv_ref, kk_ref, gg_ref):
    x = x_ref[...]
    rows = x.shape[0]
    h = _rms(x, ng_ref[0:1, :]) * (1.0 + _mod_rows(modt_ref, 1, rows)) + _mod_rows(modt_ref, 0, rows)
    hb = h.astype(BF16)

    def proj(c0, n):
        return _dot(hb, w_ref[:, c0:c0 + n])

    cos = cos_ref[...]
    sin = sin_ref[...]
    lane = lax.broadcasted_iota(jnp.int32, (rows, LANES), 1)
    first_half = (lane % 32) < 16
    lo = lane < HEAD_DIM

    def norm_rope(blk, g):
        b2 = blk * blk
        s_lo = jnp.sum(jnp.where(lo, b2, 0.0), axis=-1, keepdims=True)
        s_hi = jnp.sum(jnp.where(lo, 0.0, b2), axis=-1, keepdims=True)
        v = blk * lax.rsqrt(jnp.where(lo, s_lo, s_hi) * (1.0 / HEAD_DIM) + EPS) * g
        partner = jnp.where(first_half, pltpu.roll(v, LANES - 16, 1), pltpu.roll(v, 16, 1))
        return v * cos + partner * sin

    qa = proj(0, 4 * LANES)
    for j in range(4):
        qn = norm_rope(qa[:, LANES * j:LANES * (j + 1)], qg_ref[...]) * (ATTN_SCALE * LOG2_E)
        sw = pltpu.roll(qn, HEAD_DIM, 1)
        if j < 2:
            q_ref[2 * j] = jnp.where(lo, qn, 0.0).astype(BF16)
            q_ref[2 * j + 1] = jnp.where(lo, sw, 0.0).astype(BF16)
        else:
            q_ref[2 * j] = jnp.where(lo, 0.0, sw).astype(BF16)
            q_ref[2 * j + 1] = jnp.where(lo, 0.0, qn).astype(BF16)

    kv = proj(4 * LANES, 2 * LANES)
    k_ref[...] = norm_rope(kv[:, :LANES], kg_ref[...]).astype(BF16)
    vt = kv[:, LANES:].T
    ones = jnp.ones((VT_ROWS - HEAD_DIM, TM), F32)
    for s in range(rows // TM):
        for g in range(2):
            vt_ref[g, s] = jnp.concatenate(
                [vt[HEAD_DIM * g:HEAD_DIM * (g + 1), TM * s:TM * (s + 1)], ones], axis=0).astype(BF16)

    c0 = 6 * LANES
    c3 = proj(c0, 6 * LANES)
    cw = 2 * LANES
    u_ref[...] = c3[:, cw:2 * cw] * c3[:, 2 * cw:3 * cw]

    hg = proj(c0 + 6 * LANES, 10 * LANES)
    zf, zb, iv, hq, og = (hg[:, cw * i:cw * (i + 1)] for i in range(5))
    cbog_ref[:, :cw] = c3[:, :cw].astype(BF16)
    cbog_ref[:, cw:] = og.astype(BF16)
    qv_ref[:, :cw] = (hq * _sigmoid(hq)).astype(BF16)
    qv_ref[:, cw:] = iv.astype(BF16)
    for d, z in enumerate((zf, zb)):
        log_lb = gc_ref[0:1, cw * d:cw * (d + 1)]
        log1m_lb = gc_ref[1:2, cw * d:cw * (d + 1)]
        one_m_lb = gc_ref[2:3, cw * d:cw * (d + 1)]
        t = jnp.exp(-jnp.abs(z))
        log_sig = jnp.minimum(z, 0.0) - jnp.log(1.0 + t)
        a2 = log1m_lb + log_sig
        log_f = jnp.maximum(log_lb, a2) + jnp.log(1.0 + jnp.exp(-jnp.abs(log_lb - a2)))
        sig_neg = jnp.where(z >= 0, t, 1.0) / (1.0 + t)
        gg_ref[:, cw * d:cw * (d + 1)] = log_f
        kk_ref[:, cw * d:cw * (d + 1)] = (one_m_lb * sig_neg).astype(BF16)


def _inproj_call(x_flat, modt, ng, w1, cos_f, sin_f, qg, kg, gc):
    n, d = x_flat.shape
    nsub = FFN_TM // TM
    tok = lambda w: pl.BlockSpec((FFN_TM, w), lambda i: (i, 0))
    const = lambda a: pl.BlockSpec(a.shape, lambda i: (0,) * a.ndim, pipeline_mode=pl.Buffered(1))
    return pl.pallas_call(
        _inproj_kernel,
        grid=(n // FFN_TM,),
        in_specs=[
            tok(d),
            pl.BlockSpec((nsub, N_MOD, d), lambda i: (i, 0, 0)),
            const(ng), const(w1), tok(LANES), tok(LANES), const(qg), const(kg), const(gc),
        ],
        out_specs=[
            pl.BlockSpec((8, FFN_TM, LANES), lambda i: (0, i, 0)),
            tok(LANES),
            pl.BlockSpec((2, nsub, VT_ROWS, TM), lambda i: (0, i, 0, 0)),
            tok(4 * LANES),
            tok(2 * LANES),
            tok(4 * LANES),
            tok(4 * LANES),
            tok(4 * LANES),
        ],
        out_shape=[
            jax.ShapeDtypeStruct((8, n, LANES), BF16),
            jax.ShapeDtypeStruct((n, LANES), BF16),
            jax.ShapeDtypeStruct((2, n // TM, VT_ROWS, TM), BF16),
            jax.ShapeDtypeStruct((n, 4 * LANES), BF16),
            jax.ShapeDtypeStruct((n, 2 * LANES), F32),
            jax.ShapeDtypeStruct((n, 4 * LANES), BF16),
            jax.ShapeDtypeStruct((n, 4 * LANES), BF16),
            jax.ShapeDtypeStruct((n, 4 * LANES), F32),
        ],
        compiler_params=_cparams("parallel"),
        name="inproj",
    )(x_flat, modt, ng, w1, cos_f, sin_f, qg, kg, gc)


def _attn_kernel(q_ref, k_ref, vt_ref, o_ref, m_ref, acc_ref, *, n_kv):
    def score(j, h):
        return _dot_nt(k_ref[TM * j:TM * (j + 1), :], q_ref[h])

    def attend(n_tiles):
        steps = [(j, h) for j in range(n_tiles) for h in range(8)]
        scores = [score(j, h) for j, h in steps[:ATTN_AHEAD]]
        for i, (j, h) in enumerate(steps):
            if i + ATTN_AHEAD < len(steps):
                scores.append(score(*steps[i + ATTN_AHEAD]))
            s = scores.pop(0)
            m_tile = jnp.max(s, axis=0, keepdims=True)
            if j == 0:
                m_new = m_tile
                acc_ref[h] = _dot(vt_ref[h // 4, j], jnp.exp2(s - m_new).astype(BF16))
            else:
                m_old = m_ref[h]
                m_new = jnp.maximum(m_old, m_tile)
                pv = _dot(vt_ref[h // 4, j], jnp.exp2(s - m_new).astype(BF16))
                acc_ref[h] = jnp.exp2(m_old - m_new) * acc_ref[h] + pv
            m_ref[h] = m_new
        outs = [acc_ref[h, :HEAD_DIM, :] / acc_ref[h, HEAD_DIM:HEAD_DIM + 1, :] for h in range(8)]
        o_ref[...] = jnp.concatenate(outs, axis=0).T.astype(BF16)

    @pl.when(pl.program_id(1) == 0)
    def _():
        attend(1)

    @pl.when(pl.program_id(1) > 0)
    def _():
        attend(n_kv)


def _attn_call(q, k, vt, b):
    n = k.shape[0]
    t = n // b
    nt = t // TM
    return pl.pallas_call(
        functools.partial(_attn_kernel, n_kv=nt),
        grid=(b, nt),
        in_specs=[
            pl.BlockSpec((8, TM, LANES), lambda i, j: (0, i * nt + j, 0)),
            pl.BlockSpec((t, LANES), lambda i, j: (i, 0)),
            pl.BlockSpec((2, nt, VT_ROWS, TM), lambda i, j: (0, i, 0, 0)),
        ],
        out_specs=pl.BlockSpec((TM, 4 * LANES), lambda i, j: (i * nt + j, 0)),
        out_shape=jax.ShapeDtypeStruct((n, 4 * LANES), BF16),
        scratch_shapes=[pltpu.VMEM((8, 1, TM), F32), pltpu.VMEM((8, VT_ROWS, TM), F32)],
        compiler_params=_cparams("parallel", "parallel"),
        name="attn",
    )(q, k, vt)


def _split3(g):
    hi = g.astype(BF16)
    r1 = g - hi.astype(F32)
    mid = r1.astype(BF16)
    lo = (r1 - mid.astype(F32)).astype(BF16)
    return hi, mid, lo


def _hgrn_kernel(qvf_ref, kf_ref, gf_ref, qvb_ref, kb_ref, gb_ref, tri_ref, trij_ref, j_ref, e_ref, bd_ref,
                 md_ref, mo_ref, of_ref, ob_ref, st_ref, x_s):
    hw = 2 * LANES
    nch = TM // HGRN_CHUNK

    @pl.when(pl.program_id(1) == 0)
    def _():
        st_ref[...] = jnp.zeros(st_ref.shape, F32)

    flip = j_ref[...]

    def cumsum(m_ref, g):
        m = m_ref[...]
        hi, mid, lo = _split3(g)
        return _dot(m, hi) + _dot(m, mid) + _dot(m, lo)

    qvf = qvf_ref[...]
    fl = _dot(flip, jnp.concatenate([qvb_ref[...], kb_ref[...]], axis=1))
    q = jnp.concatenate([qvf[:, :hw].astype(F32), fl[:, :hw]], axis=0)
    v = jnp.concatenate([qvf[:, hw:].astype(F32), fl[:, hw:2 * hw]], axis=0)
    k = jnp.concatenate([kf_ref[...].astype(F32), fl[:, 2 * hw:]], axis=0)
    b = jnp.concatenate([cumsum(tri_ref, gf_ref[...]), cumsum(trij_ref, gb_ref[...])], axis=0) * LOG2_E

    ngr = 2 * TM // HGRN_SUB
    nsub = HGRN_CHUNK // HGRN_SUB
    q3 = q.reshape(ngr, HGRN_SUB, hw)
    b3 = b.reshape(ngr, HGRN_SUB, hw)
    k3 = k.reshape(ngr, HGRN_SUB, hw)
    for s in range(HGRN_SUB):
        x = q3 * jnp.exp2(jnp.minimum(b3 - b3[:, s:s + 1, :], 0.0)) * k3[:, s:s + 1, :]
        x_s[:, hw * s:hw * (s + 1)] = x.reshape(2 * TM, hw).astype(BF16)
    a = _dot(x_s[...], e_ref[...])

    bd = bd_ref[...]
    bd16 = bd.astype(BF16)
    md = md_ref[...]
    mo = mo_ref[...]
    for d in range(2):
        st = st_ref[d]
        outs = []
        for c in range(nch):
            r0 = TM * d + HGRN_CHUNK * c
            bc = b[r0:r0 + HGRN_CHUNK]
            qc = q[r0:r0 + HGRN_CHUNK]
            kc = k[r0:r0 + HGRN_CHUNK]
            vc = v[r0:r0 + HGRN_CHUNK]
            ac = a[r0:r0 + HGRN_CHUNK] * md
            r_end = [bc[HGRN_SUB * (j + 1) - 1:HGRN_SUB * (j + 1), :] for j in range(nsub)]
            r_own = jnp.concatenate([jnp.broadcast_to(r, (HGRN_SUB, hw)) for r in r_end], axis=0)
            kt_bd = jnp.concatenate([(kc * jnp.exp2(r_own - bc)).astype(BF16)] * 4, axis=0) * bd16
            q_rel = jnp.concatenate(
                [qc * jnp.exp2(jnp.minimum(bc - r_end[j], 0.0)) for j in range(nsub - 1)], axis=0).astype(BF16)
            p = _dot_nt(q_rel, kt_bd) * mo
            for j in range(nsub - 1):
                ac = ac + p[HGRN_CHUNK * j:HGRN_CHUNK * (j + 1)]
            bl = r_end[-1]
            qe = (qc * jnp.exp2(bc)).astype(BF16)
            ke = (kc * jnp.exp2(bl - bc)).astype(BF16)
            vcb = vc.astype(BF16)
            vbd = jnp.concatenate([vcb] * 4, axis=0) * bd16
            outs.append(_dot(ac.astype(BF16), vbd) + _dot_nt(qe, st.astype(BF16)))
            st = st * jnp.exp2(bl) + _dot_tn(vcb, ke) * bd
        st_ref[d] = st
        od = jnp.concatenate(outs, axis=0).astype(BF16)
        if d == 0:
            of_ref[...] = od
        else:
            ob_ref[...] = _dot(flip, od).astype(BF16)


def _hgrn_consts():
    hw = 2 * LANES
    r = jnp.arange(TM)
    same_chunk = (r[:, None] // HGRN_CHUNK) == (r[None, :] // HGRN_CHUNK)
    tri = (same_chunk & (r[None, :] <= r[:, None])).astype(F32)
    flip = (r[:, None] + r[None, :] == TM - 1).astype(F32)
    trij = tri @ flip
    c = jnp.arange(hw)
    s = jnp.arange(HGRN_SUB)
    e = ((c[None, :, None] // HEAD_DIM == c[None, None, :] // HGRN_CHUNK)
         & (c[None, None, :] % HGRN_SUB == s[:, None, None])).astype(BF16).reshape(HGRN_SUB * hw, hw)
    bd = (c[:, None] // HEAD_DIM == c[None, :] // HEAD_DIM).astype(F32)
    tt = jnp.arange(HGRN_CHUNK)[:, None]
    ss = (c % HGRN_CHUNK)[None, :]
    md = ((ss // HGRN_SUB == tt // HGRN_SUB) & (ss <= tt)).astype(F32)
    mo = jnp.concatenate([((ss // HGRN_SUB == j) & (tt // HGRN_SUB > j)).astype(F32)
                          for j in range(HGRN_CHUNK // HGRN_SUB - 1)], axis=0)
    return tri.astype(BF16), trij.astype(BF16), flip.astype(BF16), e, bd, md, mo


def _hgrn_call(qv, kk, gg, consts, b):
    n = qv.shape[0]
    nt = n // b // TM
    hw = 2 * LANES
    tri, trij, flip, e, bd, md, mo = consts
    fwd = lambda i, j: i * nt + j
    bwd = lambda i, j: i * nt + jnp.where(j == 0, 0, nt - j)
    const2 = lambda a: pl.BlockSpec(a.shape, lambda i, j: (0,) * a.ndim)
    return pl.pallas_call(
        _hgrn_kernel,
        grid=(b, nt),
        in_specs=[
            pl.BlockSpec((TM, 2 * hw), lambda i, j: (fwd(i, j), 0)),
            pl.BlockSpec((TM, hw), lambda i, j: (fwd(i, j), 0)),
            pl.BlockSpec((TM, hw), lambda i, j: (fwd(i, j), 0)),
            pl.BlockSpec((TM, 2 * hw), lambda i, j: (bwd(i, j), 0)),
            pl.BlockSpec((TM, hw), lambda i, j: (bwd(i, j), 1)),
            pl.BlockSpec((TM, hw), lambda i, j: (bwd(i, j), 1)),
            const2(tri), const2(trij), const2(flip), const2(e), const2(bd), const2(md), const2(mo),
        ],
        out_specs=[
            pl.BlockSpec((TM, hw), lambda i, j: (fwd(i, j), 0)),
            pl.BlockSpec((TM, hw), lambda i, j: (bwd(i, j), 0)),
        ],
        out_shape=[jax.ShapeDtypeStruct((n, hw), BF16)] * 2,
        scratch_shapes=[
            pltpu.VMEM((2, hw, hw), F32),
            pltpu.VMEM((2 * TM, HGRN_SUB * hw), BF16),
        ],
        compiler_params=_cparams("parallel", "arbitrary"),
        name="hgrn",
    )(qv, kk, gg, qv, kk, gg, tri, trij, flip, e, bd, md, mo)


def _mix_out(x_ref, modt_ref, ng_ref, attn_ref, cbog_ref, u_ref, up_ref, un_ref, of_ref, ob_ref,
             hgg_ref, cw_ref, hn_ref, w_ref, *, n_tiles):
    rows = x_ref.shape[0]
    nsub = rows // TM
    cw = 2 * LANES
    u = u_ref[...]
    row = lax.broadcasted_iota(jnp.int32, (rows, cw), 0)
    u_prev = jnp.where(row == 0, up_ref[7:8, :], pltpu.roll(u, 1, 0))
    u_next = jnp.where(row == rows - 1, un_ref[0:1, :], pltpu.roll(u, rows - 1, 0))
    for s in range(nsub):
        t = (pl.program_id(0) * nsub + s) % n_tiles
        u_prev = jnp.where(t >= 2, u_prev, jnp.where(row == s * TM, 0.0, u_prev))
        u_next = jnp.where((t >= 1) & (t < n_tiles - 1), u_next, jnp.where(row == s * TM + TM - 1, 0.0, u_next))
    cbog = cbog_ref[...].astype(F32)
    conv = cbog[:, :cw] * (cw_ref[0:1, :] * u_prev + cw_ref[1:2, :] * u + cw_ref[2:3, :] * u_next)

    o = of_ref[...].astype(F32) + ob_ref[...].astype(F32)
    ms = _dot((o * o).astype(BF16), hn_ref[...])
    og = cbog[:, cw:]
    hg = o * lax.rsqrt(ms + EPS) * hgg_ref[...] * (og * _sigmoid(og))

    y = (_dot(attn_ref[...], w_ref[0:2 * cw, :]) + _dot(conv.astype(BF16), w_ref[2 * cw:3 * cw, :])
         + _dot(hg.astype(BF16), w_ref[3 * cw:4 * cw, :]))
    return x_ref[...] + _mod_rows(modt_ref, 2, rows) * _rms(y, ng_ref[1:2, :])


def _mod_rows(modt_ref, i, rows):
    sub = rows // modt_ref.shape[0]
    return jnp.concatenate(
        [jnp.broadcast_to(modt_ref[s, i:i + 1, :], (sub, modt_ref.shape[2])) for s in range(modt_ref.shape[0])],
        axis=0)


def _store_split(ref, val):
    w = ref.shape[2]
    for c in range(ref.shape[0]):
        lo = val[:, 2 * w * c:2 * w * c + w].astype(BF16).astype(F32)
        hi = val[:, 2 * w * c + w:2 * w * (c + 1)].astype(BF16).astype(F32)
        ref[c] = lax.bitcast_convert_type(hi, jnp.uint32) | (lax.bitcast_convert_type(lo, jnp.uint32) >> 16)


def _load_split(ref):
    cols = []
    for c in range(ref.shape[0]):
        word = ref[c]
        cols.append(lax.bitcast_convert_type(word << 16, F32))
        cols.append(lax.bitcast_convert_type(word & jnp.uint32(0xFFFF0000), F32))
    return jnp.concatenate(cols, axis=1)


def _split_shape(n, d):
    return jax.ShapeDtypeStruct((GATHER_SPLIT, n, d // GATHER_SPLIT // 2), jnp.uint32)


def _split_spec(rows):
    return lambda d: pl.BlockSpec((GATHER_SPLIT, rows, d // GATHER_SPLIT // 2), lambda i, *_: (0, i, 0))


def _ffn_body(x, modt_ref, ng_ref, wg_ref, wu_ref, wd_ref):
    rows = x.shape[0]
    h = (_rms(x, ng_ref[2:3, :]) * (1.0 + _mod_rows(modt_ref, 4, rows)) + _mod_rows(modt_ref, 3, rows)).astype(BF16)
    acc = jnp.zeros(x.shape, F32)
    dff = wg_ref.shape[1]
    for f0 in range(0, dff, FFN_TF):
        f1 = min(f0 + FFN_TF, dff)
        g = _dot(h, wg_ref[:, f0:f1])
        up = _dot(h, wu_ref[:, f0:f1])
        acc = acc + _dot((g * _sigmoid(g) * up).astype(BF16), wd_ref[f0:f1, :])
    return x + _mod_rows(modt_ref, 5, rows) * _rms(acc, ng_ref[3:4, :])


def _router_body(x, modt_ref, ng_ref, wr_ref, h_ref, r_ref, rt_ref):
    rows = x.shape[0]
    h = _rms(x, ng_ref[2:3, :]) * (1.0 + _mod_rows(modt_ref, 4, rows)) + _mod_rows(modt_ref, 3, rows)
    _store_split(h_ref, h)
    h_hi = h.astype(BF16)
    h_lo = (h - h_hi.astype(F32)).astype(BF16)
    logits = _dot(h_hi, wr_ref[0]) + _dot(h_lo, wr_ref[0]) + _dot(h_hi, wr_ref[1])
    lane = lax.broadcasted_iota(jnp.int32, logits.shape, 1)
    lg = jnp.where(lane < N_EXPERTS, logits, NEG_BIG)
    m1 = jnp.max(lg, axis=-1, keepdims=True)
    i1 = jnp.min(jnp.where(lg == m1, lane, LANES), axis=-1, keepdims=True)
    lg2 = jnp.where(lane == i1, NEG_BIG, lg)
    m2 = jnp.max(lg2, axis=-1, keepdims=True)
    i2 = jnp.min(jnp.where(lg2 == m2, lane, LANES), axis=-1, keepdims=True)
    e2 = jnp.exp(m2 - m1)
    w1 = 1.0 / (1.0 + e2)
    w2 = e2 / (1.0 + e2)
    r = jnp.where(lane == 0, i1.astype(F32),
                  jnp.where(lane == 1, i2.astype(F32), jnp.where(lane == 2, w1, jnp.where(lane == 3, w2, 0.0))))
    r_ref[...] = r
    rt_ref[...] = r.T[0:8, :]


def _mixer_kernel(*refs, fused, mode, n_tiles):
    x_ref, modt_ref, ng_ref = refs[:3]
    pos = 3
    if fused:
        x = _mix_out(x_ref, modt_ref, ng_ref, *refs[pos:pos + N_MIX_REFS], n_tiles=n_tiles)
        pos += N_MIX_REFS
    else:
        x = x_ref[...]
    if mode == "ffn":
        wg_ref, wu_ref, wd_ref, o_ref = refs[pos:]
        o_ref[...] = _ffn_body(x, modt_ref, ng_ref, wg_ref, wu_ref, wd_ref)
    elif mode == "router":
        wr_ref, *outs = refs[pos:]
        _router_body(x, modt_ref, ng_ref, wr_ref, *outs)
    else:
        refs[pos][...] = x


def _mixer_call(x_flat, modt, ng, mix, mode, weights, n_tiles=0):
    n, d = x_flat.shape
    nsub = FFN_TM // TM
    cw = 2 * LANES
    tok = lambda w: pl.BlockSpec((FFN_TM, w), lambda i: (i, 0))
    resident = lambda a: pl.BlockSpec(a.shape, lambda i: (0,) * a.ndim, pipeline_mode=pl.Buffered(1))
    stream = jax.ShapeDtypeStruct((n, d), F32)
    args = [x_flat, modt, ng]
    in_specs = [tok(d), pl.BlockSpec((nsub, N_MOD, d), lambda i: (i, 0, 0)), pl.BlockSpec((8, d), lambda i: (0, 0))]
    fused = mix is not None
    if fused:
        attn, cbog, u, o_f, o_b, hgg, conv_w, hn, w_out = mix
        nb8 = n // 8
        args += [attn, cbog, u, u, u, o_f, o_b, hgg, conv_w, hn, w_out]
        in_specs += [
            tok(2 * cw), tok(2 * cw), tok(cw),
            pl.BlockSpec((8, cw), lambda i: (jnp.maximum(i * (FFN_TM // 8) - 1, 0), 0)),
            pl.BlockSpec((8, cw), lambda i: (jnp.minimum((i + 1) * (FFN_TM // 8), nb8 - 1), 0)),
            tok(cw), tok(cw), resident(hgg), resident(conv_w), resident(hn), resident(w_out),
        ]
        assert len(in_specs) == 3 + N_MIX_REFS
    args += list(weights)
    in_specs += [resident(w) for w in weights]
    if mode == "router":
        out_specs = [_split_spec(FFN_TM)(d), tok(LANES), pl.BlockSpec((8, FFN_TM), lambda i: (0, i))]
        out_shape = [_split_shape(n, d),
                     jax.ShapeDtypeStruct((n, LANES), F32), jax.ShapeDtypeStruct((8, n), F32)]
    else:
        out_specs, out_shape = tok(d), stream
    return pl.pallas_call(
        functools.partial(_mixer_kernel, fused=fused, mode=mode, n_tiles=n_tiles),
        grid=(n // FFN_TM,),
        in_specs=in_specs,
        out_specs=out_specs,
        out_shape=out_shape,
        compiler_params=_cparams("parallel"),
        name=("mix_" if fused else "") + mode,
    )(*args)


def _gather_flat(data, idx):
    m = idx.shape[0]
    w = data.shape[1]
    mesh = plsc.VectorSubcoreMesh(core_axis_name="core", subcore_axis_name="subcore")

    @functools.partial(pl.kernel, out_type=jax.ShapeDtypeStruct((m, w), data.dtype), mesh=mesh)
    def gather(x_hbm, i_hbm, o_hbm):
        def body(i_vmem, o_vmem):
            pltpu.sync_copy(x_hbm.at[i_vmem.at[0]], o_vmem)

        pltpu.emit_pipeline(
            body,
            grid=(m // GATHER_WIN,),
            in_specs=[pl.BlockSpec((1, GATHER_WIN), lambda i: (0, i))],
            out_specs=[pl.BlockSpec((GATHER_WIN, w), lambda i: (i, 0))],
            core_axis_name=("core", "subcore"),
            dimension_semantics=(pltpu.PARALLEL,),
        )(i_hbm, o_hbm)

    return gather(data, idx.reshape(1, m))


def _scatter_flat(rows, idx, n_out):
    m, w = rows.shape
    mesh = plsc.VectorSubcoreMesh(core_axis_name="core", subcore_axis_name="subcore")

    @functools.partial(pl.kernel, out_type=jax.ShapeDtypeStruct((n_out, w), rows.dtype), mesh=mesh)
    def scatter(x_hbm, i_hbm, o_hbm):
        def body(x_vmem, i_vmem):
            pltpu.sync_copy(x_vmem, o_hbm.at[i_vmem.at[0]])

        pltpu.emit_pipeline(
            body,
            grid=(m // GATHER_WIN,),
            in_specs=[pl.BlockSpec((GATHER_WIN, w), lambda i: (i, 0)),
                      pl.BlockSpec((1, GATHER_WIN), lambda i: (0, i))],
            out_specs=[],
            core_axis_name=("core", "subcore"),
            dimension_semantics=(pltpu.PARALLEL,),
        )(x_hbm, i_hbm)

    return scatter(rows, idx.reshape(1, m))


def _col0_kernel(x_ref, o_ref):
    o_ref[...] = x_ref[...].T[0:1, :]


def _first_column(rows):
    n = rows.shape[0]
    blk = 1024
    return pl.pallas_call(
        _col0_kernel,
        grid=(n // blk,),
        in_specs=[pl.BlockSpec((blk, LANES), lambda i: (i, 0))],
        out_specs=pl.BlockSpec((1, blk), lambda i: (0, i)),
        out_shape=jax.ShapeDtypeStruct((1, n), rows.dtype),
        compiler_params=_cparams("parallel"),
        name="col0",
    )(rows)


def _cast_kernel(*refs):
    o_ref = refs[-1]
    cw = refs[0].shape[1]
    for s, x_ref in enumerate(refs[:-1]):
        o_ref[:, cw * s:cw * (s + 1)] = x_ref[...].astype(o_ref.dtype)


def _cast_bf16(w, layer):
    _, e, r, c = w.shape
    tr = 256
    cw = c // CAST_STREAMS
    chunk = lambda s: pl.BlockSpec((None, None, tr, cw), lambda i, j: (layer, i, j, s))
    return pl.pallas_call(
        _cast_kernel,
        grid=(e, r // tr),
        in_specs=[chunk(s) for s in range(CAST_STREAMS)],
        out_specs=pl.BlockSpec((None, tr, c), lambda i, j: (i, j, 0)),
        out_shape=jax.ShapeDtypeStruct((e, r, c), BF16),
        compiler_params=_cparams("parallel", "parallel"),
        name="cast",
    )(*([w] * CAST_STREAMS))


def _gather_rows(data, idx):
    s, r, w = data.shape
    m = idx.shape[0]
    idx_all = (idx[None, :] + (jnp.arange(s, dtype=jnp.int32) * r)[:, None]).reshape(s * m)
    return _gather_flat(data.reshape(s * r, w), idx_all).reshape(s, m, w)


def _expert_kernel(be_ref, nv_ref, xs_ref, wg_ref, wu_ref, wd_ref, o_ref):
    valid = pl.program_id(0) < nv_ref[0]

    @pl.when(valid)
    def _():
        xb = _load_split(xs_ref).astype(BF16)
        acc = jnp.zeros((xb.shape[0], wd_ref.shape[1]), F32)
        for f0 in range(0, wg_ref.shape[1], MOE_TF):
            g = _dot(xb, wg_ref[:, f0:f0 + MOE_TF])
            up = _dot(xb, wu_ref[:, f0:f0 + MOE_TF])
            acc = acc + _dot((g * _sigmoid(g) * up).astype(BF16), wd_ref[f0:f0 + MOE_TF, :])
        _store_split(o_ref, acc)

    @pl.when(jnp.logical_not(valid))
    def _():
        o_ref[...] = jnp.zeros(o_ref.shape, o_ref.dtype)


def _expert_call(block_expert, n_valid, xs, wg, wu, wd):
    n_slots = xs.shape[1]
    d = wg.shape[1]
    dff = wg.shape[2]
    nb = n_slots // MOE_BLK
    resident = lambda r, c: pl.BlockSpec((None, r, c), lambda i, be, nv: (be[i], 0, 0), pipeline_mode=pl.Buffered(1))
    grid_spec = pltpu.PrefetchScalarGridSpec(
        num_scalar_prefetch=2,
        grid=(nb,),
        in_specs=[_split_spec(MOE_BLK)(d), resident(d, dff), resident(d, dff), resident(dff, d)],
        out_specs=_split_spec(MOE_BLK)(d),
    )
    return pl.pallas_call(
        _expert_kernel,
        grid_spec=grid_spec,
        out_shape=_split_shape(n_slots, d),
        compiler_params=_cparams("arbitrary"),
        name="experts",
    )(block_expert, n_valid, xs, wg, wu, wd)


def _combine_kernel(x_ref, modt_ref, ng_ref, y1_ref, y2_ref, r_ref, o_ref):
    x = x_ref[...]
    rows = x.shape[0]
    f = r_ref[:, 2:3] * _load_split(y1_ref) + r_ref[:, 3:4] * _load_split(y2_ref)
    o_ref[...] = x + _mod_rows(modt_ref, 5, rows) * _rms(f, ng_ref[3:4, :])


def _combine_call(x_flat, modt, ng, y12, rinfo):
    n, d = x_flat.shape
    nsub = FFN_TM // TM
    nblk = n // FFN_TM
    tok = lambda w: pl.BlockSpec((FFN_TM, w), lambda i: (i, 0))
    second = pl.BlockSpec((GATHER_SPLIT, FFN_TM, d // GATHER_SPLIT // 2), lambda i: (0, i + nblk, 0))
    return pl.pallas_call(
        _combine_kernel,
        grid=(nblk,),
        in_specs=[tok(d), pl.BlockSpec((nsub, N_MOD, d), lambda i: (i, 0, 0)), pl.BlockSpec((8, d), lambda i: (0, 0)),
                  _split_spec(FFN_TM)(d), second, tok(LANES)],
        out_specs=tok(d),
        out_shape=jax.ShapeDtypeStruct((n, d), F32),
        compiler_params=_cparams("parallel"),
        name="combine",
    )(x_flat, modt, ng, y12, y12, rinfo)


def _router_weights(w_router):
    wr = jnp.pad(w_router, ((0, 0), (0, LANES - N_EXPERTS)))
    wr_hi = wr.astype(BF16)
    wr_lo = (wr - wr_hi.astype(F32)).astype(BF16)
    return jnp.stack([wr_hi, wr_lo])


def _moe_layer(x_flat, h, rinfo, rinfo_t, modt, ng, wg, wu, wd):
    n, d = x_flat.shape
    n_assign = 2 * n
    expert = jnp.concatenate([rinfo_t[0], rinfo_t[1]]).astype(jnp.int32)
    onehot = (expert[None, :] == jnp.arange(N_EXPERTS, dtype=jnp.int32)[:, None]).astype(jnp.int32)
    csum = jnp.cumsum(onehot, axis=1)
    rank = jnp.sum((csum - onehot) * onehot, axis=0)
    counts = csum[:, -1]
    padded = (counts + MOE_BLK - 1) // MOE_BLK * MOE_BLK
    pad_end = jnp.cumsum(padded)
    pad_start = pad_end - padded
    dest = jnp.sum(onehot * pad_start[:, None], axis=0) + rank
    nb = -(-n_assign // MOE_BLK) + N_EXPERTS
    row_quantum = 32 * GATHER_WIN // GATHER_SPLIT
    nb = -(-nb * MOE_BLK // row_quantum) * row_quantum // MOE_BLK
    n_slots = nb * MOE_BLK
    block_expert = jnp.minimum(
        jnp.searchsorted(pad_end, jnp.arange(nb, dtype=jnp.int32) * MOE_BLK, side="right"), N_EXPERTS - 1
    ).astype(jnp.int32)
    n_valid = (pad_end[-1:] // MOE_BLK).astype(jnp.int32)
    tok_rows = jnp.broadcast_to((jnp.arange(n_assign, dtype=jnp.int32) % n)[:, None], (n_assign, LANES))
    slot = jnp.arange(n_slots, dtype=jnp.int32)
    slot_used = (slot - jnp.repeat(pad_start[block_expert], MOE_BLK)) < jnp.repeat(counts[block_expert], MOE_BLK)
    slot_tok = jnp.where(slot_used, _first_column(_scatter_flat(tok_rows, dest, n_slots)).reshape(n_slots), 0)

    xs = _gather_rows(h, slot_tok)
    ys = _expert_call(block_expert, n_valid, xs, wg, wu, wd)
    y12 = _gather_rows(ys, dest)
    return _combine_call(x_flat, modt, ng, y12, rinfo)


def _rope_tables(n_ctx, length):
    rows = length // GRID_W
    row = jnp.repeat(jnp.arange(rows, dtype=F32), GRID_W)
    col = jnp.tile(jnp.arange(GRID_W, dtype=F32), rows)
    axis_dim = HEAD_DIM // 2
    inv_freq = ROPE_THETA ** (-jnp.arange(0, axis_dim, 2, dtype=F32) / axis_dim)
    ar = row[:, None] * inv_freq
    ac = col[:, None] * inv_freq
    cos = jnp.concatenate([jnp.cos(ar), jnp.cos(ar), jnp.cos(ac), jnp.cos(ac)], axis=1)
    sin = jnp.concatenate([-jnp.sin(ar), jnp.sin(ar), -jnp.sin(ac), jnp.sin(ac)], axis=1)
    cos = jnp.concatenate([jnp.ones((n_ctx, HEAD_DIM), F32), cos], axis=0)
    sin = jnp.concatenate([jnp.zeros((n_ctx, HEAD_DIM), F32), sin], axis=0)
    return jnp.tile(cos, (1, 2)), jnp.tile(sin, (1, 2))


def kernel(x, c, ctx, c_ctx, ada_w, ada_b, norm_g, w_in, w_out, attn_q_g, attn_k_g, conv_w, hgrn_lb, hgrn_g,
           ffn_w_gate, ffn_w_up, ffn_w_down, moe_router, moe_w_gate, moe_w_up, moe_w_down):
    bsz, length, d = x.shape
    n_ctx = ctx.shape[1]
    depth = w_in.shape[0]
    t = n_ctx + length
    nt = t // TM
    assert n_ctx == TM and length % TM == 0 and d == 8 * LANES
    for n_tok in (bsz * t, bsz * length):
        assert n_tok % FFN_TM == 0 and (2 * n_tok) % (32 * GATHER_WIN) == 0

    r_pad = -(-(bsz + 1) // 8) * 8
    c_all = jnp.concatenate([c, c_ctx[None, :], jnp.zeros((r_pad - bsz - 1, d), F32)], axis=0)
    mod_all = _ada_call(c_all, ada_w, ada_b).reshape(depth, r_pad, N_MOD, d)

    cos_t, sin_t = _rope_tables(n_ctx, length)
    lb_all = jnp.cumsum(jax.nn.softmax(hgrn_lb.astype(F32), axis=0), axis=0)
    lb_all = lb_all - lb_all[0]
    hconsts = _hgrn_consts()
    cw = 2 * LANES
    cidx = jnp.arange(cw)
    head_mean = ((cidx[:, None] // HEAD_DIM == cidx[None, :] // HEAD_DIM).astype(F32) / HEAD_DIM).astype(BF16)

    cos_f = jnp.tile(cos_t, (bsz, 1))
    sin_f = jnp.tile(sin_t, (bsz, 1))
    x_flat = jnp.concatenate([ctx, x], axis=1).reshape(bsz * t, d)
    for l in range(depth):
        mod_lat = mod_all[l, :bsz]
        mod_ctx = jnp.broadcast_to(mod_all[l, bsz][None], (bsz, N_MOD, d))
        mod = jnp.stack([mod_ctx, mod_lat], axis=1)
        modt = jnp.concatenate([mod[:, :1], jnp.broadcast_to(mod[:, 1:], (bsz, nt - 1, N_MOD, d))], axis=1)
        modt = modt.reshape(bsz * nt, N_MOD, d)
        ng = jnp.pad(norm_g[l], ((0, 4), (0, 0)))
        lb = lb_all[l].reshape(1, 2 * cw)
        gc = jnp.concatenate([jnp.log(lb), jnp.log1p(-lb), 1.0 - lb, jnp.zeros((5, 2 * cw), F32)], axis=0)
        qg = jnp.tile(attn_q_g[l], 2)[None, :]
        kg = jnp.tile(attn_k_g[l], 2)[None, :]

        q, k, vt, cbog, u, qv, kk, gg = _inproj_call(x_flat, modt, ng, w_in[l].astype(BF16), cos_f, sin_f, qg, kg, gc)
        attn = _attn_call(q, k, vt, bsz)
        o_f, o_b = _hgrn_call(qv, kk, gg, hconsts, bsz)
        mix = (attn, cbog, u, o_f, o_b, hgrn_g[l][None, :],
               jnp.pad(conv_w[l], ((0, 5), (0, 0))), head_mean, w_out[l].astype(BF16))
        if l % 2 == 0:
            weights = (ffn_w_gate[l // 2].astype(BF16), ffn_w_up[l // 2].astype(BF16), ffn_w_down[l // 2].astype(BF16))
        else:
            weights = (_router_weights(moe_router[l // 2]),)
            experts = (_cast_bf16(moe_w_gate, l // 2), _cast_bf16(moe_w_up, l // 2), _cast_bf16(moe_w_down, l // 2))
        mode = "ffn" if l % 2 == 0 else "router"

        if l < depth - 1:
            if mode == "ffn":
                x_flat = _mixer_call(x_flat, modt, ng, mix, mode, weights, nt)
            else:
                x_flat = _mixer_call(x_flat, modt, ng, mix, "plain", (), nt)
                h, rinfo, rinfo_t = _mixer_call(x_flat, modt, ng, None, mode, weights)
                x_flat = _moe_layer(x_flat, h, rinfo, rinfo_t, modt, ng, *experts)
            continue

        x_all = _mixer_call(x_flat, modt, ng, mix, "plain", (), nt).reshape(bsz, t, d)
        x_flat = x_all[:, n_ctx:, :].reshape(bsz * length, d)
        modt = jnp.broadcast_to(mod[:, 1:], (bsz, length // TM, N_MOD, d)).reshape(bsz * (length // TM), N_MOD, d)
        if mode == "ffn":
            x_flat = _mixer_call(x_flat, modt, ng, None, mode, weights)
        else:
            h, rinfo, rinfo_t = _mixer_call(x_flat, modt, ng, None, mode, weights)
            x_flat = _moe_layer(x_flat, h, rinfo, rinfo_t, modt, ng, *experts)
        return x_flat.reshape(bsz, length, d)
```

```python
import functools

import jax
import jax.numpy as jnp
from jax import lax
from jax.experimental import pallas as pl
from jax.experimental.pallas import tpu as pltpu
from jax.experimental.pallas import tpu_sc as plsc

F32 = jnp.float32
BF16 = jnp.bfloat16

HEAD_DIM = 64
GRID_W = 64
ROPE_THETA = 10000.0
ATTN_SCALE = HEAD_DIM ** -0.5
LOG2_E = 1.4426950408889634
HGRN_CHUNK = 64
HGRN_SUB = 8
VT_ROWS = HEAD_DIM + 16
ATTN_AHEAD = 6
N_EXPERTS = 8
N_MOD = 6
EPS = 1e-6

LANES = 128
TM = 256
FFN_TM = 512
FFN_TF = 512
N_MIX_REFS = 11
CAST_STREAMS = 4
MOE_BLK = 512
MOE_TF = 512
GATHER_WIN = 128
GATHER_SPLIT = 2
NEG_BIG = -1e30
VMEM_LIMIT = 56 * 1024 * 1024


def _cparams(*sem):
    return pltpu.CompilerParams(dimension_semantics=sem, vmem_limit_bytes=VMEM_LIMIT)


def _sigmoid(z):
    return 1.0 / (1.0 + jnp.exp(-z))


def _dot(a, b):
    return jnp.dot(a, b, preferred_element_type=F32)


def _dot_nt(a, b):
    return lax.dot_general(a, b, (((1,), (1,)), ((), ())), preferred_element_type=F32)


def _dot_tn(a, b):
    return lax.dot_general(a, b, (((0,), (0,)), ((), ())), preferred_element_type=F32)


def _rms(x, g):
    return x * lax.rsqrt(jnp.mean(x * x, axis=-1, keepdims=True) + EPS) * g


def _ada_kernel(c_ref, w_ref, b_ref, o_ref):
    c = c_ref[...]
    s = (c * _sigmoid(c)).astype(BF16)
    o_ref[...] = _dot(s, w_ref[...].astype(BF16)) + b_ref[...]


def _ada_call(c_all, ada_w, ada_b):
    depth, d, n = ada_w.shape
    r = c_all.shape[0]
    tn = 512
    return pl.pallas_call(
        _ada_kernel,
        grid=(depth, n // tn),
        in_specs=[
            pl.BlockSpec((r, d), lambda l, j: (0, 0)),
            pl.BlockSpec((None, d, tn), lambda l, j: (l, 0, j)),
            pl.BlockSpec((None, 1, tn), lambda l, j: (l, 0, j)),
        ],
        out_specs=pl.BlockSpec((None, r, tn), lambda l, j: (l, 0, j)),
        out_shape=jax.ShapeDtypeStruct((depth, r, n), F32),
        compiler_params=_cparams("parallel", "parallel"),
        name="ada",
    )(c_all, ada_w, ada_b.reshape(depth, 1, n))


def _inproj_kernel(x_ref, modt_ref, ng_ref, w_ref, cos_ref, sin_ref, qg_ref, kg_ref, gc_ref,
                   q_ref, k_ref, vt_ref, cbog_ref, u_ref, qv_ref, kk_ref, gg_ref):
    x = x_ref[...]
    rows = x.shape[0]
    h = _rms(x, ng_ref[0:1, :]) * (1.0 + _mod_rows(modt_ref, 1, rows)) + _mod_rows(modt_ref, 0, rows)
    hb = h.astype(BF16)

    def proj(c0, n):
        return _dot(hb, w_ref[:, c0:c0 + n])

    cos = cos_ref[...]
    sin = sin_ref[...]
    lane = lax.broadcasted_iota(jnp.int32, (rows, LANES), 1)
    first_half = (lane % 32) < 16
    lo = lane < HEAD_DIM

    def norm_rope(blk, g):
        b2 = blk * blk
        s_lo = jnp.sum(jnp.where(lo, b2, 0.0), axis=-1, keepdims=True)
        s_hi = jnp.sum(jnp.where(lo, 0.0, b2), axis=-1, keepdims=True)
        v = blk * lax.rsqrt(jnp.where(lo, s_lo, s_hi) * (1.0 / HEAD_DIM) + EPS) * g
        partner = jnp.where(first_half, pltpu.roll(v, LANES - 16, 1), pltpu.roll(v, 16, 1))
        return v * cos + partner * sin

    qa = proj(0, 4 * LANES)
    for j in range(4):
        qn = norm_rope(qa[:, LANES * j:LANES * (j + 1)], qg_ref[...]) * (ATTN_SCALE * LOG2_E)
        sw = pltpu.roll(qn, HEAD_DIM, 1)
        if j < 2:
            q_ref[2 * j] = jnp.where(lo, qn, 0.0).astype(BF16)
            q_ref[2 * j + 1] = jnp.where(lo, sw, 0.0).astype(BF16)
        else:
            q_ref[2 * j] = jnp.where(lo, 0.0, sw).astype(BF16)
            q_ref[2 * j + 1] = jnp.where(lo, 0.0, qn).astype(BF16)

    kv = proj(4 * LANES, 2 * LANES)
    k_ref[...] = norm_rope(kv[:, :LANES], kg_ref[...]).astype(BF16)
    vt = kv[:, LANES:].T
    ones = jnp.ones((VT_ROWS - HEAD_DIM, TM), F32)
    for s in range(rows // TM):
        for g in range(2):
            vt_ref[g, s] = jnp.concatenate(
                [vt[HEAD_DIM * g:HEAD_DIM * (g + 1), TM * s:TM * (s + 1)], ones], axis=0).astype(BF16)

    c0 = 6 * LANES
    c3 = proj(c0, 6 * LANES)
    cw = 2 * LANES
    u_ref[...] = c3[:, cw:2 * cw] * c3[:, 2 * cw:3 * cw]

    hg = proj(c0 + 6 * LANES, 10 * LANES)
    zf, zb, iv, hq, og = (hg[:, cw * i:cw * (i + 1)] for i in range(5))
    cbog_ref[:, :cw] = c3[:, :cw].astype(BF16)
    cbog_ref[:, cw:] = og.astype(BF16)
    qv_ref[:, :cw] = (hq * _sigmoid(hq)).astype(BF16)
    qv_ref[:, cw:] = iv.astype(BF16)
    for d, z in enumerate((zf, zb)):
        log_lb = gc_ref[0:1, cw * d:cw * (d + 1)]
        log1m_lb = gc_ref[1:2, cw * d:cw * (d + 1)]
        one_m_lb = gc_ref[2:3, cw * d:cw * (d + 1)]
        t = jnp.exp(-jnp.abs(z))
        log_sig = jnp.minimum(z, 0.0) - jnp.log(1.0 + t)
        a2 = log1m_lb + log_sig
        log_f = jnp.maximum(log_lb, a2) + jnp.log(1.0 + jnp.exp(-jnp.abs(log_lb - a2)))
        sig_neg = jnp.where(z >= 0, t, 1.0) / (1.0 + t)
        gg_ref[:, cw * d:cw * (d + 1)] = log_f
        kk_ref[:, cw * d:cw * (d + 1)] = (one_m_lb * sig_neg).astype(BF16)


def _inproj_call(x_flat, modt, ng, w1, cos_f, sin_f, qg, kg, gc):
    n, d = x_flat.shape
    nsub = FFN_TM // TM
    tok = lambda w: pl.BlockSpec((FFN_TM, w), lambda i: (i, 0))
    const = lambda a: pl.BlockSpec(a.shape, lambda i: (0,) * a.ndim, pipeline_mode=pl.Buffered(1))
    return pl.pallas_call(
        _inproj_kernel,
        grid=(n // FFN_TM,),
        in_specs=[
            tok(d),
            pl.BlockSpec((nsub, N_MOD, d), lambda i: (i, 0, 0)),
            const(ng), const(w1), tok(LANES), tok(LANES), const(qg), const(kg), const(gc),
        ],
        out_specs=[
            pl.BlockSpec((8, FFN_TM, LANES), lambda i: (0, i, 0)),
            tok(LANES),
            pl.BlockSpec((2, nsub, VT_ROWS, TM), lambda i: (0, i, 0, 0)),
            tok(4 * LANES),
            tok(2 * LANES),
            tok(4 * LANES),
            tok(4 * LANES),
            tok(4 * LANES),
        ],
        out_shape=[
            jax.ShapeDtypeStruct((8, n, LANES), BF16),
            jax.ShapeDtypeStruct((n, LANES), BF16),
            jax.ShapeDtypeStruct((2, n // TM, VT_ROWS, TM), BF16),
            jax.ShapeDtypeStruct((n, 4 * LANES), BF16),
            jax.ShapeDtypeStruct((n, 2 * LANES), F32),
            jax.ShapeDtypeStruct((n, 4 * LANES), BF16),
            jax.ShapeDtypeStruct((n, 4 * LANES), BF16),
            jax.ShapeDtypeStruct((n, 4 * LANES), F32),
        ],
        compiler_params=_cparams("parallel"),
        name="inproj",
    )(x_flat, modt, ng, w1, cos_f, sin_f, qg, kg, gc)


def _attn_kernel(q_ref, k_ref, vt_ref, o_ref, *, n_kv):
    def score(j, h):
        return _dot_nt(k_ref[TM * j:TM * (j + 1), :], q_ref[h])

    def attend(n_tiles):
        steps = [(j, h) for j in range(n_tiles) for h in range(8)]
        scores = [score(j, h) for j, h in steps[:ATTN_AHEAD]]
        m = [None] * 8
        acc = [None] * 8
        for i, (j, h) in enumerate(steps):
            if i + ATTN_AHEAD < len(steps):
                scores.append(score(*steps[i + ATTN_AHEAD]))
            s = scores.pop(0)
            m_tile = jnp.max(s, axis=0, keepdims=True)
            if j == 0:
                m_new = m_tile
                acc[h] = _dot(vt_ref[h // 4, j], jnp.exp2(s - m_new).astype(BF16))
            else:
                m_new = jnp.maximum(m[h], m_tile)
                pv = _dot(vt_ref[h // 4, j], jnp.exp2(s - m_new).astype(BF16))
                acc[h] = jnp.exp2(m[h] - m_new) * acc[h] + pv
            m[h] = m_new
        outs = [acc[h][:HEAD_DIM, :] / acc[h][HEAD_DIM:HEAD_DIM + 1, :] for h in range(8)]
        o_ref[...] = jnp.concatenate(outs, axis=0).T.astype(BF16)

    @pl.when(pl.program_id(1) == 0)
    def _():
        attend(1)

    @pl.when(pl.program_id(1) > 0)
    def _():
        attend(n_kv)


def _attn_call(q, k, vt, b):
    n = k.shape[0]
    t = n // b
    nt = t // TM
    return pl.pallas_call(
        functools.partial(_attn_kernel, n_kv=nt),
        grid=(b, nt),
        in_specs=[
            pl.BlockSpec((8, TM, LANES), lambda i, j: (0, i * nt + j, 0)),
            pl.BlockSpec((t, LANES), lambda i, j: (i, 0)),
            pl.BlockSpec((2, nt, VT_ROWS, TM), lambda i, j: (0, i, 0, 0)),
        ],
        out_specs=pl.BlockSpec((TM, 4 * LANES), lambda i, j: (i * nt + j, 0)),
        out_shape=jax.ShapeDtypeStruct((n, 4 * LANES), BF16),
        compiler_params=_cparams("parallel", "parallel"),
        name="attn",
    )(q, k, vt)


def _split3(g):
    hi = g.astype(BF16)
    r1 = g - hi.astype(F32)
    mid = r1.astype(BF16)
    lo = (r1 - mid.astype(F32)).astype(BF16)
    return hi, mid, lo


def _hgrn_kernel(qvf_ref, kf_ref, gf_ref, qvb_ref, kb_ref, gb_ref, tri_ref, trij_ref, j_ref, e_ref, bd_ref,
                 md_ref, mo_ref, of_ref, ob_ref, st_ref, x_s):
    hw = 2 * LANES
    nch = TM // HGRN_CHUNK

    @pl.when(pl.program_id(1) == 0)
    def _():
        st_ref[...] = jnp.zeros(st_ref.shape, F32)

    flip = j_ref[...]

    def cumsum(m_ref, g):
        m = m_ref[...]
        hi, mid, lo = _split3(g)
        return _dot(m, hi) + _dot(m, mid) + _dot(m, lo)

    qvf = qvf_ref[...]
    fl = _dot(flip, jnp.concatenate([qvb_ref[...], kb_ref[...]], axis=1))
    q = jnp.concatenate([qvf[:, :hw].astype(F32), fl[:, :hw]], axis=0)
    v = jnp.concatenate([qvf[:, hw:].astype(F32), fl[:, hw:2 * hw]], axis=0)
    k = jnp.concatenate([kf_ref[...].astype(F32), fl[:, 2 * hw:]], axis=0)
    b = jnp.concatenate([cumsum(tri_ref, gf_ref[...]), cumsum(trij_ref, gb_ref[...])], axis=0) * LOG2_E

    ngr = 2 * TM // HGRN_SUB
    nsub = HGRN_CHUNK // HGRN_SUB
    q3 = q.reshape(ngr, HGRN_SUB, hw)
    b3 = b.reshape(ngr, HGRN_SUB, hw)
    k3 = k.reshape(ngr, HGRN_SUB, hw)
    for s in range(HGRN_SUB):
        x = q3 * jnp.exp2(jnp.minimum(b3 - b3[:, s:s + 1, :], 0.0)) * k3[:, s:s + 1, :]
        x_s[:, hw * s:hw * (s + 1)] = x.reshape(2 * TM, hw).astype(BF16)
    a = _dot(x_s[...], e_ref[...])

    bd = bd_ref[...]
    bd16 = bd.astype(BF16)
    md = md_ref[...]
    mo = mo_ref[...]
    for d in range(2):
        st = st_ref[d]
        outs = []
        for c in range(nch):
            r0 = TM * d + HGRN_CHUNK * c
            bc = b[r0:r0 + HGRN_CHUNK]
            qc = q[r0:r0 + HGRN_CHUNK]
            kc = k[r0:r0 + HGRN_CHUNK]
            vc = v[r0:r0 + HGRN_CHUNK]
            ac = a[r0:r0 + HGRN_CHUNK] * md
            r_end = [bc[HGRN_SUB * (j + 1) - 1:HGRN_SUB * (j + 1), :] for j in range(nsub)]
            r_own = jnp.concatenate([jnp.broadcast_to(r, (HGRN_SUB, hw)) for r in r_end], axis=0)
            kt_bd = jnp.concatenate([(kc * jnp.exp2(r_own - bc)).astype(BF16)] * 4, axis=0) * bd16
            q_rel = jnp.concatenate(
                [qc * jnp.exp2(jnp.minimum(bc - r_end[j], 0.0)) for j in range(nsub - 1)], axis=0).astype(BF16)
            p = _dot_nt(q_rel, kt_bd) * mo
            for j in range(nsub - 1):
                ac = ac + p[HGRN_CHUNK * j:HGRN_CHUNK * (j + 1)]
            bl = r_end[-1]
            qe = (qc * jnp.exp2(bc)).astype(BF16)
            ke = (kc * jnp.exp2(bl - bc)).astype(BF16)
            vcb = vc.astype(BF16)
            vbd = jnp.concatenate([vcb] * 4, axis=0) * bd16
            outs.append(_dot(ac.astype(BF16), vbd) + _dot_nt(qe, st.astype(BF16)))
            st = st * jnp.exp2(bl) + _dot_tn(vcb, ke) * bd
        st_ref[d] = st
        od = jnp.concatenate(outs, axis=0).astype(BF16)
        if d == 0:
            of_ref[...] = od
        else:
            ob_ref[...] = _dot(flip, od).astype(BF16)


def _hgrn_consts():
    hw = 2 * LANES
    r = jnp.arange(TM)
    same_chunk = (r[:, None] // HGRN_CHUNK) == (r[None, :] // HGRN_CHUNK)
    tri = (same_chunk & (r[None, :] <= r[:, None])).astype(F32)
    flip = (r[:, None] + r[None, :] == TM - 1).astype(F32)
    trij = tri @ flip
    c = jnp.arange(hw)
    s = jnp.arange(HGRN_SUB)
    e = ((c[None, :, None] // HEAD_DIM == c[None, None, :] // HGRN_CHUNK)
         & (c[None, None, :] % HGRN_SUB == s[:, None, None])).astype(BF16).reshape(HGRN_SUB * hw, hw)
    bd = (c[:, None] // HEAD_DIM == c[None, :] // HEAD_DIM).astype(F32)
    tt = jnp.arange(HGRN_CHUNK)[:, None]
    ss = (c % HGRN_CHUNK)[None, :]
    md = ((ss // HGRN_SUB == tt // HGRN_SUB) & (ss <= tt)).astype(F32)
    mo = jnp.concatenate([((ss // HGRN_SUB == j) & (tt // HGRN_SUB > j)).astype(F32)
                          for j in range(HGRN_CHUNK // HGRN_SUB - 1)], axis=0)
    return tri.astype(BF16), trij.astype(BF16), flip.astype(BF16), e, bd, md, mo


def _hgrn_call(qv, kk, gg, consts, b):
    n = qv.shape[0]
    nt = n // b // TM
    hw = 2 * LANES
    tri, trij, flip, e, bd, md, mo = consts
    fwd = lambda i, j: i * nt + j
    bwd = lambda i, j: i * nt + jnp.where(j == 0, 0, nt - j)
    const2 = lambda a: pl.BlockSpec(a.shape, lambda i, j: (0,) * a.ndim)
    return pl.pallas_call(
        _hgrn_kernel,
        grid=(b, nt),
        in_specs=[
            pl.BlockSpec((TM, 2 * hw), lambda i, j: (fwd(i, j), 0)),
            pl.BlockSpec((TM, hw), lambda i, j: (fwd(i, j), 0)),
            pl.BlockSpec((TM, hw), lambda i, j: (fwd(i, j), 0)),
            pl.BlockSpec((TM, 2 * hw), lambda i, j: (bwd(i, j), 0)),
            pl.BlockSpec((TM, hw), lambda i, j: (bwd(i, j), 1)),
            pl.BlockSpec((TM, hw), lambda i, j: (bwd(i, j), 1)),
            const2(tri), const2(trij), const2(flip), const2(e), const2(bd), const2(md), const2(mo),
        ],
        out_specs=[
            pl.BlockSpec((TM, hw), lambda i, j: (fwd(i, j), 0)),
            pl.BlockSpec((TM, hw), lambda i, j: (bwd(i, j), 0)),
        ],
        out_shape=[jax.ShapeDtypeStruct((n, hw), BF16)] * 2,
        scratch_shapes=[
            pltpu.VMEM((2, hw, hw), F32),
            pltpu.VMEM((2 * TM, HGRN_SUB * hw), BF16),
        ],
        compiler_params=_cparams("parallel", "arbitrary"),
        name="hgrn",
    )(qv, kk, gg, qv, kk, gg, tri, trij, flip, e, bd, md, mo)


def _mix_out(x_ref, modt_ref, ng_ref, attn_ref, cbog_ref, u_ref, up_ref, un_ref, of_ref, ob_ref,
             hgg_ref, cw_ref, hn_ref, w_ref, *, n_tiles):
    rows = x_ref.shape[0]
    nsub = rows // TM
    cw = 2 * LANES
    u = u_ref[...]
    row = lax.broadcasted_iota(jnp.int32, (rows, cw), 0)
    u_prev = jnp.where(row == 0, up_ref[7:8, :], pltpu.roll(u, 1, 0))
    u_next = jnp.where(row == rows - 1, un_ref[0:1, :], pltpu.roll(u, rows - 1, 0))
    for s in range(nsub):
        t = (pl.program_id(0) * nsub + s) % n_tiles
        u_prev = jnp.where(t >= 2, u_prev, jnp.where(row == s * TM, 0.0, u_prev))
        u_next = jnp.where((t >= 1) & (t < n_tiles - 1), u_next, jnp.where(row == s * TM + TM - 1, 0.0, u_next))
    cbog = cbog_ref[...].astype(F32)
    conv = cbog[:, :cw] * (cw_ref[0:1, :] * u_prev + cw_ref[1:2, :] * u + cw_ref[2:3, :] * u_next)

    o = of_ref[...].astype(F32) + ob_ref[...].astype(F32)
    ms = _dot((o * o).astype(BF16), hn_ref[...])
    og = cbog[:, cw:]
    hg = o * lax.rsqrt(ms + EPS) * hgg_ref[...] * (og * _sigmoid(og))

    y = (_dot(attn_ref[...], w_ref[0:2 * cw, :]) + _dot(conv.astype(BF16), w_ref[2 * cw:3 * cw, :])
         + _dot(hg.astype(BF16), w_ref[3 * cw:4 * cw, :]))
    return x_ref[...] + _mod_rows(modt_ref, 2, rows) * _rms(y, ng_ref[1:2, :])


def _mod_rows(modt_ref, i, rows):
    sub = rows // modt_ref.shape[0]
    return jnp.concatenate(
        [jnp.broadcast_to(modt_ref[s, i:i + 1, :], (sub, modt_ref.shape[2])) for s in range(modt_ref.shape[0])],
        axis=0)


def _store_split(ref, val):
    w = ref.shape[2]
    for c in range(ref.shape[0]):
        lo = val[:, 2 * w * c:2 * w * c + w].astype(BF16).astype(F32)
        hi = val[:, 2 * w * c + w:2 * w * (c + 1)].astype(BF16).astype(F32)
        ref[c] = lax.bitcast_convert_type(hi, jnp.uint32) | (lax.bitcast_convert_type(lo, jnp.uint32) >> 16)


def _load_split(ref):
    cols = []
    for c in range(ref.shape[0]):
        word = ref[c]
        cols.append(lax.bitcast_convert_type(word << 16, F32))
        cols.append(lax.bitcast_convert_type(word & jnp.uint32(0xFFFF0000), F32))
    return jnp.concatenate(cols, axis=1)


def _split_shape(n, d):
    return jax.ShapeDtypeStruct((GATHER_SPLIT, n, d // GATHER_SPLIT // 2), jnp.uint32)


def _split_spec(rows):
    return lambda d: pl.BlockSpec((GATHER_SPLIT, rows, d // GATHER_SPLIT // 2), lambda i, *_: (0, i, 0))


def _ffn_body(x, modt_ref, ng_ref, wg_ref, wu_ref, wd_ref):
    rows = x.shape[0]
    h = (_rms(x, ng_ref[2:3, :]) * (1.0 + _mod_rows(modt_ref, 4, rows)) + _mod_rows(modt_ref, 3, rows)).astype(BF16)
    acc = jnp.zeros(x.shape, F32)
    dff = wg_ref.shape[1]
    for f0 in range(0, dff, FFN_TF):
        f1 = min(f0 + FFN_TF, dff)
        g = _dot(h, wg_ref[:, f0:f1])
        up = _dot(h, wu_ref[:, f0:f1])
        acc = acc + _dot((g * _sigmoid(g) * up).astype(BF16), wd_ref[f0:f1, :])
    return x + _mod_rows(modt_ref, 5, rows) * _rms(acc, ng_ref[3:4, :])


def _router_body(x, modt_ref, ng_ref, wr_ref, h_ref, r_ref, rt_ref):
    rows = x.shape[0]
    h = _rms(x, ng_ref[2:3, :]) * (1.0 + _mod_rows(modt_ref, 4, rows)) + _mod_rows(modt_ref, 3, rows)
    _store_split(h_ref, h)
    h_hi = h.astype(BF16)
    h_lo = (h - h_hi.astype(F32)).astype(BF16)
    logits = _dot(h_hi, wr_ref[0]) + _dot(h_lo, wr_ref[0]) + _dot(h_hi, wr_ref[1])
    lane = lax.broadcasted_iota(jnp.int32, logits.shape, 1)
    lg = jnp.where(lane < N_EXPERTS, logits, NEG_BIG)
    m1 = jnp.max(lg, axis=-1, keepdims=True)
    i1 = jnp.min(jnp.where(lg == m1, lane, LANES), axis=-1, keepdims=True)
    lg2 = jnp.where(lane == i1, NEG_BIG, lg)
    m2 = jnp.max(lg2, axis=-1, keepdims=True)
    i2 = jnp.min(jnp.where(lg2 == m2, lane, LANES), axis=-1, keepdims=True)
    e2 = jnp.exp(m2 - m1)
    w1 = 1.0 / (1.0 + e2)
    w2 = e2 / (1.0 + e2)
    r = jnp.where(lane == 0, i1.astype(F32),
                  jnp.where(lane == 1, i2.astype(F32), jnp.where(lane == 2, w1, jnp.where(lane == 3, w2, 0.0))))
    r_ref[...] = r
    rt_ref[...] = r.T[0:8, :]


def _mixer_kernel(*refs, fused, mode, n_tiles):
    x_ref, modt_ref, ng_ref = refs[:3]
    pos = 3
    if fused:
        x = _mix_out(x_ref, modt_ref, ng_ref, *refs[pos:pos + N_MIX_REFS], n_tiles=n_tiles)
        pos += N_MIX_REFS
    else:
        x = x_ref[...]
    if mode == "ffn":
        wg_ref, wu_ref, wd_ref, o_ref = refs[pos:]
        o_ref[...] = _ffn_body(x, modt_ref, ng_ref, wg_ref, wu_ref, wd_ref)
    elif mode == "router":
        wr_ref, *outs = refs[pos:]
        _router_body(x, modt_ref, ng_ref, wr_ref, *outs)
    else:
        refs[pos][...] = x


def _mixer_call(x_flat, modt, ng, mix, mode, weights, n_tiles=0):
    n, d = x_flat.shape
    nsub = FFN_TM // TM
    cw = 2 * LANES
    tok = lambda w: pl.BlockSpec((FFN_TM, w), lambda i: (i, 0))
    resident = lambda a: pl.BlockSpec(a.shape, lambda i: (0,) * a.ndim, pipeline_mode=pl.Buffered(1))
    stream = jax.ShapeDtypeStruct((n, d), F32)
    args = [x_flat, modt, ng]
    in_specs = [tok(d), pl.BlockSpec((nsub, N_MOD, d), lambda i: (i, 0, 0)), pl.BlockSpec((8, d), lambda i: (0, 0))]
    fused = mix is not None
    if fused:
        attn, cbog, u, o_f, o_b, hgg, conv_w, hn, w_out = mix
        nb8 = n // 8
        args += [attn, cbog, u, u, u, o_f, o_b, hgg, conv_w, hn, w_out]
        in_specs += [
            tok(2 * cw), tok(2 * cw), tok(cw),
            pl.BlockSpec((8, cw), lambda i: (jnp.maximum(i * (FFN_TM // 8) - 1, 0), 0)),
            pl.BlockSpec((8, cw), lambda i: (jnp.minimum((i + 1) * (FFN_TM // 8), nb8 - 1), 0)),
            tok(cw), tok(cw), resident(hgg), resident(conv_w), resident(hn), resident(w_out),
        ]
        assert len(in_specs) == 3 + N_MIX_REFS
    args += list(weights)
    in_specs += [resident(w) for w in weights]
    if mode == "router":
        out_specs = [_split_spec(FFN_TM)(d), tok(LANES), pl.BlockSpec((8, FFN_TM), lambda i: (0, i))]
        out_shape = [_split_shape(n, d),
                     jax.ShapeDtypeStruct((n, LANES), F32), jax.ShapeDtypeStruct((8, n), F32)]
    else:
        out_specs, out_shape = tok(d), stream
    return pl.pallas_call(
        functools.partial(_mixer_kernel, fused=fused, mode=mode, n_tiles=n_tiles),
        grid=(n // FFN_TM,),
        in_specs=in_specs,
        out_specs=out_specs,
        out_shape=out_shape,
        compiler_params=_cparams("parallel"),
        name=("mix_" if fused else "") + mode,
    )(*args)


def _gather_flat(data, idx):
    m = idx.shape[0]
    w = data.shape[1]
    mesh = plsc.VectorSubcoreMesh(core_axis_name="core", subcore_axis_name="subcore")

    @functools.partial(pl.kernel, out_type=jax.ShapeDtypeStruct((m, w), data.dtype), mesh=mesh)
    def gather(x_hbm, i_hbm, o_hbm):
        def body(i_vmem, o_vmem):
            pltpu.sync_copy(x_hbm.at[i_vmem.at[0]], o_vmem)

        pltpu.emit_pipeline(
            body,
            grid=(m // GATHER_WIN,),
            in_specs=[pl.BlockSpec((1, GATHER_WIN), lambda i: (0, i))],
            out_specs=[pl.BlockSpec((GATHER_WIN, w), lambda i: (i, 0))],
            core_axis_name=("core", "subcore"),
            dimension_semantics=(pltpu.PARALLEL,),
        )(i_hbm, o_hbm)

    return gather(data, idx.reshape(1, m))


def _scatter_flat(rows, idx, n_out):
    m, w = rows.shape
    mesh = plsc.VectorSubcoreMesh(core_axis_name="core", subcore_axis_name="subcore")

    @functools.partial(pl.kernel, out_type=jax.ShapeDtypeStruct((n_out, w), rows.dtype), mesh=mesh)
    def scatter(x_hbm, i_hbm, o_hbm):
        def body(x_vmem, i_vmem):
            pltpu.sync_copy(x_vmem, o_hbm.at[i_vmem.at[0]])

        pltpu.emit_pipeline(
            body,
            grid=(m // GATHER_WIN,),
            in_specs=[pl.BlockSpec((GATHER_WIN, w), lambda i: (i, 0)),
                      pl.BlockSpec((1, GATHER_WIN), lambda i: (0, i))],
            out_specs=[],
            core_axis_name=("core", "subcore"),
            dimension_semantics=(pltpu.PARALLEL,),
        )(x_hbm, i_hbm)

    return scatter(rows, idx.reshape(1, m))


def _col0_kernel(x_ref, o_ref):
    o_ref[...] = x_ref[...].T[0:1, :]


def _first_column(rows):
    n = rows.shape[0]
    blk = 1024
    return pl.pallas_call(
        _col0_kernel,
        grid=(n // blk,),
        in_specs=[pl.BlockSpec((blk, LANES), lambda i: (i, 0))],
        out_specs=pl.BlockSpec((1, blk), lambda i: (0, i)),
        out_shape=jax.ShapeDtypeStruct((1, n), rows.dtype),
        compiler_params=_cparams("parallel"),
        name="col0",
    )(rows)


def _cast_kernel(*refs):
    o_ref = refs[-1]
    cw = refs[0].shape[1]
    for s, x_ref in enumerate(refs[:-1]):
        o_ref[:, cw * s:cw * (s + 1)] = x_ref[...].astype(o_ref.dtype)


def _cast_bf16(w, layer):
    _, e, r, c = w.shape
    tr = 256
    cw = c // CAST_STREAMS
    chunk = lambda s: pl.BlockSpec((None, None, tr, cw), lambda i, j: (layer, i, j, s))
    return pl.pallas_call(
        _cast_kernel,
        grid=(e, r // tr),
        in_specs=[chunk(s) for s in range(CAST_STREAMS)],
        out_specs=pl.BlockSpec((None, tr, c), lambda i, j: (i, j, 0)),
        out_shape=jax.ShapeDtypeStruct((e, r, c), BF16),
        compiler_params=_cparams("parallel", "parallel"),
        name="cast",
    )(*([w] * CAST_STREAMS))


def _gather_rows(data, idx):
    s, r, w = data.shape
    m = idx.shape[0]
    idx_all = (idx[None, :] + (jnp.arange(s, dtype=jnp.int32) * r)[:, None]).reshape(s * m)
    return _gather_flat(data.reshape(s * r, w), idx_all).reshape(s, m, w)


def _expert_kernel(be_ref, nv_ref, xs_ref, wg_ref, wu_ref, wd_ref, o_ref):
    valid = pl.program_id(0) < nv_ref[0]

    @pl.when(valid)
    def _():
        xb = _load_split(xs_ref).astype(BF16)
        acc = jnp.zeros((xb.shape[0], wd_ref.shape[1]), F32)
        for f0 in range(0, wg_ref.shape[1], MOE_TF):
            g = _dot(xb, wg_ref[:, f0:f0 + MOE_TF])
            up = _dot(xb, wu_ref[:, f0:f0 + MOE_TF])
            acc = acc + _dot((g * _sigmoid(g) * up).astype(BF16), wd_ref[f0:f0 + MOE_TF, :])
        _store_split(o_ref, acc)

    @pl.when(jnp.logical_not(valid))
    def _():
        o_ref[...] = jnp.zeros(o_ref.shape, o_ref.dtype)


def _expert_call(block_expert, n_valid, xs, wg, wu, wd):
    n_slots = xs.shape[1]
    d = wg.shape[1]
    dff = wg.shape[2]
    nb = n_slots // MOE_BLK
    resident = lambda r, c: pl.BlockSpec((None, r, c), lambda i, be, nv: (be[i], 0, 0), pipeline_mode=pl.Buffered(1))
    grid_spec = pltpu.PrefetchScalarGridSpec(
        num_scalar_prefetch=2,
        grid=(nb,),
        in_specs=[_split_spec(MOE_BLK)(d), resident(d, dff), resident(d, dff), resident(dff, d)],
        out_specs=_split_spec(MOE_BLK)(d),
    )
    return pl.pallas_call(
        _expert_kernel,
        grid_spec=grid_spec,
        out_shape=_split_shape(n_slots, d),
        compiler_params=_cparams("arbitrary"),
        name="experts",
    )(block_expert, n_valid, xs, wg, wu, wd)


def _combine_kernel(x_ref, modt_ref, ng_ref, y1_ref, y2_ref, r_ref, o_ref):
    x = x_ref[...]
    rows = x.shape[0]
    f = r_ref[:, 2:3] * _load_split(y1_ref) + r_ref[:, 3:4] * _load_split(y2_ref)
    o_ref[...] = x + _mod_rows(modt_ref, 5, rows) * _rms(f, ng_ref[3:4, :])


def _combine_call(x_flat, modt, ng, y12, rinfo):
    n, d = x_flat.shape
    nsub = FFN_TM // TM
    nblk = n // FFN_TM
    tok = lambda w: pl.BlockSpec((FFN_TM, w), lambda i: (i, 0))
    second = pl.BlockSpec((GATHER_SPLIT, FFN_TM, d // GATHER_SPLIT // 2), lambda i: (0, i + nblk, 0))
    return pl.pallas_call(
        _combine_kernel,
        grid=(nblk,),
        in_specs=[tok(d), pl.BlockSpec((nsub, N_MOD, d), lambda i: (i, 0, 0)), pl.BlockSpec((8, d), lambda i: (0, 0)),
                  _split_spec(FFN_TM)(d), second, tok(LANES)],
        out_specs=tok(d),
        out_shape=jax.ShapeDtypeStruct((n, d), F32),
        compiler_params=_cparams("parallel"),
        name="combine",
    )(x_flat, modt, ng, y12, y12, rinfo)


def _router_weights(w_router):
    wr = jnp.pad(w_router, ((0, 0), (0, LANES - N_EXPERTS)))
    wr_hi = wr.astype(BF16)
    wr_lo = (wr - wr_hi.astype(F32)).astype(BF16)
    return jnp.stack([wr_hi, wr_lo])


def _moe_layer(x_flat, h, rinfo, rinfo_t, modt, ng, wg, wu, wd):
    n, d = x_flat.shape
    n_assign = 2 * n
    expert = jnp.concatenate([rinfo_t[0], rinfo_t[1]]).astype(jnp.int32)
    onehot = (expert[None, :] == jnp.arange(N_EXPERTS, dtype=jnp.int32)[:, None]).astype(jnp.int32)
    csum = jnp.cumsum(onehot, axis=1)
    rank = jnp.sum((csum - onehot) * onehot, axis=0)
    counts = csum[:, -1]
    padded = (counts + MOE_BLK - 1) // MOE_BLK * MOE_BLK
    pad_end = jnp.cumsum(padded)
    pad_start = pad_end - padded
    dest = jnp.sum(onehot * pad_start[:, None], axis=0) + rank
    nb = -(-n_assign // MOE_BLK) + N_EXPERTS
    row_quantum = 32 * GATHER_WIN // GATHER_SPLIT
    nb = -(-nb * MOE_BLK // row_quantum) * row_quantum // MOE_BLK
    n_slots = nb * MOE_BLK
    block_expert = jnp.minimum(
        jnp.searchsorted(pad_end, jnp.arange(nb, dtype=jnp.int32) * MOE_BLK, side="right"), N_EXPERTS - 1
    ).astype(jnp.int32)
    n_valid = (pad_end[-1:] // MOE_BLK).astype(jnp.int32)
    tok_rows = jnp.broadcast_to((jnp.arange(n_assign, dtype=jnp.int32) % n)[:, None], (n_assign, LANES))
    slot = jnp.arange(n_slots, dtype=jnp.int32)
    slot_used = (slot - jnp.repeat(pad_start[block_expert], MOE_BLK)) < jnp.repeat(counts[block_expert], MOE_BLK)
    slot_tok = jnp.where(slot_used, _first_column(_scatter_flat(tok_rows, dest, n_slots)).reshape(n_slots), 0)

    xs = _gather_rows(h, slot_tok)
    ys = _expert_call(block_expert, n_valid, xs, wg, wu, wd)
    y12 = _gather_rows(ys, dest)
    return _combine_call(x_flat, modt, ng, y12, rinfo)


def _rope_tables(n_ctx, length):
    rows = length // GRID_W
    row = jnp.repeat(jnp.arange(rows, dtype=F32), GRID_W)
    col = jnp.tile(jnp.arange(GRID_W, dtype=F32), rows)
    axis_dim = HEAD_DIM // 2
    inv_freq = ROPE_THETA ** (-jnp.arange(0, axis_dim, 2, dtype=F32) / axis_dim)
    ar = row[:, None] * inv_freq
    ac = col[:, None] * inv_freq
    cos = jnp.concatenate([jnp.cos(ar), jnp.cos(ar), jnp.cos(ac), jnp.cos(ac)], axis=1)
    sin = jnp.concatenate([-jnp.sin(ar), jnp.sin(ar), -jnp.sin(ac), jnp.sin(ac)], axis=1)
    cos = jnp.concatenate([jnp.ones((n_ctx, HEAD_DIM), F32), cos], axis=0)
    sin = jnp.concatenate([jnp.zeros((n_ctx, HEAD_DIM), F32), sin], axis=0)
    return jnp.tile(cos, (1, 2)), jnp.tile(sin, (1, 2))


def kernel(x, c, ctx, c_ctx, ada_w, ada_b, norm_g, w_in, w_out, attn_q_g, attn_k_g, conv_w, hgrn_lb, hgrn_g,
           ffn_w_gate, ffn_w_up, ffn_w_down, moe_router, moe_w_gate, moe_w_up, moe_w_down):
    bsz, length, d = x.shape
    n_ctx = ctx.shape[1]
    depth = w_in.shape[0]
    t = n_ctx + length
    nt = t // TM
    assert n_ctx == TM and length % TM == 0 and d == 8 * LANES
    for n_tok in (bsz * t, bsz * length):
        assert n_tok % FFN_TM == 0 and (2 * n_tok) % (32 * GATHER_WIN) == 0

    r_pad = -(-(bsz + 1) // 8) * 8
    c_all = jnp.concatenate([c, c_ctx[None, :], jnp.zeros((r_pad - bsz - 1, d), F32)], axis=0)
    mod_all = _ada_call(c_all, ada_w, ada_b).reshape(depth, r_pad, N_MOD, d)

    cos_t, sin_t = _rope_tables(n_ctx, length)
    lb_all = jnp.cumsum(jax.nn.softmax(hgrn_lb.astype(F32), axis=0), axis=0)
    lb_all = lb_all - lb_all[0]
    hconsts = _hgrn_consts()
    cw = 2 * LANES
    cidx = jnp.arange(cw)
    head_mean = ((cidx[:, None] // HEAD_DIM == cidx[None, :] // HEAD_DIM).astype(F32) / HEAD_DIM).astype(BF16)

    cos_f = jnp.tile(cos_t, (bsz, 1))
    sin_f = jnp.tile(sin_t, (bsz, 1))
    x_flat = jnp.concatenate([ctx, x], axis=1).reshape(bsz * t, d)
    for l in range(depth):
        mod_lat = mod_all[l, :bsz]
        mod_ctx = jnp.broadcast_to(mod_all[l, bsz][None], (bsz, N_MOD, d))
        mod = jnp.stack([mod_ctx, mod_lat], axis=1)
        modt = jnp.concatenate([mod[:, :1], jnp.broadcast_to(mod[:, 1:], (bsz, nt - 1, N_MOD, d))], axis=1)
        modt = modt.reshape(bsz * nt, N_MOD, d)
        ng = jnp.pad(norm_g[l], ((0, 4), (0, 0)))
        lb = lb_all[l].reshape(1, 2 * cw)
        gc = jnp.concatenate([jnp.log(lb), jnp.log1p(-lb), 1.0 - lb, jnp.zeros((5, 2 * cw), F32)], axis=0)
        qg = jnp.tile(attn_q_g[l], 2)[None, :]
        kg = jnp.tile(attn_k_g[l], 2)[None, :]

        q, k, vt, cbog, u, qv, kk, gg = _inproj_call(x_flat, modt, ng, w_in[l].astype(BF16), cos_f, sin_f, qg, kg, gc)
        attn = _attn_call(q, k, vt, bsz)
        o_f, o_b = _hgrn_call(qv, kk, gg, hconsts, bsz)
        mix = (attn, cbog, u, o_f, o_b, hgrn_g[l][None, :],
               jnp.pad(conv_w[l], ((0, 5), (0, 0))), head_mean, w_out[l].astype(BF16))
        if l % 2 == 0:
            weights = (ffn_w_gate[l // 2].astype(BF16), ffn_w_up[l // 2].astype(BF16), ffn_w_down[l // 2].astype(BF16))
        else:
            weights = (_router_weights(moe_router[l // 2]),)
            experts = (_cast_bf16(moe_w_gate, l // 2), _cast_bf16(moe_w_up, l // 2), _cast_bf16(moe_w_down, l // 2))
        mode = "ffn" if l % 2 == 0 else "router"

        if l < depth - 1:
            if mode == "ffn":
                x_flat = _mixer_call(x_flat, modt, ng, mix, mode, weights, nt)
            else:
                x_flat = _mixer_call(x_flat, modt, ng, mix, "plain", (), nt)
                h, rinfo, rinfo_t = _mixer_call(x_flat, modt, ng, None, mode, weights)
                x_flat = _moe_layer(x_flat, h, rinfo, rinfo_t, modt, ng, *experts)
            continue

        x_all = _mixer_call(x_flat, modt, ng, mix, "plain", (), nt).reshape(bsz, t, d)
        x_flat = x_all[:, n_ctx:, :].reshape(bsz * length, d)
        modt = jnp.broadcast_to(mod[:, 1:], (bsz, length // TM, N_MOD, d)).reshape(bsz * (length // TM), N_MOD, d)
        if mode == "ffn":
            x_flat = _mixer_call(x_flat, modt, ng, None, mode, weights)
        else:
            h, rinfo, rinfo_t = _mixer_call(x_flat, modt, ng, None, mode, weights)
            x_flat = _moe_layer(x_flat, h, rinfo, rinfo_t, modt, ng, *experts)
        return x_flat.reshape(bsz, length, d)
```

```python
import functools

import jax
import jax.numpy as jnp
from jax import lax
from jax.experimental import pallas as pl
from jax.experimental.pallas import tpu as pltpu
from jax.experimental.pallas import tpu_sc as plsc

F32 = jnp.float32
BF16 = jnp.bfloat16

HEAD_DIM = 64
GRID_W = 64
ROPE_THETA = 10000.0
ATTN_SCALE = HEAD_DIM ** -0.5
LOG2_E = 1.4426950408889634
HGRN_CHUNK = 64
HGRN_SUB = 8
VT_ROWS = HEAD_DIM + 16
ATTN_AHEAD = 5
N_EXPERTS = 8
N_MOD = 6
EPS = 1e-6

LANES = 128
TM = 256
FFN_TM = 512
FFN_TF = 512
N_MIX_REFS = 11
CAST_STREAMS = 4
MOE_BLK = 512
MOE_TF = 512
GATHER_WIN = 128
GATHER_SPLIT = 2
NEG_BIG = -1e30
VMEM_LIMIT = 56 * 1024 * 1024


def _cparams(*sem):
    return pltpu.CompilerParams(dimension_semantics=sem, vmem_limit_bytes=VMEM_LIMIT)


def _sigmoid(z):
    return 1.0 / (1.0 + jnp.exp(-z))


def _dot(a, b):
    return jnp.dot(a, b, preferred_element_type=F32)


def _dot_nt(a, b):
    return lax.dot_general(a, b, (((1,), (1,)), ((), ())), preferred_element_type=F32)


def _dot_tn(a, b):
    return lax.dot_general(a, b, (((0,), (0,)), ((), ())), preferred_element_type=F32)


def _rms(x, g):
    return x * lax.rsqrt(jnp.mean(x * x, axis=-1, keepdims=True) + EPS) * g


def _ada_kernel(c_ref, w_ref, b_ref, o_ref):
    c = c_ref[...]
    s = (c * _sigmoid(c)).astype(BF16)
    o_ref[...] = _dot(s, w_ref[...].astype(BF16)) + b_ref[...]


def _ada_call(c_all, ada_w, ada_b):
    depth, d, n = ada_w.shape
    r = c_all.shape[0]
    tn = 512
    return pl.pallas_call(
        _ada_kernel,
        grid=(depth, n // tn),
        in_specs=[
            pl.BlockSpec((r, d), lambda l, j: (0, 0)),
            pl.BlockSpec((None, d, tn), lambda l, j: (l, 0, j)),
            pl.BlockSpec((None, 1, tn), lambda l, j: (l, 0, j)),
        ],
        out_specs=pl.BlockSpec((None, r, tn), lambda l, j: (l, 0, j)),
        out_shape=jax.ShapeDtypeStruct((depth, r, n), F32),
        compiler_params=_cparams("parallel", "parallel"),
        name="ada",
    )(c_all, ada_w, ada_b.reshape(depth, 1, n))


def _inproj_kernel(x_ref, modt_ref, ng_ref, w_ref, cos_ref, sin_ref, qg_ref, kg_ref, gc_ref,
                   q_ref, k_ref, vt_ref, cbog_ref, u_ref, qv_ref, kk_ref, gg_ref):
    x = x_ref[...]
    rows = x.shape[0]
    h = _rms(x, ng_ref[0:1, :]) * (1.0 + _mod_rows(modt_ref, 1, rows)) + _mod_rows(modt_ref, 0, rows)
    hb = h.astype(BF16)

    def proj(c0, n):
        return _dot(hb, w_ref[:, c0:c0 + n])

    cos = cos_ref[...]
    sin = sin_ref[...]
    lane = lax.broadcasted_iota(jnp.int32, (rows, LANES), 1)
    first_half = (lane % 32) < 16
    lo = lane < HEAD_DIM

    def norm_rope(blk, g):
        b2 = blk * blk
        s_lo = jnp.sum(jnp.where(lo, b2, 0.0), axis=-1, keepdims=True)
        s_hi = jnp.sum(jnp.where(lo, 0.0, b2), axis=-1, keepdims=True)
        v = blk * lax.rsqrt(jnp.where(lo, s_lo, s_hi) * (1.0 / HEAD_DIM) + EPS) * g
        partner = jnp.where(first_half, pltpu.roll(v, LANES - 16, 1), pltpu.roll(v, 16, 1))
        return v * cos + partner * sin

    qa = proj(0, 4 * LANES)
    for j in range(4):
        qn = norm_rope(qa[:, LANES * j:LANES * (j + 1)], qg_ref[...]) * (ATTN_SCALE * LOG2_E)
        sw = pltpu.roll(qn, HEAD_DIM, 1)
        if j < 2:
            q_ref[2 * j] = jnp.where(lo, qn, 0.0).astype(BF16)
            q_ref[2 * j + 1] = jnp.where(lo, sw, 0.0).astype(BF16)
        else:
            q_ref[2 * j] = jnp.where(lo, 0.0, sw).astype(BF16)
            q_ref[2 * j + 1] = jnp.where(lo, 0.0, qn).astype(BF16)

    kv = proj(4 * LANES, 2 * LANES)
    k_ref[...] = norm_rope(kv[:, :LANES], kg_ref[...]).astype(BF16)
    vt = kv[:, LANES:].T
    ones = jnp.ones((VT_ROWS - HEAD_DIM, TM), F32)
    for s in range(rows // TM):
        for g in range(2):
            vt_ref[g, s] = jnp.concatenate(
                [vt[HEAD_DIM * g:HEAD_DIM * (g + 1), TM * s:TM * (s + 1)], ones], axis=0).astype(BF16)

    c0 = 6 * LANES
    c3 = proj(c0, 6 * LANES)
    cw = 2 * LANES
    u_ref[...] = c3[:, cw:2 * cw] * c3[:, 2 * cw:3 * cw]

    hg = proj(c0 + 6 * LANES, 10 * LANES)
    zf, zb, iv, hq, og = (hg[:, cw * i:cw * (i + 1)] for i in range(5))
    cbog_ref[:, :cw] = c3[:, :cw].astype(BF16)
    cbog_ref[:, cw:] = og.astype(BF16)
    qv_ref[:, :cw] = (hq * _sigmoid(hq)).astype(BF16)
    qv_ref[:, cw:] = iv.astype(BF16)
    for d, z in enumerate((zf, zb)):
        log_lb = gc_ref[0:1, cw * d:cw * (d + 1)]
        log1m_lb = gc_ref[1:2, cw * d:cw * (d + 1)]
        one_m_lb = gc_ref[2:3, cw * d:cw * (d + 1)]
        t = jnp.exp(-jnp.abs(z))
        log_sig = jnp.minimum(z, 0.0) - jnp.log(1.0 + t)
        a2 = log1m_lb + log_sig
        log_f = jnp.maximum(log_lb, a2) + jnp.log(1.0 + jnp.exp(-jnp.abs(log_lb - a2)))
        sig_neg = jnp.where(z >= 0, t, 1.0) / (1.0 + t)
        gg_ref[:, cw * d:cw * (d + 1)] = log_f
        kk_ref[:, cw * d:cw * (d + 1)] = (one_m_lb * sig_neg).astype(BF16)


def _inproj_call(x_flat, modt, ng, w1, cos_f, sin_f, qg, kg, gc):
    n, d = x_flat.shape
    nsub = FFN_TM // TM
    tok = lambda w: pl.BlockSpec((FFN_TM, w), lambda i: (i, 0))
    const = lambda a: pl.BlockSpec(a.shape, lambda i: (0,) * a.ndim, pipeline_mode=pl.Buffered(1))
    return pl.pallas_call(
        _inproj_kernel,
        grid=(n // FFN_TM,),
        in_specs=[
            tok(d),
            pl.BlockSpec((nsub, N_MOD, d), lambda i: (i, 0, 0)),
            const(ng), const(w1), tok(LANES), tok(LANES), const(qg), const(kg), const(gc),
        ],
        out_specs=[
            pl.BlockSpec((8, FFN_TM, LANES), lambda i: (0, i, 0)),
            tok(LANES),
            pl.BlockSpec((2, nsub, VT_ROWS, TM), lambda i: (0, i, 0, 0)),
            tok(4 * LANES),
            tok(2 * LANES),
            tok(4 * LANES),
            tok(4 * LANES),
            tok(4 * LANES),
        ],
        out_shape=[
            jax.ShapeDtypeStruct((8, n, LANES), BF16),
            jax.ShapeDtypeStruct((n, LANES), BF16),
            jax.ShapeDtypeStruct((2, n // TM, VT_ROWS, TM), BF16),
            jax.ShapeDtypeStruct((n, 4 * LANES), BF16),
            jax.ShapeDtypeStruct((n, 2 * LANES), F32),
            jax.ShapeDtypeStruct((n, 4 * LANES), BF16),
            jax.ShapeDtypeStruct((n, 4 * LANES), BF16),
            jax.ShapeDtypeStruct((n, 4 * LANES), F32),
        ],
        compiler_params=_cparams("parallel"),
        name="inproj",
    )(x_flat, modt, ng, w1, cos_f, sin_f, qg, kg, gc)


def _attn_kernel(q_ref, k_ref, vt_ref, o_ref, *, n_kv):
    def score(j, h):
        return _dot_nt(k_ref[TM * j:TM * (j + 1), :], q_ref[h])

    def attend(n_tiles):
        steps = [(j, h) for j in range(n_tiles) for h in range(8)]
        scores = [score(j, h) for j, h in steps[:ATTN_AHEAD]]
        m = [None] * 8
        acc = [None] * 8
        for i, (j, h) in enumerate(steps):
            if i + ATTN_AHEAD < len(steps):
                scores.append(score(*steps[i + ATTN_AHEAD]))
            s = scores.pop(0)
            m_tile = jnp.max(s, axis=0, keepdims=True)
            if j == 0:
                m_new = m_tile
                acc[h] = _dot(vt_ref[h // 4, j], jnp.exp2(s - m_new).astype(BF16))
            else:
                m_new = jnp.maximum(m[h], m_tile)
                pv = _dot(vt_ref[h // 4, j], jnp.exp2(s - m_new).astype(BF16))
                acc[h] = jnp.exp2(m[h] - m_new) * acc[h] + pv
            m[h] = m_new
        outs = [acc[h][:HEAD_DIM, :] / acc[h][HEAD_DIM:HEAD_DIM + 1, :] for h in range(8)]
        o_ref[...] = jnp.concatenate(outs, axis=0).T.astype(BF16)

    @pl.when(pl.program_id(1) == 0)
    def _():
        attend(1)

    @pl.when(pl.program_id(1) > 0)
    def _():
        attend(n_kv)


def _attn_call(q, k, vt, b):
    n = k.shape[0]
    t = n // b
    nt = t // TM
    return pl.pallas_call(
        functools.partial(_attn_kernel, n_kv=nt),
        grid=(b, nt),
        in_specs=[
            pl.BlockSpec((8, TM, LANES), lambda i, j: (0, i * nt + j, 0)),
            pl.BlockSpec((t, LANES), lambda i, j: (i, 0)),
            pl.BlockSpec((2, nt, VT_ROWS, TM), lambda i, j: (0, i, 0, 0)),
        ],
        out_specs=pl.BlockSpec((TM, 4 * LANES), lambda i, j: (i * nt + j, 0)),
        out_shape=jax.ShapeDtypeStruct((n, 4 * LANES), BF16),
        compiler_params=_cparams("parallel", "parallel"),
        name="attn",
    )(q, k, vt)


def _split3(g):
    hi = g.astype(BF16)
    r1 = g - hi.astype(F32)
    mid = r1.astype(BF16)
    lo = (r1 - mid.astype(F32)).astype(BF16)
    return hi, mid, lo


def _hgrn_kernel(qvf_ref, kf_ref, gf_ref, qvb_ref, kb_ref, gb_ref, tri_ref, trij_ref, j_ref, e_ref, bd_ref,
                 md_ref, mo_ref, of_ref, ob_ref, st_ref, x_s):
    hw = 2 * LANES
    nch = TM // HGRN_CHUNK

    @pl.when(pl.program_id(1) == 0)
    def _():
        st_ref[...] = jnp.zeros(st_ref.shape, F32)

    flip = j_ref[...]

    def cumsum(m_ref, g):
        m = m_ref[...]
        hi, mid, lo = _split3(g)
        return _dot(m, hi) + _dot(m, mid) + _dot(m, lo)

    qvf = qvf_ref[...]
    fl = _dot(flip, jnp.concatenate([qvb_ref[...], kb_ref[...]], axis=1))
    q = jnp.concatenate([qvf[:, :hw].astype(F32), fl[:, :hw]], axis=0)
    v = jnp.concatenate([qvf[:, hw:].astype(F32), fl[:, hw:2 * hw]], axis=0)
    k = jnp.concatenate([kf_ref[...].astype(F32), fl[:, 2 * hw:]], axis=0)
    b = jnp.concatenate([cumsum(tri_ref, gf_ref[...]), cumsum(trij_ref, gb_ref[...])], axis=0) * LOG2_E

    ngr = 2 * TM // HGRN_SUB
    nsub = HGRN_CHUNK // HGRN_SUB
    q3 = q.reshape(ngr, HGRN_SUB, hw)
    b3 = b.reshape(ngr, HGRN_SUB, hw)
    k3 = k.reshape(ngr, HGRN_SUB, hw)
    for s in range(HGRN_SUB):
        x = q3 * jnp.exp2(jnp.minimum(b3 - b3[:, s:s + 1, :], 0.0)) * k3[:, s:s + 1, :]
        x_s[:, hw * s:hw * (s + 1)] = x.reshape(2 * TM, hw).astype(BF16)
    a = _dot(x_s[...], e_ref[...])

    bd = bd_ref[...]
    bd16 = bd.astype(BF16)
    md = md_ref[...]
    mo = mo_ref[...]
    for d in range(2):
        st = st_ref[d]
        outs = []
        for c in range(nch):
            r0 = TM * d + HGRN_CHUNK * c
            bc = b[r0:r0 + HGRN_CHUNK]
            qc = q[r0:r0 + HGRN_CHUNK]
            kc = k[r0:r0 + HGRN_CHUNK]
            vc = v[r0:r0 + HGRN_CHUNK]
            ac = a[r0:r0 + HGRN_CHUNK] * md
            r_end = [bc[HGRN_SUB * (j + 1) - 1:HGRN_SUB * (j + 1), :] for j in range(nsub)]
            r_own = jnp.concatenate([jnp.broadcast_to(r, (HGRN_SUB, hw)) for r in r_end], axis=0)
            kt_bd = jnp.concatenate([(kc * jnp.exp2(r_own - bc)).astype(BF16)] * 4, axis=0) * bd16
            q_rel = jnp.concatenate(
                [qc * jnp.exp2(jnp.minimum(bc - r_end[j], 0.0)) for j in range(nsub - 1)], axis=0).astype(BF16)
            p = _dot_nt(q_rel, kt_bd) * mo
            for j in range(nsub - 1):
                ac = ac + p[HGRN_CHUNK * j:HGRN_CHUNK * (j + 1)]
            bl = r_end[-1]
            qe = (qc * jnp.exp2(bc)).astype(BF16)
            ke = (kc * jnp.exp2(bl - bc)).astype(BF16)
            vcb = vc.astype(BF16)
            vbd = jnp.concatenate([vcb] * 4, axis=0) * bd16
            outs.append(_dot(ac.astype(BF16), vbd) + _dot_nt(qe, st.astype(BF16)))
            st = st * jnp.exp2(bl) + _dot_tn(vcb, ke) * bd
        st_ref[d] = st
        od = jnp.concatenate(outs, axis=0).astype(BF16)
        if d == 0:
            of_ref[...] = od
        else:
            ob_ref[...] = _dot(flip, od).astype(BF16)


def _hgrn_consts():
    hw = 2 * LANES
    r = jnp.arange(TM)
    same_chunk = (r[:, None] // HGRN_CHUNK) == (r[None, :] // HGRN_CHUNK)
    tri = (same_chunk & (r[None, :] <= r[:, None])).astype(F32)
    flip = (r[:, None] + r[None, :] == TM - 1).astype(F32)
    trij = tri @ flip
    c = jnp.arange(hw)
    s = jnp.arange(HGRN_SUB)
    e = ((c[None, :, None] // HEAD_DIM == c[None, None, :] // HGRN_CHUNK)
         & (c[None, None, :] % HGRN_SUB == s[:, None, None])).astype(BF16).reshape(HGRN_SUB * hw, hw)
    bd = (c[:, None] // HEAD_DIM == c[None, :] // HEAD_DIM).astype(F32)
    tt = jnp.arange(HGRN_CHUNK)[:, None]
    ss = (c % HGRN_CHUNK)[None, :]
    md = ((ss // HGRN_SUB == tt // HGRN_SUB) & (ss <= tt)).astype(F32)
    mo = jnp.concatenate([((ss // HGRN_SUB == j) & (tt // HGRN_SUB > j)).astype(F32)
                          for j in range(HGRN_CHUNK // HGRN_SUB - 1)], axis=0)
    return tri.astype(BF16), trij.astype(BF16), flip.astype(BF16), e, bd, md, mo


def _hgrn_call(qv, kk, gg, consts, b):
    n = qv.shape[0]
    nt = n // b // TM
    hw = 2 * LANES
    tri, trij, flip, e, bd, md, mo = consts
    fwd = lambda i, j: i * nt + j
    bwd = lambda i, j: i * nt + jnp.where(j == 0, 0, nt - j)
    const2 = lambda a: pl.BlockSpec(a.shape, lambda i, j: (0,) * a.ndim)
    return pl.pallas_call(
        _hgrn_kernel,
        grid=(b, nt),
        in_specs=[
            pl.BlockSpec((TM, 2 * hw), lambda i, j: (fwd(i, j), 0)),
            pl.BlockSpec((TM, hw), lambda i, j: (fwd(i, j), 0)),
            pl.BlockSpec((TM, hw), lambda i, j: (fwd(i, j), 0)),
            pl.BlockSpec((TM, 2 * hw), lambda i, j: (bwd(i, j), 0)),
            pl.BlockSpec((TM, hw), lambda i, j: (bwd(i, j), 1)),
            pl.BlockSpec((TM, hw), lambda i, j: (bwd(i, j), 1)),
            const2(tri), const2(trij), const2(flip), const2(e), const2(bd), const2(md), const2(mo),
        ],
        out_specs=[
            pl.BlockSpec((TM, hw), lambda i, j: (fwd(i, j), 0)),
            pl.BlockSpec((TM, hw), lambda i, j: (bwd(i, j), 0)),
        ],
        out_shape=[jax.ShapeDtypeStruct((n, hw), BF16)] * 2,
        scratch_shapes=[
            pltpu.VMEM((2, hw, hw), F32),
            pltpu.VMEM((2 * TM, HGRN_SUB * hw), BF16),
        ],
        compiler_params=_cparams("parallel", "arbitrary"),
        name="hgrn",
    )(qv, kk, gg, qv, kk, gg, tri, trij, flip, e, bd, md, mo)


def _mix_out(x_ref, modt_ref, ng_ref, attn_ref, cbog_ref, u_ref, up_ref, un_ref, of_ref, ob_ref,
             hgg_ref, cw_ref, hn_ref, w_ref, *, n_tiles):
    rows = x_ref.shape[0]
    nsub = rows // TM
    cw = 2 * LANES
    u = u_ref[...]
    row = lax.broadcasted_iota(jnp.int32, (rows, cw), 0)
    u_prev = jnp.where(row == 0, up_ref[7:8, :], pltpu.roll(u, 1, 0))
    u_next = jnp.where(row == rows - 1, un_ref[0:1, :], pltpu.roll(u, rows - 1, 0))
    for s in range(nsub):
        t = (pl.program_id(0) * nsub + s) % n_tiles
        u_prev = jnp.where(t >= 2, u_prev, jnp.where(row == s * TM, 0.0, u_prev))
        u_next = jnp.where((t >= 1) & (t < n_tiles - 1), u_next, jnp.where(row == s * TM + TM - 1, 0.0, u_next))
    cbog = cbog_ref[...].astype(F32)
    conv = cbog[:, :cw] * (cw_ref[0:1, :] * u_prev + cw_ref[1:2, :] * u + cw_ref[2:3, :] * u_next)

    o = of_ref[...].astype(F32) + ob_ref[...].astype(F32)
    ms = _dot((o * o).astype(BF16), hn_ref[...])
    og = cbog[:, cw:]
    hg = o * lax.rsqrt(ms + EPS) * hgg_ref[...] * (og * _sigmoid(og))

    y = (_dot(attn_ref[...], w_ref[0:2 * cw, :]) + _dot(conv.astype(BF16), w_ref[2 * cw:3 * cw, :])
         + _dot(hg.astype(BF16), w_ref[3 * cw:4 * cw, :]))
    return x_ref[...] + _mod_rows(modt_ref, 2, rows) * _rms(y, ng_ref[1:2, :])


def _mod_rows(modt_ref, i, rows):
    sub = rows // modt_ref.shape[0]
    return jnp.concatenate(
        [jnp.broadcast_to(modt_ref[s, i:i + 1, :], (sub, modt_ref.shape[2])) for s in range(modt_ref.shape[0])],
        axis=0)


def _store_split(ref, val):
    w = ref.shape[2]
    for c in range(ref.shape[0]):
        lo = val[:, 2 * w * c:2 * w * c + w].astype(BF16).astype(F32)
        hi = val[:, 2 * w * c + w:2 * w * (c + 1)].astype(BF16).astype(F32)
        ref[c] = lax.bitcast_convert_type(hi, jnp.uint32) | (lax.bitcast_convert_type(lo, jnp.uint32) >> 16)


def _load_split(ref):
    cols = []
    for c in range(ref.shape[0]):
        word = ref[c]
        cols.append(lax.bitcast_convert_type(word << 16, F32))
        cols.append(lax.bitcast_convert_type(word & jnp.uint32(0xFFFF0000), F32))
    return jnp.concatenate(cols, axis=1)


def _split_shape(n, d):
    return jax.ShapeDtypeStruct((GATHER_SPLIT, n, d // GATHER_SPLIT // 2), jnp.uint32)


def _split_spec(rows):
    return lambda d: pl.BlockSpec((GATHER_SPLIT, rows, d // GATHER_SPLIT // 2), lambda i, *_: (0, i, 0))


def _ffn_body(x, modt_ref, ng_ref, wg_ref, wu_ref, wd_ref):
    rows = x.shape[0]
    h = (_rms(x, ng_ref[2:3, :]) * (1.0 + _mod_rows(modt_ref, 4, rows)) + _mod_rows(modt_ref, 3, rows)).astype(BF16)
    acc = jnp.zeros(x.shape, F32)
    dff = wg_ref.shape[1]
    for f0 in range(0, dff, FFN_TF):
        f1 = min(f0 + FFN_TF, dff)
        g = _dot(h, wg_ref[:, f0:f1])
        up = _dot(h, wu_ref[:, f0:f1])
        acc = acc + _dot((g * _sigmoid(g) * up).astype(BF16), wd_ref[f0:f1, :])
    return x + _mod_rows(modt_ref, 5, rows) * _rms(acc, ng_ref[3:4, :])


def _router_body(x, modt_ref, ng_ref, wr_ref, h_ref, r_ref, rt_ref):
    rows = x.shape[0]
    h = _rms(x, ng_ref[2:3, :]) * (1.0 + _mod_rows(modt_ref, 4, rows)) + _mod_rows(modt_ref, 3, rows)
    _store_split(h_ref, h)
    h_hi = h.astype(BF16)
    h_lo = (h - h_hi.astype(F32)).astype(BF16)
    logits = _dot(h_hi, wr_ref[0]) + _dot(h_lo, wr_ref[0]) + _dot(h_hi, wr_ref[1])
    lane = lax.broadcasted_iota(jnp.int32, logits.shape, 1)
    lg = jnp.where(lane < N_EXPERTS, logits, NEG_BIG)
    m1 = jnp.max(lg, axis=-1, keepdims=True)
    i1 = jnp.min(jnp.where(lg == m1, lane, LANES), axis=-1, keepdims=True)
    lg2 = jnp.where(lane == i1, NEG_BIG, lg)
    m2 = jnp.max(lg2, axis=-1, keepdims=True)
    i2 = jnp.min(jnp.where(lg2 == m2, lane, LANES), axis=-1, keepdims=True)
    e2 = jnp.exp(m2 - m1)
    w1 = 1.0 / (1.0 + e2)
    w2 = e2 / (1.0 + e2)
    r = jnp.where(lane == 0, i1.astype(F32),
                  jnp.where(lane == 1, i2.astype(F32), jnp.where(lane == 2, w1, jnp.where(lane == 3, w2, 0.0))))
    r_ref[...] = r
    rt_ref[...] = r.T[0:8, :]


def _mixer_kernel(*refs, fused, mode, n_tiles):
    x_ref, modt_ref, ng_ref = refs[:3]
    pos = 3
    if fused:
        x = _mix_out(x_ref, modt_ref, ng_ref, *refs[pos:pos + N_MIX_REFS], n_tiles=n_tiles)
        pos += N_MIX_REFS
    else:
        x = x_ref[...]
    if mode == "ffn":
        wg_ref, wu_ref, wd_ref, o_ref = refs[pos:]
        o_ref[...] = _ffn_body(x, modt_ref, ng_ref, wg_ref, wu_ref, wd_ref)
    elif mode == "router":
        wr_ref, *outs = refs[pos:]
        _router_body(x, modt_ref, ng_ref, wr_ref, *outs)
    else:
        refs[pos][...] = x


def _mixer_call(x_flat, modt, ng, mix, mode, weights, n_tiles=0):
    n, d = x_flat.shape
    nsub = FFN_TM // TM
    cw = 2 * LANES
    tok = lambda w: pl.BlockSpec((FFN_TM, w), lambda i: (i, 0))
    resident = lambda a: pl.BlockSpec(a.shape, lambda i: (0,) * a.ndim, pipeline_mode=pl.Buffered(1))
    stream = jax.ShapeDtypeStruct((n, d), F32)
    args = [x_flat, modt, ng]
    in_specs = [tok(d), pl.BlockSpec((nsub, N_MOD, d), lambda i: (i, 0, 0)), pl.BlockSpec((8, d), lambda i: (0, 0))]
    fused = mix is not None
    if fused:
        attn, cbog, u, o_f, o_b, hgg, conv_w, hn, w_out = mix
        nb8 = n // 8
        args += [attn, cbog, u, u, u, o_f, o_b, hgg, conv_w, hn, w_out]
        in_specs += [
            tok(2 * cw), tok(2 * cw), tok(cw),
            pl.BlockSpec((8, cw), lambda i: (jnp.maximum(i * (FFN_TM // 8) - 1, 0), 0)),
            pl.BlockSpec((8, cw), lambda i: (jnp.minimum((i + 1) * (FFN_TM // 8), nb8 - 1), 0)),
            tok(cw), tok(cw), resident(hgg), resident(conv_w), resident(hn), resident(w_out),
        ]
        assert len(in_specs) == 3 + N_MIX_REFS
    args += list(weights)
    in_specs += [resident(w) for w in weights]
    if mode == "router":
        out_specs = [_split_spec(FFN_TM)(d), tok(LANES), pl.BlockSpec((8, FFN_TM), lambda i: (0, i))]
        out_shape = [_split_shape(n, d),
                     jax.ShapeDtypeStruct((n, LANES), F32), jax.ShapeDtypeStruct((8, n), F32)]
    else:
        out_specs, out_shape = tok(d), stream
    return pl.pallas_call(
        functools.partial(_mixer_kernel, fused=fused, mode=mode, n_tiles=n_tiles),
        grid=(n // FFN_TM,),
        in_specs=in_specs,
        out_specs=out_specs,
        out_shape=out_shape,
        compiler_params=_cparams("parallel"),
        name=("mix_" if fused else "") + mode,
    )(*args)


def _gather_flat(data, idx):
    m = idx.shape[0]
    w = data.shape[1]
    mesh = plsc.VectorSubcoreMesh(core_axis_name="core", subcore_axis_name="subcore")

    @functools.partial(pl.kernel, out_type=jax.ShapeDtypeStruct((m, w), data.dtype), mesh=mesh)
    def gather(x_hbm, i_hbm, o_hbm):
        def body(i_vmem, o_vmem):
            pltpu.sync_copy(x_hbm.at[i_vmem.at[0]], o_vmem)

        pltpu.emit_pipeline(
            body,
            grid=(m // GATHER_WIN,),
            in_specs=[pl.BlockSpec((1, GATHER_WIN), lambda i: (0, i))],
            out_specs=[pl.BlockSpec((GATHER_WIN, w), lambda i: (i, 0))],
            core_axis_name=("core", "subcore"),
            dimension_semantics=(pltpu.PARALLEL,),
        )(i_hbm, o_hbm)

    return gather(data, idx.reshape(1, m))


def _scatter_flat(rows, idx, n_out):
    m, w = rows.shape
    mesh = plsc.VectorSubcoreMesh(core_axis_name="core", subcore_axis_name="subcore")

    @functools.partial(pl.kernel, out_type=jax.ShapeDtypeStruct((n_out, w), rows.dtype), mesh=mesh)
    def scatter(x_hbm, i_hbm, o_hbm):
        def body(x_vmem, i_vmem):
            pltpu.sync_copy(x_vmem, o_hbm.at[i_vmem.at[0]])

        pltpu.emit_pipeline(
            body,
            grid=(m // GATHER_WIN,),
            in_specs=[pl.BlockSpec((GATHER_WIN, w), lambda i: (i, 0)),
                      pl.BlockSpec((1, GATHER_WIN), lambda i: (0, i))],
            out_specs=[],
            core_axis_name=("core", "subcore"),
            dimension_semantics=(pltpu.PARALLEL,),
        )(x_hbm, i_hbm)

    return scatter(rows, idx.reshape(1, m))


def _col0_kernel(x_ref, o_ref):
    o_ref[...] = x_ref[...].T[0:1, :]


def _first_column(rows):
    n = rows.shape[0]
    blk = 1024
    return pl.pallas_call(
        _col0_kernel,
        grid=(n // blk,),
        in_specs=[pl.BlockSpec((blk, LANES), lambda i: (i, 0))],
        out_specs=pl.BlockSpec((1, blk), lambda i: (0, i)),
        out_shape=jax.ShapeDtypeStruct((1, n), rows.dtype),
        compiler_params=_cparams("parallel"),
        name="col0",
    )(rows)


def _cast_kernel(*refs):
    o_ref = refs[-1]
    cw = refs[0].shape[1]
    for s, x_ref in enumerate(refs[:-1]):
        o_ref[:, cw * s:cw * (s + 1)] = x_ref[...].astype(o_ref.dtype)


def _cast_bf16(w, layer):
    _, e, r, c = w.shape
    tr = 256
    cw = c // CAST_STREAMS
    chunk = lambda s: pl.BlockSpec((None, None, tr, cw), lambda i, j: (layer, i, j, s))
    return pl.pallas_call(
        _cast_kernel,
        grid=(e, r // tr),
        in_specs=[chunk(s) for s in range(CAST_STREAMS)],
        out_specs=pl.BlockSpec((None, tr, c), lambda i, j: (i, j, 0)),
        out_shape=jax.ShapeDtypeStruct((e, r, c), BF16),
        compiler_params=_cparams("parallel", "parallel"),
        name="cast",
    )(*([w] * CAST_STREAMS))


def _gather_rows(data, idx):
    s, r, w = data.shape
    m = idx.shape[0]
    idx_all = (idx[None, :] + (jnp.arange(s, dtype=jnp.int32) * r)[:, None]).reshape(s * m)
    return _gather_flat(data.reshape(s * r, w), idx_all).reshape(s, m, w)


def _expert_kernel(be_ref, nv_ref, xs_ref, wg_ref, wu_ref, wd_ref, o_ref):
    valid = pl.program_id(0) < nv_ref[0]

    @pl.when(valid)
    def _():
        xb = _load_split(xs_ref).astype(BF16)
        acc = jnp.zeros((xb.shape[0], wd_ref.shape[1]), F32)
        for f0 in range(0, wg_ref.shape[1], MOE_TF):
            g = _dot(xb, wg_ref[:, f0:f0 + MOE_TF])
            up = _dot(xb, wu_ref[:, f0:f0 + MOE_TF])
            acc = acc + _dot((g * _sigmoid(g) * up).astype(BF16), wd_ref[f0:f0 + MOE_TF, :])
        _store_split(o_ref, acc)

    @pl.when(jnp.logical_not(valid))
    def _():
        o_ref[...] = jnp.zeros(o_ref.shape, o_ref.dtype)


def _expert_call(block_expert, n_valid, xs, wg, wu, wd):
    n_slots = xs.shape[1]
    d = wg.shape[1]
    dff = wg.shape[2]
    nb = n_slots // MOE_BLK
    resident = lambda r, c: pl.BlockSpec((None, r, c), lambda i, be, nv: (be[i], 0, 0), pipeline_mode=pl.Buffered(1))
    grid_spec = pltpu.PrefetchScalarGridSpec(
        num_scalar_prefetch=2,
        grid=(nb,),
        in_specs=[_split_spec(MOE_BLK)(d), resident(d, dff), resident(d, dff), resident(dff, d)],
        out_specs=_split_spec(MOE_BLK)(d),
    )
    return pl.pallas_call(
        _expert_kernel,
        grid_spec=grid_spec,
        out_shape=_split_shape(n_slots, d),
        compiler_params=_cparams("arbitrary"),
        name="experts",
    )(block_expert, n_valid, xs, wg, wu, wd)


def _combine_kernel(x_ref, modt_ref, ng_ref, y1_ref, y2_ref, r_ref, o_ref):
    x = x_ref[...]
    rows = x.shape[0]
    f = r_ref[:, 2:3] * _load_split(y1_ref) + r_ref[:, 3:4] * _load_split(y2_ref)
    o_ref[...] = x + _mod_rows(modt_ref, 5, rows) * _rms(f, ng_ref[3:4, :])


def _combine_call(x_flat, modt, ng, y12, rinfo):
    n, d = x_flat.shape
    nsub = FFN_TM // TM
    nblk = n // FFN_TM
    tok = lambda w: pl.BlockSpec((FFN_TM, w), lambda i: (i, 0))
    second = pl.BlockSpec((GATHER_SPLIT, FFN_TM, d // GATHER_SPLIT // 2), lambda i: (0, i + nblk, 0))
    return pl.pallas_call(
        _combine_kernel,
        grid=(nblk,),
        in_specs=[tok(d), pl.BlockSpec((nsub, N_MOD, d), lambda i: (i, 0, 0)), pl.BlockSpec((8, d), lambda i: (0, 0)),
                  _split_spec(FFN_TM)(d), second, tok(LANES)],
        out_specs=tok(d),
        out_shape=jax.ShapeDtypeStruct((n, d), F32),
        compiler_params=_cparams("parallel"),
        name="combine",
    )(x_flat, modt, ng, y12, y12, rinfo)


def _router_weights(w_router):
    wr = jnp.pad(w_router, ((0, 0), (0, LANES - N_EXPERTS)))
    wr_hi = wr.astype(BF16)
    wr_lo = (wr - wr_hi.astype(F32)).astype(BF16)
    return jnp.stack([wr_hi, wr_lo])


def _moe_layer(x_flat, h, rinfo, rinfo_t, modt, ng, wg, wu, wd):
    n, d = x_flat.shape
    n_assign = 2 * n
    expert = jnp.concatenate([rinfo_t[0], rinfo_t[1]]).astype(jnp.int32)
    onehot = (expert[None, :] == jnp.arange(N_EXPERTS, dtype=jnp.int32)[:, None]).astype(jnp.int32)
    csum = jnp.cumsum(onehot, axis=1)
    rank = jnp.sum((csum - onehot) * onehot, axis=0)
    counts = csum[:, -1]
    padded = (counts + MOE_BLK - 1) // MOE_BLK * MOE_BLK
    pad_end = jnp.cumsum(padded)
    pad_start = pad_end - padded
    dest = jnp.sum(onehot * pad_start[:, None], axis=0) + rank
    nb = -(-n_assign // MOE_BLK) + N_EXPERTS
    row_quantum = 32 * GATHER_WIN // GATHER_SPLIT
    nb = -(-nb * MOE_BLK // row_quantum) * row_quantum // MOE_BLK
    n_slots = nb * MOE_BLK
    block_expert = jnp.minimum(
        jnp.searchsorted(pad_end, jnp.arange(nb, dtype=jnp.int32) * MOE_BLK, side="right"), N_EXPERTS - 1
    ).astype(jnp.int32)
    n_valid = (pad_end[-1:] // MOE_BLK).astype(jnp.int32)
    tok_rows = jnp.broadcast_to((jnp.arange(n_assign, dtype=jnp.int32) % n)[:, None], (n_assign, LANES))
    slot = jnp.arange(n_slots, dtype=jnp.int32)
    slot_used = (slot - jnp.repeat(pad_start[block_expert], MOE_BLK)) < jnp.repeat(counts[block_expert], MOE_BLK)
    slot_tok = jnp.where(slot_used, _first_column(_scatter_flat(tok_rows, dest, n_slots)).reshape(n_slots), 0)

    xs = _gather_rows(h, slot_tok)
    ys = _expert_call(block_expert, n_valid, xs, wg, wu, wd)
    y12 = _gather_rows(ys, dest)
    return _combine_call(x_flat, modt, ng, y12, rinfo)


def _rope_tables(n_ctx, length):
    rows = length // GRID_W
    row = jnp.repeat(jnp.arange(rows, dtype=F32), GRID_W)
    col = jnp.tile(jnp.arange(GRID_W, dtype=F32), rows)
    axis_dim = HEAD_DIM // 2
    inv_freq = ROPE_THETA ** (-jnp.arange(0, axis_dim, 2, dtype=F32) / axis_dim)
    ar = row[:, None] * inv_freq
    ac = col[:, None] * inv_freq
    cos = jnp.concatenate([jnp.cos(ar), jnp.cos(ar), jnp.cos(ac), jnp.cos(ac)], axis=1)
    sin = jnp.concatenate([-jnp.sin(ar), jnp.sin(ar), -jnp.sin(ac), jnp.sin(ac)], axis=1)
    cos = jnp.concatenate([jnp.ones((n_ctx, HEAD_DIM), F32), cos], axis=0)
    sin = jnp.concatenate([jnp.zeros((n_ctx, HEAD_DIM), F32), sin], axis=0)
    return jnp.tile(cos, (1, 2)), jnp.tile(sin, (1, 2))


def kernel(x, c, ctx, c_ctx, ada_w, ada_b, norm_g, w_in, w_out, attn_q_g, attn_k_g, conv_w, hgrn_lb, hgrn_g,
           ffn_w_gate, ffn_w_up, ffn_w_down, moe_router, moe_w_gate, moe_w_up, moe_w_down):
    bsz, length, d = x.shape
    n_ctx = ctx.shape[1]
    depth = w_in.shape[0]
    t = n_ctx + length
    nt = t // TM
    assert n_ctx == TM and length % TM == 0 and d == 8 * LANES
    for n_tok in (bsz * t, bsz * length):
        assert n_tok % FFN_TM == 0 and (2 * n_tok) % (32 * GATHER_WIN) == 0

    r_pad = -(-(bsz + 1) // 8) * 8
    c_all = jnp.concatenate([c, c_ctx[None, :], jnp.zeros((r_pad - bsz - 1, d), F32)], axis=0)
    mod_all = _ada_call(c_all, ada_w, ada_b).reshape(depth, r_pad, N_MOD, d)

    cos_t, sin_t = _rope_tables(n_ctx, length)
    lb_all = jnp.cumsum(jax.nn.softmax(hgrn_lb.astype(F32), axis=0), axis=0)
    lb_all = lb_all - lb_all[0]
    hconsts = _hgrn_consts()
    cw = 2 * LANES
    cidx = jnp.arange(cw)
    head_mean = ((cidx[:, None] // HEAD_DIM == cidx[None, :] // HEAD_DIM).astype(F32) / HEAD_DIM).astype(BF16)

    cos_f = jnp.tile(cos_t, (bsz, 1))
    sin_f = jnp.tile(sin_t, (bsz, 1))
    x_flat = jnp.concatenate([ctx, x], axis=1).reshape(bsz * t, d)
    for l in range(depth):
        mod_lat = mod_all[l, :bsz]
        mod_ctx = jnp.broadcast_to(mod_all[l, bsz][None], (bsz, N_MOD, d))
        mod = jnp.stack([mod_ctx, mod_lat], axis=1)
        modt = jnp.concatenate([mod[:, :1], jnp.broadcast_to(mod[:, 1:], (bsz, nt - 1, N_MOD, d))], axis=1)
        modt = modt.reshape(bsz * nt, N_MOD, d)
        ng = jnp.pad(norm_g[l], ((0, 4), (0, 0)))
        lb = lb_all[l].reshape(1, 2 * cw)
        gc = jnp.concatenate([jnp.log(lb), jnp.log1p(-lb), 1.0 - lb, jnp.zeros((5, 2 * cw), F32)], axis=0)
        qg = jnp.tile(attn_q_g[l], 2)[None, :]
        kg = jnp.tile(attn_k_g[l], 2)[None, :]

        q, k, vt, cbog, u, qv, kk, gg = _inproj_call(x_flat, modt, ng, w_in[l].astype(BF16), cos_f, sin_f, qg, kg, gc)
        attn = _attn_call(q, k, vt, bsz)
        o_f, o_b = _hgrn_call(qv, kk, gg, hconsts, bsz)
        mix = (attn, cbog, u, o_f, o_b, hgrn_g[l][None, :],
               jnp.pad(conv_w[l], ((0, 5), (0, 0))), head_mean, w_out[l].astype(BF16))
        if l % 2 == 0:
            weights = (ffn_w_gate[l // 2].astype(BF16), ffn_w_up[l // 2].astype(BF16), ffn_w_down[l // 2].astype(BF16))
        else:
            weights = (_router_weights(moe_router[l // 2]),)
            experts = (_cast_bf16(moe_w_gate, l // 2), _cast_bf16(moe_w_up, l // 2), _cast_bf16(moe_w_down, l // 2))
        mode = "ffn" if l % 2 == 0 else "router"

        if l < depth - 1:
            if mode == "ffn":
                x_flat = _mixer_call(x_flat, modt, ng, mix, mode, weights, nt)
            else:
                x_flat = _mixer_call(x_flat, modt, ng, mix, "plain", (), nt)
                h, rinfo, rinfo_t = _mixer_call(x_flat, modt, ng, None, mode, weights)
                x_flat = _moe_layer(x_flat, h, rinfo, rinfo_t, modt, ng, *experts)
            continue

        x_all = _mixer_call(x_flat, modt, ng, mix, "plain", (), nt).reshape(bsz, t, d)
        x_flat = x_all[:, n_ctx:, :].reshape(bsz * length, d)
        modt = jnp.broadcast_to(mod[:, 1:], (bsz, length // TM, N_MOD, d)).reshape(bsz * (length // TM), N_MOD, d)
        if mode == "ffn":
            x_flat = _mixer_call(x_flat, modt, ng, None, mode, weights)
        else:
            h, rinfo, rinfo_t = _mixer_call(x_flat, modt, ng, None, mode, weights)
            x_flat = _moe_layer(x_flat, h, rinfo, rinfo_t, modt, ng, *experts)
        return x_flat.reshape(bsz, length, d)
```

```python
import functools

import jax
import jax.numpy as jnp
from jax import lax
from jax.experimental import pallas as pl
from jax.experimental.pallas import tpu as pltpu
from jax.experimental.pallas import tpu_sc as plsc

F32 = jnp.float32
BF16 = jnp.bfloat16

HEAD_DIM = 64
GRID_W = 64
ROPE_THETA = 10000.0
ATTN_SCALE = HEAD_DIM ** -0.5
LOG2_E = 1.4426950408889634
HGRN_CHUNK = 64
HGRN_SUB = 8
VT_ROWS = HEAD_DIM + 16
ATTN_AHEAD = 7
N_EXPERTS = 8
N_MOD = 6
EPS = 1e-6

LANES = 128
TM = 256
FFN_TM = 512
FFN_TF = 512
N_MIX_REFS = 11
CAST_STREAMS = 4
MOE_BLK = 512
MOE_TF = 512
GATHER_WIN = 128
GATHER_SPLIT = 2
NEG_BIG = -1e30
VMEM_LIMIT = 56 * 1024 * 1024


def _cparams(*sem):
    return pltpu.CompilerParams(dimension_semantics=sem, vmem_limit_bytes=VMEM_LIMIT)


def _sigmoid(z):
    return 1.0 / (1.0 + jnp.exp(-z))


def _dot(a, b):
    return jnp.dot(a, b, preferred_element_type=F32)


def _dot_nt(a, b):
    return lax.dot_general(a, b, (((1,), (1,)), ((), ())), preferred_element_type=F32)


def _dot_tn(a, b):
    return lax.dot_general(a, b, (((0,), (0,)), ((), ())), preferred_element_type=F32)


def _rms(x, g):
    return x * lax.rsqrt(jnp.mean(x * x, axis=-1, keepdims=True) + EPS) * g


def _ada_kernel(c_ref, w_ref, b_ref, o_ref):
    c = c_ref[...]
    s = (c * _sigmoid(c)).astype(BF16)
    o_ref[...] = _dot(s, w_ref[...].astype(BF16)) + b_ref[...]


def _ada_call(c_all, ada_w, ada_b):
    depth, d, n = ada_w.shape
    r = c_all.shape[0]
    tn = 512
    return pl.pallas_call(
        _ada_kernel,
        grid=(depth, n // tn),
        in_specs=[
            pl.BlockSpec((r, d), lambda l, j: (0, 0)),
            pl.BlockSpec((None, d, tn), lambda l, j: (l, 0, j)),
            pl.BlockSpec((None, 1, tn), lambda l, j: (l, 0, j)),
        ],
        out_specs=pl.BlockSpec((None, r, tn), lambda l, j: (l, 0, j)),
        out_shape=jax.ShapeDtypeStruct((depth, r, n), F32),
        compiler_params=_cparams("parallel", "parallel"),
        name="ada",
    )(c_all, ada_w, ada_b.reshape(depth, 1, n))


def _inproj_kernel(x_ref, modt_ref, ng_ref, w_ref, cos_ref, sin_ref, qg_ref, kg_ref, gc_ref,
                   q_ref, k_ref, vt_ref, cbog_ref, u_ref, qv_ref, kk_ref, gg_ref):
    x = x_ref[...]
    rows = x.shape[0]
    h = _rms(x, ng_ref[0:1, :]) * (1.0 + _mod_rows(modt_ref, 1, rows)) + _mod_rows(modt_ref, 0, rows)
    hb = h.astype(BF16)

    def proj(c0, n):
        return _dot(hb, w_ref[:, c0:c0 + n])

    cos = cos_ref[...]
    sin = sin_ref[...]
    lane = lax.broadcasted_iota(jnp.int32, (rows, LANES), 1)
    first_half = (lane % 32) < 16
    lo = lane < HEAD_DIM

    def norm_rope(blk, g):
        b2 = blk * blk
        s_lo = jnp.sum(jnp.where(lo, b2, 0.0), axis=-1, keepdims=True)
        s_hi = jnp.sum(jnp.where(lo, 0.0, b2), axis=-1, keepdims=True)
        v = blk * lax.rsqrt(jnp.where(lo, s_lo, s_hi) * (1.0 / HEAD_DIM) + EPS) * g
        partner = jnp.where(first_half, pltpu.roll(v, LANES - 16, 1), pltpu.roll(v, 16, 1))
        return v * cos + partner * sin

    qa = proj(0, 4 * LANES)
    for j in range(4):
        qn = norm_rope(qa[:, LANES * j:LANES * (j + 1)], qg_ref[...]) * (ATTN_SCALE * LOG2_E)
        sw = pltpu.roll(qn, HEAD_DIM, 1)
        if j < 2:
            q_ref[2 * j] = jnp.where(lo, qn, 0.0).astype(BF16)
            q_ref[2 * j + 1] = jnp.where(lo, sw, 0.0).astype(BF16)
        else:
            q_ref[2 * j] = jnp.where(lo, 0.0, sw).astype(BF16)
            q_ref[2 * j + 1] = jnp.where(lo, 0.0, qn).astype(BF16)

    kv = proj(4 * LANES, 2 * LANES)
    k_ref[...] = norm_rope(kv[:, :LANES], kg_ref[...]).astype(BF16)
    vt = kv[:, LANES:].T
    ones = jnp.ones((VT_ROWS - HEAD_DIM, TM), F32)
    for s in range(rows // TM):
        for g in range(2):
            vt_ref[g, s] = jnp.concatenate(
                [vt[HEAD_DIM * g:HEAD_DIM * (g + 1), TM * s:TM * (s + 1)], ones], axis=0).astype(BF16)

    c0 = 6 * LANES
    c3 = proj(c0, 6 * LANES)
    cw = 2 * LANES
    u_ref[...] = c3[:, cw:2 * cw] * c3[:, 2 * cw:3 * cw]

    hg = proj(c0 + 6 * LANES, 10 * LANES)
    zf, zb, iv, hq, og = (hg[:, cw * i:cw * (i + 1)] for i in range(5))
    cbog_ref[:, :cw] = c3[:, :cw].astype(BF16)
    cbog_ref[:, cw:] = og.astype(BF16)
    qv_ref[:, :cw] = (hq * _sigmoid(hq)).astype(BF16)
    qv_ref[:, cw:] = iv.astype(BF16)
    for d, z in enumerate((zf, zb)):
        log_lb = gc_ref[0:1, cw * d:cw * (d + 1)]
        log1m_lb = gc_ref[1:2, cw * d:cw * (d + 1)]
        one_m_lb = gc_ref[2:3, cw * d:cw * (d + 1)]
        t = jnp.exp(-jnp.abs(z))
        log_sig = jnp.minimum(z, 0.0) - jnp.log(1.0 + t)
        a2 = log1m_lb + log_sig
        log_f = jnp.maximum(log_lb, a2) + jnp.log(1.0 + jnp.exp(-jnp.abs(log_lb - a2)))
        sig_neg = jnp.where(z >= 0, t, 1.0) / (1.0 + t)
        gg_ref[:, cw * d:cw * (d + 1)] = log_f
        kk_ref[:, cw * d:cw * (d + 1)] = (one_m_lb * sig_neg).astype(BF16)


def _inproj_call(x_flat, modt, ng, w1, cos_f, sin_f, qg, kg, gc):
    n, d = x_flat.shape
    nsub = FFN_TM // TM
    tok = lambda w: pl.BlockSpec((FFN_TM, w), lambda i: (i, 0))
    const = lambda a: pl.BlockSpec(a.shape, lambda i: (0,) * a.ndim, pipeline_mode=pl.Buffered(1))
    return pl.pallas_call(
        _inproj_kernel,
        grid=(n // FFN_TM,),
        in_specs=[
            tok(d),
            pl.BlockSpec((nsub, N_MOD, d), lambda i: (i, 0, 0)),
            const(ng), const(w1), tok(LANES), tok(LANES), const(qg), const(kg), const(gc),
        ],
        out_specs=[
            pl.BlockSpec((8, FFN_TM, LANES), lambda i: (0, i, 0)),
            tok(LANES),
            pl.BlockSpec((2, nsub, VT_ROWS, TM), lambda i: (0, i, 0, 0)),
            tok(4 * LANES),
            tok(2 * LANES),
            tok(4 * LANES),
            tok(4 * LANES),
            tok(4 * LANES),
        ],
        out_shape=[
            jax.ShapeDtypeStruct((8, n, LANES), BF16),
            jax.ShapeDtypeStruct((n, LANES), BF16),
            jax.ShapeDtypeStruct((2, n // TM, VT_ROWS, TM), BF16),
            jax.ShapeDtypeStruct((n, 4 * LANES), BF16),
            jax.ShapeDtypeStruct((n, 2 * LANES), F32),
            jax.ShapeDtypeStruct((n, 4 * LANES), BF16),
            jax.ShapeDtypeStruct((n, 4 * LANES), BF16),
            jax.ShapeDtypeStruct((n, 4 * LANES), F32),
        ],
        compiler_params=_cparams("parallel"),
        name="inproj",
    )(x_flat, modt, ng, w1, cos_f, sin_f, qg, kg, gc)


def _attn_kernel(q_ref, k_ref, vt_ref, o_ref, *, n_kv):
    def score(j, h):
        return _dot_nt(k_ref[TM * j:TM * (j + 1), :], q_ref[h])

    def attend(n_tiles):
        steps = [(j, h) for j in range(n_tiles) for h in range(8)]
        scores = [score(j, h) for j, h in steps[:ATTN_AHEAD]]
        m = [None] * 8
        acc = [None] * 8
        for i, (j, h) in enumerate(steps):
            if i + ATTN_AHEAD < len(steps):
                scores.append(score(*steps[i + ATTN_AHEAD]))
            s = scores.pop(0)
            m_tile = jnp.max(s, axis=0, keepdims=True)
            if j == 0:
                m_new = m_tile
                acc[h] = _dot(vt_ref[h // 4, j], jnp.exp2(s - m_new).astype(BF16))
            else:
                m_new = jnp.maximum(m[h], m_tile)
                pv = _dot(vt_ref[h // 4, j], jnp.exp2(s - m_new).astype(BF16))
                acc[h] = jnp.exp2(m[h] - m_new) * acc[h] + pv
            m[h] = m_new
        outs = [acc[h][:HEAD_DIM, :] / acc[h][HEAD_DIM:HEAD_DIM + 1, :] for h in range(8)]
        o_ref[...] = jnp.concatenate(outs, axis=0).T.astype(BF16)

    @pl.when(pl.program_id(1) == 0)
    def _():
        attend(1)

    @pl.when(pl.program_id(1) > 0)
    def _():
        attend(n_kv)


def _attn_call(q, k, vt, b):
    n = k.shape[0]
    t = n // b
    nt = t // TM
    return pl.pallas_call(
        functools.partial(_attn_kernel, n_kv=nt),
        grid=(b, nt),
        in_specs=[
            pl.BlockSpec((8, TM, LANES), lambda i, j: (0, i * nt + j, 0)),
            pl.BlockSpec((t, LANES), lambda i, j: (i, 0)),
            pl.BlockSpec((2, nt, VT_ROWS, TM), lambda i, j: (0, i, 0, 0)),
        ],
        out_specs=pl.BlockSpec((TM, 4 * LANES), lambda i, j: (i * nt + j, 0)),
        out_shape=jax.ShapeDtypeStruct((n, 4 * LANES), BF16),
        compiler_params=_cparams("parallel", "parallel"),
        name="attn",
    )(q, k, vt)


def _split3(g):
    hi = g.astype(BF16)
    r1 = g - hi.astype(F32)
    mid = r1.astype(BF16)
    lo = (r1 - mid.astype(F32)).astype(BF16)
    return hi, mid, lo


def _hgrn_kernel(qvf_ref, kf_ref, gf_ref, qvb_ref, kb_ref, gb_ref, tri_ref, trij_ref, j_ref, e_ref, bd_ref,
                 md_ref, mo_ref, of_ref, ob_ref, st_ref, x_s):
    hw = 2 * LANES
    nch = TM // HGRN_CHUNK

    @pl.when(pl.program_id(1) == 0)
    def _():
        st_ref[...] = jnp.zeros(st_ref.shape, F32)

    flip = j_ref[...]

    def cumsum(m_ref, g):
        m = m_ref[...]
        hi, mid, lo = _split3(g)
        return _dot(m, hi) + _dot(m, mid) + _dot(m, lo)

    qvf = qvf_ref[...]
    fl = _dot(flip, jnp.concatenate([qvb_ref[...], kb_ref[...]], axis=1))
    q = jnp.concatenate([qvf[:, :hw].astype(F32), fl[:, :hw]], axis=0)
    v = jnp.concatenate([qvf[:, hw:].astype(F32), fl[:, hw:2 * hw]], axis=0)
    k = jnp.concatenate([kf_ref[...].astype(F32), fl[:, 2 * hw:]], axis=0)
    b = jnp.concatenate([cumsum(tri_ref, gf_ref[...]), cumsum(trij_ref, gb_ref[...])], axis=0) * LOG2_E

    ngr = 2 * TM // HGRN_SUB
    nsub = HGRN_CHUNK // HGRN_SUB
    q3 = q.reshape(ngr, HGRN_SUB, hw)
    b3 = b.reshape(ngr, HGRN_SUB, hw)
    k3 = k.reshape(ngr, HGRN_SUB, hw)
    for s in range(HGRN_SUB):
        x = q3 * jnp.exp2(jnp.minimum(b3 - b3[:, s:s + 1, :], 0.0)) * k3[:, s:s + 1, :]
        x_s[:, hw * s:hw * (s + 1)] = x.reshape(2 * TM, hw).astype(BF16)
    a = _dot(x_s[...], e_ref[...])

    bd = bd_ref[...]
    bd16 = bd.astype(BF16)
    md = md_ref[...]
    mo = mo_ref[...]
    for d in range(2):
        st = st_ref[d]
        outs = []
        for c in range(nch):
            r0 = TM * d + HGRN_CHUNK * c
            bc = b[r0:r0 + HGRN_CHUNK]
            qc = q[r0:r0 + HGRN_CHUNK]
            kc = k[r0:r0 + HGRN_CHUNK]
            vc = v[r0:r0 + HGRN_CHUNK]
            ac = a[r0:r0 + HGRN_CHUNK] * md
            r_end = [bc[HGRN_SUB * (j + 1) - 1:HGRN_SUB * (j + 1), :] for j in range(nsub)]
            r_own = jnp.concatenate([jnp.broadcast_to(r, (HGRN_SUB, hw)) for r in r_end], axis=0)
            kt_bd = jnp.concatenate([(kc * jnp.exp2(r_own - bc)).astype(BF16)] * 4, axis=0) * bd16
            q_rel = jnp.concatenate(
                [qc * jnp.exp2(jnp.minimum(bc - r_end[j], 0.0)) for j in range(nsub - 1)], axis=0).astype(BF16)
            p = _dot_nt(q_rel, kt_bd) * mo
            for j in range(nsub - 1):
                ac = ac + p[HGRN_CHUNK * j:HGRN_CHUNK * (j + 1)]
            bl = r_end[-1]
            qe = (qc * jnp.exp2(bc)).astype(BF16)
            ke = (kc * jnp.exp2(bl - bc)).astype(BF16)
            vcb = vc.astype(BF16)
            vbd = jnp.concatenate([vcb] * 4, axis=0) * bd16
            outs.append(_dot(ac.astype(BF16), vbd) + _dot_nt(qe, st.astype(BF16)))
            st = st * jnp.exp2(bl) + _dot_tn(vcb, ke) * bd
        st_ref[d] = st
        od = jnp.concatenate(outs, axis=0).astype(BF16)
        if d == 0:
            of_ref[...] = od
        else:
            ob_ref[...] = _dot(flip, od).astype(BF16)


def _hgrn_consts():
    hw = 2 * LANES
    r = jnp.arange(TM)
    same_chunk = (r[:, None] // HGRN_CHUNK) == (r[None, :] // HGRN_CHUNK)
    tri = (same_chunk & (r[None, :] <= r[:, None])).astype(F32)
    flip = (r[:, None] + r[None, :] == TM - 1).astype(F32)
    trij = tri @ flip
    c = jnp.arange(hw)
    s = jnp.arange(HGRN_SUB)
    e = ((c[None, :, None] // HEAD_DIM == c[None, None, :] // HGRN_CHUNK)
         & (c[None, None, :] % HGRN_SUB == s[:, None, None])).astype(BF16).reshape(HGRN_SUB * hw, hw)
    bd = (c[:, None] // HEAD_DIM == c[None, :] // HEAD_DIM).astype(F32)
    tt = jnp.arange(HGRN_CHUNK)[:, None]
    ss = (c % HGRN_CHUNK)[None, :]
    md = ((ss // HGRN_SUB == tt // HGRN_SUB) & (ss <= tt)).astype(F32)
    mo = jnp.concatenate([((ss // HGRN_SUB == j) & (tt // HGRN_SUB > j)).astype(F32)
                          for j in range(HGRN_CHUNK // HGRN_SUB - 1)], axis=0)
    return tri.astype(BF16), trij.astype(BF16), flip.astype(BF16), e, bd, md, mo


def _hgrn_call(qv, kk, gg, consts, b):
    n = qv.shape[0]
    nt = n // b // TM
    hw = 2 * LANES
    tri, trij, flip, e, bd, md, mo = consts
    fwd = lambda i, j: i * nt + j
    bwd = lambda i, j: i * nt + jnp.where(j == 0, 0, nt - j)
    const2 = lambda a: pl.BlockSpec(a.shape, lambda i, j: (0,) * a.ndim)
    return pl.pallas_call(
        _hgrn_kernel,
        grid=(b, nt),
        in_specs=[
            pl.BlockSpec((TM, 2 * hw), lambda i, j: (fwd(i, j), 0)),
            pl.BlockSpec((TM, hw), lambda i, j: (fwd(i, j), 0)),
            pl.BlockSpec((TM, hw), lambda i, j: (fwd(i, j), 0)),
            pl.BlockSpec((TM, 2 * hw), lambda i, j: (bwd(i, j), 0)),
            pl.BlockSpec((TM, hw), lambda i, j: (bwd(i, j), 1)),
            pl.BlockSpec((TM, hw), lambda i, j: (bwd(i, j), 1)),
            const2(tri), const2(trij), const2(flip), const2(e), const2(bd), const2(md), const2(mo),
        ],
        out_specs=[
            pl.BlockSpec((TM, hw), lambda i, j: (fwd(i, j), 0)),
            pl.BlockSpec((TM, hw), lambda i, j: (bwd(i, j), 0)),
        ],
        out_shape=[jax.ShapeDtypeStruct((n, hw), BF16)] * 2,
        scratch_shapes=[
            pltpu.VMEM((2, hw, hw), F32),
            pltpu.VMEM((2 * TM, HGRN_SUB * hw), BF16),
        ],
        compiler_params=_cparams("parallel", "arbitrary"),
        name="hgrn",
    )(qv, kk, gg, qv, kk, gg, tri, trij, flip, e, bd, md, mo)


def _mix_out(x_ref, modt_ref, ng_ref, attn_ref, cbog_ref, u_ref, up_ref, un_ref, of_ref, ob_ref,
             hgg_ref, cw_ref, hn_ref, w_ref, *, n_tiles):
    rows = x_ref.shape[0]
    nsub = rows // TM
    cw = 2 * LANES
    u = u_ref[...]
    row = lax.broadcasted_iota(jnp.int32, (rows, cw), 0)
    u_prev = jnp.where(row == 0, up_ref[7:8, :], pltpu.roll(u, 1, 0))
    u_next = jnp.where(row == rows - 1, un_ref[0:1, :], pltpu.roll(u, rows - 1, 0))
    for s in range(nsub):
        t = (pl.program_id(0) * nsub + s) % n_tiles
        u_prev = jnp.where(t >= 2, u_prev, jnp.where(row == s * TM, 0.0, u_prev))
        u_next = jnp.where((t >= 1) & (t < n_tiles - 1), u_next, jnp.where(row == s * TM + TM - 1, 0.0, u_next))
    cbog = cbog_ref[...].astype(F32)
    conv = cbog[:, :cw] * (cw_ref[0:1, :] * u_prev + cw_ref[1:2, :] * u + cw_ref[2:3, :] * u_next)

    o = of_ref[...].astype(F32) + ob_ref[...].astype(F32)
    ms = _dot((o * o).astype(BF16), hn_ref[...])
    og = cbog[:, cw:]
    hg = o * lax.rsqrt(ms + EPS) * hgg_ref[...] * (og * _sigmoid(og))

    y = (_dot(attn_ref[...], w_ref[0:2 * cw, :]) + _dot(conv.astype(BF16), w_ref[2 * cw:3 * cw, :])
         + _dot(hg.astype(BF16), w_ref[3 * cw:4 * cw, :]))
    return x_ref[...] + _mod_rows(modt_ref, 2, rows) * _rms(y, ng_ref[1:2, :])


def _mod_rows(modt_ref, i, rows):
    sub = rows // modt_ref.shape[0]
    return jnp.concatenate(
        [jnp.broadcast_to(modt_ref[s, i:i + 1, :], (sub, modt_ref.shape[2])) for s in range(modt_ref.shape[0])],
        axis=0)


def _store_split(ref, val):
    w = ref.shape[2]
    for c in range(ref.shape[0]):
        lo = val[:, 2 * w * c:2 * w * c + w].astype(BF16).astype(F32)
        hi = val[:, 2 * w * c + w:2 * w * (c + 1)].astype(BF16).astype(F32)
        ref[c] = lax.bitcast_convert_type(hi, jnp.uint32) | (lax.bitcast_convert_type(lo, jnp.uint32) >> 16)


def _load_split(ref):
    cols = []
    for c in range(ref.shape[0]):
        word = ref[c]
        cols.append(lax.bitcast_convert_type(word << 16, F32))
        cols.append(lax.bitcast_convert_type(word & jnp.uint32(0xFFFF0000), F32))
    return jnp.concatenate(cols, axis=1)


def _split_shape(n, d):
    return jax.ShapeDtypeStruct((GATHER_SPLIT, n, d // GATHER_SPLIT // 2), jnp.uint32)


def _split_spec(rows):
    return lambda d: pl.BlockSpec((GATHER_SPLIT, rows, d // GATHER_SPLIT // 2), lambda i, *_: (0, i, 0))


def _ffn_body(x, modt_ref, ng_ref, wg_ref, wu_ref, wd_ref):
    rows = x.shape[0]
    h = (_rms(x, ng_ref[2:3, :]) * (1.0 + _mod_rows(modt_ref, 4, rows)) + _mod_rows(modt_ref, 3, rows)).astype(BF16)
    acc = jnp.zeros(x.shape, F32)
    dff = wg_ref.shape[1]
    for f0 in range(0, dff, FFN_TF):
        f1 = min(f0 + FFN_TF, dff)
        g = _dot(h, wg_ref[:, f0:f1])
        up = _dot(h, wu_ref[:, f0:f1])
        acc = acc + _dot((g * _sigmoid(g) * up).astype(BF16), wd_ref[f0:f1, :])
    return x + _mod_rows(modt_ref, 5, rows) * _rms(acc, ng_ref[3:4, :])


def _router_body(x, modt_ref, ng_ref, wr_ref, h_ref, r_ref, rt_ref):
    rows = x.shape[0]
    h = _rms(x, ng_ref[2:3, :]) * (1.0 + _mod_rows(modt_ref, 4, rows)) + _mod_rows(modt_ref, 3, rows)
    _store_split(h_ref, h)
    h_hi = h.astype(BF16)
    h_lo = (h - h_hi.astype(F32)).astype(BF16)
    logits = _dot(h_hi, wr_ref[0]) + _dot(h_lo, wr_ref[0]) + _dot(h_hi, wr_ref[1])
    lane = lax.broadcasted_iota(jnp.int32, logits.shape, 1)
    lg = jnp.where(lane < N_EXPERTS, logits, NEG_BIG)
    m1 = jnp.max(lg, axis=-1, keepdims=True)
    i1 = jnp.min(jnp.where(lg == m1, lane, LANES), axis=-1, keepdims=True)
    lg2 = jnp.where(lane == i1, NEG_BIG, lg)
    m2 = jnp.max(lg2, axis=-1, keepdims=True)
    i2 = jnp.min(jnp.where(lg2 == m2, lane, LANES), axis=-1, keepdims=True)
    e2 = jnp.exp(m2 - m1)
    w1 = 1.0 / (1.0 + e2)
    w2 = e2 / (1.0 + e2)
    r = jnp.where(lane == 0, i1.astype(F32),
                  jnp.where(lane == 1, i2.astype(F32), jnp.where(lane == 2, w1, jnp.where(lane == 3, w2, 0.0))))
    r_ref[...] = r
    rt_ref[...] = r.T[0:8, :]


def _mixer_kernel(*refs, fused, mode, n_tiles):
    x_ref, modt_ref, ng_ref = refs[:3]
    pos = 3
    if fused:
        x = _mix_out(x_ref, modt_ref, ng_ref, *refs[pos:pos + N_MIX_REFS], n_tiles=n_tiles)
        pos += N_MIX_REFS
    else:
        x = x_ref[...]
    if mode == "ffn":
        wg_ref, wu_ref, wd_ref, o_ref = refs[pos:]
        o_ref[...] = _ffn_body(x, modt_ref, ng_ref, wg_ref, wu_ref, wd_ref)
    elif mode == "router":
        wr_ref, *outs = refs[pos:]
        _router_body(x, modt_ref, ng_ref, wr_ref, *outs)
    else:
        refs[pos][...] = x


def _mixer_call(x_flat, modt, ng, mix, mode, weights, n_tiles=0):
    n, d = x_flat.shape
    nsub = FFN_TM // TM
    cw = 2 * LANES
    tok = lambda w: pl.BlockSpec((FFN_TM, w), lambda i: (i, 0))
    resident = lambda a: pl.BlockSpec(a.shape, lambda i: (0,) * a.ndim, pipeline_mode=pl.Buffered(1))
    stream = jax.ShapeDtypeStruct((n, d), F32)
    args = [x_flat, modt, ng]
    in_specs = [tok(d), pl.BlockSpec((nsub, N_MOD, d), lambda i: (i, 0, 0)), pl.BlockSpec((8, d), lambda i: (0, 0))]
    fused = mix is not None
    if fused:
        attn, cbog, u, o_f, o_b, hgg, conv_w, hn, w_out = mix
        nb8 = n // 8
        args += [attn, cbog, u, u, u, o_f, o_b, hgg, conv_w, hn, w_out]
        in_specs += [
            tok(2 * cw), tok(2 * cw), tok(cw),
            pl.BlockSpec((8, cw), lambda i: (jnp.maximum(i * (FFN_TM // 8) - 1, 0), 0)),
            pl.BlockSpec((8, cw), lambda i: (jnp.minimum((i + 1) * (FFN_TM // 8), nb8 - 1), 0)),
            tok(cw), tok(cw), resident(hgg), resident(conv_w), resident(hn), resident(w_out),
        ]
        assert len(in_specs) == 3 + N_MIX_REFS
    args += list(weights)
    in_specs += [resident(w) for w in weights]
    if mode == "router":
        out_specs = [_split_spec(FFN_TM)(d), tok(LANES), pl.BlockSpec((8, FFN_TM), lambda i: (0, i))]
        out_shape = [_split_shape(n, d),
                     jax.ShapeDtypeStruct((n, LANES), F32), jax.ShapeDtypeStruct((8, n), F32)]
    else:
        out_specs, out_shape = tok(d), stream
    return pl.pallas_call(
        functools.partial(_mixer_kernel, fused=fused, mode=mode, n_tiles=n_tiles),
        grid=(n // FFN_TM,),
        in_specs=in_specs,
        out_specs=out_specs,
        out_shape=out_shape,
        compiler_params=_cparams("parallel"),
        name=("mix_" if fused else "") + mode,
    )(*args)


def _gather_flat(data, idx):
    m = idx.shape[0]
    w = data.shape[1]
    mesh = plsc.VectorSubcoreMesh(core_axis_name="core", subcore_axis_name="subcore")

    @functools.partial(pl.kernel, out_type=jax.ShapeDtypeStruct((m, w), data.dtype), mesh=mesh)
    def gather(x_hbm, i_hbm, o_hbm):
        def body(i_vmem, o_vmem):
            pltpu.sync_copy(x_hbm.at[i_vmem.at[0]], o_vmem)

        pltpu.emit_pipeline(
            body,
            grid=(m // GATHER_WIN,),
            in_specs=[pl.BlockSpec((1, GATHER_WIN), lambda i: (0, i))],
            out_specs=[pl.BlockSpec((GATHER_WIN, w), lambda i: (i, 0))],
            core_axis_name=("core", "subcore"),
            dimension_semantics=(pltpu.PARALLEL,),
        )(i_hbm, o_hbm)

    return gather(data, idx.reshape(1, m))


def _scatter_flat(rows, idx, n_out):
    m, w = rows.shape
    mesh = plsc.VectorSubcoreMesh(core_axis_name="core", subcore_axis_name="subcore")

    @functools.partial(pl.kernel, out_type=jax.ShapeDtypeStruct((n_out, w), rows.dtype), mesh=mesh)
    def scatter(x_hbm, i_hbm, o_hbm):
        def body(x_vmem, i_vmem):
            pltpu.sync_copy(x_vmem, o_hbm.at[i_vmem.at[0]])

        pltpu.emit_pipeline(
            body,
            grid=(m // GATHER_WIN,),
            in_specs=[pl.BlockSpec((GATHER_WIN, w), lambda i: (i, 0)),
                      pl.BlockSpec((1, GATHER_WIN), lambda i: (0, i))],
            out_specs=[],
            core_axis_name=("core", "subcore"),
            dimension_semantics=(pltpu.PARALLEL,),
        )(x_hbm, i_hbm)

    return scatter(rows, idx.reshape(1, m))


def _col0_kernel(x_ref, o_ref):
    o_ref[...] = x_ref[...].T[0:1, :]


def _first_column(rows):
    n = rows.shape[0]
    blk = 1024
    return pl.pallas_call(
        _col0_kernel,
        grid=(n // blk,),
        in_specs=[pl.BlockSpec((blk, LANES), lambda i: (i, 0))],
        out_specs=pl.BlockSpec((1, blk), lambda i: (0, i)),
        out_shape=jax.ShapeDtypeStruct((1, n), rows.dtype),
        compiler_params=_cparams("parallel"),
        name="col0",
    )(rows)


def _cast_kernel(*refs):
    o_ref = refs[-1]
    cw = refs[0].shape[1]
    for s, x_ref in enumerate(refs[:-1]):
        o_ref[:, cw * s:cw * (s + 1)] = x_ref[...].astype(o_ref.dtype)


def _cast_bf16(w, layer):
    _, e, r, c = w.shape
    tr = 256
    cw = c // CAST_STREAMS
    chunk = lambda s: pl.BlockSpec((None, None, tr, cw), lambda i, j: (layer, i, j, s))
    return pl.pallas_call(
        _cast_kernel,
        grid=(e, r // tr),
        in_specs=[chunk(s) for s in range(CAST_STREAMS)],
        out_specs=pl.BlockSpec((None, tr, c), lambda i, j: (i, j, 0)),
        out_shape=jax.ShapeDtypeStruct((e, r, c), BF16),
        compiler_params=_cparams("parallel", "parallel"),
        name="cast",
    )(*([w] * CAST_STREAMS))


def _gather_rows(data, idx):
    s, r, w = data.shape
    m = idx.shape[0]
    idx_all = (idx[None, :] + (jnp.arange(s, dtype=jnp.int32) * r)[:, None]).reshape(s * m)
    return _gather_flat(data.reshape(s * r, w), idx_all).reshape(s, m, w)


def _expert_kernel(be_ref, nv_ref, xs_ref, wg_ref, wu_ref, wd_ref, o_ref):
    valid = pl.program_id(0) < nv_ref[0]

    @pl.when(valid)
    def _():
        xb = _load_split(xs_ref).astype(BF16)
        acc = jnp.zeros((xb.shape[0], wd_ref.shape[1]), F32)
        for f0 in range(0, wg_ref.shape[1], MOE_TF):
            g = _dot(xb, wg_ref[:, f0:f0 + MOE_TF])
            up = _dot(xb, wu_ref[:, f0:f0 + MOE_TF])
            acc = acc + _dot((g * _sigmoid(g) * up).astype(BF16), wd_ref[f0:f0 + MOE_TF, :])
        _store_split(o_ref, acc)

    @pl.when(jnp.logical_not(valid))
    def _():
        o_ref[...] = jnp.zeros(o_ref.shape, o_ref.dtype)


def _expert_call(block_expert, n_valid, xs, wg, wu, wd):
    n_slots = xs.shape[1]
    d = wg.shape[1]
    dff = wg.shape[2]
    nb = n_slots // MOE_BLK
    resident = lambda r, c: pl.BlockSpec((None, r, c), lambda i, be, nv: (be[i], 0, 0), pipeline_mode=pl.Buffered(1))
    grid_spec = pltpu.PrefetchScalarGridSpec(
        num_scalar_prefetch=2,
        grid=(nb,),
        in_specs=[_split_spec(MOE_BLK)(d), resident(d, dff), resident(d, dff), resident(dff, d)],
        out_specs=_split_spec(MOE_BLK)(d),
    )
    return pl.pallas_call(
        _expert_kernel,
        grid_spec=grid_spec,
        out_shape=_split_shape(n_slots, d),
        compiler_params=_cparams("arbitrary"),
        name="experts",
    )(block_expert, n_valid, xs, wg, wu, wd)


def _combine_kernel(x_ref, modt_ref, ng_ref, y1_ref, y2_ref, r_ref, o_ref):
    x = x_ref[...]
    rows = x.shape[0]
    f = r_ref[:, 2:3] * _load_split(y1_ref) + r_ref[:, 3:4] * _load_split(y2_ref)
    o_ref[...] = x + _mod_rows(modt_ref, 5, rows) * _rms(f, ng_ref[3:4, :])


def _combine_call(x_flat, modt, ng, y12, rinfo):
    n, d = x_flat.shape
    nsub = FFN_TM // TM
    nblk = n // FFN_TM
    tok = lambda w: pl.BlockSpec((FFN_TM, w), lambda i: (i, 0))
    second = pl.BlockSpec((GATHER_SPLIT, FFN_TM, d // GATHER_SPLIT // 2), lambda i: (0, i + nblk, 0))
    return pl.pallas_call(
        _combine_kernel,
        grid=(nblk,),
        in_specs=[tok(d), pl.BlockSpec((nsub, N_MOD, d), lambda i: (i, 0, 0)), pl.BlockSpec((8, d), lambda i: (0, 0)),
                  _split_spec(FFN_TM)(d), second, tok(LANES)],
        out_specs=tok(d),
        out_shape=jax.ShapeDtypeStruct((n, d), F32),
        compiler_params=_cparams("parallel"),
        name="combine",
    )(x_flat, modt, ng, y12, y12, rinfo)


def _router_weights(w_router):
    wr = jnp.pad(w_router, ((0, 0), (0, LANES - N_EXPERTS)))
    wr_hi = wr.astype(BF16)
    wr_lo = (wr - wr_hi.astype(F32)).astype(BF16)
    return jnp.stack([wr_hi, wr_lo])


def _moe_layer(x_flat, h, rinfo, rinfo_t, modt, ng, wg, wu, wd):
    n, d = x_flat.shape
    n_assign = 2 * n
    expert = jnp.concatenate([rinfo_t[0], rinfo_t[1]]).astype(jnp.int32)
    onehot = (expert[None, :] == jnp.arange(N_EXPERTS, dtype=jnp.int32)[:, None]).astype(jnp.int32)
    csum = jnp.cumsum(onehot, axis=1)
    rank = jnp.sum((csum - onehot) * onehot, axis=0)
    counts = csum[:, -1]
    padded = (counts + MOE_BLK - 1) // MOE_BLK * MOE_BLK
    pad_end = jnp.cumsum(padded)
    pad_start = pad_end - padded
    dest = jnp.sum(onehot * pad_start[:, None], axis=0) + rank
    nb = -(-n_assign // MOE_BLK) + N_EXPERTS
    row_quantum = 32 * GATHER_WIN // GATHER_SPLIT
    nb = -(-nb * MOE_BLK // row_quantum) * row_quantum // MOE_BLK
    n_slots = nb * MOE_BLK
    block_expert = jnp.minimum(
        jnp.searchsorted(pad_end, jnp.arange(nb, dtype=jnp.int32) * MOE_BLK, side="right"), N_EXPERTS - 1
    ).astype(jnp.int32)
    n_valid = (pad_end[-1:] // MOE_BLK).astype(jnp.int32)
    tok_rows = jnp.broadcast_to((jnp.arange(n_assign, dtype=jnp.int32) % n)[:, None], (n_assign, LANES))
    slot = jnp.arange(n_slots, dtype=jnp.int32)
    slot_used = (slot - jnp.repeat(pad_start[block_expert], MOE_BLK)) < jnp.repeat(counts[block_expert], MOE_BLK)
    slot_tok = jnp.where(slot_used, _first_column(_scatter_flat(tok_rows, dest, n_slots)).reshape(n_slots), 0)

    xs = _gather_rows(h, slot_tok)
    ys = _expert_call(block_expert, n_valid, xs, wg, wu, wd)
    y12 = _gather_rows(ys, dest)
    return _combine_call(x_flat, modt, ng, y12, rinfo)


def _rope_tables(n_ctx, length):
    rows = length // GRID_W
    row = jnp.repeat(jnp.arange(rows, dtype=F32), GRID_W)
    col = jnp.tile(jnp.arange(GRID_W, dtype=F32), rows)
    axis_dim = HEAD_DIM // 2
    inv_freq = ROPE_THETA ** (-jnp.arange(0, axis_dim, 2, dtype=F32) / axis_dim)
    ar = row[:, None] * inv_freq
    ac = col[:, None] * inv_freq
    cos = jnp.concatenate([jnp.cos(ar), jnp.cos(ar), jnp.cos(ac), jnp.cos(ac)], axis=1)
    sin = jnp.concatenate([-jnp.sin(ar), jnp.sin(ar), -jnp.sin(ac), jnp.sin(ac)], axis=1)
    cos = jnp.concatenate([jnp.ones((n_ctx, HEAD_DIM), F32), cos], axis=0)
    sin = jnp.concatenate([jnp.zeros((n_ctx, HEAD_DIM), F32), sin], axis=0)
    return jnp.tile(cos, (1, 2)), jnp.tile(sin, (1, 2))


def kernel(x, c, ctx, c_ctx, ada_w, ada_b, norm_g, w_in, w_out, attn_q_g, attn_k_g, conv_w, hgrn_lb, hgrn_g,
           ffn_w_gate, ffn_w_up, ffn_w_down, moe_router, moe_w_gate, moe_w_up, moe_w_down):
    bsz, length, d = x.shape
    n_ctx = ctx.shape[1]
    depth = w_in.shape[0]
    t = n_ctx + length
    nt = t // TM
    assert n_ctx == TM and length % TM == 0 and d == 8 * LANES
    for n_tok in (bsz * t, bsz * length):
        assert n_tok % FFN_TM == 0 and (2 * n_tok) % (32 * GATHER_WIN) == 0

    r_pad = -(-(bsz + 1) // 8) * 8
    c_all = jnp.concatenate([c, c_ctx[None, :], jnp.zeros((r_pad - bsz - 1, d), F32)], axis=0)
    mod_all = _ada_call(c_all, ada_w, ada_b).reshape(depth, r_pad, N_MOD, d)

    cos_t, sin_t = _rope_tables(n_ctx, length)
    lb_all = jnp.cumsum(jax.nn.softmax(hgrn_lb.astype(F32), axis=0), axis=0)
    lb_all = lb_all - lb_all[0]
    hconsts = _hgrn_consts()
    cw = 2 * LANES
    cidx = jnp.arange(cw)
    head_mean = ((cidx[:, None] // HEAD_DIM == cidx[None, :] // HEAD_DIM).astype(F32) / HEAD_DIM).astype(BF16)

    cos_f = jnp.tile(cos_t, (bsz, 1))
    sin_f = jnp.tile(sin_t, (bsz, 1))
    x_flat = jnp.concatenate([ctx, x], axis=1).reshape(bsz * t, d)
    for l in range(depth):
        mod_lat = mod_all[l, :bsz]
        mod_ctx = jnp.broadcast_to(mod_all[l, bsz][None], (bsz, N_MOD, d))
        mod = jnp.stack([mod_ctx, mod_lat], axis=1)
        modt = jnp.concatenate([mod[:, :1], jnp.broadcast_to(mod[:, 1:], (bsz, nt - 1, N_MOD, d))], axis=1)
        modt = modt.reshape(bsz * nt, N_MOD, d)
        ng = jnp.pad(norm_g[l], ((0, 4), (0, 0)))
        lb = lb_all[l].reshape(1, 2 * cw)
        gc = jnp.concatenate([jnp.log(lb), jnp.log1p(-lb), 1.0 - lb, jnp.zeros((5, 2 * cw), F32)], axis=0)
        qg = jnp.tile(attn_q_g[l], 2)[None, :]
        kg = jnp.tile(attn_k_g[l], 2)[None, :]

        q, k, vt, cbog, u, qv, kk, gg = _inproj_call(x_flat, modt, ng, w_in[l].astype(BF16), cos_f, sin_f, qg, kg, gc)
        attn = _attn_call(q, k, vt, bsz)
        o_f, o_b = _hgrn_call(qv, kk, gg, hconsts, bsz)
        mix = (attn, cbog, u, o_f, o_b, hgrn_g[l][None, :],
               jnp.pad(conv_w[l], ((0, 5), (0, 0))), head_mean, w_out[l].astype(BF16))
        if l % 2 == 0:
            weights = (ffn_w_gate[l // 2].astype(BF16), ffn_w_up[l // 2].astype(BF16), ffn_w_down[l // 2].astype(BF16))
        else:
            weights = (_router_weights(moe_router[l // 2]),)
            experts = (_cast_bf16(moe_w_gate, l // 2), _cast_bf16(moe_w_up, l // 2), _cast_bf16(moe_w_down, l // 2))
        mode = "ffn" if l % 2 == 0 else "router"

        if l < depth - 1:
            if mode == "ffn":
                x_flat = _mixer_call(x_flat, modt, ng, mix, mode, weights, nt)
            else:
                x_flat = _mixer_call(x_flat, modt, ng, mix, "plain", (), nt)
                h, rinfo, rinfo_t = _mixer_call(x_flat, modt, ng, None, mode, weights)
                x_flat = _moe_layer(x_flat, h, rinfo, rinfo_t, modt, ng, *experts)
            continue

        x_all = _mixer_call(x_flat, modt, ng, mix, "plain", (), nt).reshape(bsz, t, d)
        x_flat = x_all[:, n_ctx:, :].reshape(bsz * length, d)
        modt = jnp.broadcast_to(mod[:, 1:], (bsz, length // TM, N_MOD, d)).reshape(bsz * (length // TM), N_MOD, d)
        if mode == "ffn":
            x_flat = _mixer_call(x_flat, modt, ng, None, mode, weights)
        else:
            h, rinfo, rinfo_t = _mixer_call(x_flat, modt, ng, None, mode, weights)
            x_flat = _moe_layer(x_flat, h, rinfo, rinfo_t, modt, ng, *experts)
        return x_flat.reshape(bsz, length, d)
```
